```python
import math
import jax, jax.numpy as jnp
from jax import lax
import numpy as np

D_MODEL = 4096
BATCH = 2
SEQ = 4096
DEPTH = 2

ROPE_THETA = 500000.0
Q_BLOCK = 128

GLA_HEADS = 4
GLA_DK = 192
GLA_DV = 384
GLA_GATE_RANK = 16
GLA_TAU = 16.0
GLA_CHUNK = 64

DIFF_HEADS = 10
DIFF_D = 64

DSA_HEADS = 10
DSA_KV_HEADS = 2
DSA_HEAD_DIM = 128
IDX_HEADS = 32
IDX_DIM = 64
IDX_TOPK_MAX = 256

MIX_GLA = GLA_HEADS * GLA_DV
MIX_DIFF = DIFF_HEADS * 2 * DIFF_D
MIX_DSA = DSA_HEADS * DSA_HEAD_DIM
MIX_WIDTH = MIX_GLA + MIX_DIFF + MIX_DSA

IN_SPLITS = (
    GLA_HEADS * GLA_DK, GLA_HEADS * GLA_DK, GLA_HEADS * GLA_DV, GLA_HEADS * GLA_DV, GLA_GATE_RANK,
    DIFF_HEADS * 2 * DIFF_D, DIFF_HEADS * 2 * DIFF_D, DIFF_HEADS * 2 * DIFF_D,
    DSA_HEADS * DSA_HEAD_DIM, DSA_KV_HEADS * DSA_HEAD_DIM, DSA_KV_HEADS * DSA_HEAD_DIM,
    IDX_HEADS * IDX_DIM, IDX_DIM, IDX_HEADS,
)
N_IN = sum(IN_SPLITS)
SPLIT_POINTS = tuple(int(v) for v in np.cumsum(IN_SPLITS)[:-1])

D_FF = 11008
N_EXPERTS = 8
MOE_TOP_K = 2
D_FF_EXPERT = 4096
MOE_BLOCK = 128
N_DENSE = (DEPTH + 1) // 2
N_MOE = DEPTH // 2

DEEPNORM_ALPHA = (2 * DEPTH) ** 0.25
DEEPNORM_BETA = (8 * DEPTH) ** -0.25

kernel_name = "hybrid_gla_diff_dsa_deepnorm_moe"


def layer_norm(x, g, b, eps=1e-5):
    xf = x.astype(jnp.float32)
    mu = jnp.mean(xf, -1, keepdims=True)
    var = jnp.mean(jnp.square(xf - mu), -1, keepdims=True)
    return ((xf - mu) * lax.rsqrt(var + eps) * g.astype(jnp.float32) + b.astype(jnp.float32)).astype(x.dtype)


def rms_norm(x, g, eps=1e-5):
    xf = x.astype(jnp.float32)
    return (xf * lax.rsqrt(jnp.mean(xf * xf, -1, keepdims=True) + eps) * g.astype(jnp.float32)).astype(x.dtype)


def partial_rope(x, positions):
    d = x.shape[-1]
    rot = d // 4
    half = rot // 2
    inv_freq = 1.0 / (ROPE_THETA ** (jnp.arange(half, dtype=jnp.float32) * 2.0 / rot))
    ang = positions.astype(jnp.float32)[:, :, None, None] * inv_freq
    cos, sin = jnp.cos(ang), jnp.sin(ang)
    xf = x.astype(jnp.float32)
    x1, x2, rest = xf[..., :half], xf[..., half:rot], xf[..., rot:]
    out = jnp.concatenate([x1 * cos - x2 * sin, x2 * cos + x1 * sin, rest], axis=-1)
    return out.astype(x.dtype)


def gla_mixer(q, k, v, log_a):
    B, S, H, dk = q.shape
    dv = v.shape[-1]
    C = GLA_CHUNK
    nc = S // C

    def to_chunks(t):
        return t.astype(jnp.float32).reshape(B, nc, C, H, t.shape[-1]).transpose(1, 0, 3, 2, 4)

    qc = to_chunks(q) * (dk ** -0.5)
    kc, vc, ac = to_chunks(k), to_chunks(v), to_chunks(log_a)
    causal = jnp.tril(jnp.ones((C, C), dtype=bool))

    def step(state, inp):
        qi, ki, vi, ai = inp
        b = jnp.cumsum(ai, axis=-2)
        rel = jnp.where(causal[:, :, None], b[..., :, None, :] - b[..., None, :, :], -jnp.inf)
        scores = jnp.einsum('bhid,bhjd,bhijd->bhij', qi, ki, jnp.exp(rel))
        o = (jnp.einsum('bhij,bhje->bhie', scores, vi)
             + jnp.einsum('bhid,bhde->bhie', qi * jnp.exp(b), state))
        b_last = b[..., -1:, :]
        state = (jnp.exp(b_last[..., 0, :])[..., None] * state
                 + jnp.einsum('bhjd,bhje->bhde', ki * jnp.exp(b_last - b), vi))
        return state, o

    state0 = jnp.zeros((B, H, dk, dv), jnp.float32)
    _, o = lax.scan(step, state0, (qc, kc, vc, ac))
    return o.transpose(1, 0, 3, 2, 4).reshape(B, S, H, dv)


def diff_attention(q, k, v, lam, lambda_init, norm_g):
    B, S, H, _, d = q.shape
    nb = S // Q_BLOCK
    qb = q.reshape(B, nb, Q_BLOCK, H, 2, d).swapaxes(0, 1)
    key_pos = jnp.arange(S)
    scale = d ** -0.5

    def block(args):
        qi, bi = args
        qpos = bi * Q_BLOCK + jnp.arange(Q_BLOCK)
        s = jnp.einsum('bqhcd,bkhcd->bchqk', qi, k).astype(jnp.float32) * scale
        s = jnp.where(key_pos[None, :] <= qpos[:, None], s, -jnp.inf)
        p = jax.nn.softmax(s, axis=-1)
        a = p[:, 0] - lam * p[:, 1]
        return jnp.einsum('bhqk,bkhe->bqhe', a.astype(v.dtype), v)

    o = lax.map(block, (qb, jnp.arange(nb)))
    o = o.swapaxes(0, 1).reshape(B, S, H, 2 * d)
    return rms_norm(o, norm_g) * (1.0 - lambda_init)


def dsa_attention(q, k, v, q_idx, k_idx, w_idx):
    B, S, H, dh = q.shape
    G = k.shape[2]
    R = H // G
    k_sel = min(IDX_TOPK_MAX, S // 4)
    nb = S // Q_BLOCK
    key_pos = jnp.arange(S)
    idx_scale = (IDX_DIM ** -0.5) * (IDX_HEADS ** -0.5)

    def blocks(t):
        return t.reshape(B, nb, Q_BLOCK, *t.shape[2:]).swapaxes(0, 1)

    def block(args):
        qb, qib, wib, bi = args
        qpos = bi * Q_BLOCK + jnp.arange(Q_BLOCK)
        causal = key_pos[None, :] <= qpos[:, None]
        logits = jax.nn.relu(jnp.einsum('bqhd,bsd->bqhs', qib, k_idx).astype(jnp.float32))
        score = jnp.einsum('bqh,bqhs->bqs', wib.astype(jnp.float32), logits) * idx_scale
        score = jnp.where(causal[None], score, -jnp.inf)
        _, sel = lax.top_k(score, k_sel)
        kg = jax.vmap(lambda kk, ii: kk[ii])(k, sel)
        vg = jax.vmap(lambda vv, ii: vv[ii])(v, sel)
        valid = sel <= qpos[None, :, None]
        qg = qb.reshape(B, Q_BLOCK, G, R, dh)
        s = jnp.einsum('bqgrd,bqkgd->bqgrk', qg, kg).astype(jnp.float32) * (dh ** -0.5)
        s = jnp.where(valid[:, :, None, None, :], s, -jnp.inf)
        p = jax.nn.softmax(s, axis=-1).astype(v.dtype)
        o = jnp.einsum('bqgrk,bqkgd->bqgrd', p, vg)
        return o.reshape(B, Q_BLOCK, H * dh)

    o = lax.map(block, (blocks(q), blocks(q_idx), blocks(w_idx), jnp.arange(nb)))
    return o.swapaxes(0, 1).reshape(B, S, H * dh)


def swiglu(h, w1, w3, w2):
    return (jax.nn.silu(h @ w1) * (h @ w3)) @ w2


def moe_swiglu(h, router_w, router_b, w1, w3, w2):
    B, S, D = h.shape
    N = B * S
    E = w1.shape[0]
    hf = h.reshape(N, D)
    logits = (hf @ router_w).astype(jnp.float32) + router_b.astype(jnp.float32)
    top_logit, top_e = lax.top_k(logits, MOE_TOP_K)
    gates = jax.nn.softmax(top_logit, axis=-1)
    n_assign = N * MOE_TOP_K
    flat_e = top_e.reshape(-1)
    flat_tok = jnp.repeat(jnp.arange(N, dtype=jnp.int32), MOE_TOP_K)
    flat_g = gates.reshape(-1)
    order = jnp.argsort(flat_e)
    se = flat_e[order]
    counts = jnp.bincount(flat_e, length=E)
    padded = (counts + MOE_BLOCK - 1) // MOE_BLOCK * MOE_BLOCK
    start_sorted = jnp.cumsum(counts) - counts
    end_padded = jnp.cumsum(padded)
    start_padded = end_padded - padded
    dest = start_padded[se] + jnp.arange(n_assign) - start_sorted[se]
    n_blocks = -(-(n_assign + E * MOE_BLOCK) // MOE_BLOCK)
    n_rows = n_blocks * MOE_BLOCK
    row_tok = jnp.zeros((n_rows,), jnp.int32).at[dest].set(flat_tok[order])
    row_gate = jnp.zeros((n_rows,), jnp.float32).at[dest].set(flat_g[order])
    block_start = jnp.arange(n_blocks) * MOE_BLOCK
    block_e = jnp.minimum(jnp.searchsorted(end_padded, block_start, side='right'), E - 1)

    def block(args):
        toks, e = args
        xb = hf[toks]
        return (jax.nn.silu(xb @ w1[e]) * (xb @ w3[e])) @ w2[e]

    y = lax.map(block, (row_tok.reshape(n_blocks, MOE_BLOCK), block_e))
    y = y.reshape(n_rows, D) * row_gate[:, None].astype(h.dtype)
    out = jnp.zeros((N, D), h.dtype).at[row_tok].add(y)
    return out.reshape(B, S, D)


def setup_inputs(seed: int = 0) -> dict:
    key = jax.random.key(seed)
    ks = jax.random.split(key, 24)
    n = jax.random.normal
    f32 = jnp.float32
    return {
        "x": n(ks[0], (BATCH, SEQ, D_MODEL), f32),
        "positions": jnp.broadcast_to(jnp.arange(SEQ, dtype=jnp.int32), (BATCH, SEQ)),
        "w_in": n(ks[1], (DEPTH, D_MODEL, N_IN), f32) * D_MODEL ** -0.5,
        "w_gate_up": n(ks[2], (DEPTH, GLA_GATE_RANK, GLA_HEADS * GLA_DK), f32) * GLA_GATE_RANK ** -0.5,
        "b_gate": n(ks[3], (DEPTH, GLA_HEADS * GLA_DK), f32) * 0.1,
        "gla_norm_g": 1.0 + 0.02 * n(ks[4], (DEPTH, GLA_DV), f32),
        "lambda_q1": n(ks[5], (DEPTH, DIFF_D), f32) * 0.1,
        "lambda_k1": n(ks[6], (DEPTH, DIFF_D), f32) * 0.1,
        "lambda_q2": n(ks[7], (DEPTH, DIFF_D), f32) * 0.1,
        "lambda_k2": n(ks[8], (DEPTH, DIFF_D), f32) * 0.1,
        "diff_norm_g": 1.0 + 0.02 * n(ks[9], (DEPTH, 2 * DIFF_D), f32),
        "w_out": n(ks[10], (DEPTH, MIX_WIDTH, D_MODEL), f32) * (MIX_WIDTH ** -0.5) * DEEPNORM_BETA,
        "ln1_g": 1.0 + 0.02 * n(ks[11], (DEPTH, D_MODEL), f32),
        "ln1_b": 0.02 * n(ks[12], (DEPTH, D_MODEL), f32),
        "ln2_g": 1.0 + 0.02 * n(ks[13], (DEPTH, D_MODEL), f32),
        "ln2_b": 0.02 * n(ks[14], (DEPTH, D_MODEL), f32),
        "ffn_w1": n(ks[15], (N_DENSE, D_MODEL, D_FF), f32) * D_MODEL ** -0.5,
        "ffn_w3": n(ks[16], (N_DENSE, D_MODEL, D_FF), f32) * D_MODEL ** -0.5,
        "ffn_w2": n(ks[17], (N_DENSE, D_FF, D_MODEL), f32) * (D_FF ** -0.5) * DEEPNORM_BETA,
        "router_w": n(ks[18], (N_MOE, D_MODEL, N_EXPERTS), f32) * D_MODEL ** -0.5,
        "router_b": n(ks[19], (N_MOE, N_EXPERTS), f32) * 0.01,
        "moe_w1": n(ks[20], (N_MOE, N_EXPERTS, D_MODEL, D_FF_EXPERT), f32) * D_MODEL ** -0.5,
        "moe_w3": n(ks[21], (N_MOE, N_EXPERTS, D_MODEL, D_FF_EXPERT), f32) * D_MODEL ** -0.5,
        "moe_w2": n(ks[22], (N_MOE, N_EXPERTS, D_FF_EXPERT, D_MODEL), f32) * (D_FF_EXPERT ** -0.5) * DEEPNORM_BETA,
    }


def reference(x, positions, w_in, w_gate_up, b_gate, gla_norm_g, lambda_q1, lambda_k1, lambda_q2,
              lambda_k2, diff_norm_g, w_out, ln1_g, ln1_b, ln2_g, ln2_b, ffn_w1, ffn_w3, ffn_w2,
              router_w, router_b, moe_w1, moe_w3, moe_w2):
    B, S, _ = x.shape
    for l in range(DEPTH):
        h = x @ w_in[l]
        (g_q, g_k, g_v, g_r, g_lr, d_q, d_k, d_v,
         s_q, s_k, s_v, i_q, i_k, i_w) = jnp.split(h, SPLIT_POINTS, axis=-1)

        log_a = jax.nn.log_sigmoid((g_lr @ w_gate_up[l] + b_gate[l]).astype(jnp.float32)) / GLA_TAU
        o_gla = gla_mixer(g_q.reshape(B, S, GLA_HEADS, GLA_DK), g_k.reshape(B, S, GLA_HEADS, GLA_DK),
                          g_v.reshape(B, S, GLA_HEADS, GLA_DV), log_a.reshape(B, S, GLA_HEADS, GLA_DK))
        o_gla = rms_norm(o_gla, gla_norm_g[l]).astype(x.dtype) * jax.nn.silu(g_r).reshape(B, S, GLA_HEADS, GLA_DV)

        lambda_init = 0.8 - 0.6 * math.exp(-0.3 * l)
        lam = (jnp.exp(jnp.sum((lambda_q1[l] * lambda_k1[l]).astype(jnp.float32)))
               - jnp.exp(jnp.sum((lambda_q2[l] * lambda_k2[l]).astype(jnp.float32))) + lambda_init)
        dq = partial_rope(d_q.reshape(B, S, DIFF_HEADS * 2, DIFF_D), positions).reshape(B, S, DIFF_HEADS, 2, DIFF_D)
        dk = partial_rope(d_k.reshape(B, S, DIFF_HEADS * 2, DIFF_D), positions).reshape(B, S, DIFF_HEADS, 2, DIFF_D)
        o_diff = diff_attention(dq, dk, d_v.reshape(B, S, DIFF_HEADS, 2 * DIFF_D), lam, lambda_init, diff_norm_g[l])

        sq = partial_rope(s_q.reshape(B, S, DSA_HEADS, DSA_HEAD_DIM), positions)
        sk = partial_rope(s_k.reshape(B, S, DSA_KV_HEADS, DSA_HEAD_DIM), positions)
        sv = s_v.reshape(B, S, DSA_KV_HEADS, DSA_HEAD_DIM)
        iq = partial_rope(i_q.reshape(B, S, IDX_HEADS, IDX_DIM), positions)
        ik = partial_rope(i_k.reshape(B, S, 1, IDX_DIM), positions)[:, :, 0]
        o_dsa = dsa_attention(sq, sk, sv, iq, ik, i_w)

        mix = jnp.concatenate([o_gla.reshape(B, S, MIX_GLA), o_diff.reshape(B, S, MIX_DIFF), o_dsa], axis=-1) @ w_out[l]
        x = layer_norm(DEEPNORM_ALPHA * x + mix, ln1_g[l], ln1_b[l])

        j = l // 2
        if l % 2 == 0:
            f = swiglu(x, ffn_w1[j], ffn_w3[j], ffn_w2[j])
        else:
            f = moe_swiglu(x, router_w[j], router_b[j], moe_w1[j], moe_w3[j], moe_w2[j])
        x = layer_norm(DEEPNORM_ALPHA * x + f, ln2_g[l], ln2_b[l])
    return x
```

```python
import functools
import math

import jax
import jax.numpy as jnp
from jax import lax
from jax.experimental import pallas as pl
from jax.experimental.pallas import tpu as pltpu

F32 = jnp.float32
BF16 = jnp.bfloat16
HIGHEST = lax.Precision.HIGHEST
INT_MIN = -(2 ** 31)

ROPE_THETA = 500000.0
GLA_HEADS, GLA_DK, GLA_DV, GLA_RANK, GLA_TAU, GLA_CHUNK = 4, 192, 384, 16, 16.0, 64
GLA_SUB = 16
DIFF_HEADS, DIFF_D = 10, 64
DSA_HEADS, DSA_KV, DSA_DH = 10, 2, 128
IDX_HEADS, IDX_DIM, IDX_TOPK = 32, 64, 256
MOE_E, MOE_TOPK = 8, 2
LN_EPS = 1e-5

W_GLA = 2 * GLA_HEADS * GLA_DK + 2 * GLA_HEADS * GLA_DV
OFF_GLR = W_GLA
OFF_DQ = OFF_GLR + GLA_RANK
W_DIFF = DIFF_HEADS * 2 * DIFF_D
OFF_DK, OFF_DV = OFF_DQ + W_DIFF, OFF_DQ + 2 * W_DIFF
OFF_SQ = OFF_DQ + 3 * W_DIFF
W_SQ, W_SKV = DSA_HEADS * DSA_DH, DSA_KV * DSA_DH
OFF_SK, OFF_SV = OFF_SQ + W_SQ, OFF_SQ + W_SQ + W_SKV
OFF_IQ = OFF_SV + W_SKV
W_IQ = IDX_HEADS * IDX_DIM
OFF_IK = OFF_IQ + W_IQ
OFF_IW = OFF_IK + IDX_DIM
N_IN = OFF_IW + IDX_HEADS

HB_IQ, HB_SK, HB_SV, HB_SQ = 0, W_IQ, W_IQ + W_SKV, W_IQ + 2 * W_SKV
HB_DQ = HB_SQ + W_SQ
HB_DK, HB_DV = HB_DQ + W_DIFF, HB_DQ + 2 * W_DIFF
W_HB = HB_DV + W_DIFF

MIX_DSA, MIX_GLA, MIX_DIFF = 0, 1536, 3072
W_MIX = MIX_DIFF + W_DIFF
MIX_BLK = 256

VMEM_LIMIT = 56 * 1024 * 1024


def _cparams(sem, vmem=VMEM_LIMIT):
    return pltpu.CompilerParams(dimension_semantics=sem, vmem_limit_bytes=vmem)


def _mm_kernel(a_ref, w_ref, o_ref, acc_ref):
    k = pl.program_id(2)

    @pl.when(k == 0)
    def _():
        acc_ref[...] = jnp.zeros_like(acc_ref)

    acc_ref[...] += jnp.dot(a_ref[...], w_ref[...].astype(BF16), preferred_element_type=F32)

    @pl.when(k == pl.num_programs(2) - 1)
    def _():
        o_ref[...] = acc_ref[...].astype(o_ref.dtype)


def _matmul(a, w, *, n_out, k_steps, tm, tn, tk, out_dtype, a_map, w_map, w_lead=False, name="mm"):
    m = a.shape[0]
    w_block = (None, tk, tn) if w_lead else (tk, tn)
    return pl.pallas_call(
        _mm_kernel,
        out_shape=jax.ShapeDtypeStruct((m, n_out), out_dtype),
        grid=(m // tm, pl.cdiv(n_out, tn), k_steps),
        in_specs=[pl.BlockSpec((tm, tk), a_map), pl.BlockSpec(w_block, w_map)],
        out_specs=pl.BlockSpec((tm, tn), lambda i, j, k: (i, j)),
        scratch_shapes=[pltpu.VMEM((tm, tn), F32)],
        compiler_params=_cparams(("parallel", "parallel", "arbitrary")),
        name=name,
    )(a, w)


def _swiglu_kernel(a_ref, w1_ref, w3_ref, o_ref, acc1_ref, acc3_ref):
    k = pl.program_id(2)

    @pl.when(k == 0)
    def _():
        acc1_ref[...] = jnp.zeros_like(acc1_ref)
        acc3_ref[...] = jnp.zeros_like(acc3_ref)

    a = a_ref[...]
    acc1_ref[...] += jnp.dot(a, w1_ref[...].astype(BF16), preferred_element_type=F32)
    acc3_ref[...] += jnp.dot(a, w3_ref[...].astype(BF16), preferred_element_type=F32)

    @pl.when(k == pl.num_programs(2) - 1)
    def _():
        g = acc1_ref[...]
        o_ref[...] = (g * jax.nn.sigmoid(g) * acc3_ref[...]).astype(o_ref.dtype)


def _swiglu_up(a, w1, w3, lead, *, tm, tn, tk):
    m, kdim = a.shape
    f = w1.shape[-1]
    wspec = pl.BlockSpec((None, tk, tn), lambda i, j, k: (lead, k, j))
    return pl.pallas_call(
        _swiglu_kernel,
        out_shape=jax.ShapeDtypeStruct((m, f), BF16),
        grid=(m // tm, pl.cdiv(f, tn), kdim // tk),
        in_specs=[pl.BlockSpec((tm, tk), lambda i, j, k: (i, k)), wspec, wspec],
        out_specs=pl.BlockSpec((tm, tn), lambda i, j, k: (i, j)),
        scratch_shapes=[pltpu.VMEM((tm, tn), F32), pltpu.VMEM((tm, tn), F32)],
        compiler_params=_cparams(("parallel", "parallel", "arbitrary")),
        name="swiglu_up",
    )(a, w1, w3)


def _add_ln_kernel(x_ref, y_ref, g_ref, b_ref, o_ref, ob_ref, *, alpha):
    z = alpha * x_ref[...] + y_ref[...].astype(F32)
    mu = jnp.mean(z, axis=-1, keepdims=True)
    zc = z - mu
    var = jnp.mean(zc * zc, axis=-1, keepdims=True)
    out = zc * lax.rsqrt(var + LN_EPS) * g_ref[...] + b_ref[...]
    o_ref[...] = out
    ob_ref[...] = out.astype(BF16)


def _add_ln(x, y, g, b, lead, *, alpha, tm=256):
    m, d = x.shape
    row = pl.BlockSpec((tm, d), lambda i: (i, 0))
    par = pl.BlockSpec((None, 1, d), lambda i: (lead, 0, 0))
    g, b = g.reshape(-1, 1, d), b.reshape(-1, 1, d)
    return pl.pallas_call(
        functools.partial(_add_ln_kernel, alpha=alpha),
        out_shape=(jax.ShapeDtypeStruct((m, d), F32), jax.ShapeDtypeStruct((m, d), BF16)),
        grid=(m // tm,),
        in_specs=[row, row, par, par],
        out_specs=(row, row),
        compiler_params=_cparams(("parallel",)),
        name="add_ln",
    )(x, y, g, b)


def _rope_tables(positions, width):
    rot = width // 4
    half = rot // 2
    inv_freq = 1.0 / (ROPE_THETA ** (jnp.arange(half, dtype=F32) * 2.0 / rot))
    ang = positions.astype(F32)[:, None] * inv_freq
    cos, sin = jnp.cos(ang), jnp.sin(ang)
    n = positions.shape[0]
    one = jnp.ones((n, width - rot), F32)
    zero_h = jnp.zeros((n, half), F32)
    zero_r = jnp.zeros((n, width - rot), F32)
    c = jnp.concatenate([cos, cos, one], axis=1)
    s_up = jnp.concatenate([-sin, zero_h, zero_r], axis=1)
    s_dn = jnp.concatenate([zero_h, sin, zero_r], axis=1)
    rep = 128 // width
    return tuple(jnp.tile(t, (1, rep)) for t in (c, s_up, s_dn)), half


def _rope_tile(x, c, s_up, s_dn, half):
    return x * c + pltpu.roll(x, 128 - half, 1) * s_up + pltpu.roll(x, half, 1) * s_dn


def _rope_kernel(iq_ref, sk_ref, sq_ref, dq_ref, dk_ref, hc_ref,
                 ca_ref, ua_ref, da_ref, cb_ref, ub_ref, db_ref,
                 iq_o, sk_o, sq_o, dq_o, dk_o, ik_o, *, half_a, half_b):
    ta = (ca_ref[...], ua_ref[...], da_ref[...])
    tb = (cb_ref[...], ub_ref[...], db_ref[...])
    for src, dst, tab, half in ((iq_ref, iq_o, ta, half_a), (sk_ref, sk_o, tb, half_b),
                                (sq_ref, sq_o, tb, half_b), (dq_ref, dq_o, ta, half_a),
                                (dk_ref, dk_o, ta, half_a)):
        for t in range(src.shape[1] // 128):
            sl = slice(128 * t, 128 * (t + 1))
            dst[:, sl] = _rope_tile(src[:, sl].astype(F32), *tab, half).astype(BF16)
    ik = _rope_tile(hc_ref[...], *ta, half_a)
    lane = lax.broadcasted_iota(jnp.int32, ik.shape, 1)
    ik_o[...] = jnp.where(lane < IDX_DIM, ik, pltpu.roll(ik, IDX_DIM, 1)).astype(BF16)


def _rope_all(hb, hc, tab_a, half_a, tab_b, half_b, *, tm=256):
    n = hb.shape[0]

    def cols(width, off):
        return pl.BlockSpec((tm, width), lambda i: (i, off // width))

    tile = pl.BlockSpec((tm, 128), lambda i: (i, 0))

    def out(width):
        return pl.BlockSpec((tm, width), lambda i: (i, 0))

    widths = (W_IQ, W_SKV, W_SQ, W_DIFF, W_DIFF, 128)
    return pl.pallas_call(
        functools.partial(_rope_kernel, half_a=half_a, half_b=half_b),
        out_shape=tuple(jax.ShapeDtypeStruct((n, w), BF16) for w in widths),
        grid=(n // tm,),
        in_specs=[cols(W_IQ, HB_IQ), cols(W_SKV, HB_SK), cols(W_SQ, HB_SQ), cols(W_DIFF, HB_DQ),
                  cols(W_DIFF, HB_DK), tile] + [tile] * 6,
        out_specs=tuple(out(w) for w in widths),
        compiler_params=_cparams(("parallel",)),
        name="rope",
    )(hb, hb, hb, hb, hb, hc, *tab_a, *tab_b)


def _gla_kernel(q_ref, k_ref, v_ref, r_ref, glr_ref, wg_ref, bg_ref, ng_ref, mix_ref, o_ref, state_ref):
    del mix_ref
    C, SUB, DK, DV = GLA_CHUNK, GLA_SUB, GLA_DK, GLA_DV
    NT = (((1,), (1,)), ((), ()))
    TN = (((0,), (0,)), ((), ()))

    @pl.when(pl.program_id(2) == 0)
    def _():
        state_ref[...] = jnp.zeros_like(state_ref)

    row = lax.broadcasted_iota(jnp.int32, (C, C), 0)
    col = lax.broadcasted_iota(jnp.int32, (C, C), 1)
    tril = (col <= row).astype(F32)
    gi = lax.broadcasted_iota(jnp.int32, (SUB, SUB * SUB), 0)
    gr = lax.broadcasted_iota(jnp.int32, (SUB, SUB * SUB), 1)
    gsum = (gr // SUB == gi).astype(F32)
    pr = lax.broadcasted_iota(jnp.int32, (SUB * SUB, C), 0)
    pc = lax.broadcasted_iota(jnp.int32, (SUB * SUB, C), 1)
    srow = lax.broadcasted_iota(jnp.int32, (SUB, C), 0)
    scol = lax.broadcasted_iota(jnp.int32, (SUB, C), 1)
    glr = glr_ref[...]

    for hh in range(2):
        ks = slice(hh * DK, (hh + 1) * DK)
        vs = slice(hh * DV, (hh + 1) * DV)
        q = q_ref[:, ks].astype(F32) * (DK ** -0.5)
        k = k_ref[:, ks].astype(F32)
        v = v_ref[:, vs]
        z = jnp.dot(glr, wg_ref[:, ks], precision=HIGHEST, preferred_element_type=F32) + bg_ref[:, ks]
        log_a = -(jnp.maximum(-z, 0.0) + jnp.log1p(jnp.exp(-jnp.abs(z)))) * (1.0 / GLA_TAU)
        b = jnp.dot(tril, log_a, precision=HIGHEST, preferred_element_type=F32)

        st = state_ref[hh]
        qe = (q * jnp.exp(b)).astype(BF16)
        o = lax.dot_general(qe, st.astype(BF16), NT, preferred_element_type=F32)

        s_rows = []
        for blk in range(C // SUB):
            r0 = blk * SUB
            b_i, q_i, k_i = b[r0:r0 + SUB], q[r0:r0 + SUB], k[r0:r0 + SUB]
            pair = (q_i[:, None, :] * k_i[None, :, :]
                    * jnp.exp(jnp.minimum(b_i[:, None, :] - b_i[None, :, :], 0.0)))
            pair_sum = jnp.sum(pair.reshape(SUB * SUB, DK), axis=-1, keepdims=True)
            place = (pc == r0 + pr % SUB).astype(F32)
            s_blk = jnp.dot(gsum, pair_sum * place, precision=HIGHEST, preferred_element_type=F32)
            s_blk = jnp.where(scol <= srow + r0, s_blk, 0.0)
            if blk > 0:
                beta = b[r0:r0 + 1]
                q_t = (q_i * jnp.exp(b_i - beta)).astype(BF16)
                k_t = (k * jnp.exp(jnp.minimum(beta - b, 0.0))).astype(BF16)
                off = lax.dot_general(q_t, k_t, NT, preferred_element_type=F32)
                s_blk = jnp.where(scol < r0, off, s_blk)
            s_rows.append(s_blk)
        s = jnp.concatenate(s_rows, axis=0)
        o = o + jnp.dot(s.astype(BF16), v, preferred_element_type=F32)

        b_last = b[C - 1:C]
        k_d = (k * jnp.exp(b_last - b)).astype(BF16)
        state_ref[hh] = st * jnp.exp(b_last) + lax.dot_general(v, k_d, TN, preferred_element_type=F32)

        ms = jnp.mean(o * o, axis=-1, keepdims=True)
        o_n = o * lax.rsqrt(ms + LN_EPS) * ng_ref[...]
        r = r_ref[:, vs].astype(F32)
        o_ref[:, vs] = (o_n * (r * jax.nn.sigmoid(r))).astype(BF16)


def _gla(ha, glr, w_gate_up, b_gate, gla_norm_g, mix, lead, *, batch, seq):
    C = GLA_CHUNK
    nc = seq // C
    hp = GLA_HEADS // 2
    wq, wv = 2 * GLA_DK, 2 * GLA_DV

    def rows(width, first):
        return pl.BlockSpec((C, width), lambda b, p, c: (b * nc + c, first + p))

    return pl.pallas_call(
        _gla_kernel,
        out_shape=jax.ShapeDtypeStruct(mix.shape, mix.dtype),
        grid=(batch, hp, nc),
        in_specs=[rows(wq, 0), rows(wq, hp), rows(wv, hp), rows(wv, 2 * hp),
                  pl.BlockSpec((C, GLA_RANK), lambda b, p, c: (b * nc + c, 0)),
                  pl.BlockSpec((None, GLA_RANK, wq), lambda b, p, c: (lead, 0, p)),
                  pl.BlockSpec((None, 1, wq), lambda b, p, c: (lead, 0, p)),
                  pl.BlockSpec((None, 1, GLA_DV), lambda b, p, c: (lead, 0, 0)),
                  pl.BlockSpec(memory_space=pl.ANY)],
        out_specs=rows(wv, MIX_GLA // wv),
        scratch_shapes=[pltpu.VMEM((2, GLA_DV, GLA_DK), F32)],
        input_output_aliases={8: 0},
        compiler_params=_cparams(("parallel", "parallel", "arbitrary")),
        name="gla",
    )(ha, ha, ha, ha, glr, w_gate_up, b_gate.reshape(-1, 1, GLA_HEADS * GLA_DK),
      gla_norm_g.reshape(-1, 1, GLA_DV), mix)


def _diff_kernel(lq1_ref, lk1_ref, lq2_ref, lk2_ref, q_ref, k_ref, v_ref, g_ref, mix_ref, o_ref,
                 qs_ref, m_ref, l_ref, acc_ref, *, tq, tk, lambda_init):
    del mix_ref
    qi, ki = pl.program_id(2), pl.program_id(3)
    NT = (((1,), (1,)), ((), ()))

    @pl.when(ki == 0)
    def _():
        q = q_ref[...].astype(F32) * (DIFF_D ** -0.5)
        lane = lax.broadcasted_iota(jnp.int32, q.shape, 1)
        qs_ref[:tq] = jnp.where(lane < DIFF_D, q, 0.0).astype(BF16)
        qs_ref[tq:] = jnp.where(lane >= DIFF_D, q, 0.0).astype(BF16)
        m_ref[...] = jnp.full_like(m_ref, -jnp.inf)
        l_ref[...] = jnp.zeros_like(l_ref)
        acc_ref[...] = jnp.zeros_like(acc_ref)

    @pl.when(ki * tk <= qi * tq + tq - 1)
    def _():
        s = lax.dot_general(qs_ref[...], k_ref[...], NT, preferred_element_type=F32)
        r = lax.broadcasted_iota(jnp.int32, s.shape, 0)
        qpos = qi * tq + jnp.where(r >= tq, r - tq, r)
        kpos = ki * tk + lax.broadcasted_iota(jnp.int32, s.shape, 1)
        s = jnp.where(kpos <= qpos, s, -jnp.inf)
        m_prev = m_ref[...]
        m_new = jnp.maximum(m_prev, jnp.max(s, axis=-1, keepdims=True))
        alpha = jnp.exp(m_prev - m_new)
        p = jnp.exp(s - m_new)
        l_ref[...] = alpha * l_ref[...] + jnp.sum(p, axis=-1, keepdims=True)
        acc_ref[...] = alpha * acc_ref[...] + jnp.dot(p.astype(BF16), v_ref[...], preferred_element_type=F32)
        m_ref[...] = m_new

    @pl.when(ki == pl.num_programs(3) - 1)
    def _():
        lam = (jnp.exp(jnp.sum(lq1_ref[...] * lk1_ref[...], axis=-1, keepdims=True))
               - jnp.exp(jnp.sum(lq2_ref[...] * lk2_ref[...], axis=-1, keepdims=True)) + lambda_init)
        o = acc_ref[...] / l_ref[...]
        o = o[:tq] - lam * o[tq:]
        ms = jnp.mean(o * o, axis=-1, keepdims=True)
        o_ref[...] = (o * lax.rsqrt(ms + LN_EPS) * g_ref[...] * (1.0 - lambda_init)).astype(BF16)


def _diff_attention(dq, dk, hb, lq1, lk1, lq2, lk2, norm_g, mix, lead, *, batch, seq, lambda_init, tq, tk):
    nq, nk = seq // tq, seq // tk
    hw = 2 * DIFF_D

    def kv_row(b, qi, ki):
        return b * nk + jnp.minimum(ki, (qi * tq + tq - 1) // tk)

    lam_spec = pl.BlockSpec((None, 1, DIFF_D), lambda b, h, qi, ki: (lead, 0, 0))
    lq1, lk1, lq2, lk2 = (t.reshape(-1, 1, DIFF_D) for t in (lq1, lk1, lq2, lk2))
    return pl.pallas_call(
        functools.partial(_diff_kernel, tq=tq, tk=tk, lambda_init=lambda_init),
        out_shape=jax.ShapeDtypeStruct(mix.shape, mix.dtype),
        grid=(batch, DIFF_HEADS, nq, nk),
        in_specs=[lam_spec] * 4 + [
            pl.BlockSpec((tq, hw), lambda b, h, qi, ki: (b * nq + qi, h)),
            pl.BlockSpec((tk, hw), lambda b, h, qi, ki: (kv_row(b, qi, ki), h)),
            pl.BlockSpec((tk, hw), lambda b, h, qi, ki: (kv_row(b, qi, ki), HB_DV // hw + h)),
            pl.BlockSpec((None, 1, hw), lambda b, h, qi, ki: (lead, 0, 0)),
            pl.BlockSpec(memory_space=pl.ANY)],
        out_specs=pl.BlockSpec((tq, hw), lambda b, h, qi, ki: (b * nq + qi, MIX_DIFF // hw + h)),
        scratch_shapes=[pltpu.VMEM((2 * tq, hw), BF16), pltpu.VMEM((2 * tq, 1), F32),
                        pltpu.VMEM((2 * tq, 1), F32), pltpu.VMEM((2 * tq, hw), F32)],
        input_output_aliases={8: 0},
        compiler_params=_cparams(("parallel", "parallel", "parallel", "arbitrary")),
        name="diff_attn",
    )(lq1, lk1, lq2, lk2, dq, dk, hb, norm_g.reshape(-1, 1, hw), mix)


def _dsa_kernel(iq_ref, sq_ref, ikd_ref, wt_ref, sk_ref, svt_ref, mix_ref, o_ref,
                iqm_ref, keys_ref, bias_ref, *, tq, tk, seq, k_sel):
    del mix_ref
    qi = pl.program_id(1)
    NT = (((1,), (1,)), ((), ()))
    n_tiles = (qi * tq + tq + tk - 1) // tk
    qpos = qi * tq + lax.broadcasted_iota(jnp.int32, (1, tq), 1)
    krow = lax.broadcasted_iota(jnp.int32, (tk, tq), 0)
    idx_scale = (IDX_DIM ** -0.5) * (IDX_HEADS ** -0.5)
    neg_inf = F32(-jnp.inf)

    def tile_start(j):
        return pl.multiple_of(j * tk, tk)

    lane = lax.broadcasted_iota(jnp.int32, (tq, 128), 1)
    for t in range(IDX_HEADS // 2):
        a = iq_ref[:, 128 * t:128 * (t + 1)]
        zero = jnp.zeros_like(a)
        iqm_ref[2 * t] = jnp.where(lane < IDX_DIM, a, zero)
        iqm_ref[2 * t + 1] = jnp.where(lane >= IDX_DIM, a, zero)
    wt = wt_ref[...]

    def score_tile(j, carry):
        r0 = tile_start(j)
        kd = ikd_ref[pl.ds(r0, tk), :]
        acc = jnp.zeros((tk, tq), F32)
        for h in range(IDX_HEADS):
            lg = lax.dot_general(kd, iqm_ref[h], NT, preferred_element_type=F32)
            acc = acc + jnp.maximum(lg, 0.0) * wt[h:h + 1, :]
        bits = pltpu.bitcast(acc * idx_scale, jnp.int32)
        key = jnp.where(bits < 0, (bits ^ 0x7FFFFFFF) + 1, bits)
        keys_ref[pl.ds(r0, tk), :] = jnp.where(r0 + krow <= qpos, key, INT_MIN)
        return carry

    lax.fori_loop(0, n_tiles, score_tile, 0)

    def count_ones(ones):
        def body(j, cnt):
            r0 = tile_start(j)
            return cnt + jnp.sum(ones(keys_ref[pl.ds(r0, tk), :], r0 + krow), axis=0, keepdims=True)
        return lax.fori_loop(0, n_tiles, body, jnp.zeros((1, tq), jnp.int32))

    def count(pred):
        return count_ones(lambda kk, kp: jnp.where(pred(kk, kp), 1, 0))

    def thr_bit(i, t_u):
        cand_u = t_u | jnp.left_shift(jnp.int32(1), 31 - i)
        cand = cand_u ^ INT_MIN
        return jnp.where(count(lambda kk, kp: kk >= cand) >= k_sel, cand_u, t_u)

    thr = lax.fori_loop(0, 32, thr_bit, jnp.zeros((1, tq), jnp.int32)) ^ INT_MIN
    need = k_sel - count(lambda kk, kp: kk > thr)
    n_eq = count(lambda kk, kp: kk == thr)
    excess = jnp.where(thr == INT_MIN, 0, jnp.where(n_eq > need, 1, 0))

    def tie_search():
        def idx_bit(i, j_lo):
            cand = j_lo | jnp.left_shift(jnp.int32(1), (seq - 1).bit_length() - 1 - i)
            c = count_ones(lambda kk, kp: jnp.where(kk == thr, jnp.where(kp < cand, 1, 0), 0))
            return jnp.where(c < need, cand, j_lo)
        return lax.fori_loop(0, (seq - 1).bit_length(), idx_bit, jnp.zeros((1, tq), jnp.int32))

    j_star = lax.cond(jnp.max(excess) > 0, tie_search, lambda: jnp.full((1, tq), seq, jnp.int32))
    j_star = jnp.where(thr == INT_MIN, -1, jnp.where(excess > 0, j_star, seq))

    def bias_tile(j, carry):
        r0 = tile_start(j)
        kk = keys_ref[pl.ds(r0, tk), :]
        tie = jnp.where(r0 + krow <= j_star, 0.0, neg_inf)
        bias_ref[pl.ds(r0, tk), :] = jnp.where(kk > thr, 0.0, jnp.where(kk == thr, tie, neg_inf))
        return carry

    lax.fori_loop(0, n_tiles, bias_tile, 0)

    scale = DSA_DH ** -0.5
    for h in range(DSA_HEADS):
        g = h // (DSA_HEADS // DSA_KV)
        gs = slice(DSA_DH * g, DSA_DH * (g + 1))
        qh = sq_ref[:, DSA_DH * h:DSA_DH * (h + 1)]

        def attn_tile(j, carry):
            m, l, acc = carry
            r0 = tile_start(j)
            s = lax.dot_general(sk_ref[pl.ds(r0, tk), gs], qh, NT, preferred_element_type=F32)
            s = s * scale + bias_ref[pl.ds(r0, tk), :]
            m_new = jnp.maximum(m, jnp.max(s, axis=0, keepdims=True))
            m_safe = jnp.where(m_new == neg_inf, 0.0, m_new)
            alpha = jnp.exp(m - m_safe)
            p = jnp.exp(s - m_safe)
            l = alpha * l + jnp.sum(p, axis=0, keepdims=True)
            acc = alpha * acc + jnp.dot(svt_ref[gs, pl.ds(r0, tk)], p.astype(BF16), preferred_element_type=F32)
            return m_new, l, acc

        init = (jnp.full((1, tq), neg_inf), jnp.zeros((1, tq), F32), jnp.zeros((DSA_DH, tq), F32))
        _, l, acc = lax.fori_loop(0, n_tiles, attn_tile, init)
        o_ref[:, DSA_DH * h:DSA_DH * (h + 1)] = (acc / l).T.astype(BF16)


def _dsa(iq, sq, ikd, wt, sk, svt, mix, *, batch, seq, tq, tk):
    nq = seq // tq
    k_sel = min(IDX_TOPK, seq // 4)
    return pl.pallas_call(
        functools.partial(_dsa_kernel, tq=tq, tk=tk, seq=seq, k_sel=k_sel),
        out_shape=jax.ShapeDtypeStruct(mix.shape, mix.dtype),
        grid=(batch, nq),
        in_specs=[pl.BlockSpec((tq, W_IQ), lambda b, qi: (b * nq + qi, 0)),
                  pl.BlockSpec((tq, W_SQ), lambda b, qi: (b * nq + qi, 0)),
                  pl.BlockSpec((seq, 128), lambda b, qi: (b, 0)),
                  pl.BlockSpec((None, IDX_HEADS, tq), lambda b, qi: (b, 0, qi)),
                  pl.BlockSpec((seq, W_SKV), lambda b, qi: (b, 0)),
                  pl.BlockSpec((None, W_SKV, seq), lambda b, qi: (b, 0, 0)),
                  pl.BlockSpec(memory_space=pl.ANY)],
        out_specs=pl.BlockSpec((tq, W_SQ), lambda b, qi: (b * nq + qi, MIX_DSA // W_SQ)),
        scratch_shapes=[pltpu.VMEM((IDX_HEADS, tq, 128), BF16), pltpu.VMEM((seq, tq), jnp.int32),
                        pltpu.VMEM((seq, tq), F32)],
        input_output_aliases={6: 0},
        compiler_params=_cparams(("parallel", "arbitrary")),
        name="dsa",
    )(iq, sq, ikd, wt, sk, svt, mix)


def _router_kernel(x_ref, w_ref, b_ref, e_ref, g_ref, xp_ref):
    x = x_ref[...]
    logits = jnp.dot(x, w_ref[...], precision=HIGHEST, preferred_element_type=F32) + b_ref[...]
    lane = lax.broadcasted_iota(jnp.int32, logits.shape, 1)
    neg_inf = F32(-jnp.inf)
    lg = jnp.where(lane < MOE_E, logits, neg_inf)
    m1 = jnp.max(lg, axis=-1, keepdims=True)
    i1 = jnp.min(jnp.where(lg == m1, lane, 128), axis=-1, keepdims=True)
    lg2 = jnp.where(lane == i1, neg_inf, lg)
    m2 = jnp.max(lg2, axis=-1, keepdims=True)
    i2 = jnp.min(jnp.where(lg2 == m2, lane, 128), axis=-1, keepdims=True)
    e21 = jnp.exp(m2 - m1)
    g1 = 1.0 / (1.0 + e21)
    e_ref[...] = jnp.where(lane == 0, i1, jnp.where(lane == 1, i2, 0))
    g_ref[...] = jnp.where(lane == 0, g1, jnp.where(lane == 1, e21 * g1, 0.0))
    half = x.shape[1] // 2
    lo = pltpu.bitcast(x[:, :half].astype(BF16).astype(F32), jnp.uint32)
    hi = pltpu.bitcast(x[:, half:].astype(BF16).astype(F32), jnp.uint32)
    xp_ref[...] = (hi & jnp.uint32(0xFFFF0000)) | (lo >> 16)


def _router(x, router_w, router_b, *, tm=256):
    n, d = x.shape
    row = lambda w: pl.BlockSpec((tm, w), lambda i: (i, 0))
    return pl.pallas_call(
        _router_kernel,
        out_shape=(jax.ShapeDtypeStruct((n, 128), jnp.int32), jax.ShapeDtypeStruct((n, 128), F32),
                   jax.ShapeDtypeStruct((n, d // 2), jnp.uint32)),
        grid=(n // tm,),
        in_specs=[row(d), pl.BlockSpec((d, 128), lambda i: (0, 0)), pl.BlockSpec((1, 128), lambda i: (0, 0))],
        out_specs=(row(128), row(128), row(d // 2)),
        compiler_params=_cparams(("parallel",)),
        name="router",
    )(x, router_w, router_b)


def _dispatch_kernel(pos_ref, xp_ref, buf_ref, o_ref, sem, *, tb):
    del buf_ref
    base = pl.program_id(0) * tb

    def row_copy(i, s):
        return pltpu.make_async_copy(xp_ref.at[pl.ds(i, 1)], o_ref.at[pl.ds(pos_ref[2 * (base + i) + s], 1)], sem)

    def start(i, c):
        row_copy(i, 0).start()
        row_copy(i, 1).start()
        return c

    def wait(i, c):
        row_copy(i, 0).wait()
        row_copy(i, 1).wait()
        return c

    lax.fori_loop(0, tb, start, 0)
    lax.fori_loop(0, tb, wait, 0)


def _dispatch(pos, xp, buf, *, tb=256):
    n, w = xp.shape
    return pl.pallas_call(
        functools.partial(_dispatch_kernel, tb=tb),
        out_shape=jax.ShapeDtypeStruct(buf.shape, buf.dtype),
        grid_spec=pltpu.PrefetchScalarGridSpec(
            num_scalar_prefetch=1,
            grid=(n // tb,),
            in_specs=[pl.BlockSpec((tb, w), lambda i, pos: (i, 0)), pl.BlockSpec(memory_space=pl.ANY)],
            out_specs=pl.BlockSpec(memory_space=pl.ANY),
            scratch_shapes=[pltpu.SemaphoreType.DMA],
        ),
        input_output_aliases={2: 0},
        compiler_params=_cparams(("arbitrary",)),
        name="moe_dispatch",
    )(pos, xp, buf)


def _unpack_rows(word):
    lo = pltpu.bitcast(word << 16, F32).astype(BF16)
    hi = pltpu.bitcast(word & jnp.uint32(0xFFFF0000), F32).astype(BF16)
    return lo, hi


def _moe_up_kernel(be_ref, bf_ref, na_ref, a_ref, w1_ref, w3_ref, o_ref, w1b_ref, w3b_ref):
    del be_ref
    i = pl.program_id(1)

    @pl.when(i < na_ref[0])
    def _():
        @pl.when(bf_ref[i] == 1)
        def _():
            w1b_ref[...] = w1_ref[...].astype(BF16)
            w3b_ref[...] = w3_ref[...].astype(BF16)

        lo, hi = _unpack_rows(a_ref[...])
        half = lo.shape[1]
        g = (jnp.dot(lo, w1b_ref[:half], preferred_element_type=F32)
             + jnp.dot(hi, w1b_ref[half:], preferred_element_type=F32))
        u = (jnp.dot(lo, w3b_ref[:half], preferred_element_type=F32)
             + jnp.dot(hi, w3b_ref[half:], preferred_element_type=F32))
        o_ref[...] = (g * jax.nn.sigmoid(g) * u).astype(BF16)

    @pl.when(i >= na_ref[0])
    def _():
        o_ref[...] = jnp.zeros_like(o_ref)


def _moe_down_kernel(be_ref, bf_ref, na_ref, h_ref, w2_ref, o_ref, w2b_ref):
    del be_ref
    i = pl.program_id(1)

    @pl.when(i < na_ref[0])
    def _():
        @pl.when(bf_ref[i] == 1)
        def _():
            w2b_ref[...] = w2_ref[...].astype(BF16)

        o_ref[...] = jnp.dot(h_ref[...], w2b_ref[...], preferred_element_type=F32)

    @pl.when(i >= na_ref[0])
    def _():
        o_ref[...] = jnp.zeros_like(o_ref)


def _moe_ffn(block_e, block_first, n_active, a_sorted, w1, w3, w2, lead, *, tm, tn):
    r, half = a_sorted.shape
    d = 2 * half
    f = w1.shape[-1]
    nb = r // tm

    def rows(i, na):
        return jnp.minimum(i, na[0] - 1)

    h = pl.pallas_call(
        _moe_up_kernel,
        out_shape=jax.ShapeDtypeStruct((r, f), BF16),
        grid_spec=pltpu.PrefetchScalarGridSpec(
            num_scalar_prefetch=3,
            grid=(f // tn, nb),
            in_specs=[pl.BlockSpec((tm, half), lambda n, i, be, bf, na: (rows(i, na), 0)),
                      pl.BlockSpec((None, None, d, tn), lambda n, i, be, bf, na: (lead, be[i], 0, n)),
                      pl.BlockSpec((None, None, d, tn), lambda n, i, be, bf, na: (lead, be[i], 0, n))],
            out_specs=pl.BlockSpec((tm, tn), lambda n, i, be, bf, na: (i, n)),
            scratch_shapes=[pltpu.VMEM((d, tn), BF16), pltpu.VMEM((d, tn), BF16)],
        ),
        compiler_params=_cparams(("arbitrary", "arbitrary")),
        name="moe_up",
    )(block_e, block_first, n_active, a_sorted, w1, w3)
    return pl.pallas_call(
        _moe_down_kernel,
        out_shape=jax.ShapeDtypeStruct((r, d), F32),
        grid_spec=pltpu.PrefetchScalarGridSpec(
            num_scalar_prefetch=3,
            grid=(d // tn, nb),
            in_specs=[pl.BlockSpec((tm, f), lambda n, i, be, bf, na: (rows(i, na), 0)),
                      pl.BlockSpec((None, None, f, tn), lambda n, i, be, bf, na: (lead, be[i], 0, n))],
            out_specs=pl.BlockSpec((tm, tn), lambda n, i, be, bf, na: (i, n)),
            scratch_shapes=[pltpu.VMEM((f, tn), BF16)],
        ),
        compiler_params=_cparams(("arbitrary", "arbitrary")),
        name="moe_down",
    )(block_e, block_first, n_active, h, w2)


def _combine_ln_kernel(pos_ref, x_ref, gate_ref, g_ref, b_ref, y_ref, o_ref, ybuf_ref, sem, *, tb, alpha):
    base = pl.program_id(0) * tb

    def row_copy(i, s):
        return pltpu.make_async_copy(y_ref.at[pl.ds(pos_ref[2 * (base + i) + s], 1)],
                                     ybuf_ref.at[s, pl.ds(i, 1)], sem)

    def start(i, c):
        row_copy(i, 0).start()
        row_copy(i, 1).start()
        return c

    def wait(i, c):
        row_copy(i, 0).wait()
        row_copy(i, 1).wait()
        return c

    lax.fori_loop(0, tb, start, 0)
    lax.fori_loop(0, tb, wait, 0)
    gate = gate_ref[...]
    f = ybuf_ref[0] * gate[:, 0:1] + ybuf_ref[1] * gate[:, 1:2]
    z = alpha * x_ref[...] + f
    mu = jnp.mean(z, axis=-1, keepdims=True)
    zc = z - mu
    var = jnp.mean(zc * zc, axis=-1, keepdims=True)
    o_ref[...] = zc * lax.rsqrt(var + LN_EPS) * g_ref[...] + b_ref[...]


def _combine_ln(pos, x, gates, g, b, y, lead, *, alpha, tb=128):
    n, d = x.shape
    par = pl.BlockSpec((None, 1, d), lambda i, pos: (lead, 0, 0))
    return pl.pallas_call(
        functools.partial(_combine_ln_kernel, tb=tb, alpha=alpha),
        out_shape=jax.ShapeDtypeStruct((n, d), F32),
        grid_spec=pltpu.PrefetchScalarGridSpec(
            num_scalar_prefetch=1,
            grid=(n // tb,),
            in_specs=[pl.BlockSpec((tb, d), lambda i, pos: (i, 0)),
                      pl.BlockSpec((tb, 128), lambda i, pos: (i, 0)),
                      par, par, pl.BlockSpec(memory_space=pl.ANY)],
            out_specs=pl.BlockSpec((tb, d), lambda i, pos: (i, 0)),
            scratch_shapes=[pltpu.VMEM((2, tb, d), F32), pltpu.SemaphoreType.DMA],
        ),
        compiler_params=_cparams(("arbitrary",)),
        name="moe_combine_ln",
    )(pos, x, gates, g.reshape(-1, 1, d), b.reshape(-1, 1, d), y)


def _moe_layer(x, router_w, router_b, w1, w3, w2, ln_g, ln_b, lead_moe, lead_ln, *, alpha, tm=256, tn=512):
    n, d = x.shape
    rw = jnp.pad(router_w[lead_moe], ((0, 0), (0, 128 - MOE_E)))
    rb = jnp.pad(router_b[lead_moe], (0, 128 - MOE_E)).reshape(1, 128)
    top_e, gates, xp = _router(x, rw, rb)
    flat_e = top_e[:, :MOE_TOPK].reshape(-1)
    onehot = (flat_e[:, None] == jnp.arange(MOE_E, dtype=jnp.int32)[None, :]).astype(jnp.int32)
    rank = jnp.sum((jnp.cumsum(onehot, axis=0) - onehot) * onehot, axis=1)
    counts = jnp.sum(onehot, axis=0)
    padded = (counts + tm - 1) // tm * tm
    end_padded = jnp.cumsum(padded)
    start_padded = end_padded - padded
    pos = (start_padded[flat_e] + rank).astype(jnp.int32)
    nb = -(-(n * MOE_TOPK + MOE_E * (tm - 1)) // tm)
    block_start = jnp.arange(nb, dtype=jnp.int32) * tm
    block_e = jnp.minimum(jnp.searchsorted(end_padded, block_start, side="right"), MOE_E - 1).astype(jnp.int32)
    n_active = (end_padded[-1] // tm).astype(jnp.int32).reshape(1)
    block_first = jnp.concatenate([jnp.ones((1,), jnp.int32), (block_e[1:] != block_e[:-1]).astype(jnp.int32)])
    a_sorted = _dispatch(pos, xp, jnp.zeros((nb * tm, d // 2), jnp.uint32))
    y = _moe_ffn(block_e, block_first, n_active, a_sorted, w1, w3, w2, lead_moe, tm=tm, tn=tn)
    return _combine_ln(pos, x, gates, ln_g, ln_b, y, lead_ln, alpha=alpha)


def kernel(x, positions, w_in, w_gate_up, b_gate, gla_norm_g, lambda_q1, lambda_k1, lambda_q2, lambda_k2,
           diff_norm_g, w_out, ln1_g, ln1_b, ln2_g, ln2_b, ffn_w1, ffn_w3, ffn_w2, router_w, router_b,
           moe_w1, moe_w3, moe_w2):
    batch, seq, d = x.shape
    n = batch * seq
    depth = w_in.shape[0]
    alpha = (2 * depth) ** 0.25
    xf = x.reshape(n, d)
    xb = xf.astype(BF16)
    pos = positions.reshape(n)
    tab_a, half_a = _rope_tables(pos, DIFF_D)
    tab_b, half_b = _rope_tables(pos, DSA_DH)
    plain = lambda i, j, k: (i, k)
    for l in range(depth):
        ha = _matmul(xb, w_in, n_out=W_GLA, k_steps=d // 512, tm=1024, tn=1536, tk=512, out_dtype=BF16,
                     a_map=plain, w_map=lambda i, j, k, l=l: (l, k, j), w_lead=True, name="in_proj_gla")
        wl = w_in[l]
        w_b = jnp.concatenate([wl[:, OFF_IQ:OFF_IQ + W_IQ], wl[:, OFF_SK:OFF_SK + W_SKV],
                               wl[:, OFF_SV:OFF_SV + W_SKV], wl[:, OFF_SQ:OFF_SQ + W_SQ],
                               wl[:, OFF_DQ:OFF_DQ + 3 * W_DIFF]], axis=1).astype(BF16)
        hb = _matmul(xb, w_b, n_out=W_HB, k_steps=d // 512, tm=1024, tn=1536, tk=512, out_dtype=BF16,
                     a_map=plain, w_map=lambda i, j, k: (k, j), name="in_proj_attn")
        w_c = jnp.concatenate([wl[:, OFF_IK:OFF_IK + IDX_DIM + IDX_HEADS], wl[:, OFF_GLR:OFF_GLR + GLA_RANK],
                               jnp.zeros((d, 128 - IDX_DIM - IDX_HEADS - GLA_RANK), F32)], axis=1)
        hc = _matmul(xb, w_c, n_out=128, k_steps=d // 512, tm=1024, tn=128, tk=512, out_dtype=F32,
                     a_map=plain, w_map=lambda i, j, k: (k, j), name="in_proj_small")
        glr = hc[:, IDX_DIM + IDX_HEADS:IDX_DIM + IDX_HEADS + GLA_RANK]
        wt = hc[:, IDX_DIM:IDX_DIM + IDX_HEADS].reshape(batch, seq, IDX_HEADS).transpose(0, 2, 1)
        iq, sk, sq, dq, dk, ikd = _rope_all(hb, hc, tab_a, half_a, tab_b, half_b)
        svt = hb[:, HB_SV:HB_SV + W_SKV].reshape(batch, seq, W_SKV).transpose(0, 2, 1)

        mix = jnp.zeros((n, W_MIX), BF16)
        mix = _gla(ha, glr, w_gate_up, b_gate, gla_norm_g, mix, l, batch=batch, seq=seq)
        lambda_init = 0.8 - 0.6 * math.exp(-0.3 * l)
        mix = _diff_attention(dq, dk, hb, lambda_q1, lambda_k1, lambda_q2, lambda_k2, diff_norm_g, mix, l,
                              batch=batch, seq=seq, lambda_init=lambda_init, tq=256, tk=512)
        mix = _dsa(iq, sq, ikd, wt, sk, svt, mix, batch=batch, seq=seq, tq=128, tk=512)

        n_dsa, n_skip = W_SQ // MIX_BLK, (MIX_GLA - W_SQ) // MIX_BLK
        first_dsa_row = (GLA_HEADS * GLA_DV + W_DIFF) // MIX_BLK
        proj = _matmul(
            mix, w_out, n_out=d, k_steps=d // MIX_BLK, tm=1024, tn=1024, tk=MIX_BLK, out_dtype=F32,
            a_map=lambda i, j, k: (i, jnp.where(k < n_dsa, k, k + n_skip)),
            w_map=lambda i, j, k, l=l: (l, jnp.where(k < n_dsa, k + first_dsa_row, k - n_dsa), j),
            w_lead=True, name="out_proj")
        xf, xb = _add_ln(xf, proj, ln1_g, ln1_b, l, alpha=alpha)

        j = l // 2
        if l % 2 == 0:
            hid = _swiglu_up(xb, ffn_w1, ffn_w3, j, tm=2048, tn=1024, tk=512)
            f_dim = ffn_w2.shape[1]
            down = _matmul(hid, ffn_w2, n_out=d, k_steps=f_dim // 256, tm=2048, tn=1024, tk=256, out_dtype=F32,
                           a_map=plain, w_map=lambda i, j_, k, j=j: (j, k, j_), w_lead=True, name="ffn_down")
            xf, xb = _add_ln(xf, down, ln2_g, ln2_b, l, alpha=alpha)
        else:
            xf = _moe_layer(xf, router_w, router_b, moe_w1, moe_w3, moe_w2, ln2_g, ln2_b, j, l, alpha=alpha)
            xb = xf.astype(BF16)
    return xf.reshape(batch, seq, d)
```

```python
import functools
import math

import jax
import jax.numpy as jnp
from jax import lax
from jax.experimental import pallas as pl
from jax.experimental.pallas import tpu as pltpu

F32 = jnp.float32
BF16 = jnp.bfloat16
HIGHEST = lax.Precision.HIGHEST
INT_MIN = -(2 ** 31)

ROPE_THETA = 500000.0
GLA_HEADS, GLA_DK, GLA_DV, GLA_RANK, GLA_TAU, GLA_CHUNK = 4, 192, 384, 16, 16.0, 64
GLA_SUB = 16
DIFF_HEADS, DIFF_D = 10, 64
DSA_HEADS, DSA_KV, DSA_DH = 10, 2, 128
IDX_HEADS, IDX_DIM, IDX_TOPK = 32, 64, 256
MOE_E, MOE_TOPK = 8, 2
LN_EPS = 1e-5

W_GLA = 2 * GLA_HEADS * GLA_DK + 2 * GLA_HEADS * GLA_DV
OFF_GLR = W_GLA
OFF_DQ = OFF_GLR + GLA_RANK
W_DIFF = DIFF_HEADS * 2 * DIFF_D
OFF_DK, OFF_DV = OFF_DQ + W_DIFF, OFF_DQ + 2 * W_DIFF
OFF_SQ = OFF_DQ + 3 * W_DIFF
W_SQ, W_SKV = DSA_HEADS * DSA_DH, DSA_KV * DSA_DH
OFF_SK, OFF_SV = OFF_SQ + W_SQ, OFF_SQ + W_SQ + W_SKV
OFF_IQ = OFF_SV + W_SKV
W_IQ = IDX_HEADS * IDX_DIM
OFF_IK = OFF_IQ + W_IQ
OFF_IW = OFF_IK + IDX_DIM
N_IN = OFF_IW + IDX_HEADS

HB_IQ, HB_SK, HB_SV, HB_SQ = 0, W_IQ, W_IQ + W_SKV, W_IQ + 2 * W_SKV
HB_DQ = HB_SQ + W_SQ
HB_DK, HB_DV = HB_DQ + W_DIFF, HB_DQ + 2 * W_DIFF
W_HB = HB_DV + W_DIFF

MIX_GLA, MIX_DIFF, MIX_DSA = 0, 2560, 3840
W_MIX = MIX_DSA + DSA_HEADS * DSA_DH
MIX_BLK = 512

VMEM_LIMIT = 56 * 1024 * 1024


def _cparams(sem, vmem=VMEM_LIMIT):
    return pltpu.CompilerParams(dimension_semantics=sem, vmem_limit_bytes=vmem)


def _mm_kernel(a_ref, w_ref, o_ref, acc_ref, *, k_valid):
    k = pl.program_id(2)

    @pl.when(k == 0)
    def _():
        acc_ref[...] = jnp.zeros_like(acc_ref)

    w = w_ref[...]
    if k_valid is not None:
        row = k * w.shape[0] + lax.broadcasted_iota(jnp.int32, w.shape, 0)
        w = jnp.where(row < k_valid, w, 0.0)
    acc_ref[...] += jnp.dot(a_ref[...], w.astype(BF16), preferred_element_type=F32)

    @pl.when(k == pl.num_programs(2) - 1)
    def _():
        o_ref[...] = acc_ref[...].astype(o_ref.dtype)


def _matmul(a, w, *, n_out, k_steps, tm, tn, tk, out_dtype, a_map, w_map, w_lead=False, k_valid=None,
            name="mm"):
    m = a.shape[0]
    w_block = (None, tk, tn) if w_lead else (tk, tn)
    return pl.pallas_call(
        functools.partial(_mm_kernel, k_valid=k_valid),
        out_shape=jax.ShapeDtypeStruct((m, n_out), out_dtype),
        grid=(m // tm, pl.cdiv(n_out, tn), k_steps),
        in_specs=[pl.BlockSpec((tm, tk), a_map), pl.BlockSpec(w_block, w_map)],
        out_specs=pl.BlockSpec((tm, tn), lambda i, j, k: (i, j)),
        scratch_shapes=[pltpu.VMEM((tm, tn), F32)],
        compiler_params=_cparams(("parallel", "parallel", "arbitrary")),
        name=name,
    )(a, w)


def _swiglu_kernel(a_ref, w1_ref, w3_ref, o_ref, acc1_ref, acc3_ref, *, n_valid):
    k = pl.program_id(2)

    @pl.when(k == 0)
    def _():
        acc1_ref[...] = jnp.zeros_like(acc1_ref)
        acc3_ref[...] = jnp.zeros_like(acc3_ref)

    a = a_ref[...]
    acc1_ref[...] += jnp.dot(a, w1_ref[...].astype(BF16), preferred_element_type=F32)
    acc3_ref[...] += jnp.dot(a, w3_ref[...].astype(BF16), preferred_element_type=F32)

    @pl.when(k == pl.num_programs(2) - 1)
    def _():
        g = acc1_ref[...]
        h = g * jax.nn.sigmoid(g) * acc3_ref[...]
        col = pl.program_id(1) * h.shape[1] + lax.broadcasted_iota(jnp.int32, h.shape, 1)
        o_ref[...] = jnp.where(col < n_valid, h, 0.0).astype(o_ref.dtype)


def _swiglu_up(a, w1, w3, lead, *, tm, tn, tk, f_out):
    m, kdim = a.shape
    f = w1.shape[-1]
    wspec = pl.BlockSpec((None, tk, tn), lambda i, j, k: (lead, k, j))
    return pl.pallas_call(
        functools.partial(_swiglu_kernel, n_valid=f),
        out_shape=jax.ShapeDtypeStruct((m, f_out), BF16),
        grid=(m // tm, f_out // tn, kdim // tk),
        in_specs=[pl.BlockSpec((tm, tk), lambda i, j, k: (i, k)), wspec, wspec],
        out_specs=pl.BlockSpec((tm, tn), lambda i, j, k: (i, j)),
        scratch_shapes=[pltpu.VMEM((tm, tn), F32), pltpu.VMEM((tm, tn), F32)],
        compiler_params=_cparams(("parallel", "parallel", "arbitrary")),
        name="swiglu_up",
    )(a, w1, w3)


def _add_ln_kernel(x_ref, y_ref, g_ref, b_ref, o_ref, ob_ref, *, alpha):
    z = alpha * x_ref[...] + y_ref[...].astype(F32)
    mu = jnp.mean(z, axis=-1, keepdims=True)
    zc = z - mu
    var = jnp.mean(zc * zc, axis=-1, keepdims=True)
    out = zc * lax.rsqrt(var + LN_EPS) * g_ref[...] + b_ref[...]
    o_ref[...] = out
    ob_ref[...] = out.astype(BF16)


def _add_ln(x, y, g, b, lead, *, alpha, tm=256):
    m, d = x.shape
    row = pl.BlockSpec((tm, d), lambda i: (i, 0))
    par = pl.BlockSpec((None, 1, d), lambda i: (lead, 0, 0))
    g, b = g.reshape(-1, 1, d), b.reshape(-1, 1, d)
    return pl.pallas_call(
        functools.partial(_add_ln_kernel, alpha=alpha),
        out_shape=(jax.ShapeDtypeStruct((m, d), F32), jax.ShapeDtypeStruct((m, d), BF16)),
        grid=(m // tm,),
        in_specs=[row, row, par, par],
        out_specs=(row, row),
        compiler_params=_cparams(("parallel",)),
        name="add_ln",
    )(x, y, g, b)


def _rope_tables(positions, width):
    rot = width // 4
    half = rot // 2
    inv_freq = 1.0 / (ROPE_THETA ** (jnp.arange(half, dtype=F32) * 2.0 / rot))
    ang = positions.astype(F32)[:, None] * inv_freq
    cos, sin = jnp.cos(ang), jnp.sin(ang)
    n = positions.shape[0]
    one = jnp.ones((n, width - rot), F32)
    zero_h = jnp.zeros((n, half), F32)
    zero_r = jnp.zeros((n, width - rot), F32)
    c = jnp.concatenate([cos, cos, one], axis=1)
    s_up = jnp.concatenate([-sin, zero_h, zero_r], axis=1)
    s_dn = jnp.concatenate([zero_h, sin, zero_r], axis=1)
    rep = 128 // width
    return tuple(jnp.tile(t, (1, rep)) for t in (c, s_up, s_dn)), half


def _rope_tile(x, c, s_up, s_dn, half):
    return x * c + pltpu.roll(x, 128 - half, 1) * s_up + pltpu.roll(x, half, 1) * s_dn


def _rope_kernel(iq_ref, sk_ref, sq_ref, dq_ref, dk_ref, hc_ref,
                 ca_ref, ua_ref, da_ref, cb_ref, ub_ref, db_ref,
                 iq_o, sk_o, sq_o, dq_o, dk_o, ik_o, *, half_a, half_b):
    ta = (ca_ref[...], ua_ref[...], da_ref[...])
    tb = (cb_ref[...], ub_ref[...], db_ref[...])
    for src, dst, tab, half in ((iq_ref, iq_o, ta, half_a), (sk_ref, sk_o, tb, half_b),
                                (sq_ref, sq_o, tb, half_b), (dq_ref, dq_o, ta, half_a),
                                (dk_ref, dk_o, ta, half_a)):
        for t in range(src.shape[1] // 128):
            sl = slice(128 * t, 128 * (t + 1))
            dst[:, sl] = _rope_tile(src[:, sl].astype(F32), *tab, half).astype(BF16)
    ik = _rope_tile(hc_ref[...], *ta, half_a)
    lane = lax.broadcasted_iota(jnp.int32, ik.shape, 1)
    ik_o[...] = jnp.where(lane < IDX_DIM, ik, pltpu.roll(ik, IDX_DIM, 1)).astype(BF16)


def _rope_all(hb, hc, tab_a, half_a, tab_b, half_b, *, tm=256):
    n = hb.shape[0]

    def cols(width, off):
        return pl.BlockSpec((tm, width), lambda i: (i, off // width))

    tile = pl.BlockSpec((tm, 128), lambda i: (i, 0))

    def out(width):
        return pl.BlockSpec((tm, width), lambda i: (i, 0))

    widths = (W_IQ, W_SKV, W_SQ, W_DIFF, W_DIFF, 128)
    return pl.pallas_call(
        functools.partial(_rope_kernel, half_a=half_a, half_b=half_b),
        out_shape=tuple(jax.ShapeDtypeStruct((n, w), BF16) for w in widths),
        grid=(n // tm,),
        in_specs=[cols(W_IQ, HB_IQ), cols(W_SKV, HB_SK), cols(W_SQ, HB_SQ), cols(W_DIFF, HB_DQ),
                  cols(W_DIFF, HB_DK), tile] + [tile] * 6,
        out_specs=tuple(out(w) for w in widths),
        compiler_params=_cparams(("parallel",)),
        name="rope",
    )(hb, hb, hb, hb, hb, hc, *tab_a, *tab_b)


def _gla_kernel(q_ref, k_ref, v_ref, r_ref, glr_ref, wg_ref, bg_ref, ng_ref, mix_ref, o_ref, state_ref):
    del mix_ref
    C, SUB, DK, DV = GLA_CHUNK, GLA_SUB, GLA_DK, GLA_DV
    NT = (((1,), (1,)), ((), ()))
    TN = (((0,), (0,)), ((), ()))

    @pl.when(pl.program_id(2) == 0)
    def _():
        state_ref[...] = jnp.zeros_like(state_ref)

    row = lax.broadcasted_iota(jnp.int32, (C, C), 0)
    col = lax.broadcasted_iota(jnp.int32, (C, C), 1)
    tril = (col <= row).astype(F32)
    gi = lax.broadcasted_iota(jnp.int32, (SUB, SUB * SUB), 0)
    gr = lax.broadcasted_iota(jnp.int32, (SUB, SUB * SUB), 1)
    gsum = (gr // SUB == gi).astype(BF16)
    pr = lax.broadcasted_iota(jnp.int32, (SUB * SUB, C), 0)
    pc = lax.broadcasted_iota(jnp.int32, (SUB * SUB, C), 1)
    srow = lax.broadcasted_iota(jnp.int32, (SUB, C), 0)
    scol = lax.broadcasted_iota(jnp.int32, (SUB, C), 1)
    glr = glr_ref[...]

    for hh in range(2):
        ks = slice(hh * DK, (hh + 1) * DK)
        vs = slice(hh * DV, (hh + 1) * DV)
        q = q_ref[:, ks].astype(F32) * (DK ** -0.5)
        k = k_ref[:, ks].astype(F32)
        v = v_ref[:, vs]
        z = jnp.dot(glr, wg_ref[:, ks], precision=HIGHEST, preferred_element_type=F32) + bg_ref[:, ks]
        log_a = -(jnp.maximum(-z, 0.0) + jnp.log1p(jnp.exp(-jnp.abs(z)))) * (1.0 / GLA_TAU)
        b = jnp.dot(tril, log_a, precision=HIGHEST, preferred_element_type=F32)

        st = state_ref[hh]
        qe = (q * jnp.exp(b)).astype(BF16)
        o = lax.dot_general(qe, st.astype(BF16), NT, preferred_element_type=F32)

        s_rows = []
        for blk in range(C // SUB):
            r0 = blk * SUB
            b_i, q_i, k_i = b[r0:r0 + SUB], q[r0:r0 + SUB], k[r0:r0 + SUB]
            pair = (q_i[:, None, :] * k_i[None, :, :]
                    * jnp.exp(jnp.minimum(b_i[:, None, :] - b_i[None, :, :], 0.0)))
            pair_sum = jnp.sum(pair.reshape(SUB * SUB, DK), axis=-1, keepdims=True)
            placed = jnp.where(pc == r0 + pr % SUB, pair_sum, 0.0).astype(BF16)
            s_blk = jnp.dot(gsum, placed, preferred_element_type=F32)
            s_blk = jnp.where(scol <= srow + r0, s_blk, 0.0)
            if blk > 0:
                beta = b[r0:r0 + 1]
                q_t = (q_i * jnp.exp(b_i - beta)).astype(BF16)
                k_t = (k * jnp.exp(jnp.minimum(beta - b, 0.0))).astype(BF16)
                off = lax.dot_general(q_t, k_t, NT, preferred_element_type=F32)
                s_blk = jnp.where(scol < r0, off, s_blk)
            s_rows.append(s_blk)
        s = jnp.concatenate(s_rows, axis=0)
        o = o + jnp.dot(s.astype(BF16), v, preferred_element_type=F32)

        b_last = b[C - 1:C]
        k_d = (k * jnp.exp(b_last - b)).astype(BF16)
        state_ref[hh] = st * jnp.exp(b_last) + lax.dot_general(v, k_d, TN, preferred_element_type=F32)

        ms = jnp.mean(o * o, axis=-1, keepdims=True)
        o_n = o * lax.rsqrt(ms + LN_EPS) * ng_ref[...]
        r = r_ref[:, vs].astype(F32)
        o_ref[:, vs] = (o_n * (r * jax.nn.sigmoid(r))).astype(BF16)


def _gla(ha, glr, w_gate_up, b_gate, gla_norm_g, mix, lead, *, batch, seq):
    C = GLA_CHUNK
    nc = seq // C
    hp = GLA_HEADS // 2
    wq, wv = 2 * GLA_DK, 2 * GLA_DV

    def rows(width, first):
        return pl.BlockSpec((C, width), lambda b, p, c: (b * nc + c, first + p))

    return pl.pallas_call(
        _gla_kernel,
        out_shape=jax.ShapeDtypeStruct(mix.shape, mix.dtype),
        grid=(batch, hp, nc),
        in_specs=[rows(wq, 0), rows(wq, hp), rows(wv, hp), rows(wv, 2 * hp),
                  pl.BlockSpec((C, GLA_RANK), lambda b, p, c: (b * nc + c, 0)),
                  pl.BlockSpec((None, GLA_RANK, wq), lambda b, p, c: (lead, 0, p)),
                  pl.BlockSpec((None, 1, wq), lambda b, p, c: (lead, 0, p)),
                  pl.BlockSpec((None, 1, GLA_DV), lambda b, p, c: (lead, 0, 0)),
                  pl.BlockSpec(memory_space=pl.ANY)],
        out_specs=rows(wv, MIX_GLA // wv),
        scratch_shapes=[pltpu.VMEM((2, GLA_DV, GLA_DK), F32)],
        input_output_aliases={8: 0},
        compiler_params=_cparams(("parallel", "parallel", "arbitrary")),
        name="gla",
    )(ha, ha, ha, ha, glr, w_gate_up, b_gate.reshape(-1, 1, GLA_HEADS * GLA_DK),
      gla_norm_g.reshape(-1, 1, GLA_DV), mix)


def _diff_kernel(lq1_ref, lk1_ref, lq2_ref, lk2_ref, q_ref, k_ref, vt_ref, g_ref, mix_ref, o_ref,
                 qs_ref, m_ref, l_ref, acc_ref, *, tq, lambda_init):
    del mix_ref
    qi = pl.program_id(2)
    NT = (((1,), (1,)), ((), ()))
    hw = 2 * DIFF_D
    heads = q_ref.shape[1] // hw

    lane = lax.broadcasted_iota(jnp.int32, (tq, hw), 1)
    for hh in range(heads):
        q = q_ref[:, hw * hh:hw * (hh + 1)].astype(F32) * (DIFF_D ** -0.5)
        qs_ref[hh, :tq] = jnp.where(lane < DIFF_D, q, 0.0).astype(BF16)
        qs_ref[hh, tq:] = jnp.where(lane >= DIFF_D, q, 0.0).astype(BF16)
    m_ref[...] = jnp.full_like(m_ref, -jnp.inf)
    l_ref[...] = jnp.zeros_like(l_ref)
    acc_ref[...] = jnp.zeros_like(acc_ref)

    def tile(j, masked):
        r0 = pl.multiple_of(j * tq, tq)
        for hh in range(heads):
            hs = slice(hw * hh, hw * (hh + 1))
            s = lax.dot_general(k_ref[pl.ds(r0, tq), hs], qs_ref[hh], NT, preferred_element_type=F32)
            if masked:
                r = lax.broadcasted_iota(jnp.int32, s.shape, 0)
                c = lax.broadcasted_iota(jnp.int32, s.shape, 1)
                s = jnp.where(r <= jnp.where(c >= tq, c - tq, c), s, -jnp.inf)
            m_prev = m_ref[hh]
            m_new = jnp.maximum(m_prev, jnp.max(s, axis=0, keepdims=True))
            alpha = jnp.exp(m_prev - m_new)
            p = jnp.exp(s - m_new)
            l_ref[hh] = alpha * l_ref[hh] + jnp.sum(p, axis=0, keepdims=True)
            acc_ref[hh] = alpha * acc_ref[hh] + jnp.dot(vt_ref[hs, pl.ds(r0, tq)], p.astype(BF16),
                                                      preferred_element_type=F32)
            m_ref[hh] = m_new

    def full_tile(j, carry):
        tile(j, False)
        return carry

    lax.fori_loop(0, qi, full_tile, 0)
    tile(qi, True)

    lam = (jnp.exp(jnp.sum(lq1_ref[...] * lk1_ref[...], axis=-1, keepdims=True))
           - jnp.exp(jnp.sum(lq2_ref[...] * lk2_ref[...], axis=-1, keepdims=True)) + lambda_init)
    for hh in range(heads):
        o = acc_ref[hh] / l_ref[hh]
        o = o[:, :tq] - lam * o[:, tq:]
        ms = jnp.mean(o * o, axis=0, keepdims=True)
        o = o * lax.rsqrt(ms + LN_EPS) * g_ref[...] * (1.0 - lambda_init)
        o_ref[:, hw * hh:hw * (hh + 1)] = o.T.astype(BF16)


def _diff_attention(dq, dk, dvt, lq1, lk1, lq2, lk2, norm_g, mix, lead, *, batch, seq, lambda_init, tq, heads=2):
    nq = seq // tq
    hw = 2 * DIFF_D
    pw = heads * hw
    lam_spec = pl.BlockSpec((None, 1, DIFF_D), lambda b, p, qi: (lead, 0, 0))
    lq1, lk1, lq2, lk2 = (t.reshape(-1, 1, DIFF_D) for t in (lq1, lk1, lq2, lk2))
    return pl.pallas_call(
        functools.partial(_diff_kernel, tq=tq, lambda_init=lambda_init),
        out_shape=jax.ShapeDtypeStruct(mix.shape, mix.dtype),
        grid=(batch, DIFF_HEADS // heads, nq),
        in_specs=[lam_spec] * 4 + [
            pl.BlockSpec((tq, pw), lambda b, p, qi: (b * nq + qi, p)),
            pl.BlockSpec((seq, pw), lambda b, p, qi: (b, p)),
            pl.BlockSpec((None, pw, seq), lambda b, p, qi: (b, p, 0)),
            pl.BlockSpec((None, hw, 1), lambda b, p, qi: (lead, 0, 0)),
            pl.BlockSpec(memory_space=pl.ANY)],
        out_specs=pl.BlockSpec((tq, pw), lambda b, p, qi: (b * nq + qi, MIX_DIFF // pw + p)),
        scratch_shapes=[pltpu.VMEM((heads, 2 * tq, hw), BF16), pltpu.VMEM((heads, 1, 2 * tq), F32),
                        pltpu.VMEM((heads, 1, 2 * tq), F32), pltpu.VMEM((heads, hw, 2 * tq), F32)],
        input_output_aliases={8: 0},
        compiler_params=_cparams(("parallel", "parallel", "arbitrary")),
        name="diff_attn",
    )(lq1, lk1, lq2, lk2, dq, dk, dvt, norm_g.reshape(-1, hw, 1), mix)


def _dsa_kernel(iq_ref, sq_ref, ikd_ref, wt_ref, sk_ref, svt_ref, mix_ref, o_ref,
                iqm_ref, keys_ref, bias_ref, q5_ref, m_ref, l_ref, acc_ref, *, tq, tk, seq, k_sel):
    del mix_ref
    qi = pl.program_id(1)
    NT = (((1,), (1,)), ((), ()))
    n_tiles = (qi * tq + tq + tk - 1) // tk
    qpos = qi * tq + lax.broadcasted_iota(jnp.int32, (1, tq), 1)
    krow = lax.broadcasted_iota(jnp.int32, (tk, tq), 0)
    idx_scale = (IDX_DIM ** -0.5) * (IDX_HEADS ** -0.5)
    neg_inf = F32(-jnp.inf)

    def tile_start(j):
        return pl.multiple_of(j * tk, tk)

    lane = lax.broadcasted_iota(jnp.int32, (tq, 128), 1)
    for t in range(IDX_HEADS // 2):
        a = iq_ref[:, 128 * t:128 * (t + 1)]
        zero = jnp.zeros_like(a)
        iqm_ref[t, :tq] = jnp.where(lane < IDX_DIM, a, zero)
        iqm_ref[t, tq:] = jnp.where(lane >= IDX_DIM, a, zero)
    wt = wt_ref[...]

    def score_tile(j, carry):
        r0 = tile_start(j)
        kd = ikd_ref[pl.ds(r0, tk), :]
        acc = jnp.zeros((tk, 2 * tq), F32)
        for t in range(IDX_HEADS // 2):
            lg = lax.dot_general(kd, iqm_ref[t], NT, preferred_element_type=F32)
            w2 = jnp.concatenate([wt[2 * t:2 * t + 1], wt[2 * t + 1:2 * t + 2]], axis=1)
            acc = acc + jnp.maximum(lg, 0.0) * w2
        acc = acc[:, :tq] + acc[:, tq:]
        bits = pltpu.bitcast(acc * idx_scale, jnp.int32)
        key = jnp.where(bits < 0, (bits ^ 0x7FFFFFFF) + 1, bits)
        keys_ref[pl.ds(r0, tk), :] = jnp.where(r0 + krow <= qpos, key, INT_MIN)
        return carry

    lax.fori_loop(0, n_tiles, score_tile, 0)

    def count_ones(ones):
        def body(j, cnt):
            r0 = tile_start(j)
            return cnt + jnp.sum(ones(keys_ref[pl.ds(r0, tk), :], r0 + krow), axis=0, keepdims=True)
        return lax.fori_loop(0, n_tiles, body, jnp.zeros((1, tq), jnp.int32))

    def count(pred):
        return count_ones(lambda kk, kp: jnp.where(pred(kk, kp), 1, 0))

    def thr_bit(i, t_u):
        cand_u = t_u | jnp.left_shift(jnp.int32(1), 31 - i)
        cand = cand_u ^ INT_MIN
        return jnp.where(count(lambda kk, kp: kk >= cand) >= k_sel, cand_u, t_u)

    thr = lax.fori_loop(0, 32, thr_bit, jnp.zeros((1, tq), jnp.int32)) ^ INT_MIN
    need = k_sel - count(lambda kk, kp: kk > thr)
    n_eq = count(lambda kk, kp: kk == thr)
    excess = jnp.where(thr == INT_MIN, 0, jnp.where(n_eq > need, 1, 0))

    def tie_search():
        def idx_bit(i, j_lo):
            cand = j_lo | jnp.left_shift(jnp.int32(1), (seq - 1).bit_length() - 1 - i)
            c = count_ones(lambda kk, kp: jnp.where(kk == thr, jnp.where(kp < cand, 1, 0), 0))
            return jnp.where(c < need, cand, j_lo)
        return lax.fori_loop(0, (seq - 1).bit_length(), idx_bit, jnp.zeros((1, tq), jnp.int32))

    j_star = lax.cond(jnp.max(excess) > 0, tie_search, lambda: jnp.full((1, tq), seq, jnp.int32))
    j_star = jnp.where(thr == INT_MIN, -1, jnp.where(excess > 0, j_star, seq))

    def bias_tile(j, carry):
        r0 = tile_start(j)
        kk = keys_ref[pl.ds(r0, tk), :]
        tie = jnp.where(r0 + krow <= j_star, 0.0, neg_inf)
        bias_ref[pl.ds(r0, tk), :] = jnp.where(kk > thr, 0.0, jnp.where(kk == thr, tie, neg_inf))
        return carry

    lax.fori_loop(0, n_tiles, bias_tile, 0)

    scale = DSA_DH ** -0.5
    rep = DSA_HEADS // DSA_KV
    for g in range(DSA_KV):
        for i in range(rep):
            h = g * rep + i
            q5_ref[g, tq * i:tq * (i + 1)] = sq_ref[:, DSA_DH * h:DSA_DH * (h + 1)]
    m_ref[...] = jnp.full_like(m_ref, neg_inf)
    l_ref[...] = jnp.zeros_like(l_ref)
    acc_ref[...] = jnp.zeros_like(acc_ref)

    def attn_tile(j, carry):
        r0 = tile_start(j)
        bias = bias_ref[pl.ds(r0, tk), :]
        bias = jnp.concatenate([bias] * rep, axis=1)
        for g in range(DSA_KV):
            gs = slice(DSA_DH * g, DSA_DH * (g + 1))
            s = lax.dot_general(sk_ref[pl.ds(r0, tk), gs], q5_ref[g], NT, preferred_element_type=F32)
            s = s * scale + bias
            m_prev = m_ref[g]
            m_new = jnp.maximum(m_prev, jnp.max(s, axis=0, keepdims=True))
            m_safe = jnp.where(m_new == neg_inf, 0.0, m_new)
            alpha = jnp.exp(m_prev - m_safe)
            p = jnp.exp(s - m_safe)
            l_ref[g] = alpha * l_ref[g] + jnp.sum(p, axis=0, keepdims=True)
            acc_ref[g] = alpha * acc_ref[g] + jnp.dot(svt_ref[gs, pl.ds(r0, tk)], p.astype(BF16),
                                                    preferred_element_type=F32)
            m_ref[g] = m_new
        return carry

    lax.fori_loop(0, n_tiles, attn_tile, 0)
    for g in range(DSA_KV):
        out_t = acc_ref[g] / l_ref[g]
        for i in range(rep):
            h = g * rep + i
            o_ref[:, DSA_DH * h:DSA_DH * (h + 1)] = out_t[:, tq * i:tq * (i + 1)].T.astype(BF16)


def _dsa(iq, sq, ikd, wt, sk, svt, mix, *, batch, seq, tq, tk):
    nq = seq // tq
    k_sel = min(IDX_TOPK, seq // 4)
    rep = DSA_HEADS // DSA_KV
    return pl.pallas_call(
        functools.partial(_dsa_kernel, tq=tq, tk=tk, seq=seq, k_sel=k_sel),
        out_shape=jax.ShapeDtypeStruct(mix.shape, mix.dtype),
        grid=(batch, nq),
        in_specs=[pl.BlockSpec((tq, W_IQ), lambda b, qi: (b * nq + qi, 0)),
                  pl.BlockSpec((tq, W_SQ), lambda b, qi: (b * nq + qi, 0)),
                  pl.BlockSpec((seq, 128), lambda b, qi: (b, 0)),
                  pl.BlockSpec((None, IDX_HEADS, tq), lambda b, qi: (b, 0, qi)),
                  pl.BlockSpec((seq, W_SKV), lambda b, qi: (b, 0)),
                  pl.BlockSpec((None, W_SKV, seq), lambda b, qi: (b, 0, 0)),
                  pl.BlockSpec(memory_space=pl.ANY)],
        out_specs=pl.BlockSpec((tq, W_SQ), lambda b, qi: (b * nq + qi, MIX_DSA // W_SQ)),
        scratch_shapes=[pltpu.VMEM((IDX_HEADS // 2, 2 * tq, 128), BF16), pltpu.VMEM((seq, tq), jnp.int32),
                        pltpu.VMEM((seq, tq), F32), pltpu.VMEM((DSA_KV, rep * tq, DSA_DH), BF16),
                        pltpu.VMEM((DSA_KV, 1, rep * tq), F32), pltpu.VMEM((DSA_KV, 1, rep * tq), F32),
                        pltpu.VMEM((DSA_KV, DSA_DH, rep * tq), F32)],
        input_output_aliases={6: 0},
        compiler_params=_cparams(("parallel", "arbitrary")),
        name="dsa",
    )(iq, sq, ikd, wt, sk, svt, mix)


def _router_kernel(x_ref, w_ref, b_ref, e_ref, g_ref, xp_ref):
    x = x_ref[...]
    logits = jnp.dot(x, w_ref[...], precision=HIGHEST, preferred_element_type=F32) + b_ref[...]
    lane = lax.broadcasted_iota(jnp.int32, logits.shape, 1)
    neg_inf = F32(-jnp.inf)
    lg = jnp.where(lane < MOE_E, logits, neg_inf)
    m1 = jnp.max(lg, axis=-1, keepdims=True)
    i1 = jnp.min(jnp.where(lg == m1, lane, 128), axis=-1, keepdims=True)
    lg2 = jnp.where(lane == i1, neg_inf, lg)
    m2 = jnp.max(lg2, axis=-1, keepdims=True)
    i2 = jnp.min(jnp.where(lg2 == m2, lane, 128), axis=-1, keepdims=True)
    e21 = jnp.exp(m2 - m1)
    g1 = 1.0 / (1.0 + e21)
    e_ref[...] = jnp.where(lane == 0, i1, jnp.where(lane == 1, i2, 0))
    g_ref[...] = jnp.where(lane == 0, g1, jnp.where(lane == 1, e21 * g1, 0.0))
    half = x.shape[1] // 2
    lo = pltpu.bitcast(x[:, :half].astype(BF16).astype(F32), jnp.uint32)
    hi = pltpu.bitcast(x[:, half:].astype(BF16).astype(F32), jnp.uint32)
    xp_ref[...] = (hi & jnp.uint32(0xFFFF0000)) | (lo >> 16)


def _router(x, router_w, router_b, *, tm=256):
    n, d = x.shape
    row = lambda w: pl.BlockSpec((tm, w), lambda i: (i, 0))
    return pl.pallas_call(
        _router_kernel,
        out_shape=(jax.ShapeDtypeStruct((n, 128), jnp.int32), jax.ShapeDtypeStruct((n, 128), F32),
                   jax.ShapeDtypeStruct((n, d // 2), jnp.uint32)),
        grid=(n // tm,),
        in_specs=[row(d), pl.BlockSpec((d, 128), lambda i: (0, 0)), pl.BlockSpec((1, 128), lambda i: (0, 0))],
        out_specs=(row(128), row(128), row(d // 2)),
        compiler_params=_cparams(("parallel",)),
        name="router",
    )(x, router_w, router_b)


def _dispatch_kernel(pos_ref, xp_ref, buf_ref, o_ref, sem, *, tb):
    del buf_ref
    base = pl.program_id(0) * tb

    def row_copy(i, s):
        return pltpu.make_async_copy(xp_ref.at[pl.ds(i, 1)], o_ref.at[pl.ds(pos_ref[2 * (base + i) + s], 1)], sem)

    def start(i, c):
        row_copy(i, 0).start()
        row_copy(i, 1).start()
        return c

    def wait(i, c):
        row_copy(i, 0).wait()
        row_copy(i, 1).wait()
        return c

    lax.fori_loop(0, tb, start, 0)
    lax.fori_loop(0, tb, wait, 0)


def _dispatch(pos, xp, buf, *, tb=256):
    n, w = xp.shape
    return pl.pallas_call(
        functools.partial(_dispatch_kernel, tb=tb),
        out_shape=jax.ShapeDtypeStruct(buf.shape, buf.dtype),
        grid_spec=pltpu.PrefetchScalarGridSpec(
            num_scalar_prefetch=1,
            grid=(n // tb,),
            in_specs=[pl.BlockSpec((tb, w), lambda i, pos: (i, 0)), pl.BlockSpec(memory_space=pl.ANY)],
            out_specs=pl.BlockSpec(memory_space=pl.ANY),
            scratch_shapes=[pltpu.SemaphoreType.DMA],
        ),
        input_output_aliases={2: 0},
        compiler_params=_cparams(("arbitrary",)),
        name="moe_dispatch",
    )(pos, xp, buf)


def _unpack_rows(word):
    lo = pltpu.bitcast(word << 16, F32).astype(BF16)
    hi = pltpu.bitcast(word & jnp.uint32(0xFFFF0000), F32).astype(BF16)
    return lo, hi


def _moe_up_kernel(be_ref, bf_ref, na_ref, a_ref, w1_ref, w3_ref, o_ref, w1b_ref, w3b_ref):
    del be_ref
    i = pl.program_id(1)

    @pl.when(i < na_ref[0])
    def _():
        @pl.when(bf_ref[i] == 1)
        def _():
            w1b_ref[...] = w1_ref[...].astype(BF16)
            w3b_ref[...] = w3_ref[...].astype(BF16)

        lo, hi = _unpack_rows(a_ref[...])
        half = lo.shape[1]
        g = (jnp.dot(lo, w1b_ref[:half], preferred_element_type=F32)
             + jnp.dot(hi, w1b_ref[half:], preferred_element_type=F32))
        u = (jnp.dot(lo, w3b_ref[:half], preferred_element_type=F32)
             + jnp.dot(hi, w3b_ref[half:], preferred_element_type=F32))
        o_ref[...] = (g * jax.nn.sigmoid(g) * u).astype(BF16)

    @pl.when(i >= na_ref[0])
    def _():
        o_ref[...] = jnp.zeros_like(o_ref)


def _moe_down_kernel(be_ref, bf_ref, na_ref, h_ref, w2_ref, o_ref, w2b_ref):
    del be_ref
    i = pl.program_id(1)

    @pl.when(i < na_ref[0])
    def _():
        @pl.when(bf_ref[i] == 1)
        def _():
            w2b_ref[...] = w2_ref[...].astype(BF16)

        o_ref[...] = jnp.dot(h_ref[...], w2b_ref[...], preferred_element_type=F32)

    @pl.when(i >= na_ref[0])
    def _():
        o_ref[...] = jnp.zeros_like(o_ref)


def _moe_ffn(block_e, block_first, n_active, a_sorted, w1, w3, w2, lead, *, tm, tn):
    r, half = a_sorted.shape
    d = 2 * half
    f = w1.shape[-1]
    nb = r // tm

    def rows(i, na):
        return jnp.minimum(i, na[0] - 1)

    h = pl.pallas_call(
        _moe_up_kernel,
        out_shape=jax.ShapeDtypeStruct((r, f), BF16),
        grid_spec=pltpu.PrefetchScalarGridSpec(
            num_scalar_prefetch=3,
            grid=(f // tn, nb),
            in_specs=[pl.BlockSpec((tm, half), lambda n, i, be, bf, na: (rows(i, na), 0)),
                      pl.BlockSpec((None, None, d, tn), lambda n, i, be, bf, na: (lead, be[i], 0, n)),
                      pl.BlockSpec((None, None, d, tn), lambda n, i, be, bf, na: (lead, be[i], 0, n))],
            out_specs=pl.BlockSpec((tm, tn), lambda n, i, be, bf, na: (i, n)),
            scratch_shapes=[pltpu.VMEM((d, tn), BF16), pltpu.VMEM((d, tn), BF16)],
        ),
        compiler_params=_cparams(("arbitrary", "arbitrary")),
        name="moe_up",
    )(block_e, block_first, n_active, a_sorted, w1, w3)
    return pl.pallas_call(
        _moe_down_kernel,
        out_shape=jax.ShapeDtypeStruct((r, d), F32),
        grid_spec=pltpu.PrefetchScalarGridSpec(
            num_scalar_prefetch=3,
            grid=(d // tn, nb),
            in_specs=[pl.BlockSpec((tm, f), lambda n, i, be, bf, na: (rows(i, na), 0)),
                      pl.BlockSpec((None, None, f, tn), lambda n, i, be, bf, na: (lead, be[i], 0, n))],
            out_specs=pl.BlockSpec((tm, tn), lambda n, i, be, bf, na: (i, n)),
            scratch_shapes=[pltpu.VMEM((f, tn), BF16)],
        ),
        compiler_params=_cparams(("arbitrary", "arbitrary")),
        name="moe_down",
    )(block_e, block_first, n_active, h, w2)


def _combine_ln_kernel(pos_ref, x_ref, gate_ref, g_ref, b_ref, y_ref, o_ref, ybuf_ref, sem, *, tb, alpha):
    base = pl.program_id(0) * tb

    def row_copy(i, s):
        return pltpu.make_async_copy(y_ref.at[pl.ds(pos_ref[2 * (base + i) + s], 1)],
                                     ybuf_ref.at[s, pl.ds(i, 1)], sem)

    def start(i, c):
        row_copy(i, 0).start()
        row_copy(i, 1).start()
        return c

    def wait(i, c):
        row_copy(i, 0).wait()
        row_copy(i, 1).wait()
        return c

    lax.fori_loop(0, tb, start, 0)
    lax.fori_loop(0, tb, wait, 0)
    gate = gate_ref[...]
    f = ybuf_ref[0] * gate[:, 0:1] + ybuf_ref[1] * gate[:, 1:2]
    z = alpha * x_ref[...] + f
    mu = jnp.mean(z, axis=-1, keepdims=True)
    zc = z - mu
    var = jnp.mean(zc * zc, axis=-1, keepdims=True)
    o_ref[...] = zc * lax.rsqrt(var + LN_EPS) * g_ref[...] + b_ref[...]


def _combine_ln(pos, x, gates, g, b, y, lead, *, alpha, tb=128):
    n, d = x.shape
    par = pl.BlockSpec((None, 1, d), lambda i, pos: (lead, 0, 0))
    return pl.pallas_call(
        functools.partial(_combine_ln_kernel, tb=tb, alpha=alpha),
        out_shape=jax.ShapeDtypeStruct((n, d), F32),
        grid_spec=pltpu.PrefetchScalarGridSpec(
            num_scalar_prefetch=1,
            grid=(n // tb,),
            in_specs=[pl.BlockSpec((tb, d), lambda i, pos: (i, 0)),
                      pl.BlockSpec((tb, 128), lambda i, pos: (i, 0)),
                      par, par, pl.BlockSpec(memory_space=pl.ANY)],
            out_specs=pl.BlockSpec((tb, d), lambda i, pos: (i, 0)),
            scratch_shapes=[pltpu.VMEM((2, tb, d), F32), pltpu.SemaphoreType.DMA],
        ),
        compiler_params=_cparams(("arbitrary",)),
        name="moe_combine_ln",
    )(pos, x, gates, g.reshape(-1, 1, d), b.reshape(-1, 1, d), y)


def _moe_layer(x, router_w, router_b, w1, w3, w2, ln_g, ln_b, lead_moe, lead_ln, *, alpha, tm=256, tn=512):
    n, d = x.shape
    rw = jnp.pad(router_w[lead_moe], ((0, 0), (0, 128 - MOE_E)))
    rb = jnp.pad(router_b[lead_moe], (0, 128 - MOE_E)).reshape(1, 128)
    top_e, gates, xp = _router(x, rw, rb)
    flat_e = top_e[:, :MOE_TOPK].reshape(-1)
    onehot = (flat_e[:, None] == jnp.arange(MOE_E, dtype=jnp.int32)[None, :]).astype(jnp.int32)
    rank = jnp.sum((jnp.cumsum(onehot, axis=0) - onehot) * onehot, axis=1)
    counts = jnp.sum(onehot, axis=0)
    padded = (counts + tm - 1) // tm * tm
    end_padded = jnp.cumsum(padded)
    start_padded = end_padded - padded
    pos = (start_padded[flat_e] + rank).astype(jnp.int32)
    nb = -(-(n * MOE_TOPK + MOE_E * (tm - 1)) // tm)
    block_start = jnp.arange(nb, dtype=jnp.int32) * tm
    block_e = jnp.sum((block_start[:, None] >= end_padded[None, :]).astype(jnp.int32), axis=1)
    block_e = jnp.minimum(block_e, MOE_E - 1)
    n_active = (end_padded[-1] // tm).astype(jnp.int32).reshape(1)
    block_first = jnp.concatenate([jnp.ones((1,), jnp.int32), (block_e[1:] != block_e[:-1]).astype(jnp.int32)])
    a_sorted = _dispatch(pos, xp, jnp.zeros((nb * tm, d // 2), jnp.uint32))
    y = _moe_ffn(block_e, block_first, n_active, a_sorted, w1, w3, w2, lead_moe, tm=tm, tn=tn)
    return _combine_ln(pos, x, gates, ln_g, ln_b, y, lead_ln, alpha=alpha)


def kernel(x, positions, w_in, w_gate_up, b_gate, gla_norm_g, lambda_q1, lambda_k1, lambda_q2, lambda_k2,
           diff_norm_g, w_out, ln1_g, ln1_b, ln2_g, ln2_b, ffn_w1, ffn_w3, ffn_w2, router_w, router_b,
           moe_w1, moe_w3, moe_w2):
    batch, seq, d = x.shape
    n = batch * seq
    depth = w_in.shape[0]
    alpha = (2 * depth) ** 0.25
    xf = x.reshape(n, d)
    xb = xf.astype(BF16)
    pos = positions.reshape(n)
    tab_a, half_a = _rope_tables(pos, DIFF_D)
    tab_b, half_b = _rope_tables(pos, DSA_DH)
    plain = lambda i, j, k: (i, k)
    for l in range(depth):
        ha = _matmul(xb, w_in, n_out=W_GLA, k_steps=d // 512, tm=1024, tn=1536, tk=512, out_dtype=BF16,
                     a_map=plain, w_map=lambda i, j, k, l=l: (l, k, j), w_lead=True, name="in_proj_gla")
        wl = w_in[l]
        w_b = jnp.concatenate([wl[:, OFF_IQ:OFF_IQ + W_IQ], wl[:, OFF_SK:OFF_SK + W_SKV],
                               wl[:, OFF_SV:OFF_SV + W_SKV], wl[:, OFF_SQ:OFF_SQ + W_SQ],
                               wl[:, OFF_DQ:OFF_DQ + 3 * W_DIFF]], axis=1).astype(BF16)
        hb = _matmul(xb, w_b, n_out=W_HB, k_steps=d // 512, tm=1024, tn=1536, tk=512, out_dtype=BF16,
                     a_map=plain, w_map=lambda i, j, k: (k, j), name="in_proj_attn")
        w_c = jnp.concatenate([wl[:, OFF_IK:OFF_IK + IDX_DIM + IDX_HEADS], wl[:, OFF_GLR:OFF_GLR + GLA_RANK],
                               jnp.zeros((d, 128 - IDX_DIM - IDX_HEADS - GLA_RANK), F32)], axis=1)
        hc = _matmul(xb, w_c, n_out=128, k_steps=d // 512, tm=1024, tn=128, tk=512, out_dtype=F32,
                     a_map=plain, w_map=lambda i, j, k: (k, j), name="in_proj_small")
        glr = hc[:, IDX_DIM + IDX_HEADS:IDX_DIM + IDX_HEADS + GLA_RANK]
        wt = hc[:, IDX_DIM:IDX_DIM + IDX_HEADS].reshape(batch, seq, IDX_HEADS).transpose(0, 2, 1)
        iq, sk, sq, dq, dk, ikd = _rope_all(hb, hc, tab_a, half_a, tab_b, half_b)
        svt = hb[:, HB_SV:HB_SV + W_SKV].reshape(batch, seq, W_SKV).transpose(0, 2, 1)

        mix = jnp.zeros((n, W_MIX), BF16)
        mix = _gla(ha, glr, w_gate_up, b_gate, gla_norm_g, mix, l, batch=batch, seq=seq)
        lambda_init = 0.8 - 0.6 * math.exp(-0.3 * l)
        dvt = hb[:, HB_DV:HB_DV + W_DIFF].reshape(batch, seq, W_DIFF).transpose(0, 2, 1)
        mix = _diff_attention(dq, dk, dvt, lambda_q1, lambda_k1, lambda_q2, lambda_k2, diff_norm_g, mix, l,
                              batch=batch, seq=seq, lambda_init=lambda_init, tq=512)
        mix = _dsa(iq, sq, ikd, wt, sk, svt, mix, batch=batch, seq=seq, tq=128, tk=512)

        n_gla = GLA_HEADS * GLA_DV // MIX_BLK
        n_skip = MIX_DIFF // MIX_BLK - n_gla
        proj = _matmul(
            mix, w_out, n_out=d, k_steps=d // MIX_BLK, tm=1024, tn=1024, tk=MIX_BLK, out_dtype=F32,
            a_map=lambda i, j, k: (i, jnp.where(k < n_gla, k, k + n_skip)),
            w_map=lambda i, j, k, l=l: (l, k, j), w_lead=True, name="out_proj")
        xf, xb = _add_ln(xf, proj, ln1_g, ln1_b, l, alpha=alpha)

        j = l // 2
        if l % 2 == 0:
            f_dim = ffn_w2.shape[1]
            f_pad = -(-f_dim // 1024) * 1024
            hid = _swiglu_up(xb, ffn_w1, ffn_w3, j, tm=2048, tn=1024, tk=512, f_out=f_pad)
            down = _matmul(hid, ffn_w2, n_out=d, k_steps=f_pad // 512, tm=2048, tn=1024, tk=512, out_dtype=F32,
                           a_map=plain, w_map=lambda i, j_, k, j=j: (j, k, j_), w_lead=True, k_valid=f_dim,
                           name="ffn_down")
            xf, xb = _add_ln(xf, down, ln2_g, ln2_b, l, alpha=alpha)
        else:
            xf = _moe_layer(xf, router_w, router_b, moe_w1, moe_w3, moe_w2, ln2_g, ln2_b, j, l, alpha=alpha)
            xb = xf.astype(BF16)
    return xf.reshape(batch, seq, d)
```

```python
import functools
import math

import jax
import jax.numpy as jnp
from jax import lax
from jax.experimental import pallas as pl
from jax.experimental.pallas import tpu as pltpu

F32 = jnp.float32
BF16 = jnp.bfloat16
HIGHEST = lax.Precision.HIGHEST
INT_MIN = -(2 ** 31)

ROPE_THETA = 500000.0
GLA_HEADS, GLA_DK, GLA_DV, GLA_RANK, GLA_TAU, GLA_CHUNK = 4, 192, 384, 16, 16.0, 64
GLA_SUB = 16
DIFF_HEADS, DIFF_D = 10, 64
DSA_HEADS, DSA_KV, DSA_DH = 10, 2, 128
IDX_HEADS, IDX_DIM, IDX_TOPK = 32, 64, 256
MOE_E, MOE_TOPK = 8, 2
LN_EPS = 1e-5

W_GLA = 2 * GLA_HEADS * GLA_DK + 2 * GLA_HEADS * GLA_DV
OFF_GLR = W_GLA
OFF_DQ = OFF_GLR + GLA_RANK
W_DIFF = DIFF_HEADS * 2 * DIFF_D
OFF_DK, OFF_DV = OFF_DQ + W_DIFF, OFF_DQ + 2 * W_DIFF
OFF_SQ = OFF_DQ + 3 * W_DIFF
W_SQ, W_SKV = DSA_HEADS * DSA_DH, DSA_KV * DSA_DH
OFF_SK, OFF_SV = OFF_SQ + W_SQ, OFF_SQ + W_SQ + W_SKV
OFF_IQ = OFF_SV + W_SKV
W_IQ = IDX_HEADS * IDX_DIM
OFF_IK = OFF_IQ + W_IQ
OFF_IW = OFF_IK + IDX_DIM
N_IN = OFF_IW + IDX_HEADS

HM_BASE = W_GLA
HM_TN = 1152
HM_TILES = -(-(N_IN - HM_BASE) // HM_TN)

MIX_GLA, MIX_DIFF, MIX_DSA = 0, 2560, 3840
W_MIX = MIX_DSA + DSA_HEADS * DSA_DH
MIX_BLK = 512

VMEM_LIMIT = 56 * 1024 * 1024


def _cparams(sem, vmem=VMEM_LIMIT):
    return pltpu.CompilerParams(dimension_semantics=sem, vmem_limit_bytes=vmem)


def _mm_kernel(a_ref, w_ref, o_ref, acc_ref, *, k_valid, n_valid):
    k = pl.program_id(2)

    @pl.when(k == 0)
    def _():
        acc_ref[...] = jnp.zeros_like(acc_ref)

    w = w_ref[...]
    if k_valid is not None:
        row = k * w.shape[0] + lax.broadcasted_iota(jnp.int32, w.shape, 0)
        w = jnp.where(row < k_valid, w, 0.0)
    if n_valid is not None:
        col = pl.program_id(1) * w.shape[1] + lax.broadcasted_iota(jnp.int32, w.shape, 1)
        w = jnp.where(col < n_valid, w, 0.0)
    acc_ref[...] += jnp.dot(a_ref[...], w.astype(BF16), preferred_element_type=F32)

    @pl.when(k == pl.num_programs(2) - 1)
    def _():
        o_ref[...] = acc_ref[...].astype(o_ref.dtype)


def _matmul(a, w, *, n_out, k_steps, tm, tn, tk, out_dtype, a_map, w_map, w_lead=False, k_valid=None,
            n_valid=None, name="mm"):
    m = a.shape[0]
    w_block = (None, tk, tn) if w_lead else (tk, tn)
    return pl.pallas_call(
        functools.partial(_mm_kernel, k_valid=k_valid, n_valid=n_valid),
        out_shape=jax.ShapeDtypeStruct((m, n_out), out_dtype),
        grid=(m // tm, pl.cdiv(n_out, tn), k_steps),
        in_specs=[pl.BlockSpec((tm, tk), a_map), pl.BlockSpec(w_block, w_map)],
        out_specs=pl.BlockSpec((tm, tn), lambda i, j, k: (i, j)),
        scratch_shapes=[pltpu.VMEM((tm, tn), F32)],
        compiler_params=_cparams(("parallel", "parallel", "arbitrary")),
        name=name,
    )(a, w)


def _swiglu_kernel(a_ref, w1_ref, w3_ref, o_ref, acc1_ref, acc3_ref, *, n_valid):
    k = pl.program_id(2)

    @pl.when(k == 0)
    def _():
        acc1_ref[...] = jnp.zeros_like(acc1_ref)
        acc3_ref[...] = jnp.zeros_like(acc3_ref)

    a = a_ref[...]
    acc1_ref[...] += jnp.dot(a, w1_ref[...].astype(BF16), preferred_element_type=F32)
    acc3_ref[...] += jnp.dot(a, w3_ref[...].astype(BF16), preferred_element_type=F32)

    @pl.when(k == pl.num_programs(2) - 1)
    def _():
        g = acc1_ref[...]
        h = g * jax.nn.sigmoid(g) * acc3_ref[...]
        col = pl.program_id(1) * h.shape[1] + lax.broadcasted_iota(jnp.int32, h.shape, 1)
        o_ref[...] = jnp.where(col < n_valid, h, 0.0).astype(o_ref.dtype)


def _swiglu_up(a, w1, w3, lead, *, tm, tn, tk, f_out):
    m, kdim = a.shape
    f = w1.shape[-1]
    wspec = pl.BlockSpec((None, tk, tn), lambda i, j, k: (lead, k, j))
    return pl.pallas_call(
        functools.partial(_swiglu_kernel, n_valid=f),
        out_shape=jax.ShapeDtypeStruct((m, f_out), BF16),
        grid=(m // tm, f_out // tn, kdim // tk),
        in_specs=[pl.BlockSpec((tm, tk), lambda i, j, k: (i, k)), wspec, wspec],
        out_specs=pl.BlockSpec((tm, tn), lambda i, j, k: (i, j)),
        scratch_shapes=[pltpu.VMEM((tm, tn), F32), pltpu.VMEM((tm, tn), F32)],
        compiler_params=_cparams(("parallel", "parallel", "arbitrary")),
        name="swiglu_up",
    )(a, w1, w3)


def _add_ln_kernel(x_ref, y_ref, g_ref, b_ref, o_ref, ob_ref, *, alpha):
    z = alpha * x_ref[...] + y_ref[...].astype(F32)
    mu = jnp.mean(z, axis=-1, keepdims=True)
    zc = z - mu
    var = jnp.mean(zc * zc, axis=-1, keepdims=True)
    out = zc * lax.rsqrt(var + LN_EPS) * g_ref[...] + b_ref[...]
    o_ref[...] = out
    ob_ref[...] = out.astype(BF16)


def _add_ln(x, y, g, b, lead, *, alpha, tm=256):
    m, d = x.shape
    row = pl.BlockSpec((tm, d), lambda i: (i, 0))
    par = pl.BlockSpec((None, 1, d), lambda i: (lead, 0, 0))
    g, b = g.reshape(-1, 1, d), b.reshape(-1, 1, d)
    return pl.pallas_call(
        functools.partial(_add_ln_kernel, alpha=alpha),
        out_shape=(jax.ShapeDtypeStruct((m, d), F32), jax.ShapeDtypeStruct((m, d), BF16)),
        grid=(m // tm,),
        in_specs=[row, row, par, par],
        out_specs=(row, row),
        compiler_params=_cparams(("parallel",)),
        name="add_ln",
    )(x, y, g, b)


def _rope_tables(positions, width):
    rot = width // 4
    half = rot // 2
    inv_freq = 1.0 / (ROPE_THETA ** (jnp.arange(half, dtype=F32) * 2.0 / rot))
    ang = positions.astype(F32)[:, None] * inv_freq
    cos, sin = jnp.cos(ang), jnp.sin(ang)
    n = positions.shape[0]
    one = jnp.ones((n, width - rot), F32)
    zero_h = jnp.zeros((n, half), F32)
    zero_r = jnp.zeros((n, width - rot), F32)
    c = jnp.concatenate([cos, cos, one], axis=1)
    s_up = jnp.concatenate([-sin, zero_h, zero_r], axis=1)
    s_dn = jnp.concatenate([zero_h, sin, zero_r], axis=1)
    rep = 128 // width
    return tuple(jnp.tile(t, (1, rep)) for t in (c, s_up, s_dn)), half


def _rope_tile(x, c, s_up, s_dn, half):
    return x * c + pltpu.roll(x, 128 - half, 1) * s_up + pltpu.roll(x, half, 1) * s_dn


def _split_rope_kernel(hm_ref, hc_ref, ca_ref, ua_ref, da_ref, cb_ref, ub_ref, db_ref,
                       dq_o, dk_o, dv_o, sq_o, sk_o, sv_o, iq_o, ik_o, *, shift, half_a, half_b):
    ta = (ca_ref[...], ua_ref[...], da_ref[...])
    tb = (cb_ref[...], ub_ref[...], db_ref[...])
    lane = lax.broadcasted_iota(jnp.int32, ca_ref.shape, 1)

    def shifted(k):
        return pltpu.roll(hm_ref[:, 128 * k:128 * (k + 1)].astype(F32), 128 - shift, 1)

    m = 0
    nxt = shifted(0)
    for dst, tab, half in ((dq_o, ta, half_a), (dk_o, ta, half_a), (dv_o, None, 0), (sq_o, tb, half_b),
                           (sk_o, tb, half_b), (sv_o, None, 0), (iq_o, ta, half_a)):
        for t in range(dst.shape[1] // 128):
            cur, nxt = nxt, shifted(m + 1)
            y = jnp.where(lane < 128 - shift, cur, nxt)
            if tab is not None:
                y = _rope_tile(y, *tab, half)
            dst[:, 128 * t:128 * (t + 1)] = y.astype(BF16)
            m += 1
    ik = _rope_tile(hc_ref[...], *ta, half_a)
    ik_o[...] = jnp.where(lane < IDX_DIM, ik, pltpu.roll(ik, IDX_DIM, 1)).astype(BF16)


def _split_rope(hm, hc, tab_a, half_a, tab_b, half_b, *, tm=256):
    n, wm = hm.shape
    tile = pl.BlockSpec((tm, 128), lambda i: (i, 0))
    widths = (W_DIFF, W_DIFF, W_DIFF, W_SQ, W_SKV, W_SKV, W_IQ, 128)
    return pl.pallas_call(
        functools.partial(_split_rope_kernel, shift=OFF_DQ - HM_BASE, half_a=half_a, half_b=half_b),
        out_shape=tuple(jax.ShapeDtypeStruct((n, w), BF16) for w in widths),
        grid=(n // tm,),
        in_specs=[pl.BlockSpec((tm, wm), lambda i: (i, 0)), tile] + [tile] * 6,
        out_specs=tuple(pl.BlockSpec((tm, w), lambda i: (i, 0)) for w in widths),
        compiler_params=_cparams(("parallel",)),
        name="split_rope",
    )(hm, hc, *tab_a, *tab_b)


def _gla_prep_kernel(q_ref, k_ref, glr_ref, wg_ref, bg_ref, qe_o, kd_o, s_o, eb_o, *, chunks):
    C, SUB, DK = GLA_CHUNK, GLA_SUB, GLA_DK
    NT = (((1,), (1,)), ((), ()))

    row = lax.broadcasted_iota(jnp.int32, (C, C), 0)
    col = lax.broadcasted_iota(jnp.int32, (C, C), 1)
    tril = (col <= row).astype(F32)
    gi = lax.broadcasted_iota(jnp.int32, (SUB, SUB * SUB), 0)
    gr = lax.broadcasted_iota(jnp.int32, (SUB, SUB * SUB), 1)
    gsum = (gr // SUB == gi).astype(BF16)
    pr = lax.broadcasted_iota(jnp.int32, (SUB * SUB, C), 0)
    pc = lax.broadcasted_iota(jnp.int32, (SUB * SUB, C), 1)
    srow = lax.broadcasted_iota(jnp.int32, (SUB, C), 0)
    scol = lax.broadcasted_iota(jnp.int32, (SUB, C), 1)

    def chunk_head(ci, hh):
        r0 = pl.multiple_of(ci * C, C)
        rows = pl.ds(r0, C)
        ks = slice(hh * DK, (hh + 1) * DK)
        q = q_ref[rows, ks].astype(F32) * (DK ** -0.5)
        k = k_ref[rows, ks].astype(F32)
        z = jnp.dot(glr_ref[rows, :], wg_ref[:, ks], precision=HIGHEST, preferred_element_type=F32) + bg_ref[:, ks]
        log_a = -(jnp.maximum(-z, 0.0) + jnp.log1p(jnp.exp(-jnp.abs(z)))) * (1.0 / GLA_TAU)
        b = jnp.dot(tril, log_a, precision=HIGHEST, preferred_element_type=F32)
        qe_o[rows, ks] = (q * jnp.exp(b)).astype(BF16)

        s_rows = []
        for blk in range(C // SUB):
            r0 = blk * SUB
            b_i, q_i, k_i = b[r0:r0 + SUB], q[r0:r0 + SUB], k[r0:r0 + SUB]
            pair = (q_i[:, None, :] * k_i[None, :, :]
                    * jnp.exp(jnp.minimum(b_i[:, None, :] - b_i[None, :, :], 0.0)))
            pair_sum = jnp.sum(pair.reshape(SUB * SUB, DK), axis=-1, keepdims=True)
            placed = jnp.where(pc == r0 + pr % SUB, pair_sum, 0.0).astype(BF16)
            s_blk = jnp.dot(gsum, placed, preferred_element_type=F32)
            s_blk = jnp.where(scol <= srow + r0, s_blk, 0.0)
            if blk > 0:
                beta = b[r0:r0 + 1]
                q_t = (q_i * jnp.exp(b_i - beta)).astype(BF16)
                k_t = (k * jnp.exp(jnp.minimum(beta - b, 0.0))).astype(BF16)
                off = lax.dot_general(q_t, k_t, NT, preferred_element_type=F32)
                s_blk = jnp.where(scol < r0, off, s_blk)
            s_rows.append(s_blk)
        s_o[rows, hh * C:(hh + 1) * C] = jnp.concatenate(s_rows, axis=0).astype(BF16)
        b_last = b[C - 1:C]
        kd_o[rows, ks] = (k * jnp.exp(b_last - b)).astype(BF16)
        eb_o[pl.ds(ci, 1), ks] = jnp.exp(b_last)

    def two_chunks(i, carry):
        for ci in (2 * i, 2 * i + 1):
            for hh in range(2):
                chunk_head(ci, hh)
        return carry

    lax.fori_loop(0, chunks // 2, two_chunks, 0)


def _gla_scan_kernel(qe_ref, kd_ref, s_ref, eb_ref, v_ref, r_ref, ng_ref, mix_ref, o_ref, state_ref):
    del mix_ref
    C, DK, DV = GLA_CHUNK, GLA_DK, GLA_DV
    NT = (((1,), (1,)), ((), ()))
    TN = (((0,), (0,)), ((), ()))

    @pl.when(pl.program_id(0) == 0)
    def _():
        state_ref[...] = jnp.zeros_like(state_ref)

    for b in range(state_ref.shape[0]):
        for h in range(GLA_HEADS):
            ks = slice(h * DK, (h + 1) * DK)
            vs = slice(h * DV, (h + 1) * DV)
            st = state_ref[b, h]
            v = v_ref[b, :, vs]
            o = (lax.dot_general(qe_ref[b, :, ks], st.astype(BF16), NT, preferred_element_type=F32)
                 + jnp.dot(s_ref[b, :, h * C:(h + 1) * C], v, preferred_element_type=F32))
            state_ref[b, h] = st * eb_ref[b:b + 1, ks] + lax.dot_general(v, kd_ref[b, :, ks], TN,
                                                                         preferred_element_type=F32)
            ms = jnp.mean(o * o, axis=-1, keepdims=True)
            o_n = o * lax.rsqrt(ms + LN_EPS) * ng_ref[...]
            r = r_ref[b, :, vs].astype(F32)
            o_ref[b, :, vs] = (o_n * (r * jax.nn.sigmoid(r))).astype(BF16)


def _gla(ha, glr, w_gate_up, b_gate, gla_norm_g, mix, lead, *, batch, seq, chunks=8):
    C = GLA_CHUNK
    n = batch * seq
    nc = seq // C
    hp = GLA_HEADS // 2
    wq = 2 * GLA_DK
    wqk, wv = GLA_HEADS * GLA_DK, GLA_HEADS * GLA_DV
    tr = chunks * C

    def rows(width, first):
        return pl.BlockSpec((tr, width), lambda i, p: (i, first + p))

    qe, kd, sc, eb = pl.pallas_call(
        functools.partial(_gla_prep_kernel, chunks=chunks),
        out_shape=(jax.ShapeDtypeStruct((n, wqk), BF16), jax.ShapeDtypeStruct((n, wqk), BF16),
                   jax.ShapeDtypeStruct((n, GLA_HEADS * C), BF16), jax.ShapeDtypeStruct((n // C, wqk), F32)),
        grid=(n // tr, hp),
        in_specs=[rows(wq, 0), rows(wq, hp),
                  pl.BlockSpec((tr, GLA_RANK), lambda i, p: (i, 0)),
                  pl.BlockSpec((None, GLA_RANK, wq), lambda i, p: (lead, 0, p)),
                  pl.BlockSpec((None, 1, wq), lambda i, p: (lead, 0, p))],
        out_specs=(rows(wq, 0), rows(wq, 0), rows(2 * C, 0),
                   pl.BlockSpec((chunks, wq), lambda i, p: (i, p))),
        compiler_params=_cparams(("parallel", "parallel")),
        name="gla_prep",
    )(ha, ha, glr, w_gate_up, b_gate.reshape(-1, 1, wqk))

    def per_chunk(width, first):
        return pl.BlockSpec((batch, C, width), lambda c: (0, c, first))

    eb = eb.reshape(batch, nc, wqk).transpose(1, 0, 2)
    ha3 = ha.reshape(batch, seq, -1)
    out = pl.pallas_call(
        _gla_scan_kernel,
        out_shape=jax.ShapeDtypeStruct((batch, seq, mix.shape[1]), mix.dtype),
        grid=(nc,),
        in_specs=[per_chunk(wqk, 0), per_chunk(wqk, 0), per_chunk(GLA_HEADS * C, 0),
                  pl.BlockSpec((None, batch, wqk), lambda c: (c, 0, 0)),
                  per_chunk(wv, 1), per_chunk(wv, 2),
                  pl.BlockSpec((None, 1, GLA_DV), lambda c: (lead, 0, 0)),
                  pl.BlockSpec(memory_space=pl.ANY)],
        out_specs=per_chunk(wv, MIX_GLA // wv),
        scratch_shapes=[pltpu.VMEM((batch, GLA_HEADS, GLA_DV, GLA_DK), F32)],
        input_output_aliases={7: 0},
        compiler_params=_cparams(("arbitrary",)),
        name="gla_scan",
    )(qe.reshape(batch, seq, wqk), kd.reshape(batch, seq, wqk), sc.reshape(batch, seq, GLA_HEADS * C), eb,
      ha3, ha3, gla_norm_g.reshape(-1, 1, GLA_DV), mix.reshape(batch, seq, -1))
    return out.reshape(mix.shape)


def _diff_kernel(lq1_ref, lk1_ref, lq2_ref, lk2_ref, q_ref, k_ref, vt_ref, g_ref, mix_ref, o_ref,
                 qs_ref, m_ref, l_ref, acc_ref, *, tq, lambda_init):
    del mix_ref
    qi = pl.program_id(2)
    NT = (((1,), (1,)), ((), ()))
    hw = 2 * DIFF_D
    heads = q_ref.shape[1] // hw

    lane = lax.broadcasted_iota(jnp.int32, (tq, hw), 1)
    for hh in range(heads):
        q = q_ref[:, hw * hh:hw * (hh + 1)].astype(F32) * (DIFF_D ** -0.5)
        qs_ref[hh, :tq] = jnp.where(lane < DIFF_D, q, 0.0).astype(BF16)
        qs_ref[hh, tq:] = jnp.where(lane >= DIFF_D, q, 0.0).astype(BF16)
    m_ref[...] = jnp.full_like(m_ref, -jnp.inf)
    l_ref[...] = jnp.zeros_like(l_ref)
    acc_ref[...] = jnp.zeros_like(acc_ref)

    def tile(j, masked):
        r0 = pl.multiple_of(j * tq, tq)
        for hh in range(heads):
            hs = slice(hw * hh, hw * (hh + 1))
            s = lax.dot_general(k_ref[pl.ds(r0, tq), hs], qs_ref[hh], NT, preferred_element_type=F32)
            if masked:
                r = lax.broadcasted_iota(jnp.int32, s.shape, 0)
                c = lax.broadcasted_iota(jnp.int32, s.shape, 1)
                s = jnp.where(r <= jnp.where(c >= tq, c - tq, c), s, -jnp.inf)
            m_prev = m_ref[hh]
            m_new = jnp.maximum(m_prev, jnp.max(s, axis=0, keepdims=True))
            alpha = jnp.exp(m_prev - m_new)
            p = jnp.exp(s - m_new)
            l_ref[hh] = alpha * l_ref[hh] + jnp.sum(p, axis=0, keepdims=True)
            acc_ref[hh] = alpha * acc_ref[hh] + jnp.dot(vt_ref[hs, pl.ds(r0, tq)], p.astype(BF16),
                                                      preferred_element_type=F32)
            m_ref[hh] = m_new

    def full_tile(j, carry):
        tile(j, False)
        return carry

    lax.fori_loop(0, qi, full_tile, 0)
    tile(qi, True)

    lam = (jnp.exp(jnp.sum(lq1_ref[...] * lk1_ref[...], axis=-1, keepdims=True))
           - jnp.exp(jnp.sum(lq2_ref[...] * lk2_ref[...], axis=-1, keepdims=True)) + lambda_init)
    for hh in range(heads):
        o = acc_ref[hh] / l_ref[hh]
        o = o[:, :tq] - lam * o[:, tq:]
        ms = jnp.mean(o * o, axis=0, keepdims=True)
        o = o * lax.rsqrt(ms + LN_EPS) * g_ref[...] * (1.0 - lambda_init)
        o_ref[:, hw * hh:hw * (hh + 1)] = o.T.astype(BF16)


def _diff_attention(dq, dk, dvt, lq1, lk1, lq2, lk2, norm_g, mix, lead, *, batch, seq, lambda_init, tq, heads=2):
    nq = seq // tq
    hw = 2 * DIFF_D
    pw = heads * hw
    lam_spec = pl.BlockSpec((None, 1, DIFF_D), lambda b, p, qi: (lead, 0, 0))
    lq1, lk1, lq2, lk2 = (t.reshape(-1, 1, DIFF_D) for t in (lq1, lk1, lq2, lk2))
    return pl.pallas_call(
        functools.partial(_diff_kernel, tq=tq, lambda_init=lambda_init),
        out_shape=jax.ShapeDtypeStruct(mix.shape, mix.dtype),
        grid=(batch, DIFF_HEADS // heads, nq),
        in_specs=[lam_spec] * 4 + [
            pl.BlockSpec((tq, pw), lambda b, p, qi: (b * nq + qi, p)),
            pl.BlockSpec((seq, pw), lambda b, p, qi: (b, p)),
            pl.BlockSpec((None, pw, seq), lambda b, p, qi: (b, p, 0)),
            pl.BlockSpec((None, hw, 1), lambda b, p, qi: (lead, 0, 0)),
            pl.BlockSpec(memory_space=pl.ANY)],
        out_specs=pl.BlockSpec((tq, pw), lambda b, p, qi: (b * nq + qi, MIX_DIFF // pw + p)),
        scratch_shapes=[pltpu.VMEM((heads, 2 * tq, hw), BF16), pltpu.VMEM((heads, 1, 2 * tq), F32),
                        pltpu.VMEM((heads, 1, 2 * tq), F32), pltpu.VMEM((heads, hw, 2 * tq), F32)],
        input_output_aliases={8: 0},
        compiler_params=_cparams(("parallel", "parallel", "arbitrary")),
        name="diff_attn",
    )(lq1, lk1, lq2, lk2, dq, dk, dvt, norm_g.reshape(-1, hw, 1), mix)


def _dsa_kernel(iq_ref, sq_ref, ikd_ref, wt_ref, sk_ref, svt_ref, mix_ref, o_ref,
                iqm_ref, keys_ref, bias_ref, q5_ref, m_ref, l_ref, acc_ref, *, tq, tk, seq, k_sel):
    del mix_ref
    qi = pl.program_id(1)
    NT = (((1,), (1,)), ((), ()))
    n_tiles = (qi * tq + tq + tk - 1) // tk
    qpos = qi * tq + lax.broadcasted_iota(jnp.int32, (1, tq), 1)
    krow = lax.broadcasted_iota(jnp.int32, (tk, tq), 0)
    idx_scale = (IDX_DIM ** -0.5) * (IDX_HEADS ** -0.5)
    neg_inf = F32(-jnp.inf)

    def tile_start(j):
        return pl.multiple_of(j * tk, tk)

    lane = lax.broadcasted_iota(jnp.int32, (tq, 128), 1)
    for t in range(IDX_HEADS // 2):
        a = iq_ref[:, 128 * t:128 * (t + 1)]
        zero = jnp.zeros_like(a)
        iqm_ref[t, :tq] = jnp.where(lane < IDX_DIM, a, zero)
        iqm_ref[t, tq:] = jnp.where(lane >= IDX_DIM, a, zero)
    wt = wt_ref[...]

    def score_tile(j, carry):
        r0 = tile_start(j)
        kd = ikd_ref[pl.ds(r0, tk), :]
        acc = jnp.zeros((tk, 2 * tq), F32)
        for t in range(IDX_HEADS // 2):
            lg = lax.dot_general(kd, iqm_ref[t], NT, preferred_element_type=F32)
            w2 = jnp.concatenate([wt[2 * t:2 * t + 1], wt[2 * t + 1:2 * t + 2]], axis=1)
            acc = acc + jnp.maximum(lg, 0.0) * w2
        acc = acc[:, :tq] + acc[:, tq:]
        bits = pltpu.bitcast(acc * idx_scale, jnp.int32)
        key = jnp.where(bits < 0, (bits ^ 0x7FFFFFFF) + 1, bits)
        keys_ref[pl.ds(r0, tk), :] = jnp.where(r0 + krow <= qpos, key, INT_MIN)
        return carry

    lax.fori_loop(0, n_tiles, score_tile, 0)

    def count_ones(ones):
        def body(j, cnt):
            r0 = tile_start(j)
            return cnt + jnp.sum(ones(keys_ref[pl.ds(r0, tk), :], r0 + krow), axis=0, keepdims=True)
        return lax.fori_loop(0, n_tiles, body, jnp.zeros((1, tq), jnp.int32))

    def count(pred):
        return count_ones(lambda kk, kp: jnp.where(pred(kk, kp), 1, 0))

    def thr_bit(i, t_u):
        cand_u = t_u | jnp.left_shift(jnp.int32(1), 31 - i)
        cand = cand_u ^ INT_MIN
        return jnp.where(count(lambda kk, kp: kk >= cand) >= k_sel, cand_u, t_u)

    thr = lax.fori_loop(0, 32, thr_bit, jnp.zeros((1, tq), jnp.int32)) ^ INT_MIN
    need = k_sel - count(lambda kk, kp: kk > thr)
    n_eq = count(lambda kk, kp: kk == thr)
    excess = jnp.where(thr == INT_MIN, 0, jnp.where(n_eq > need, 1, 0))

    def tie_search():
        def idx_bit(i, j_lo):
            cand = j_lo | jnp.left_shift(jnp.int32(1), (seq - 1).bit_length() - 1 - i)
            c = count_ones(lambda kk, kp: jnp.where(kk == thr, jnp.where(kp < cand, 1, 0), 0))
            return jnp.where(c < need, cand, j_lo)
        return lax.fori_loop(0, (seq - 1).bit_length(), idx_bit, jnp.zeros((1, tq), jnp.int32))

    j_star = lax.cond(jnp.max(excess) > 0, tie_search, lambda: jnp.full((1, tq), seq, jnp.int32))
    j_star = jnp.where(thr == INT_MIN, -1, jnp.where(excess > 0, j_star, seq))

    def bias_tile(j, carry):
        r0 = tile_start(j)
        kk = keys_ref[pl.ds(r0, tk), :]
        tie = jnp.where(r0 + krow <= j_star, 0.0, neg_inf)
        bias_ref[pl.ds(r0, tk), :] = jnp.where(kk > thr, 0.0, jnp.where(kk == thr, tie, neg_inf))
        return carry

    lax.fori_loop(0, n_tiles, bias_tile, 0)

    scale = DSA_DH ** -0.5
    rep = DSA_HEADS // DSA_KV
    for g in range(DSA_KV):
        for i in range(rep):
            h = g * rep + i
            q5_ref[g, tq * i:tq * (i + 1)] = sq_ref[:, DSA_DH * h:DSA_DH * (h + 1)]
    m_ref[...] = jnp.full_like(m_ref, neg_inf)
    l_ref[...] = jnp.zeros_like(l_ref)
    acc_ref[...] = jnp.zeros_like(acc_ref)

    def attn_tile(j, carry):
        r0 = tile_start(j)
        bias = bias_ref[pl.ds(r0, tk), :]
        bias = jnp.concatenate([bias] * rep, axis=1)
        for g in range(DSA_KV):
            gs = slice(DSA_DH * g, DSA_DH * (g + 1))
            s = lax.dot_general(sk_ref[pl.ds(r0, tk), gs], q5_ref[g], NT, preferred_element_type=F32)
            s = s * scale + bias
            m_prev = m_ref[g]
            m_new = jnp.maximum(m_prev, jnp.max(s, axis=0, keepdims=True))
            m_safe = jnp.where(m_new == neg_inf, 0.0, m_new)
            alpha = jnp.exp(m_prev - m_safe)
            p = jnp.exp(s - m_safe)
            l_ref[g] = alpha * l_ref[g] + jnp.sum(p, axis=0, keepdims=True)
            acc_ref[g] = alpha * acc_ref[g] + jnp.dot(svt_ref[gs, pl.ds(r0, tk)], p.astype(BF16),
                                                    preferred_element_type=F32)
            m_ref[g] = m_new
        return carry

    lax.fori_loop(0, n_tiles, attn_tile, 0)
    for g in range(DSA_KV):
        out_t = acc_ref[g] / l_ref[g]
        for i in range(rep):
            h = g * rep + i
            o_ref[:, DSA_DH * h:DSA_DH * (h + 1)] = out_t[:, tq * i:tq * (i + 1)].T.astype(BF16)


def _dsa(iq, sq, ikd, wt, sk, svt, mix, *, batch, seq, tq, tk):
    nq = seq // tq
    k_sel = min(IDX_TOPK, seq // 4)
    rep = DSA_HEADS // DSA_KV
    return pl.pallas_call(
        functools.partial(_dsa_kernel, tq=tq, tk=tk, seq=seq, k_sel=k_sel),
        out_shape=jax.ShapeDtypeStruct(mix.shape, mix.dtype),
        grid=(batch, nq),
        in_specs=[pl.BlockSpec((tq, W_IQ), lambda b, qi: (b * nq + qi, 0)),
                  pl.BlockSpec((tq, W_SQ), lambda b, qi: (b * nq + qi, 0)),
                  pl.BlockSpec((seq, 128), lambda b, qi: (b, 0)),
                  pl.BlockSpec((None, IDX_HEADS, tq), lambda b, qi: (b, 0, qi)),
                  pl.BlockSpec((seq, W_SKV), lambda b, qi: (b, 0)),
                  pl.BlockSpec((None, W_SKV, seq), lambda b, qi: (b, 0, 0)),
                  pl.BlockSpec(memory_space=pl.ANY)],
        out_specs=pl.BlockSpec((tq, W_SQ), lambda b, qi: (b * nq + qi, MIX_DSA // W_SQ)),
        scratch_shapes=[pltpu.VMEM((IDX_HEADS // 2, 2 * tq, 128), BF16), pltpu.VMEM((seq, tq), jnp.int32),
                        pltpu.VMEM((seq, tq), F32), pltpu.VMEM((DSA_KV, rep * tq, DSA_DH), BF16),
                        pltpu.VMEM((DSA_KV, 1, rep * tq), F32), pltpu.VMEM((DSA_KV, 1, rep * tq), F32),
                        pltpu.VMEM((DSA_KV, DSA_DH, rep * tq), F32)],
        input_output_aliases={6: 0},
        compiler_params=_cparams(("parallel", "arbitrary")),
        name="dsa",
    )(iq, sq, ikd, wt, sk, svt, mix)


def _router_kernel(x_ref, w_ref, b_ref, e_ref, g_ref, xp_ref):
    x = x_ref[...]
    logits = jnp.dot(x, w_ref[...], precision=HIGHEST, preferred_element_type=F32) + b_ref[...]
    lane = lax.broadcasted_iota(jnp.int32, logits.shape, 1)
    neg_inf = F32(-jnp.inf)
    lg = jnp.where(lane < MOE_E, logits, neg_inf)
    m1 = jnp.max(lg, axis=-1, keepdims=True)
    i1 = jnp.min(jnp.where(lg == m1, lane, 128), axis=-1, keepdims=True)
    lg2 = jnp.where(lane == i1, neg_inf, lg)
    m2 = jnp.max(lg2, axis=-1, keepdims=True)
    i2 = jnp.min(jnp.where(lg2 == m2, lane, 128), axis=-1, keepdims=True)
    e21 = jnp.exp(m2 - m1)
    g1 = 1.0 / (1.0 + e21)
    e_ref[...] = jnp.where(lane == 0, i1, jnp.where(lane == 1, i2, 0))
    g_ref[...] = jnp.where(lane == 0, g1, jnp.where(lane == 1, e21 * g1, 0.0))
    half = x.shape[1] // 2
    lo = pltpu.bitcast(x[:, :half].astype(BF16).astype(F32), jnp.uint32)
    hi = pltpu.bitcast(x[:, half:].astype(BF16).astype(F32), jnp.uint32)
    xp_ref[...] = (hi & jnp.uint32(0xFFFF0000)) | (lo >> 16)


def _router(x, router_w, router_b, *, tm=256):
    n, d = x.shape
    row = lambda w: pl.BlockSpec((tm, w), lambda i: (i, 0))
    return pl.pallas_call(
        _router_kernel,
        out_shape=(jax.ShapeDtypeStruct((n, 128), jnp.int32), jax.ShapeDtypeStruct((n, 128), F32),
                   jax.ShapeDtypeStruct((n, d // 2), jnp.uint32)),
        grid=(n // tm,),
        in_specs=[row(d), pl.BlockSpec((d, 128), lambda i: (0, 0)), pl.BlockSpec((1, 128), lambda i: (0, 0))],
        out_specs=(row(128), row(128), row(d // 2)),
        compiler_params=_cparams(("parallel",)),
        name="router",
    )(x, router_w, router_b)


def _dispatch_kernel(pos_ref, xp_ref, buf_ref, o_ref, sem, *, tb):
    del buf_ref
    base = pl.program_id(0) * tb

    def row_copy(i, s):
        return pltpu.make_async_copy(xp_ref.at[pl.ds(i, 1)], o_ref.at[pl.ds(pos_ref[2 * (base + i) + s], 1)], sem)

    def start(i, c):
        row_copy(i, 0).start()
        row_copy(i, 1).start()
        return c

    def wait(i, c):
        row_copy(i, 0).wait()
        row_copy(i, 1).wait()
        return c

    lax.fori_loop(0, tb, start, 0)
    lax.fori_loop(0, tb, wait, 0)


def _dispatch(pos, xp, buf, *, tb=256):
    n, w = xp.shape
    return pl.pallas_call(
        functools.partial(_dispatch_kernel, tb=tb),
        out_shape=jax.ShapeDtypeStruct(buf.shape, buf.dtype),
        grid_spec=pltpu.PrefetchScalarGridSpec(
            num_scalar_prefetch=1,
            grid=(n // tb,),
            in_specs=[pl.BlockSpec((tb, w), lambda i, pos: (i, 0)), pl.BlockSpec(memory_space=pl.ANY)],
            out_specs=pl.BlockSpec(memory_space=pl.ANY),
            scratch_shapes=[pltpu.SemaphoreType.DMA],
        ),
        input_output_aliases={2: 0},
        compiler_params=_cparams(("arbitrary",)),
        name="moe_dispatch",
    )(pos, xp, buf)


def _unpack_rows(word):
    lo = pltpu.bitcast(word << 16, F32).astype(BF16)
    hi = pltpu.bitcast(word & jnp.uint32(0xFFFF0000), F32).astype(BF16)
    return lo, hi


def _moe_up_kernel(be_ref, bf_ref, na_ref, a_ref, w1_ref, w3_ref, o_ref, w1b_ref, w3b_ref):
    del be_ref
    i = pl.program_id(1)

    @pl.when(i < na_ref[0])
    def _():
        @pl.when(bf_ref[i] == 1)
        def _():
            w1b_ref[...] = w1_ref[...].astype(BF16)
            w3b_ref[...] = w3_ref[...].astype(BF16)

        lo, hi = _unpack_rows(a_ref[...])
        half = lo.shape[1]
        g = (jnp.dot(lo, w1b_ref[:half], preferred_element_type=F32)
             + jnp.dot(hi, w1b_ref[half:], preferred_element_type=F32))
        u = (jnp.dot(lo, w3b_ref[:half], preferred_element_type=F32)
             + jnp.dot(hi, w3b_ref[half:], preferred_element_type=F32))
        o_ref[...] = (g * jax.nn.sigmoid(g) * u).astype(BF16)

    @pl.when(i >= na_ref[0])
    def _():
        o_ref[...] = jnp.zeros_like(o_ref)


def _moe_down_kernel(be_ref, bf_ref, na_ref, h_ref, w2_ref, o_ref, w2b_ref):
    del be_ref
    i = pl.program_id(1)

    @pl.when(i < na_ref[0])
    def _():
        @pl.when(bf_ref[i] == 1)
        def _():
            w2b_ref[...] = w2_ref[...].astype(BF16)

        o_ref[...] = jnp.dot(h_ref[...], w2b_ref[...], preferred_element_type=F32)

    @pl.when(i >= na_ref[0])
    def _():
        o_ref[...] = jnp.zeros_like(o_ref)


def _moe_ffn(block_e, block_first, n_active, a_sorted, w1, w3, w2, lead, *, tm, tn):
    r, half = a_sorted.shape
    d = 2 * half
    f = w1.shape[-1]
    nb = r // tm

    def rows(i, na):
        return jnp.minimum(i, na[0] - 1)

    h = pl.pallas_call(
        _moe_up_kernel,
        out_shape=jax.ShapeDtypeStruct((r, f), BF16),
        grid_spec=pltpu.PrefetchScalarGridSpec(
            num_scalar_prefetch=3,
            grid=(f // tn, nb),
            in_specs=[pl.BlockSpec((tm, half), lambda n, i, be, bf, na: (rows(i, na), 0)),
                      pl.BlockSpec((None, None, d, tn), lambda n, i, be, bf, na: (lead, be[i], 0, n)),
                      pl.BlockSpec((None, None, d, tn), lambda n, i, be, bf, na: (lead, be[i], 0, n))],
            out_specs=pl.BlockSpec((tm, tn), lambda n, i, be, bf, na: (i, n)),
            scratch_shapes=[pltpu.VMEM((d, tn), BF16), pltpu.VMEM((d, tn), BF16)],
        ),
        compiler_params=_cparams(("arbitrary", "arbitrary")),
        name="moe_up",
    )(block_e, block_first, n_active, a_sorted, w1, w3)
    return pl.pallas_call(
        _moe_down_kernel,
        out_shape=jax.ShapeDtypeStruct((r, d), F32),
        grid_spec=pltpu.PrefetchScalarGridSpec(
            num_scalar_prefetch=3,
            grid=(d // tn, nb),
            in_specs=[pl.BlockSpec((tm, f), lambda n, i, be, bf, na: (rows(i, na), 0)),
                      pl.BlockSpec((None, None, f, tn), lambda n, i, be, bf, na: (lead, be[i], 0, n))],
            out_specs=pl.BlockSpec((tm, tn), lambda n, i, be, bf, na: (i, n)),
            scratch_shapes=[pltpu.VMEM((f, tn), BF16)],
        ),
        compiler_params=_cparams(("arbitrary", "arbitrary")),
        name="moe_down",
    )(block_e, block_first, n_active, h, w2)


def _combine_ln_kernel(pos_ref, x_ref, gate_ref, g_ref, b_ref, y_ref, o_ref, ybuf_ref, sem, *, tb, alpha):
    base = pl.program_id(0) * tb

    def row_copy(i, s):
        return pltpu.make_async_copy(y_ref.at[pl.ds(pos_ref[2 * (base + i) + s], 1)],
                                     ybuf_ref.at[s, pl.ds(i, 1)], sem)

    def start(i, c):
        row_copy(i, 0).start()
        row_copy(i, 1).start()
        return c

    def wait(i, c):
        row_copy(i, 0).wait()
        row_copy(i, 1).wait()
        return c

    lax.fori_loop(0, tb, start, 0)
    lax.fori_loop(0, tb, wait, 0)
    gate = gate_ref[...]
    f = ybuf_ref[0] * gate[:, 0:1] + ybuf_ref[1] * gate[:, 1:2]
    z = alpha * x_ref[...] + f
    mu = jnp.mean(z, axis=-1, keepdims=True)
    zc = z - mu
    var = jnp.mean(zc * zc, axis=-1, keepdims=True)
    o_ref[...] = zc * lax.rsqrt(var + LN_EPS) * g_ref[...] + b_ref[...]


def _combine_ln(pos, x, gates, g, b, y, lead, *, alpha, tb=128):
    n, d = x.shape
    par = pl.BlockSpec((None, 1, d), lambda i, pos: (lead, 0, 0))
    return pl.pallas_call(
        functools.partial(_combine_ln_kernel, tb=tb, alpha=alpha),
        out_shape=jax.ShapeDtypeStruct((n, d), F32),
        grid_spec=pltpu.PrefetchScalarGridSpec(
            num_scalar_prefetch=1,
            grid=(n // tb,),
            in_specs=[pl.BlockSpec((tb, d), lambda i, pos: (i, 0)),
                      pl.BlockSpec((tb, 128), lambda i, pos: (i, 0)),
                      par, par, pl.BlockSpec(memory_space=pl.ANY)],
            out_specs=pl.BlockSpec((tb, d), lambda i, pos: (i, 0)),
            scratch_shapes=[pltpu.VMEM((2, tb, d), F32), pltpu.SemaphoreType.DMA],
        ),
        compiler_params=_cparams(("arbitrary",)),
        name="moe_combine_ln",
    )(pos, x, gates, g.reshape(-1, 1, d), b.reshape(-1, 1, d), y)


def _moe_layer(x, router_w, router_b, w1, w3, w2, ln_g, ln_b, lead_moe, lead_ln, *, alpha, tm=512, tn=512):
    n, d = x.shape
    rw = jnp.pad(router_w[lead_moe], ((0, 0), (0, 128 - MOE_E)))
    rb = jnp.pad(router_b[lead_moe], (0, 128 - MOE_E)).reshape(1, 128)
    top_e, gates, xp = _router(x, rw, rb)
    flat_e = top_e[:, :MOE_TOPK].reshape(-1)
    onehot = (flat_e[:, None] == jnp.arange(MOE_E, dtype=jnp.int32)[None, :]).astype(jnp.int32)
    rank = jnp.sum((jnp.cumsum(onehot, axis=0) - onehot) * onehot, axis=1)
    counts = jnp.sum(onehot, axis=0)
    padded = (counts + tm - 1) // tm * tm
    end_padded = jnp.cumsum(padded)
    start_padded = end_padded - padded
    pos = (start_padded[flat_e] + rank).astype(jnp.int32)
    nb = -(-(n * MOE_TOPK + MOE_E * (tm - 1)) // tm)
    block_start = jnp.arange(nb, dtype=jnp.int32) * tm
    block_e = jnp.sum((block_start[:, None] >= end_padded[None, :]).astype(jnp.int32), axis=1)
    block_e = jnp.minimum(block_e, MOE_E - 1)
    n_active = (end_padded[-1] // tm).astype(jnp.int32).reshape(1)
    block_first = jnp.concatenate([jnp.ones((1,), jnp.int32), (block_e[1:] != block_e[:-1]).astype(jnp.int32)])
    a_sorted = _dispatch(pos, xp, jnp.zeros((nb * tm, d // 2), jnp.uint32))
    y = _moe_ffn(block_e, block_first, n_active, a_sorted, w1, w3, w2, lead_moe, tm=tm, tn=tn)
    return _combine_ln(pos, x, gates, ln_g, ln_b, y, lead_ln, alpha=alpha)


def kernel(x, positions, w_in, w_gate_up, b_gate, gla_norm_g, lambda_q1, lambda_k1, lambda_q2, lambda_k2,
           diff_norm_g, w_out, ln1_g, ln1_b, ln2_g, ln2_b, ffn_w1, ffn_w3, ffn_w2, router_w, router_b,
           moe_w1, moe_w3, moe_w2):
    batch, seq, d = x.shape
    n = batch * seq
    depth = w_in.shape[0]
    alpha = (2 * depth) ** 0.25
    xf = x.reshape(n, d)
    xb = xf.astype(BF16)
    pos = positions.reshape(n)
    tab_a, half_a = _rope_tables(pos, DIFF_D)
    tab_b, half_b = _rope_tables(pos, DSA_DH)
    plain = lambda i, j, k: (i, k)
    for l in range(depth):
        ha = _matmul(xb, w_in, n_out=W_GLA, k_steps=d // 1024, tm=1024, tn=1536, tk=1024, out_dtype=BF16,
                     a_map=plain, w_map=lambda i, j, k, l=l: (l, k, j), w_lead=True, name="in_proj_gla")
        hm = _matmul(xb, w_in, n_out=HM_TILES * HM_TN, k_steps=d // 1024, tm=1024, tn=HM_TN, tk=1024,
                     out_dtype=BF16, a_map=plain,
                     w_map=lambda i, j, k, l=l: (l, k, HM_BASE // HM_TN + j), w_lead=True,
                     n_valid=N_IN - HM_BASE, name="in_proj_attn")
        wl = w_in[l]
        w_c = jnp.concatenate([wl[:, OFF_IK:OFF_IK + IDX_DIM + IDX_HEADS], wl[:, OFF_GLR:OFF_GLR + GLA_RANK],
                               jnp.zeros((d, 128 - IDX_DIM - IDX_HEADS - GLA_RANK), F32)], axis=1)
        hc = _matmul(xb, w_c, n_out=128, k_steps=d // 512, tm=1024, tn=128, tk=512, out_dtype=F32,
                     a_map=plain, w_map=lambda i, j, k: (k, j), name="in_proj_small")
        glr = hc[:, IDX_DIM + IDX_HEADS:IDX_DIM + IDX_HEADS + GLA_RANK]
        wt = hc[:, IDX_DIM:IDX_DIM + IDX_HEADS].reshape(batch, seq, IDX_HEADS).transpose(0, 2, 1)
        dq, dk, dv, sq, sk, sv, iq, ikd = _split_rope(hm, hc, tab_a, half_a, tab_b, half_b)
        svt = sv.reshape(batch, seq, W_SKV).transpose(0, 2, 1)
        dvt = dv.reshape(batch, seq, W_DIFF).transpose(0, 2, 1)

        mix = jnp.zeros((n, W_MIX), BF16)
        mix = _gla(ha, glr, w_gate_up, b_gate, gla_norm_g, mix, l, batch=batch, seq=seq)
        lambda_init = 0.8 - 0.6 * math.exp(-0.3 * l)
        mix = _diff_attention(dq, dk, dvt, lambda_q1, lambda_k1, lambda_q2, lambda_k2, diff_norm_g, mix, l,
                              batch=batch, seq=seq, lambda_init=lambda_init, tq=512)
        mix = _dsa(iq, sq, ikd, wt, sk, svt, mix, batch=batch, seq=seq, tq=128, tk=512)

        n_gla = GLA_HEADS * GLA_DV // MIX_BLK
        n_skip = MIX_DIFF // MIX_BLK - n_gla
        proj = _matmul(
            mix, w_out, n_out=d, k_steps=d // MIX_BLK, tm=1024, tn=1024, tk=MIX_BLK, out_dtype=F32,
            a_map=lambda i, j, k: (i, jnp.where(k < n_gla, k, k + n_skip)),
            w_map=lambda i, j, k, l=l: (l, k, j), w_lead=True, name="out_proj")
        xf, xb = _add_ln(xf, proj, ln1_g, ln1_b, l, alpha=alpha)

        j = l // 2
        if l % 2 == 0:
            f_dim = ffn_w2.shape[1]
            f_pad = -(-f_dim // 1024) * 1024
            hid = _swiglu_up(xb, ffn_w1, ffn_w3, j, tm=2048, tn=1024, tk=512, f_out=f_pad)
            down = _matmul(hid, ffn_w2, n_out=d, k_steps=f_pad // 512, tm=2048, tn=1024, tk=512, out_dtype=F32,
                           a_map=plain, w_map=lambda i, j_, k, j=j: (j, k, j_), w_lead=True, k_valid=f_dim,
                           name="ffn_down")
            xf, xb = _add_ln(xf, down, ln2_g, ln2_b, l, alpha=alpha)
        else:
            xf = _moe_layer(xf, router_w, router_b, moe_w1, moe_w3, moe_w2, ln2_g, ln2_b, j, l, alpha=alpha)
            xb = xf.astype(BF16)
    return xf.reshape(batch, seq, d)
```

```python
import functools
import math

import jax
import jax.numpy as jnp
from jax import lax
from jax.experimental import pallas as pl
from jax.experimental.pallas import tpu as pltpu

F32 = jnp.float32
BF16 = jnp.bfloat16
HIGHEST = lax.Precision.HIGHEST
INT_MIN = -(2 ** 31)

ROPE_THETA = 500000.0
GLA_HEADS, GLA_DK, GLA_DV, GLA_RANK, GLA_TAU, GLA_CHUNK = 4, 192, 384, 16, 16.0, 64
GLA_SUB = 16
DIFF_HEADS, DIFF_D = 10, 64
DSA_HEADS, DSA_KV, DSA_DH = 10, 2, 128
IDX_HEADS, IDX_DIM, IDX_TOPK = 32, 64, 256
MOE_E, MOE_TOPK = 8, 2
LN_EPS = 1e-5

W_GLA = 2 * GLA_HEADS * GLA_DK + 2 * GLA_HEADS * GLA_DV
OFF_GLR = W_GLA
OFF_DQ = OFF_GLR + GLA_RANK
W_DIFF = DIFF_HEADS * 2 * DIFF_D
OFF_DK, OFF_DV = OFF_DQ + W_DIFF, OFF_DQ + 2 * W_DIFF
OFF_SQ = OFF_DQ + 3 * W_DIFF
W_SQ, W_SKV = DSA_HEADS * DSA_DH, DSA_KV * DSA_DH
OFF_SK, OFF_SV = OFF_SQ + W_SQ, OFF_SQ + W_SQ + W_SKV
OFF_IQ = OFF_SV + W_SKV
W_IQ = IDX_HEADS * IDX_DIM
OFF_IK = OFF_IQ + W_IQ
OFF_IW = OFF_IK + IDX_DIM
N_IN = OFF_IW + IDX_HEADS

HM_BASE = W_GLA
HM_TN = 1152
HM_TILES = -(-(N_IN - HM_BASE) // HM_TN)

MIX_GLA, MIX_DIFF, MIX_DSA = 0, 2560, 3840
W_MIX = MIX_DSA + DSA_HEADS * DSA_DH
MIX_BLK = 512

VMEM_LIMIT = 56 * 1024 * 1024


def _cparams(sem, vmem=VMEM_LIMIT):
    return pltpu.CompilerParams(dimension_semantics=sem, vmem_limit_bytes=vmem)


def _mm_kernel(a_ref, w_ref, o_ref, acc_ref, *, k_valid, n_valid, col_start, w_t):
    k = pl.program_id(2)
    k_ax, n_ax = (1, 0) if w_t else (0, 1)

    def partial_product():
        w = w_ref[...]
        if k_valid is not None:
            kk = k * w.shape[k_ax] + lax.broadcasted_iota(jnp.int32, w.shape, k_ax)
            w = jnp.where(kk < k_valid, w, 0.0)
        if n_valid is not None:
            col = col_start(pl.program_id(1)) + lax.broadcasted_iota(jnp.int32, w.shape, n_ax)
            w = jnp.where(col < n_valid, w, 0.0)
        return lax.dot_general(a_ref[...], w.astype(BF16), (((1,), (k_ax,)), ((), ())),
                               preferred_element_type=F32)

    @pl.when(k == 0)
    def _():
        acc_ref[...] = partial_product()

    @pl.when(k > 0)
    def _():
        acc_ref[...] += partial_product()

    @pl.when(k == pl.num_programs(2) - 1)
    def _():
        o_ref[...] = acc_ref[...].astype(o_ref.dtype)


def _matmul(a, w, *, n_out, k_steps, tm, tn, tk, out_dtype, a_map, w_map, w_lead=False, k_valid=None,
            n_valid=None, col_start=None, w_t=False, name="mm"):
    m = a.shape[0]
    w_block = (tn, tk) if w_t else (tk, tn)
    if w_lead:
        w_block = (None,) + w_block
    return pl.pallas_call(
        functools.partial(_mm_kernel, k_valid=k_valid, n_valid=n_valid, col_start=col_start, w_t=w_t),
        out_shape=jax.ShapeDtypeStruct((m, n_out), out_dtype),
        grid=(m // tm, pl.cdiv(n_out, tn), k_steps),
        in_specs=[pl.BlockSpec((tm, tk), a_map), pl.BlockSpec(w_block, w_map)],
        out_specs=pl.BlockSpec((tm, tn), lambda i, j, k: (i, j)),
        scratch_shapes=[pltpu.VMEM((tm, tn), F32)],
        compiler_params=_cparams(("parallel", "parallel", "arbitrary")),
        name=name,
    )(a, w)


def _swiglu_kernel(a_ref, w1_ref, w3_ref, o_ref, acc1_ref, acc3_ref, *, n_valid):
    k = pl.program_id(2)

    def partial_product(w_ref):
        return jnp.dot(a_ref[...], w_ref[...].astype(BF16), preferred_element_type=F32)

    @pl.when(k == 0)
    def _():
        acc1_ref[...] = partial_product(w1_ref)
        acc3_ref[...] = partial_product(w3_ref)

    @pl.when(k > 0)
    def _():
        acc1_ref[...] += partial_product(w1_ref)
        acc3_ref[...] += partial_product(w3_ref)

    @pl.when(k == pl.num_programs(2) - 1)
    def _():
        g = acc1_ref[...]
        h = g * jax.nn.sigmoid(g) * acc3_ref[...]
        col = pl.program_id(1) * h.shape[1] + lax.broadcasted_iota(jnp.int32, h.shape, 1)
        o_ref[...] = jnp.where(col < n_valid, h, 0.0).astype(o_ref.dtype)


def _swiglu_up(a, w1, w3, lead, *, tm, tn, tk, f_out):
    m, kdim = a.shape
    f = w1.shape[-1]
    wspec = pl.BlockSpec((None, tk, tn), lambda i, j, k: (lead, k, j))
    return pl.pallas_call(
        functools.partial(_swiglu_kernel, n_valid=f),
        out_shape=jax.ShapeDtypeStruct((m, f_out), BF16),
        grid=(m // tm, f_out // tn, kdim // tk),
        in_specs=[pl.BlockSpec((tm, tk), lambda i, j, k: (i, k)), wspec, wspec],
        out_specs=pl.BlockSpec((tm, tn), lambda i, j, k: (i, j)),
        scratch_shapes=[pltpu.VMEM((tm, tn), F32), pltpu.VMEM((tm, tn), F32)],
        compiler_params=_cparams(("parallel", "parallel", "arbitrary")),
        name="swiglu_up",
    )(a, w1, w3)


def _add_ln_kernel(x_ref, y_ref, g_ref, b_ref, o_ref, ob_ref, *, alpha):
    z = alpha * x_ref[...] + y_ref[...].astype(F32)
    mu = jnp.mean(z, axis=-1, keepdims=True)
    zc = z - mu
    var = jnp.mean(zc * zc, axis=-1, keepdims=True)
    out = zc * lax.rsqrt(var + LN_EPS) * g_ref[...] + b_ref[...]
    o_ref[...] = out
    ob_ref[...] = out.astype(BF16)


def _add_ln(x, y, g, b, lead, *, alpha, tm=256):
    m, d = x.shape
    row = pl.BlockSpec((tm, d), lambda i: (i, 0))
    par = pl.BlockSpec((None, 1, d), lambda i: (lead, 0, 0))
    g, b = g.reshape(-1, 1, d), b.reshape(-1, 1, d)
    return pl.pallas_call(
        functools.partial(_add_ln_kernel, alpha=alpha),
        out_shape=(jax.ShapeDtypeStruct((m, d), F32), jax.ShapeDtypeStruct((m, d), BF16)),
        grid=(m // tm,),
        in_specs=[row, row, par, par],
        out_specs=(row, row),
        compiler_params=_cparams(("parallel",)),
        name="add_ln",
    )(x, y, g, b)


def _rope_tables(positions, width):
    rot = width // 4
    half = rot // 2
    inv_freq = 1.0 / (ROPE_THETA ** (jnp.arange(half, dtype=F32) * 2.0 / rot))
    ang = positions.astype(F32)[:, None] * inv_freq
    cos, sin = jnp.cos(ang), jnp.sin(ang)
    n = positions.shape[0]
    one = jnp.ones((n, width - rot), F32)
    zero_h = jnp.zeros((n, half), F32)
    zero_r = jnp.zeros((n, width - rot), F32)
    c = jnp.concatenate([cos, cos, one], axis=1)
    s_up = jnp.concatenate([-sin, zero_h, zero_r], axis=1)
    s_dn = jnp.concatenate([zero_h, sin, zero_r], axis=1)
    rep = 128 // width
    return tuple(jnp.tile(t, (1, rep)) for t in (c, s_up, s_dn)), half


def _rope_tile(x, c, s_up, s_dn, half):
    return x * c + pltpu.roll(x, 128 - half, 1) * s_up + pltpu.roll(x, half, 1) * s_dn


def _split_rope_kernel(hm_ref, hc_ref, ca_ref, ua_ref, da_ref, cb_ref, ub_ref, db_ref,
                       dq_o, dk_o, dv_o, sq_o, sk_o, sv_o, iq_o, ik_o, *, shift, ik_shift, half_a, half_b):
    ta = (ca_ref[...], ua_ref[...], da_ref[...])
    tb = (cb_ref[...], ub_ref[...], db_ref[...])
    lane = lax.broadcasted_iota(jnp.int32, ca_ref.shape, 1)

    def shifted(k):
        return pltpu.roll(hm_ref[:, 128 * k:128 * (k + 1)].astype(F32), 128 - shift, 1)

    m = 0
    nxt = shifted(0)
    for dst, tab, half in ((dq_o, ta, half_a), (dk_o, ta, half_a), (dv_o, None, 0), (sq_o, tb, half_b),
                           (sk_o, tb, half_b), (sv_o, None, 0), (iq_o, ta, half_a)):
        for t in range(dst.shape[1] // 128):
            cur, nxt = nxt, shifted(m + 1)
            y = jnp.where(lane < 128 - shift, cur, nxt)
            if tab is not None:
                y = _rope_tile(y, *tab, half)
            dst[:, 128 * t:128 * (t + 1)] = y.astype(BF16)
            m += 1
    ik = _rope_tile(pltpu.roll(hc_ref[...], 128 - ik_shift, 1), *ta, half_a)
    ik_o[...] = jnp.where(lane < IDX_DIM, ik, pltpu.roll(ik, IDX_DIM, 1)).astype(BF16)


def _split_rope(hm, hs, tab_a, half_a, tab_b, half_b, *, tm=256):
    n, wm = hm.shape
    tile = pl.BlockSpec((tm, 128), lambda i: (i, 0))
    widths = (W_DIFF, W_DIFF, W_DIFF, W_SQ, W_SKV, W_SKV, W_IQ, 128)
    return pl.pallas_call(
        functools.partial(_split_rope_kernel, shift=OFF_DQ - HM_BASE, ik_shift=OFF_IK % 128,
                          half_a=half_a, half_b=half_b),
        out_shape=tuple(jax.ShapeDtypeStruct((n, w), BF16) for w in widths),
        grid=(n // tm,),
        in_specs=[pl.BlockSpec((tm, wm), lambda i: (i, 0)), pl.BlockSpec((tm, 128), lambda i: (i, 1))] + [tile] * 6,
        out_specs=tuple(pl.BlockSpec((tm, w), lambda i: (i, 0)) for w in widths),
        compiler_params=_cparams(("parallel",)),
        name="split_rope",
    )(hm, hs, *tab_a, *tab_b)


def _gla_prep_kernel(q_ref, k_ref, glr_ref, wg_ref, bg_ref, qe_o, kd_o, s_o, eb_o, *, chunks):
    C, SUB, DK = GLA_CHUNK, GLA_SUB, GLA_DK
    NT = (((1,), (1,)), ((), ()))

    row = lax.broadcasted_iota(jnp.int32, (C, C), 0)
    col = lax.broadcasted_iota(jnp.int32, (C, C), 1)
    tril = (col <= row).astype(F32)
    gi = lax.broadcasted_iota(jnp.int32, (SUB, SUB * SUB), 0)
    gr = lax.broadcasted_iota(jnp.int32, (SUB, SUB * SUB), 1)
    gsum = (gr // SUB == gi).astype(BF16)
    pr = lax.broadcasted_iota(jnp.int32, (SUB * SUB, C), 0)
    pc = lax.broadcasted_iota(jnp.int32, (SUB * SUB, C), 1)
    srow = lax.broadcasted_iota(jnp.int32, (SUB, C), 0)
    scol = lax.broadcasted_iota(jnp.int32, (SUB, C), 1)

    def chunk_head(ci, hh):
        r0 = pl.multiple_of(ci * C, C)
        rows = pl.ds(r0, C)
        ks = slice(hh * DK, (hh + 1) * DK)
        q = q_ref[rows, ks].astype(F32) * (DK ** -0.5)
        k = k_ref[rows, ks].astype(F32)
        z = jnp.dot(glr_ref[rows, :], wg_ref[:, ks], precision=HIGHEST, preferred_element_type=F32) + bg_ref[:, ks]
        log_a = -(jnp.maximum(-z, 0.0) + jnp.log1p(jnp.exp(-jnp.abs(z)))) * (1.0 / GLA_TAU)
        b = jnp.dot(tril, log_a, precision=HIGHEST, preferred_element_type=F32)
        qe_o[rows, ks] = (q * jnp.exp(b)).astype(BF16)

        s_rows = []
        for blk in range(C // SUB):
            r0 = blk * SUB
            b_i, q_i, k_i = b[r0:r0 + SUB], q[r0:r0 + SUB], k[r0:r0 + SUB]
            pair = (q_i[:, None, :] * k_i[None, :, :]
                    * jnp.exp(jnp.minimum(b_i[:, None, :] - b_i[None, :, :], 0.0)))
            pair_sum = jnp.sum(pair.reshape(SUB * SUB, DK), axis=-1, keepdims=True)
            placed = jnp.where(pc == r0 + pr % SUB, pair_sum, 0.0).astype(BF16)
            s_blk = jnp.dot(gsum, placed, preferred_element_type=F32)
            s_blk = jnp.where(scol <= srow + r0, s_blk, 0.0)
            if blk > 0:
                beta = b[r0:r0 + 1]
                q_t = (q_i * jnp.exp(b_i - beta)).astype(BF16)
                k_t = (k * jnp.exp(jnp.minimum(beta - b, 0.0))).astype(BF16)
                off = lax.dot_general(q_t, k_t, NT, preferred_element_type=F32)
                s_blk = jnp.where(scol < r0, off, s_blk)
            s_rows.append(s_blk)
        s_o[rows, hh * C:(hh + 1) * C] = jnp.concatenate(s_rows, axis=0).astype(BF16)
        b_last = b[C - 1:C]
        kd_o[rows, ks] = (k * jnp.exp(b_last - b)).astype(BF16)
        eb_o[pl.ds(ci, 1), ks] = jnp.exp(b_last)

    def two_chunks(i, carry):
        for ci in (2 * i, 2 * i + 1):
            for hh in range(2):
                chunk_head(ci, hh)
        return carry

    lax.fori_loop(0, chunks // 2, two_chunks, 0)


def _gla_scan_kernel(qe_ref, kd_ref, s_ref, eb_ref, v_ref, r_ref, ng_ref, mix_ref, o_ref, state_ref):
    del mix_ref
    C, DK, DV = GLA_CHUNK, GLA_DK, GLA_DV
    NT = (((1,), (1,)), ((), ()))
    TN = (((0,), (0,)), ((), ()))

    @pl.when(pl.program_id(0) == 0)
    def _():
        state_ref[...] = jnp.zeros_like(state_ref)

    for b in range(state_ref.shape[0]):
        for h in range(GLA_HEADS):
            ks = slice(h * DK, (h + 1) * DK)
            vs = slice(h * DV, (h + 1) * DV)
            st = state_ref[b, h]
            v = v_ref[b, :, vs]
            o = (lax.dot_general(qe_ref[b, :, ks], st.astype(BF16), NT, preferred_element_type=F32)
                 + jnp.dot(s_ref[b, :, h * C:(h + 1) * C], v, preferred_element_type=F32))
            state_ref[b, h] = st * eb_ref[b:b + 1, ks] + lax.dot_general(v, kd_ref[b, :, ks], TN,
                                                                         preferred_element_type=F32)
            ms = jnp.mean(o * o, axis=-1, keepdims=True)
            o_n = o * lax.rsqrt(ms + LN_EPS) * ng_ref[...]
            r = r_ref[b, :, vs].astype(F32)
            o_ref[b, :, vs] = (o_n * (r * jax.nn.sigmoid(r))).astype(BF16)


def _gla(ha, glr, w_gate_up, b_gate, gla_norm_g, mix, lead, *, batch, seq, chunks=8):
    C = GLA_CHUNK
    n = batch * seq
    nc = seq // C
    hp = GLA_HEADS // 2
    wq = 2 * GLA_DK
    wqk, wv = GLA_HEADS * GLA_DK, GLA_HEADS * GLA_DV
    tr = chunks * C

    def rows(width, first):
        return pl.BlockSpec((tr, width), lambda i, p: (i, first + p))

    qe, kd, sc, eb = pl.pallas_call(
        functools.partial(_gla_prep_kernel, chunks=chunks),
        out_shape=(jax.ShapeDtypeStruct((n, wqk), BF16), jax.ShapeDtypeStruct((n, wqk), BF16),
                   jax.ShapeDtypeStruct((n, GLA_HEADS * C), BF16), jax.ShapeDtypeStruct((n // C, wqk), F32)),
        grid=(n // tr, hp),
        in_specs=[rows(wq, 0), rows(wq, hp),
                  pl.BlockSpec((tr, GLA_RANK), lambda i, p: (i, 0)),
                  pl.BlockSpec((None, GLA_RANK, wq), lambda i, p: (lead, 0, p)),
                  pl.BlockSpec((None, 1, wq), lambda i, p: (lead, 0, p))],
        out_specs=(rows(wq, 0), rows(wq, 0), rows(2 * C, 0),
                   pl.BlockSpec((chunks, wq), lambda i, p: (i, p))),
        compiler_params=_cparams(("parallel", "parallel")),
        name="gla_prep",
    )(ha, ha, glr, w_gate_up, b_gate.reshape(-1, 1, wqk))

    def per_chunk(width, first):
        return pl.BlockSpec((batch, C, width), lambda c: (0, c, first))

    eb = eb.reshape(batch, nc, wqk).transpose(1, 0, 2)
    ha3 = ha.reshape(batch, seq, -1)
    out = pl.pallas_call(
        _gla_scan_kernel,
        out_shape=jax.ShapeDtypeStruct((batch, seq, mix.shape[1]), mix.dtype),
        grid=(nc,),
        in_specs=[per_chunk(wqk, 0), per_chunk(wqk, 0), per_chunk(GLA_HEADS * C, 0),
                  pl.BlockSpec((None, batch, wqk), lambda c: (c, 0, 0)),
                  per_chunk(wv, 1), per_chunk(wv, 2),
                  pl.BlockSpec((None, 1, GLA_DV), lambda c: (lead, 0, 0)),
                  pl.BlockSpec(memory_space=pl.ANY)],
        out_specs=per_chunk(wv, MIX_GLA // wv),
        scratch_shapes=[pltpu.VMEM((batch, GLA_HEADS, GLA_DV, GLA_DK), F32)],
        input_output_aliases={7: 0},
        compiler_params=_cparams(("arbitrary",)),
        name="gla_scan",
    )(qe.reshape(batch, seq, wqk), kd.reshape(batch, seq, wqk), sc.reshape(batch, seq, GLA_HEADS * C), eb,
      ha3, ha3, gla_norm_g.reshape(-1, 1, GLA_DV), mix.reshape(batch, seq, -1))
    return out.reshape(mix.shape)


def _diff_kernel(lq1_ref, lk1_ref, lq2_ref, lk2_ref, q_ref, k_ref, vt_ref, g_ref, mix_ref, o_ref,
                 qs_ref, m_ref, l_ref, acc_ref, *, tq, lambda_init):
    del mix_ref
    qi = pl.program_id(2)
    NT = (((1,), (1,)), ((), ()))
    hw = 2 * DIFF_D
    heads = q_ref.shape[1] // hw

    lane = lax.broadcasted_iota(jnp.int32, (tq, hw), 1)
    for hh in range(heads):
        q = q_ref[:, hw * hh:hw * (hh + 1)].astype(F32) * (DIFF_D ** -0.5)
        qs_ref[hh, :tq] = jnp.where(lane < DIFF_D, q, 0.0).astype(BF16)
        qs_ref[hh, tq:] = jnp.where(lane >= DIFF_D, q, 0.0).astype(BF16)
    m_ref[...] = jnp.full_like(m_ref, -jnp.inf)
    l_ref[...] = jnp.zeros_like(l_ref)
    acc_ref[...] = jnp.zeros_like(acc_ref)

    def tile(j, masked):
        r0 = pl.multiple_of(j * tq, tq)
        for hh in range(heads):
            hs = slice(hw * hh, hw * (hh + 1))
            s = lax.dot_general(k_ref[pl.ds(r0, tq), hs], qs_ref[hh], NT, preferred_element_type=F32)
            if masked:
                r = lax.broadcasted_iota(jnp.int32, s.shape, 0)
                c = lax.broadcasted_iota(jnp.int32, s.shape, 1)
                s = jnp.where(r <= jnp.where(c >= tq, c - tq, c), s, -jnp.inf)
            m_prev = m_ref[hh]
            m_new = jnp.maximum(m_prev, jnp.max(s, axis=0, keepdims=True))
            alpha = jnp.exp(m_prev - m_new)
            p = jnp.exp(s - m_new)
            l_ref[hh] = alpha * l_ref[hh] + jnp.sum(p, axis=0, keepdims=True)
            acc_ref[hh] = alpha * acc_ref[hh] + jnp.dot(vt_ref[hs, pl.ds(r0, tq)], p.astype(BF16),
                                                      preferred_element_type=F32)
            m_ref[hh] = m_new

    def full_tile(j, carry):
        tile(j, False)
        return carry

    lax.fori_loop(0, qi, full_tile, 0)
    tile(qi, True)

    lam = (jnp.exp(jnp.sum(lq1_ref[...] * lk1_ref[...], axis=-1, keepdims=True))
           - jnp.exp(jnp.sum(lq2_ref[...] * lk2_ref[...], axis=-1, keepdims=True)) + lambda_init)
    for hh in range(heads):
        o = acc_ref[hh] / l_ref[hh]
        o = o[:, :tq] - lam * o[:, tq:]
        ms = jnp.mean(o * o, axis=0, keepdims=True)
        o = o * lax.rsqrt(ms + LN_EPS) * g_ref[...] * (1.0 - lambda_init)
        o_ref[:, hw * hh:hw * (hh + 1)] = o.T.astype(BF16)


def _diff_attention(dq, dk, dvt, lq1, lk1, lq2, lk2, norm_g, mix, lead, *, batch, seq, lambda_init, tq, heads=2):
    nq = seq // tq
    hw = 2 * DIFF_D
    pw = heads * hw
    lam_spec = pl.BlockSpec((None, 1, DIFF_D), lambda b, p, qi: (lead, 0, 0))
    lq1, lk1, lq2, lk2 = (t.reshape(-1, 1, DIFF_D) for t in (lq1, lk1, lq2, lk2))
    return pl.pallas_call(
        functools.partial(_diff_kernel, tq=tq, lambda_init=lambda_init),
        out_shape=jax.ShapeDtypeStruct(mix.shape, mix.dtype),
        grid=(batch, DIFF_HEADS // heads, nq),
        in_specs=[lam_spec] * 4 + [
            pl.BlockSpec((tq, pw), lambda b, p, qi: (b * nq + qi, p)),
            pl.BlockSpec((seq, pw), lambda b, p, qi: (b, p)),
            pl.BlockSpec((None, pw, seq), lambda b, p, qi: (b, p, 0)),
            pl.BlockSpec((None, hw, 1), lambda b, p, qi: (lead, 0, 0)),
            pl.BlockSpec(memory_space=pl.ANY)],
        out_specs=pl.BlockSpec((tq, pw), lambda b, p, qi: (b * nq + qi, MIX_DIFF // pw + p)),
        scratch_shapes=[pltpu.VMEM((heads, 2 * tq, hw), BF16), pltpu.VMEM((heads, 1, 2 * tq), F32),
                        pltpu.VMEM((heads, 1, 2 * tq), F32), pltpu.VMEM((heads, hw, 2 * tq), F32)],
        input_output_aliases={8: 0},
        compiler_params=_cparams(("parallel", "parallel", "arbitrary")),
        name="diff_attn",
    )(lq1, lk1, lq2, lk2, dq, dk, dvt, norm_g.reshape(-1, hw, 1), mix)


def _dsa_kernel(iq_ref, sq_ref, ikd_ref, wt_ref, sk_ref, svt_ref, mix_ref, o_ref,
                iqm_ref, keys_ref, bias_ref, q5_ref, m_ref, l_ref, acc_ref, *, tq, tk, seq, k_sel):
    del mix_ref
    qi = pl.program_id(1)
    NT = (((1,), (1,)), ((), ()))
    n_tiles = (qi * tq + tq + tk - 1) // tk
    qpos = qi * tq + lax.broadcasted_iota(jnp.int32, (1, tq), 1)
    krow = lax.broadcasted_iota(jnp.int32, (tk, tq), 0)
    idx_scale = (IDX_DIM ** -0.5) * (IDX_HEADS ** -0.5)
    neg_inf = F32(-jnp.inf)

    def tile_start(j):
        return pl.multiple_of(j * tk, tk)

    lane = lax.broadcasted_iota(jnp.int32, (tq, 128), 1)
    for t in range(IDX_HEADS // 2):
        a = iq_ref[:, 128 * t:128 * (t + 1)]
        zero = jnp.zeros_like(a)
        iqm_ref[t, :tq] = jnp.where(lane < IDX_DIM, a, zero)
        iqm_ref[t, tq:] = jnp.where(lane >= IDX_DIM, a, zero)
    wt = wt_ref[...]

    def score_tile(j, carry):
        r0 = tile_start(j)
        kd = ikd_ref[pl.ds(r0, tk), :]
        acc = jnp.zeros((tk, 2 * tq), F32)
        for t in range(IDX_HEADS // 2):
            lg = lax.dot_general(kd, iqm_ref[t], NT, preferred_element_type=F32)
            w2 = jnp.concatenate([wt[2 * t:2 * t + 1], wt[2 * t + 1:2 * t + 2]], axis=1)
            acc = acc + jnp.maximum(lg, 0.0) * w2
        acc = acc[:, :tq] + acc[:, tq:]
        bits = pltpu.bitcast(acc * idx_scale, jnp.int32)
        key = jnp.where(bits < 0, (bits ^ 0x7FFFFFFF) + 1, bits)
        keys_ref[pl.ds(r0, tk), :] = jnp.where(r0 + krow <= qpos, key, INT_MIN)
        return carry

    lax.fori_loop(0, n_tiles, score_tile, 0)

    def count_ones(ones):
        def body(j, cnt):
            r0 = tile_start(j)
            return cnt + jnp.sum(ones(keys_ref[pl.ds(r0, tk), :], r0 + krow), axis=0, keepdims=True)
        return lax.fori_loop(0, n_tiles, body, jnp.zeros((1, tq), jnp.int32))

    def count(pred):
        return count_ones(lambda kk, kp: jnp.where(pred(kk, kp), 1, 0))

    def thr_bit(i, t_u):
        cand_u = t_u | jnp.left_shift(jnp.int32(1), 31 - i)
        cand = cand_u ^ INT_MIN
        return jnp.where(count(lambda kk, kp: kk >= cand) >= k_sel, cand_u, t_u)

    thr = lax.fori_loop(0, 32, thr_bit, jnp.zeros((1, tq), jnp.int32)) ^ INT_MIN
    need = k_sel - count(lambda kk, kp: kk > thr)
    n_eq = count(lambda kk, kp: kk == thr)
    excess = jnp.where(thr == INT_MIN, 0, jnp.where(n_eq > need, 1, 0))

    def tie_search():
        def idx_bit(i, j_lo):
            cand = j_lo | jnp.left_shift(jnp.int32(1), (seq - 1).bit_length() - 1 - i)
            c = count_ones(lambda kk, kp: jnp.where(kk == thr, jnp.where(kp < cand, 1, 0), 0))
            return jnp.where(c < need, cand, j_lo)
        return lax.fori_loop(0, (seq - 1).bit_length(), idx_bit, jnp.zeros((1, tq), jnp.int32))

    j_star = lax.cond(jnp.max(excess) > 0, tie_search, lambda: jnp.full((1, tq), seq, jnp.int32))
    j_star = jnp.where(thr == INT_MIN, -1, jnp.where(excess > 0, j_star, seq))

    def bias_tile(j, carry):
        r0 = tile_start(j)
        kk = keys_ref[pl.ds(r0, tk), :]
        tie = jnp.where(r0 + krow <= j_star, 0.0, neg_inf)
        bias_ref[pl.ds(r0, tk), :] = jnp.where(kk > thr, 0.0, jnp.where(kk == thr, tie, neg_inf))
        return carry

    lax.fori_loop(0, n_tiles, bias_tile, 0)

    scale = DSA_DH ** -0.5
    rep = DSA_HEADS // DSA_KV
    for g in range(DSA_KV):
        for i in range(rep):
            h = g * rep + i
            q5_ref[g, tq * i:tq * (i + 1)] = sq_ref[:, DSA_DH * h:DSA_DH * (h + 1)]
    m_ref[...] = jnp.full_like(m_ref, neg_inf)
    l_ref[...] = jnp.zeros_like(l_ref)
    acc_ref[...] = jnp.zeros_like(acc_ref)

    def attn_tile(j, carry):
        r0 = tile_start(j)
        bias = bias_ref[pl.ds(r0, tk), :]
        bias = jnp.concatenate([bias] * rep, axis=1)
        for g in range(DSA_KV):
            gs = slice(DSA_DH * g, DSA_DH * (g + 1))
            s = lax.dot_general(sk_ref[pl.ds(r0, tk), gs], q5_ref[g], NT, preferred_element_type=F32)
            s = s * scale + bias
            m_prev = m_ref[g]
            m_new = jnp.maximum(m_prev, jnp.max(s, axis=0, keepdims=True))
            m_safe = jnp.where(m_new == neg_inf, 0.0, m_new)
            alpha = jnp.exp(m_prev - m_safe)
            p = jnp.exp(s - m_safe)
            l_ref[g] = alpha * l_ref[g] + jnp.sum(p, axis=0, keepdims=True)
            acc_ref[g] = alpha * acc_ref[g] + jnp.dot(svt_ref[gs, pl.ds(r0, tk)], p.astype(BF16),
                                                    preferred_element_type=F32)
            m_ref[g] = m_new
        return carry

    lax.fori_loop(0, n_tiles, attn_tile, 0)
    for g in range(DSA_KV):
        out_t = acc_ref[g] / l_ref[g]
        for i in range(rep):
            h = g * rep + i
            o_ref[:, DSA_DH * h:DSA_DH * (h + 1)] = out_t[:, tq * i:tq * (i + 1)].T.astype(BF16)


def _dsa(iq, sq, ikd, wt, sk, svt, mix, *, batch, seq, tq, tk):
    nq = seq // tq
    k_sel = min(IDX_TOPK, seq // 4)
    rep = DSA_HEADS // DSA_KV
    return pl.pallas_call(
        functools.partial(_dsa_kernel, tq=tq, tk=tk, seq=seq, k_sel=k_sel),
        out_shape=jax.ShapeDtypeStruct(mix.shape, mix.dtype),
        grid=(batch, nq),
        in_specs=[pl.BlockSpec((tq, W_IQ), lambda b, qi: (b * nq + qi, 0)),
                  pl.BlockSpec((tq, W_SQ), lambda b, qi: (b * nq + qi, 0)),
                  pl.BlockSpec((seq, 128), lambda b, qi: (b, 0)),
                  pl.BlockSpec((None, IDX_HEADS, tq), lambda b, qi: (b, 0, qi)),
                  pl.BlockSpec((seq, W_SKV), lambda b, qi: (b, 0)),
                  pl.BlockSpec((None, W_SKV, seq), lambda b, qi: (b, 0, 0)),
                  pl.BlockSpec(memory_space=pl.ANY)],
        out_specs=pl.BlockSpec((tq, W_SQ), lambda b, qi: (b * nq + qi, MIX_DSA // W_SQ)),
        scratch_shapes=[pltpu.VMEM((IDX_HEADS // 2, 2 * tq, 128), BF16), pltpu.VMEM((seq, tq), jnp.int32),
                        pltpu.VMEM((seq, tq), F32), pltpu.VMEM((DSA_KV, rep * tq, DSA_DH), BF16),
                        pltpu.VMEM((DSA_KV, 1, rep * tq), F32), pltpu.VMEM((DSA_KV, 1, rep * tq), F32),
                        pltpu.VMEM((DSA_KV, DSA_DH, rep * tq), F32)],
        input_output_aliases={6: 0},
        compiler_params=_cparams(("parallel", "arbitrary")),
        name="dsa",
    )(iq, sq, ikd, wt, sk, svt, mix)


def _router_kernel(x_ref, w_ref, b_ref, e_ref, g_ref, xp_ref):
    x = x_ref[...]
    logits = jnp.dot(x, w_ref[...], precision=HIGHEST, preferred_element_type=F32) + b_ref[...]
    lane = lax.broadcasted_iota(jnp.int32, logits.shape, 1)
    neg_inf = F32(-jnp.inf)
    lg = jnp.where(lane < MOE_E, logits, neg_inf)
    m1 = jnp.max(lg, axis=-1, keepdims=True)
    i1 = jnp.min(jnp.where(lg == m1, lane, 128), axis=-1, keepdims=True)
    lg2 = jnp.where(lane == i1, neg_inf, lg)
    m2 = jnp.max(lg2, axis=-1, keepdims=True)
    i2 = jnp.min(jnp.where(lg2 == m2, lane, 128), axis=-1, keepdims=True)
    e21 = jnp.exp(m2 - m1)
    g1 = 1.0 / (1.0 + e21)
    e_ref[...] = jnp.where(lane == 0, i1, jnp.where(lane == 1, i2, 0))
    g_ref[...] = jnp.where(lane == 0, g1, jnp.where(lane == 1, e21 * g1, 0.0))
    half = x.shape[1] // 2
    lo = pltpu.bitcast(x[:, :half].astype(BF16).astype(F32), jnp.uint32)
    hi = pltpu.bitcast(x[:, half:].astype(BF16).astype(F32), jnp.uint32)
    xp_ref[...] = (hi & jnp.uint32(0xFFFF0000)) | (lo >> 16)


def _router(x, router_w, router_b, *, tm=256):
    n, d = x.shape
    row = lambda w: pl.BlockSpec((tm, w), lambda i: (i, 0))
    return pl.pallas_call(
        _router_kernel,
        out_shape=(jax.ShapeDtypeStruct((n, 128), jnp.int32), jax.ShapeDtypeStruct((n, 128), F32),
                   jax.ShapeDtypeStruct((n, d // 2), jnp.uint32)),
        grid=(n // tm,),
        in_specs=[row(d), pl.BlockSpec((d, 128), lambda i: (0, 0)), pl.BlockSpec((1, 128), lambda i: (0, 0))],
        out_specs=(row(128), row(128), row(d // 2)),
        compiler_params=_cparams(("parallel",)),
        name="router",
    )(x, router_w, router_b)


def _dispatch_kernel(pos_ref, xp_ref, buf_ref, o_ref, sem, *, tb):
    del buf_ref
    base = pl.program_id(0) * tb

    def row_copy(i, s):
        return pltpu.make_async_copy(xp_ref.at[pl.ds(i, 1)], o_ref.at[pl.ds(pos_ref[2 * (base + i) + s], 1)], sem)

    def start(i, c):
        row_copy(i, 0).start()
        row_copy(i, 1).start()
        return c

    def wait(i, c):
        row_copy(i, 0).wait()
        row_copy(i, 1).wait()
        return c

    lax.fori_loop(0, tb, start, 0)
    lax.fori_loop(0, tb, wait, 0)


def _dispatch(pos, xp, buf, *, tb=256):
    n, w = xp.shape
    return pl.pallas_call(
        functools.partial(_dispatch_kernel, tb=tb),
        out_shape=jax.ShapeDtypeStruct(buf.shape, buf.dtype),
        grid_spec=pltpu.PrefetchScalarGridSpec(
            num_scalar_prefetch=1,
            grid=(n // tb,),
            in_specs=[pl.BlockSpec((tb, w), lambda i, pos: (i, 0)), pl.BlockSpec(memory_space=pl.ANY)],
            out_specs=pl.BlockSpec(memory_space=pl.ANY),
            scratch_shapes=[pltpu.SemaphoreType.DMA],
        ),
        input_output_aliases={2: 0},
        compiler_params=_cparams(("arbitrary",)),
        name="moe_dispatch",
    )(pos, xp, buf)


def _unpack_rows(word):
    lo = pltpu.bitcast(word << 16, F32).astype(BF16)
    hi = pltpu.bitcast(word & jnp.uint32(0xFFFF0000), F32).astype(BF16)
    return lo, hi


def _moe_up_kernel(be_ref, bf_ref, na_ref, a_ref, w1_ref, w3_ref, o_ref, w1b_ref, w3b_ref):
    del be_ref
    i = pl.program_id(1)

    @pl.when(i < na_ref[0])
    def _():
        @pl.when(bf_ref[i] == 1)
        def _():
            w1b_ref[...] = w1_ref[...].astype(BF16)
            w3b_ref[...] = w3_ref[...].astype(BF16)

        lo, hi = _unpack_rows(a_ref[...])
        half = lo.shape[1]
        g = (jnp.dot(lo, w1b_ref[:half], preferred_element_type=F32)
             + jnp.dot(hi, w1b_ref[half:], preferred_element_type=F32))
        u = (jnp.dot(lo, w3b_ref[:half], preferred_element_type=F32)
             + jnp.dot(hi, w3b_ref[half:], preferred_element_type=F32))
        o_ref[...] = (g * jax.nn.sigmoid(g) * u).astype(BF16)

    @pl.when(i >= na_ref[0])
    def _():
        o_ref[...] = jnp.zeros_like(o_ref)


def _moe_down_kernel(be_ref, bf_ref, na_ref, h_ref, w2_ref, o_ref, w2b_ref):
    del be_ref
    i = pl.program_id(1)

    @pl.when(i < na_ref[0])
    def _():
        @pl.when(bf_ref[i] == 1)
        def _():
            w2b_ref[...] = w2_ref[...].astype(BF16)

        o_ref[...] = jnp.dot(h_ref[...], w2b_ref[...], preferred_element_type=F32)

    @pl.when(i >= na_ref[0])
    def _():
        o_ref[...] = jnp.zeros_like(o_ref)


def _moe_ffn(block_e, block_first, n_active, a_sorted, w1, w3, w2, lead, *, tm, tn):
    r, half = a_sorted.shape
    d = 2 * half
    f = w1.shape[-1]
    nb = r // tm

    def rows(i, na):
        return jnp.minimum(i, na[0] - 1)

    h = pl.pallas_call(
        _moe_up_kernel,
        out_shape=jax.ShapeDtypeStruct((r, f), BF16),
        grid_spec=pltpu.PrefetchScalarGridSpec(
            num_scalar_prefetch=3,
            grid=(f // tn, nb),
            in_specs=[pl.BlockSpec((tm, half), lambda n, i, be, bf, na: (rows(i, na), 0)),
                      pl.BlockSpec((None, None, d, tn), lambda n, i, be, bf, na: (lead, be[i], 0, n)),
                      pl.BlockSpec((None, None, d, tn), lambda n, i, be, bf, na: (lead, be[i], 0, n))],
            out_specs=pl.BlockSpec((tm, tn), lambda n, i, be, bf, na: (i, n)),
            scratch_shapes=[pltpu.VMEM((d, tn), BF16), pltpu.VMEM((d, tn), BF16)],
        ),
        compiler_params=_cparams(("arbitrary", "arbitrary")),
        name="moe_up",
    )(block_e, block_first, n_active, a_sorted, w1, w3)
    return pl.pallas_call(
        _moe_down_kernel,
        out_shape=jax.ShapeDtypeStruct((r, d), F32),
        grid_spec=pltpu.PrefetchScalarGridSpec(
            num_scalar_prefetch=3,
            grid=(d // tn, nb),
            in_specs=[pl.BlockSpec((tm, f), lambda n, i, be, bf, na: (rows(i, na), 0)),
                      pl.BlockSpec((None, None, f, tn), lambda n, i, be, bf, na: (lead, be[i], 0, n))],
            out_specs=pl.BlockSpec((tm, tn), lambda n, i, be, bf, na: (i, n)),
            scratch_shapes=[pltpu.VMEM((f, tn), BF16)],
        ),
        compiler_params=_cparams(("arbitrary", "arbitrary")),
        name="moe_down",
    )(block_e, block_first, n_active, h, w2)


def _combine_ln_kernel(pos_ref, x_ref, gate_ref, g_ref, b_ref, y_ref, o_ref, ybuf_ref, sem, *, tb, alpha):
    base = pl.program_id(0) * tb

    def row_copy(i, s):
        return pltpu.make_async_copy(y_ref.at[pl.ds(pos_ref[2 * (base + i) + s], 1)],
                                     ybuf_ref.at[s, pl.ds(i, 1)], sem)

    def start(i, c):
        row_copy(i, 0).start()
        row_copy(i, 1).start()
        return c

    def wait(i, c):
        row_copy(i, 0).wait()
        row_copy(i, 1).wait()
        return c

    lax.fori_loop(0, tb, start, 0)
    lax.fori_loop(0, tb, wait, 0)
    gate = gate_ref[...]
    f = ybuf_ref[0] * gate[:, 0:1] + ybuf_ref[1] * gate[:, 1:2]
    z = alpha * x_ref[...] + f
    mu = jnp.mean(z, axis=-1, keepdims=True)
    zc = z - mu
    var = jnp.mean(zc * zc, axis=-1, keepdims=True)
    o_ref[...] = zc * lax.rsqrt(var + LN_EPS) * g_ref[...] + b_ref[...]


def _combine_ln(pos, x, gates, g, b, y, lead, *, alpha, tb=128):
    n, d = x.shape
    par = pl.BlockSpec((None, 1, d), lambda i, pos: (lead, 0, 0))
    return pl.pallas_call(
        functools.partial(_combine_ln_kernel, tb=tb, alpha=alpha),
        out_shape=jax.ShapeDtypeStruct((n, d), F32),
        grid_spec=pltpu.PrefetchScalarGridSpec(
            num_scalar_prefetch=1,
            grid=(n // tb,),
            in_specs=[pl.BlockSpec((tb, d), lambda i, pos: (i, 0)),
                      pl.BlockSpec((tb, 128), lambda i, pos: (i, 0)),
                      par, par, pl.BlockSpec(memory_space=pl.ANY)],
            out_specs=pl.BlockSpec((tb, d), lambda i, pos: (i, 0)),
            scratch_shapes=[pltpu.VMEM((2, tb, d), F32), pltpu.SemaphoreType.DMA],
        ),
        compiler_params=_cparams(("arbitrary",)),
        name="moe_combine_ln",
    )(pos, x, gates, g.reshape(-1, 1, d), b.reshape(-1, 1, d), y)


def _moe_layer(x, router_w, router_b, w1, w3, w2, ln_g, ln_b, lead_moe, lead_ln, *, alpha, tm=512, tn=512):
    n, d = x.shape
    rw = jnp.pad(router_w[lead_moe], ((0, 0), (0, 128 - MOE_E)))
    rb = jnp.pad(router_b[lead_moe], (0, 128 - MOE_E)).reshape(1, 128)
    top_e, gates, xp = _router(x, rw, rb)
    flat_e = top_e[:, :MOE_TOPK].reshape(-1)
    onehot = (flat_e[:, None] == jnp.arange(MOE_E, dtype=jnp.int32)[None, :]).astype(jnp.int32)
    rank = jnp.sum((jnp.cumsum(onehot, axis=0) - onehot) * onehot, axis=1)
    counts = jnp.sum(onehot, axis=0)
    padded = (counts + tm - 1) // tm * tm
    end_padded = jnp.cumsum(padded)
    start_padded = end_padded - padded
    pos = (start_padded[flat_e] + rank).astype(jnp.int32)
    nb = -(-(n * MOE_TOPK + MOE_E * (tm - 1)) // tm)
    block_start = jnp.arange(nb, dtype=jnp.int32) * tm
    block_e = jnp.sum((block_start[:, None] >= end_padded[None, :]).astype(jnp.int32), axis=1)
    block_e = jnp.minimum(block_e, MOE_E - 1)
    n_active = (end_padded[-1] // tm).astype(jnp.int32).reshape(1)
    block_first = jnp.concatenate([jnp.ones((1,), jnp.int32), (block_e[1:] != block_e[:-1]).astype(jnp.int32)])
    a_sorted = _dispatch(pos, xp, jnp.zeros((nb * tm, d // 2), jnp.uint32))
    y = _moe_ffn(block_e, block_first, n_active, a_sorted, w1, w3, w2, lead_moe, tm=tm, tn=tn)
    return _combine_ln(pos, x, gates, ln_g, ln_b, y, lead_ln, alpha=alpha)


def kernel(x, positions, w_in, w_gate_up, b_gate, gla_norm_g, lambda_q1, lambda_k1, lambda_q2, lambda_k2,
           diff_norm_g, w_out, ln1_g, ln1_b, ln2_g, ln2_b, ffn_w1, ffn_w3, ffn_w2, router_w, router_b,
           moe_w1, moe_w3, moe_w2):
    batch, seq, d = x.shape
    n = batch * seq
    depth = w_in.shape[0]
    alpha = (2 * depth) ** 0.25
    xf = x.reshape(n, d)
    xb = xf.astype(BF16)
    pos = positions.reshape(n)
    tab_a, half_a = _rope_tables(pos, DIFF_D)
    tab_b, half_b = _rope_tables(pos, DSA_DH)
    plain = lambda i, j, k: (i, k)
    w_in_t = jnp.swapaxes(w_in, 1, 2)
    for l in range(depth):
        ha = _matmul(xb, w_in_t, n_out=W_GLA, k_steps=d // 1024, tm=1024, tn=1536, tk=1024, out_dtype=BF16,
                     a_map=plain, w_map=lambda i, j, k, l=l: (l, j, k), w_lead=True, w_t=True,
                     name="in_proj_gla")
        hm = _matmul(xb, w_in_t, n_out=HM_TILES * HM_TN, k_steps=d // 1024, tm=1024, tn=HM_TN, tk=1024,
                     out_dtype=BF16, a_map=plain,
                     w_map=lambda i, j, k, l=l: (l, HM_BASE // HM_TN + j, k), w_lead=True, w_t=True,
                     n_valid=N_IN, col_start=lambda j: HM_BASE + j * HM_TN, name="in_proj_attn")
        small_tiles = (OFF_GLR // 128, OFF_IK // 128)
        pick = lambda j: jnp.where(j == 0, small_tiles[0], small_tiles[1])
        hs = _matmul(xb, w_in_t, n_out=256, k_steps=d // 1024, tm=1024, tn=128, tk=1024, out_dtype=F32,
                     a_map=plain, w_map=lambda i, j, k, l=l: (l, pick(j), k), w_lead=True, w_t=True,
                     n_valid=N_IN, col_start=lambda j: pick(j) * 128, name="in_proj_small")
        glr = hs[:, OFF_GLR % 128:OFF_GLR % 128 + GLA_RANK]
        iw0 = 128 + OFF_IW % 128
        wt = hs[:, iw0:iw0 + IDX_HEADS].reshape(batch, seq, IDX_HEADS).transpose(0, 2, 1)
        dq, dk, dv, sq, sk, sv, iq, ikd = _split_rope(hm, hs, tab_a, half_a, tab_b, half_b)
        svt = sv.reshape(batch, seq, W_SKV).transpose(0, 2, 1)
        dvt = dv.reshape(batch, seq, W_DIFF).transpose(0, 2, 1)

        mix = jnp.zeros((n, W_MIX), BF16)
        mix = _gla(ha, glr, w_gate_up, b_gate, gla_norm_g, mix, l, batch=batch, seq=seq)
        lambda_init = 0.8 - 0.6 * math.exp(-0.3 * l)
        mix = _diff_attention(dq, dk, dvt, lambda_q1, lambda_k1, lambda_q2, lambda_k2, diff_norm_g, mix, l,
                              batch=batch, seq=seq, lambda_init=lambda_init, tq=512)
        mix = _dsa(iq, sq, ikd, wt, sk, svt, mix, batch=batch, seq=seq, tq=128, tk=512)

        n_gla = GLA_HEADS * GLA_DV // MIX_BLK
        n_skip = MIX_DIFF // MIX_BLK - n_gla
        proj = _matmul(
            mix, w_out, n_out=d, k_steps=d // MIX_BLK, tm=1024, tn=1024, tk=MIX_BLK, out_dtype=F32,
            a_map=lambda i, j, k: (i, jnp.where(k < n_gla, k, k + n_skip)),
            w_map=lambda i, j, k, l=l: (l, k, j), w_lead=True, name="out_proj")
        xf, xb = _add_ln(xf, proj, ln1_g, ln1_b, l, alpha=alpha)

        j = l // 2
        if l % 2 == 0:
            f_dim = ffn_w2.shape[1]
            f_pad = -(-f_dim // 1024) * 1024
            hid = _swiglu_up(xb, ffn_w1, ffn_w3, j, tm=2048, tn=1024, tk=512, f_out=f_pad)
            down = _matmul(hid, ffn_w2, n_out=d, k_steps=f_pad // 512, tm=2048, tn=1024, tk=512, out_dtype=F32,
                           a_map=plain, w_map=lambda i, j_, k, j=j: (j, k, j_), w_lead=True, k_valid=f_dim,
                           name="ffn_down")
            xf, xb = _add_ln(xf, down, ln2_g, ln2_b, l, alpha=alpha)
        else:
            xf = _moe_layer(xf, router_w, router_b, moe_w1, moe_w3, moe_w2, ln2_g, ln2_b, j, l, alpha=alpha)
            xb = xf.astype(BF16)
    return xf.reshape(batch, seq, d)
```

```python
import functools
import math

import jax
import jax.numpy as jnp
from jax import lax
from jax.experimental import pallas as pl
from jax.experimental.pallas import tpu as pltpu

F32 = jnp.float32
BF16 = jnp.bfloat16
HIGHEST = lax.Precision.HIGHEST
INT_MIN = -(2 ** 31)

ROPE_THETA = 500000.0
GLA_HEADS, GLA_DK, GLA_DV, GLA_RANK, GLA_TAU, GLA_CHUNK = 4, 192, 384, 16, 16.0, 64
GLA_SUB = 16
DIFF_HEADS, DIFF_D = 10, 64
DSA_HEADS, DSA_KV, DSA_DH = 10, 2, 128
IDX_HEADS, IDX_DIM, IDX_TOPK = 32, 64, 256
MOE_E, MOE_TOPK = 8, 2
LN_EPS = 1e-5

W_GLA = 2 * GLA_HEADS * GLA_DK + 2 * GLA_HEADS * GLA_DV
OFF_GLR = W_GLA
OFF_DQ = OFF_GLR + GLA_RANK
W_DIFF = DIFF_HEADS * 2 * DIFF_D
OFF_DK, OFF_DV = OFF_DQ + W_DIFF, OFF_DQ + 2 * W_DIFF
OFF_SQ = OFF_DQ + 3 * W_DIFF
W_SQ, W_SKV = DSA_HEADS * DSA_DH, DSA_KV * DSA_DH
OFF_SK, OFF_SV = OFF_SQ + W_SQ, OFF_SQ + W_SQ + W_SKV
OFF_IQ = OFF_SV + W_SKV
W_IQ = IDX_HEADS * IDX_DIM
OFF_IK = OFF_IQ + W_IQ
OFF_IW = OFF_IK + IDX_DIM
N_IN = OFF_IW + IDX_HEADS

HM_BASE = W_GLA
HM_TN = 512
HM_TILES = -(-(N_IN - HM_BASE) // HM_TN)

MIX_GLA, MIX_DIFF, MIX_DSA = 0, 2560, 3840
W_MIX = MIX_DSA + DSA_HEADS * DSA_DH
MIX_BLK = 512

VMEM_LIMIT = 56 * 1024 * 1024


def _cparams(sem, vmem=VMEM_LIMIT):
    return pltpu.CompilerParams(dimension_semantics=sem, vmem_limit_bytes=vmem)


def _mm_kernel(a_ref, w_ref, o_ref, acc_ref=None, *, k_valid, n_valid, col_start, w_t):
    k = pl.program_id(2)
    k_ax, n_ax = (1, 0) if w_t else (0, 1)

    def partial_product():
        w = w_ref[...]
        if k_valid is not None:
            kk = k * w.shape[k_ax] + lax.broadcasted_iota(jnp.int32, w.shape, k_ax)
            w = jnp.where(kk < k_valid, w, 0.0)
        if n_valid is not None:
            col = col_start(pl.program_id(1)) + lax.broadcasted_iota(jnp.int32, w.shape, n_ax)
            w = jnp.where(col < n_valid, w, 0.0)
        return lax.dot_general(a_ref[...], w.astype(BF16), (((1,), (k_ax,)), ((), ())),
                               preferred_element_type=F32)

    if acc_ref is None:
        o_ref[...] = partial_product().astype(o_ref.dtype)
        return

    @pl.when(k == 0)
    def _():
        acc_ref[...] = partial_product()

    @pl.when(k > 0)
    def _():
        acc_ref[...] += partial_product()

    @pl.when(k == pl.num_programs(2) - 1)
    def _():
        o_ref[...] = acc_ref[...].astype(o_ref.dtype)


def _matmul(a, w, *, n_out, k_steps, tm, tn, tk, out_dtype, a_map, w_map, w_lead=False, k_valid=None,
            n_valid=None, col_start=None, w_t=False, name="mm"):
    m = a.shape[0]
    w_block = (tn, tk) if w_t else (tk, tn)
    if w_lead:
        w_block = (None,) + w_block
    return pl.pallas_call(
        functools.partial(_mm_kernel, k_valid=k_valid, n_valid=n_valid, col_start=col_start, w_t=w_t),
        out_shape=jax.ShapeDtypeStruct((m, n_out), out_dtype),
        grid=(m // tm, pl.cdiv(n_out, tn), k_steps),
        in_specs=[pl.BlockSpec((tm, tk), a_map), pl.BlockSpec(w_block, w_map)],
        out_specs=pl.BlockSpec((tm, tn), lambda i, j, k: (i, j)),
        scratch_shapes=[pltpu.VMEM((tm, tn), F32)] if k_steps > 1 else [],
        compiler_params=_cparams(("parallel", "parallel", "arbitrary")),
        name=name,
    )(a, w)


def _swiglu_kernel(a_ref, w1_ref, w3_ref, o_ref, acc1_ref=None, acc3_ref=None, *, n_valid):
    k = pl.program_id(2)

    def partial_product(w_ref):
        return jnp.dot(a_ref[...], w_ref[...].astype(BF16), preferred_element_type=F32)

    def finish(g, u):
        h = g * jax.nn.sigmoid(g) * u
        col = pl.program_id(1) * h.shape[1] + lax.broadcasted_iota(jnp.int32, h.shape, 1)
        o_ref[...] = jnp.where(col < n_valid, h, 0.0).astype(o_ref.dtype)

    if acc1_ref is None:
        finish(partial_product(w1_ref), partial_product(w3_ref))
        return

    @pl.when(k == 0)
    def _():
        acc1_ref[...] = partial_product(w1_ref)
        acc3_ref[...] = partial_product(w3_ref)

    @pl.when(k > 0)
    def _():
        acc1_ref[...] += partial_product(w1_ref)
        acc3_ref[...] += partial_product(w3_ref)

    @pl.when(k == pl.num_programs(2) - 1)
    def _():
        finish(acc1_ref[...], acc3_ref[...])


def _swiglu_up(a, w1, w3, lead, *, tm, tn, tk, f_out):
    m, kdim = a.shape
    f = w1.shape[-1]
    last = (f - 1) // tn
    wspec = pl.BlockSpec((None, tk, tn), lambda i, j, k: (lead, k, jnp.minimum(j, last)))
    k_steps = kdim // tk
    return pl.pallas_call(
        functools.partial(_swiglu_kernel, n_valid=f),
        out_shape=jax.ShapeDtypeStruct((m, f_out), BF16),
        grid=(m // tm, f_out // tn, k_steps),
        in_specs=[pl.BlockSpec((tm, tk), lambda i, j, k: (i, k)), wspec, wspec],
        out_specs=pl.BlockSpec((tm, tn), lambda i, j, k: (i, j)),
        scratch_shapes=[pltpu.VMEM((tm, tn), F32), pltpu.VMEM((tm, tn), F32)] if k_steps > 1 else [],
        compiler_params=_cparams(("parallel", "parallel", "arbitrary")),
        name="swiglu_up",
    )(a, w1, w3)


def _add_ln_kernel(x_ref, y_ref, g_ref, b_ref, o_ref, ob_ref, *, alpha):
    z = alpha * x_ref[...] + y_ref[...].astype(F32)
    mu = jnp.mean(z, axis=-1, keepdims=True)
    zc = z - mu
    var = jnp.mean(zc * zc, axis=-1, keepdims=True)
    out = zc * lax.rsqrt(var + LN_EPS) * g_ref[...] + b_ref[...]
    o_ref[...] = out
    ob_ref[...] = out.astype(BF16)


def _add_ln(x, y, g, b, lead, *, alpha, tm=256):
    m, d = x.shape
    row = pl.BlockSpec((tm, d), lambda i: (i, 0))
    par = pl.BlockSpec((None, 1, d), lambda i: (lead, 0, 0))
    g, b = g.reshape(-1, 1, d), b.reshape(-1, 1, d)
    return pl.pallas_call(
        functools.partial(_add_ln_kernel, alpha=alpha),
        out_shape=(jax.ShapeDtypeStruct((m, d), F32), jax.ShapeDtypeStruct((m, d), BF16)),
        grid=(m // tm,),
        in_specs=[row, row, par, par],
        out_specs=(row, row),
        compiler_params=_cparams(("parallel",)),
        name="add_ln",
    )(x, y, g, b)


def _rope_tables(positions, width):
    rot = width // 4
    half = rot // 2
    inv_freq = 1.0 / (ROPE_THETA ** (jnp.arange(half, dtype=F32) * 2.0 / rot))
    ang = positions.astype(F32)[:, None] * inv_freq
    cos, sin = jnp.cos(ang), jnp.sin(ang)
    n = positions.shape[0]
    one = jnp.ones((n, width - rot), F32)
    zero_h = jnp.zeros((n, half), F32)
    zero_r = jnp.zeros((n, width - rot), F32)
    c = jnp.concatenate([cos, cos, one], axis=1)
    s_up = jnp.concatenate([-sin, zero_h, zero_r], axis=1)
    s_dn = jnp.concatenate([zero_h, sin, zero_r], axis=1)
    rep = 128 // width
    return tuple(jnp.tile(t, (1, rep)) for t in (c, s_up, s_dn)), half


def _rope_tile(x, c, s_up, s_dn, half):
    return x * c + pltpu.roll(x, 128 - half, 1) * s_up + pltpu.roll(x, half, 1) * s_dn


def _split_rope_kernel(hm_ref, hc_ref, ca_ref, ua_ref, da_ref, cb_ref, ub_ref, db_ref,
                       dq_o, dk_o, dv_o, sq_o, sk_o, sv_o, iq_o, ik_o, *, shift, ik_shift, half_a, half_b):
    ta = (ca_ref[...], ua_ref[...], da_ref[...])
    tb = (cb_ref[...], ub_ref[...], db_ref[...])
    lane = lax.broadcasted_iota(jnp.int32, ca_ref.shape, 1)

    def shifted(k):
        return pltpu.roll(hm_ref[:, 128 * k:128 * (k + 1)].astype(F32), 128 - shift, 1)

    m = 0
    nxt = shifted(0)
    for dst, tab, half in ((dq_o, ta, half_a), (dk_o, ta, half_a), (dv_o, None, 0), (sq_o, tb, half_b),
                           (sk_o, tb, half_b), (sv_o, None, 0), (iq_o, ta, half_a)):
        for t in range(dst.shape[1] // 128):
            cur, nxt = nxt, shifted(m + 1)
            y = jnp.where(lane < 128 - shift, cur, nxt)
            if tab is not None:
                y = _rope_tile(y, *tab, half)
            dst[:, 128 * t:128 * (t + 1)] = y.astype(BF16)
            m += 1
    ik = _rope_tile(pltpu.roll(hc_ref[...], 128 - ik_shift, 1), *ta, half_a)
    ik_o[...] = jnp.where(lane < IDX_DIM, ik, pltpu.roll(ik, IDX_DIM, 1)).astype(BF16)


def _split_rope(hm, hs, tab_a, half_a, tab_b, half_b, *, tm=256):
    n, wm = hm.shape
    tile = pl.BlockSpec((tm, 128), lambda i: (i, 0))
    widths = (W_DIFF, W_DIFF, W_DIFF, W_SQ, W_SKV, W_SKV, W_IQ, 128)
    return pl.pallas_call(
        functools.partial(_split_rope_kernel, shift=OFF_DQ - HM_BASE, ik_shift=OFF_IK % 128,
                          half_a=half_a, half_b=half_b),
        out_shape=tuple(jax.ShapeDtypeStruct((n, w), BF16) for w in widths),
        grid=(n // tm,),
        in_specs=[pl.BlockSpec((tm, wm), lambda i: (i, 0)), pl.BlockSpec((tm, 128), lambda i: (i, 1))] + [tile] * 6,
        out_specs=tuple(pl.BlockSpec((tm, w), lambda i: (i, 0)) for w in widths),
        compiler_params=_cparams(("parallel",)),
        name="split_rope",
    )(hm, hs, *tab_a, *tab_b)


def _gla_prep_kernel(q_ref, k_ref, glr_ref, wg_ref, bg_ref, qe_o, kd_o, s_o, eb_o, *, chunks):
    C, SUB, DK = GLA_CHUNK, GLA_SUB, GLA_DK
    NT = (((1,), (1,)), ((), ()))

    row = lax.broadcasted_iota(jnp.int32, (C, C), 0)
    col = lax.broadcasted_iota(jnp.int32, (C, C), 1)
    tril = (col <= row).astype(F32)
    gi = lax.broadcasted_iota(jnp.int32, (SUB, SUB * SUB), 0)
    gr = lax.broadcasted_iota(jnp.int32, (SUB, SUB * SUB), 1)
    gsum = (gr // SUB == gi).astype(BF16)
    pr = lax.broadcasted_iota(jnp.int32, (SUB * SUB, C), 0)
    pc = lax.broadcasted_iota(jnp.int32, (SUB * SUB, C), 1)
    srow = lax.broadcasted_iota(jnp.int32, (SUB, C), 0)
    scol = lax.broadcasted_iota(jnp.int32, (SUB, C), 1)

    def chunk_head(ci, hh):
        r0 = pl.multiple_of(ci * C, C)
        rows = pl.ds(r0, C)
        ks = slice(hh * DK, (hh + 1) * DK)
        q = q_ref[rows, ks].astype(F32) * (DK ** -0.5)
        k = k_ref[rows, ks].astype(F32)
        z = jnp.dot(glr_ref[rows, :], wg_ref[:, ks], precision=HIGHEST, preferred_element_type=F32) + bg_ref[:, ks]
        log_a = -(jnp.maximum(-z, 0.0) + jnp.log1p(jnp.exp(-jnp.abs(z)))) * (1.0 / GLA_TAU)
        b = jnp.dot(tril, log_a, precision=HIGHEST, preferred_element_type=F32)
        qe_o[rows, ks] = (q * jnp.exp(b)).astype(BF16)

        s_rows = []
        for blk in range(C // SUB):
            r0 = blk * SUB
            b_i, q_i, k_i = b[r0:r0 + SUB], q[r0:r0 + SUB], k[r0:r0 + SUB]
            pair = (q_i[:, None, :] * k_i[None, :, :]
                    * jnp.exp(jnp.minimum(b_i[:, None, :] - b_i[None, :, :], 0.0)))
            pair_sum = jnp.sum(pair.reshape(SUB * SUB, DK), axis=-1, keepdims=True)
            placed = jnp.where(pc == r0 + pr % SUB, pair_sum, 0.0).astype(BF16)
            s_blk = jnp.dot(gsum, placed, preferred_element_type=F32)
            s_blk = jnp.where(scol <= srow + r0, s_blk, 0.0)
            if blk > 0:
                beta = b[r0:r0 + 1]
                q_t = (q_i * jnp.exp(b_i - beta)).astype(BF16)
                k_t = (k * jnp.exp(jnp.minimum(beta - b, 0.0))).astype(BF16)
                off = lax.dot_general(q_t, k_t, NT, preferred_element_type=F32)
                s_blk = jnp.where(scol < r0, off, s_blk)
            s_rows.append(s_blk)
        s_o[rows, hh * C:(hh + 1) * C] = jnp.concatenate(s_rows, axis=0).astype(BF16)
        b_last = b[C - 1:C]
        kd_o[rows, ks] = (k * jnp.exp(b_last - b)).astype(BF16)
        eb_o[pl.ds(ci, 1), ks] = jnp.exp(b_last)

    def two_chunks(i, carry):
        for ci in (2 * i, 2 * i + 1):
            for hh in range(2):
                chunk_head(ci, hh)
        return carry

    lax.fori_loop(0, chunks // 2, two_chunks, 0)


def _gla_scan_kernel(qe_ref, kd_ref, s_ref, eb_ref, v_ref, r_ref, ng_ref, mix_ref, o_ref, state_ref):
    del mix_ref
    C, DK, DV = GLA_CHUNK, GLA_DK, GLA_DV
    NT = (((1,), (1,)), ((), ()))
    TN = (((0,), (0,)), ((), ()))

    @pl.when(pl.program_id(0) == 0)
    def _():
        state_ref[...] = jnp.zeros_like(state_ref)

    for b in range(state_ref.shape[0]):
        for h in range(GLA_HEADS):
            ks = slice(h * DK, (h + 1) * DK)
            vs = slice(h * DV, (h + 1) * DV)
            st = state_ref[b, h]
            v = v_ref[b, :, vs]
            o = (lax.dot_general(qe_ref[b, :, ks], st.astype(BF16), NT, preferred_element_type=F32)
                 + jnp.dot(s_ref[b, :, h * C:(h + 1) * C], v, preferred_element_type=F32))
            state_ref[b, h] = st * eb_ref[b:b + 1, ks] + lax.dot_general(v, kd_ref[b, :, ks], TN,
                                                                         preferred_element_type=F32)
            ms = jnp.mean(o * o, axis=-1, keepdims=True)
            o_n = o * lax.rsqrt(ms + LN_EPS) * ng_ref[...]
            r = r_ref[b, :, vs].astype(F32)
            o_ref[b, :, vs] = (o_n * (r * jax.nn.sigmoid(r))).astype(BF16)


def _gla(ha, glr, w_gate_up, b_gate, gla_norm_g, mix, lead, *, batch, seq, chunks=8):
    C = GLA_CHUNK
    n = batch * seq
    nc = seq // C
    hp = GLA_HEADS // 2
    wq = 2 * GLA_DK
    wqk, wv = GLA_HEADS * GLA_DK, GLA_HEADS * GLA_DV
    tr = chunks * C

    def rows(width, first):
        return pl.BlockSpec((tr, width), lambda i, p: (i, first + p))

    qe, kd, sc, eb = pl.pallas_call(
        functools.partial(_gla_prep_kernel, chunks=chunks),
        out_shape=(jax.ShapeDtypeStruct((n, wqk), BF16), jax.ShapeDtypeStruct((n, wqk), BF16),
                   jax.ShapeDtypeStruct((n, GLA_HEADS * C), BF16), jax.ShapeDtypeStruct((n // C, wqk), F32)),
        grid=(n // tr, hp),
        in_specs=[rows(wq, 0), rows(wq, hp),
                  pl.BlockSpec((tr, GLA_RANK), lambda i, p: (i, 0)),
                  pl.BlockSpec((None, GLA_RANK, wq), lambda i, p: (lead, 0, p)),
                  pl.BlockSpec((None, 1, wq), lambda i, p: (lead, 0, p))],
        out_specs=(rows(wq, 0), rows(wq, 0), rows(2 * C, 0),
                   pl.BlockSpec((chunks, wq), lambda i, p: (i, p))),
        compiler_params=_cparams(("parallel", "parallel")),
        name="gla_prep",
    )(ha, ha, glr, w_gate_up, b_gate.reshape(-1, 1, wqk))

    def per_chunk(width, first):
        return pl.BlockSpec((batch, C, width), lambda c: (0, c, first))

    eb = eb.reshape(batch, nc, wqk).transpose(1, 0, 2)
    ha3 = ha.reshape(batch, seq, -1)
    out = pl.pallas_call(
        _gla_scan_kernel,
        out_shape=jax.ShapeDtypeStruct((batch, seq, mix.shape[1]), mix.dtype),
        grid=(nc,),
        in_specs=[per_chunk(wqk, 0), per_chunk(wqk, 0), per_chunk(GLA_HEADS * C, 0),
                  pl.BlockSpec((None, batch, wqk), lambda c: (c, 0, 0)),
                  per_chunk(wv, 1), per_chunk(wv, 2),
                  pl.BlockSpec((None, 1, GLA_DV), lambda c: (lead, 0, 0)),
                  pl.BlockSpec(memory_space=pl.ANY)],
        out_specs=per_chunk(wv, MIX_GLA // wv),
        scratch_shapes=[pltpu.VMEM((batch, GLA_HEADS, GLA_DV, GLA_DK), F32)],
        input_output_aliases={7: 0},
        compiler_params=_cparams(("arbitrary",)),
        name="gla_scan",
    )(qe.reshape(batch, seq, wqk), kd.reshape(batch, seq, wqk), sc.reshape(batch, seq, GLA_HEADS * C), eb,
      ha3, ha3, gla_norm_g.reshape(-1, 1, GLA_DV), mix.reshape(batch, seq, -1))
    return out.reshape(mix.shape)


def _diff_kernel(lq1_ref, lk1_ref, lq2_ref, lk2_ref, q_ref, k_ref, vt_ref, g_ref, mix_ref, o_ref,
                 qs_ref, m_ref, l_ref, acc_ref, *, tq, lambda_init):
    del mix_ref
    qi = pl.program_id(2)
    NT = (((1,), (1,)), ((), ()))
    hw = 2 * DIFF_D
    heads = q_ref.shape[1] // hw

    lane = lax.broadcasted_iota(jnp.int32, (tq, hw), 1)
    for hh in range(heads):
        q = q_ref[:, hw * hh:hw * (hh + 1)].astype(F32) * (DIFF_D ** -0.5)
        qs_ref[hh, :tq] = jnp.where(lane < DIFF_D, q, 0.0).astype(BF16)
        qs_ref[hh, tq:] = jnp.where(lane >= DIFF_D, q, 0.0).astype(BF16)
    m_ref[...] = jnp.full_like(m_ref, -jnp.inf)
    l_ref[...] = jnp.zeros_like(l_ref)
    acc_ref[...] = jnp.zeros_like(acc_ref)

    def tile(j, masked):
        r0 = pl.multiple_of(j * tq, tq)
        for hh in range(heads):
            hs = slice(hw * hh, hw * (hh + 1))
            s = lax.dot_general(k_ref[pl.ds(r0, tq), hs], qs_ref[hh], NT, preferred_element_type=F32)
            if masked:
                r = lax.broadcasted_iota(jnp.int32, s.shape, 0)
                c = lax.broadcasted_iota(jnp.int32, s.shape, 1)
                s = jnp.where(r <= jnp.where(c >= tq, c - tq, c), s, -jnp.inf)
            m_prev = m_ref[hh]
            m_new = jnp.maximum(m_prev, jnp.max(s, axis=0, keepdims=True))
            alpha = jnp.exp(m_prev - m_new)
            p = jnp.exp(s - m_new)
            l_ref[hh] = alpha * l_ref[hh] + jnp.sum(p, axis=0, keepdims=True)
            acc_ref[hh] = alpha * acc_ref[hh] + jnp.dot(vt_ref[hs, pl.ds(r0, tq)], p.astype(BF16),
                                                      preferred_element_type=F32)
            m_ref[hh] = m_new

    def full_tile(j, carry):
        tile(j, False)
        return carry

    lax.fori_loop(0, qi, full_tile, 0)
    tile(qi, True)

    lam = (jnp.exp(jnp.sum(lq1_ref[...] * lk1_ref[...], axis=-1, keepdims=True))
           - jnp.exp(jnp.sum(lq2_ref[...] * lk2_ref[...], axis=-1, keepdims=True)) + lambda_init)
    for hh in range(heads):
        o = acc_ref[hh] / l_ref[hh]
        o = o[:, :tq] - lam * o[:, tq:]
        ms = jnp.mean(o * o, axis=0, keepdims=True)
        o = o * lax.rsqrt(ms + LN_EPS) * g_ref[...] * (1.0 - lambda_init)
        o_ref[:, hw * hh:hw * (hh + 1)] = o.T.astype(BF16)


def _diff_attention(dq, dk, dvt, lq1, lk1, lq2, lk2, norm_g, mix, lead, *, batch, seq, lambda_init, tq, heads=2):
    nq = seq // tq
    hw = 2 * DIFF_D
    pw = heads * hw
    lam_spec = pl.BlockSpec((None, 1, DIFF_D), lambda b, p, qi: (lead, 0, 0))
    lq1, lk1, lq2, lk2 = (t.reshape(-1, 1, DIFF_D) for t in (lq1, lk1, lq2, lk2))
    return pl.pallas_call(
        functools.partial(_diff_kernel, tq=tq, lambda_init=lambda_init),
        out_shape=jax.ShapeDtypeStruct(mix.shape, mix.dtype),
        grid=(batch, DIFF_HEADS // heads, nq),
        in_specs=[lam_spec] * 4 + [
            pl.BlockSpec((tq, pw), lambda b, p, qi: (b * nq + qi, p)),
            pl.BlockSpec((seq, pw), lambda b, p, qi: (b, p)),
            pl.BlockSpec((None, pw, seq), lambda b, p, qi: (b, p, 0)),
            pl.BlockSpec((None, hw, 1), lambda b, p, qi: (lead, 0, 0)),
            pl.BlockSpec(memory_space=pl.ANY)],
        out_specs=pl.BlockSpec((tq, pw), lambda b, p, qi: (b * nq + qi, MIX_DIFF // pw + p)),
        scratch_shapes=[pltpu.VMEM((heads, 2 * tq, hw), BF16), pltpu.VMEM((heads, 1, 2 * tq), F32),
                        pltpu.VMEM((heads, 1, 2 * tq), F32), pltpu.VMEM((heads, hw, 2 * tq), F32)],
        input_output_aliases={8: 0},
        compiler_params=_cparams(("parallel", "parallel", "arbitrary")),
        name="diff_attn",
    )(lq1, lk1, lq2, lk2, dq, dk, dvt, norm_g.reshape(-1, hw, 1), mix)


def _dsa_kernel(iq_ref, sq_ref, ikd_ref, wt_ref, sk_ref, svt_ref, mix_ref, o_ref,
                iqm_ref, keys_ref, bias_ref, q5_ref, m_ref, l_ref, acc_ref, *, tq, tk, seq, k_sel):
    del mix_ref
    qi = pl.program_id(1)
    NT = (((1,), (1,)), ((), ()))
    n_tiles = (qi * tq + tq + tk - 1) // tk
    qpos = qi * tq + lax.broadcasted_iota(jnp.int32, (1, tq), 1)
    krow = lax.broadcasted_iota(jnp.int32, (tk, tq), 0)
    idx_scale = (IDX_DIM ** -0.5) * (IDX_HEADS ** -0.5)
    neg_inf = F32(-jnp.inf)

    def tile_start(j):
        return pl.multiple_of(j * tk, tk)

    lane = lax.broadcasted_iota(jnp.int32, (tq, 128), 1)
    for t in range(IDX_HEADS // 2):
        a = iq_ref[:, 128 * t:128 * (t + 1)]
        zero = jnp.zeros_like(a)
        iqm_ref[t, :tq] = jnp.where(lane < IDX_DIM, a, zero)
        iqm_ref[t, tq:] = jnp.where(lane >= IDX_DIM, a, zero)
    wt = wt_ref[...]

    def score_tile(j, carry):
        r0 = tile_start(j)
        kd = ikd_ref[pl.ds(r0, tk), :]
        acc = jnp.zeros((tk, 2 * tq), F32)
        for t in range(IDX_HEADS // 2):
            lg = lax.dot_general(kd, iqm_ref[t], NT, preferred_element_type=F32)
            w2 = jnp.concatenate([wt[2 * t:2 * t + 1], wt[2 * t + 1:2 * t + 2]], axis=1)
            acc = acc + jnp.maximum(lg, 0.0) * w2
        acc = acc[:, :tq] + acc[:, tq:]
        bits = pltpu.bitcast(acc * idx_scale, jnp.int32)
        key = jnp.where(bits < 0, (bits ^ 0x7FFFFFFF) + 1, bits)
        keys_ref[pl.ds(r0, tk), :] = jnp.where(r0 + krow <= qpos, key, INT_MIN)
        return carry

    lax.fori_loop(0, n_tiles, score_tile, 0)

    def count_ones(ones):
        def body(j, cnt):
            r0 = tile_start(j)
            return cnt + jnp.sum(ones(keys_ref[pl.ds(r0, tk), :], r0 + krow), axis=0, keepdims=True)
        return lax.fori_loop(0, n_tiles, body, jnp.zeros((1, tq), jnp.int32))

    def count(pred):
        return count_ones(lambda kk, kp: jnp.where(pred(kk, kp), 1, 0))

    def thr_bit(i, t_u):
        cand_u = t_u | jnp.left_shift(jnp.int32(1), 31 - i)
        cand = cand_u ^ INT_MIN
        return jnp.where(count(lambda kk, kp: kk >= cand) >= k_sel, cand_u, t_u)

    thr = lax.fori_loop(0, 32, thr_bit, jnp.zeros((1, tq), jnp.int32)) ^ INT_MIN
    need = k_sel - count(lambda kk, kp: kk > thr)
    n_eq = count(lambda kk, kp: kk == thr)
    excess = jnp.where(thr == INT_MIN, 0, jnp.where(n_eq > need, 1, 0))

    def tie_search():
        def idx_bit(i, j_lo):
            cand = j_lo | jnp.left_shift(jnp.int32(1), (seq - 1).bit_length() - 1 - i)
            c = count_ones(lambda kk, kp: jnp.where(kk == thr, jnp.where(kp < cand, 1, 0), 0))
            return jnp.where(c < need, cand, j_lo)
        return lax.fori_loop(0, (seq - 1).bit_length(), idx_bit, jnp.zeros((1, tq), jnp.int32))

    j_star = lax.cond(jnp.max(excess) > 0, tie_search, lambda: jnp.full((1, tq), seq, jnp.int32))
    j_star = jnp.where(thr == INT_MIN, -1, jnp.where(excess > 0, j_star, seq))

    def bias_tile(j, carry):
        r0 = tile_start(j)
        kk = keys_ref[pl.ds(r0, tk), :]
        tie = jnp.where(r0 + krow <= j_star, 0.0, neg_inf)
        bias_ref[pl.ds(r0, tk), :] = jnp.where(kk > thr, 0.0, jnp.where(kk == thr, tie, neg_inf))
        return carry

    lax.fori_loop(0, n_tiles, bias_tile, 0)

    scale = DSA_DH ** -0.5
    rep = DSA_HEADS // DSA_KV
    for g in range(DSA_KV):
        for i in range(rep):
            h = g * rep + i
            q5_ref[g, tq * i:tq * (i + 1)] = sq_ref[:, DSA_DH * h:DSA_DH * (h + 1)]
    m_ref[...] = jnp.full_like(m_ref, neg_inf)
    l_ref[...] = jnp.zeros_like(l_ref)
    acc_ref[...] = jnp.zeros_like(acc_ref)

    def attn_tile(j, carry):
        r0 = tile_start(j)
        bias = bias_ref[pl.ds(r0, tk), :]
        bias = jnp.concatenate([bias] * rep, axis=1)
        for g in range(DSA_KV):
            gs = slice(DSA_DH * g, DSA_DH * (g + 1))
            s = lax.dot_general(sk_ref[pl.ds(r0, tk), gs], q5_ref[g], NT, preferred_element_type=F32)
            s = s * scale + bias
            m_prev = m_ref[g]
            m_new = jnp.maximum(m_prev, jnp.max(s, axis=0, keepdims=True))
            m_safe = jnp.where(m_new == neg_inf, 0.0, m_new)
            alpha = jnp.exp(m_prev - m_safe)
            p = jnp.exp(s - m_safe)
            l_ref[g] = alpha * l_ref[g] + jnp.sum(p, axis=0, keepdims=True)
            acc_ref[g] = alpha * acc_ref[g] + jnp.dot(svt_ref[gs, pl.ds(r0, tk)], p.astype(BF16),
                                                    preferred_element_type=F32)
            m_ref[g] = m_new
        return carry

    lax.fori_loop(0, n_tiles, attn_tile, 0)
    for g in range(DSA_KV):
        out_t = acc_ref[g] / l_ref[g]
        for i in range(rep):
            h = g * rep + i
            o_ref[:, DSA_DH * h:DSA_DH * (h + 1)] = out_t[:, tq * i:tq * (i + 1)].T.astype(BF16)


def _dsa(iq, sq, ikd, wt, sk, svt, mix, *, batch, seq, tq, tk):
    nq = seq // tq
    k_sel = min(IDX_TOPK, seq // 4)
    rep = DSA_HEADS // DSA_KV
    return pl.pallas_call(
        functools.partial(_dsa_kernel, tq=tq, tk=tk, seq=seq, k_sel=k_sel),
        out_shape=jax.ShapeDtypeStruct(mix.shape, mix.dtype),
        grid=(batch, nq),
        in_specs=[pl.BlockSpec((tq, W_IQ), lambda b, qi: (b * nq + qi, 0)),
                  pl.BlockSpec((tq, W_SQ), lambda b, qi: (b * nq + qi, 0)),
                  pl.BlockSpec((seq, 128), lambda b, qi: (b, 0)),
                  pl.BlockSpec((None, IDX_HEADS, tq), lambda b, qi: (b, 0, qi)),
                  pl.BlockSpec((seq, W_SKV), lambda b, qi: (b, 0)),
                  pl.BlockSpec((None, W_SKV, seq), lambda b, qi: (b, 0, 0)),
                  pl.BlockSpec(memory_space=pl.ANY)],
        out_specs=pl.BlockSpec((tq, W_SQ), lambda b, qi: (b * nq + qi, MIX_DSA // W_SQ)),
        scratch_shapes=[pltpu.VMEM((IDX_HEADS // 2, 2 * tq, 128), BF16), pltpu.VMEM((seq, tq), jnp.int32),
                        pltpu.VMEM((seq, tq), F32), pltpu.VMEM((DSA_KV, rep * tq, DSA_DH), BF16),
                        pltpu.VMEM((DSA_KV, 1, rep * tq), F32), pltpu.VMEM((DSA_KV, 1, rep * tq), F32),
                        pltpu.VMEM((DSA_KV, DSA_DH, rep * tq), F32)],
        input_output_aliases={6: 0},
        compiler_params=_cparams(("parallel", "arbitrary")),
        name="dsa",
    )(iq, sq, ikd, wt, sk, svt, mix)


def _router_kernel(x_ref, w_ref, b_ref, e_ref, g_ref, xp_ref):
    x = x_ref[...]
    logits = jnp.dot(x, w_ref[...], precision=HIGHEST, preferred_element_type=F32) + b_ref[...]
    lane = lax.broadcasted_iota(jnp.int32, logits.shape, 1)
    neg_inf = F32(-jnp.inf)
    lg = jnp.where(lane < MOE_E, logits, neg_inf)
    m1 = jnp.max(lg, axis=-1, keepdims=True)
    i1 = jnp.min(jnp.where(lg == m1, lane, 128), axis=-1, keepdims=True)
    lg2 = jnp.where(lane == i1, neg_inf, lg)
    m2 = jnp.max(lg2, axis=-1, keepdims=True)
    i2 = jnp.min(jnp.where(lg2 == m2, lane, 128), axis=-1, keepdims=True)
    e21 = jnp.exp(m2 - m1)
    g1 = 1.0 / (1.0 + e21)
    e_ref[...] = jnp.where(lane == 0, i1, jnp.where(lane == 1, i2, 0))
    g_ref[...] = jnp.where(lane == 0, g1, jnp.where(lane == 1, e21 * g1, 0.0))
    half = x.shape[1] // 2
    lo = pltpu.bitcast(x[:, :half].astype(BF16).astype(F32), jnp.uint32)
    hi = pltpu.bitcast(x[:, half:].astype(BF16).astype(F32), jnp.uint32)
    xp_ref[...] = (hi & jnp.uint32(0xFFFF0000)) | (lo >> 16)


def _router(x, router_w, router_b, *, tm=256):
    n, d = x.shape
    row = lambda w: pl.BlockSpec((tm, w), lambda i: (i, 0))
    return pl.pallas_call(
        _router_kernel,
        out_shape=(jax.ShapeDtypeStruct((n, 128), jnp.int32), jax.ShapeDtypeStruct((n, 128), F32),
                   jax.ShapeDtypeStruct((n, d // 2), jnp.uint32)),
        grid=(n // tm,),
        in_specs=[row(d), pl.BlockSpec((d, 128), lambda i: (0, 0)), pl.BlockSpec((1, 128), lambda i: (0, 0))],
        out_specs=(row(128), row(128), row(d // 2)),
        compiler_params=_cparams(("parallel",)),
        name="router",
    )(x, router_w, router_b)


def _dispatch_kernel(pos_ref, xp_ref, buf_ref, o_ref, sem, *, tb):
    del buf_ref
    base = pl.program_id(0) * tb

    def row_copy(i, s):
        return pltpu.make_async_copy(xp_ref.at[pl.ds(i, 1)], o_ref.at[pl.ds(pos_ref[2 * (base + i) + s], 1)], sem)

    def start(i, c):
        row_copy(i, 0).start()
        row_copy(i, 1).start()
        return c

    def wait(i, c):
        row_copy(i, 0).wait()
        row_copy(i, 1).wait()
        return c

    lax.fori_loop(0, tb, start, 0)
    lax.fori_loop(0, tb, wait, 0)


def _dispatch(pos, xp, buf, *, tb=256):
    n, w = xp.shape
    return pl.pallas_call(
        functools.partial(_dispatch_kernel, tb=tb),
        out_shape=jax.ShapeDtypeStruct(buf.shape, buf.dtype),
        grid_spec=pltpu.PrefetchScalarGridSpec(
            num_scalar_prefetch=1,
            grid=(n // tb,),
            in_specs=[pl.BlockSpec((tb, w), lambda i, pos: (i, 0)), pl.BlockSpec(memory_space=pl.ANY)],
            out_specs=pl.BlockSpec(memory_space=pl.ANY),
            scratch_shapes=[pltpu.SemaphoreType.DMA],
        ),
        input_output_aliases={2: 0},
        compiler_params=_cparams(("arbitrary",)),
        name="moe_dispatch",
    )(pos, xp, buf)


def _unpack_rows(word):
    lo = pltpu.bitcast(word << 16, F32).astype(BF16)
    hi = pltpu.bitcast(word & jnp.uint32(0xFFFF0000), F32).astype(BF16)
    return lo, hi


def _moe_up_kernel(be_ref, bf_ref, na_ref, a_ref, w1_ref, w3_ref, o_ref, w1b_ref, w3b_ref):
    del be_ref
    i = pl.program_id(1)

    @pl.when(i < na_ref[0])
    def _():
        @pl.when(bf_ref[i] == 1)
        def _():
            w1b_ref[...] = w1_ref[...].astype(BF16)
            w3b_ref[...] = w3_ref[...].astype(BF16)

        lo, hi = _unpack_rows(a_ref[...])
        half = lo.shape[1]
        g = (jnp.dot(lo, w1b_ref[:half], preferred_element_type=F32)
             + jnp.dot(hi, w1b_ref[half:], preferred_element_type=F32))
        u = (jnp.dot(lo, w3b_ref[:half], preferred_element_type=F32)
             + jnp.dot(hi, w3b_ref[half:], preferred_element_type=F32))
        o_ref[...] = (g * jax.nn.sigmoid(g) * u).astype(BF16)

    @pl.when(i >= na_ref[0])
    def _():
        o_ref[...] = jnp.zeros_like(o_ref)


def _moe_down_kernel(be_ref, bf_ref, na_ref, h_ref, w2_ref, o_ref, w2b_ref):
    del be_ref
    i = pl.program_id(1)

    @pl.when(i < na_ref[0])
    def _():
        @pl.when(bf_ref[i] == 1)
        def _():
            w2b_ref[...] = w2_ref[...].astype(BF16)

        o_ref[...] = jnp.dot(h_ref[...], w2b_ref[...], preferred_element_type=F32)

    @pl.when(i >= na_ref[0])
    def _():
        o_ref[...] = jnp.zeros_like(o_ref)


def _moe_ffn(block_e, block_first, n_active, a_sorted, w1, w3, w2, lead, *, tm, tn):
    r, half = a_sorted.shape
    d = 2 * half
    f = w1.shape[-1]
    nb = r // tm

    def rows(i, na):
        return jnp.minimum(i, na[0] - 1)

    h = pl.pallas_call(
        _moe_up_kernel,
        out_shape=jax.ShapeDtypeStruct((r, f), BF16),
        grid_spec=pltpu.PrefetchScalarGridSpec(
            num_scalar_prefetch=3,
            grid=(f // tn, nb),
            in_specs=[pl.BlockSpec((tm, half), lambda n, i, be, bf, na: (rows(i, na), 0)),
                      pl.BlockSpec((None, None, d, tn), lambda n, i, be, bf, na: (lead, be[i], 0, n)),
                      pl.BlockSpec((None, None, d, tn), lambda n, i, be, bf, na: (lead, be[i], 0, n))],
            out_specs=pl.BlockSpec((tm, tn), lambda n, i, be, bf, na: (i, n)),
            scratch_shapes=[pltpu.VMEM((d, tn), BF16), pltpu.VMEM((d, tn), BF16)],
        ),
        compiler_params=_cparams(("arbitrary", "arbitrary")),
        name="moe_up",
    )(block_e, block_first, n_active, a_sorted, w1, w3)
    return pl.pallas_call(
        _moe_down_kernel,
        out_shape=jax.ShapeDtypeStruct((r, d), F32),
        grid_spec=pltpu.PrefetchScalarGridSpec(
            num_scalar_prefetch=3,
            grid=(d // tn, nb),
            in_specs=[pl.BlockSpec((tm, f), lambda n, i, be, bf, na: (rows(i, na), 0)),
                      pl.BlockSpec((None, None, f, tn), lambda n, i, be, bf, na: (lead, be[i], 0, n))],
            out_specs=pl.BlockSpec((tm, tn), lambda n, i, be, bf, na: (i, n)),
            scratch_shapes=[pltpu.VMEM((f, tn), BF16)],
        ),
        compiler_params=_cparams(("arbitrary", "arbitrary")),
        name="moe_down",
    )(block_e, block_first, n_active, h, w2)


def _combine_ln_kernel(pos_ref, x_ref, gate_ref, g_ref, b_ref, y_ref, o_ref, ybuf_ref, sem, *, tb, alpha):
    base = pl.program_id(0) * tb

    def row_copy(i, s):
        return pltpu.make_async_copy(y_ref.at[pl.ds(pos_ref[2 * (base + i) + s], 1)],
                                     ybuf_ref.at[s, pl.ds(i, 1)], sem)

    def start(i, c):
        row_copy(i, 0).start()
        row_copy(i, 1).start()
        return c

    def wait(i, c):
        row_copy(i, 0).wait()
        row_copy(i, 1).wait()
        return c

    lax.fori_loop(0, tb, start, 0)
    lax.fori_loop(0, tb, wait, 0)
    gate = gate_ref[...]
    f = ybuf_ref[0] * gate[:, 0:1] + ybuf_ref[1] * gate[:, 1:2]
    z = alpha * x_ref[...] + f
    mu = jnp.mean(z, axis=-1, keepdims=True)
    zc = z - mu
    var = jnp.mean(zc * zc, axis=-1, keepdims=True)
    o_ref[...] = zc * lax.rsqrt(var + LN_EPS) * g_ref[...] + b_ref[...]


def _combine_ln(pos, x, gates, g, b, y, lead, *, alpha, tb=128):
    n, d = x.shape
    par = pl.BlockSpec((None, 1, d), lambda i, pos: (lead, 0, 0))
    return pl.pallas_call(
        functools.partial(_combine_ln_kernel, tb=tb, alpha=alpha),
        out_shape=jax.ShapeDtypeStruct((n, d), F32),
        grid_spec=pltpu.PrefetchScalarGridSpec(
            num_scalar_prefetch=1,
            grid=(n // tb,),
            in_specs=[pl.BlockSpec((tb, d), lambda i, pos: (i, 0)),
                      pl.BlockSpec((tb, 128), lambda i, pos: (i, 0)),
                      par, par, pl.BlockSpec(memory_space=pl.ANY)],
            out_specs=pl.BlockSpec((tb, d), lambda i, pos: (i, 0)),
            scratch_shapes=[pltpu.VMEM((2, tb, d), F32), pltpu.SemaphoreType.DMA],
        ),
        compiler_params=_cparams(("arbitrary",)),
        name="moe_combine_ln",
    )(pos, x, gates, g.reshape(-1, 1, d), b.reshape(-1, 1, d), y)


def _moe_layer(x, router_w, router_b, w1, w3, w2, ln_g, ln_b, lead_moe, lead_ln, *, alpha, tm=512, tn=512):
    n, d = x.shape
    rw = jnp.pad(router_w[lead_moe], ((0, 0), (0, 128 - MOE_E)))
    rb = jnp.pad(router_b[lead_moe], (0, 128 - MOE_E)).reshape(1, 128)
    top_e, gates, xp = _router(x, rw, rb)
    flat_e = top_e[:, :MOE_TOPK].reshape(-1)
    onehot = (flat_e[:, None] == jnp.arange(MOE_E, dtype=jnp.int32)[None, :]).astype(jnp.int32)
    rank = jnp.sum((jnp.cumsum(onehot, axis=0) - onehot) * onehot, axis=1)
    counts = jnp.sum(onehot, axis=0)
    padded = (counts + tm - 1) // tm * tm
    end_padded = jnp.cumsum(padded)
    start_padded = end_padded - padded
    pos = (start_padded[flat_e] + rank).astype(jnp.int32)
    nb = -(-(n * MOE_TOPK + MOE_E * (tm - 1)) // tm)
    block_start = jnp.arange(nb, dtype=jnp.int32) * tm
    block_e = jnp.sum((block_start[:, None] >= end_padded[None, :]).astype(jnp.int32), axis=1)
    block_e = jnp.minimum(block_e, MOE_E - 1)
    n_active = (end_padded[-1] // tm).astype(jnp.int32).reshape(1)
    block_first = jnp.concatenate([jnp.ones((1,), jnp.int32), (block_e[1:] != block_e[:-1]).astype(jnp.int32)])
    a_sorted = _dispatch(pos, xp, jnp.zeros((nb * tm, d // 2), jnp.uint32))
    y = _moe_ffn(block_e, block_first, n_active, a_sorted, w1, w3, w2, lead_moe, tm=tm, tn=tn)
    return _combine_ln(pos, x, gates, ln_g, ln_b, y, lead_ln, alpha=alpha)


def kernel(x, positions, w_in, w_gate_up, b_gate, gla_norm_g, lambda_q1, lambda_k1, lambda_q2, lambda_k2,
           diff_norm_g, w_out, ln1_g, ln1_b, ln2_g, ln2_b, ffn_w1, ffn_w3, ffn_w2, router_w, router_b,
           moe_w1, moe_w3, moe_w2):
    batch, seq, d = x.shape
    n = batch * seq
    depth = w_in.shape[0]
    alpha = (2 * depth) ** 0.25
    xf = x.reshape(n, d)
    xb = xf.astype(BF16)
    pos = positions.reshape(n)
    tab_a, half_a = _rope_tables(pos, DIFF_D)
    tab_b, half_b = _rope_tables(pos, DSA_DH)
    plain = lambda i, j, k: (i, k)
    w_in_t = jnp.swapaxes(w_in, 1, 2)
    for l in range(depth):
        ha = _matmul(xb, w_in_t, n_out=W_GLA, k_steps=1, tm=1024, tn=HM_TN, tk=d, out_dtype=BF16,
                     a_map=plain, w_map=lambda i, j, k, l=l: (l, j, k), w_lead=True, w_t=True,
                     name="in_proj_gla")
        hm = _matmul(xb, w_in_t, n_out=HM_TILES * HM_TN, k_steps=1, tm=1024, tn=HM_TN, tk=d,
                     out_dtype=BF16, a_map=plain,
                     w_map=lambda i, j, k, l=l: (l, HM_BASE // HM_TN + j, k), w_lead=True, w_t=True,
                     n_valid=N_IN, col_start=lambda j: HM_BASE + j * HM_TN, name="in_proj_attn")
        small_tiles = (OFF_GLR // 128, OFF_IK // 128)
        pick = lambda j: jnp.where(j == 0, small_tiles[0], small_tiles[1])
        hs = _matmul(xb, w_in_t, n_out=256, k_steps=1, tm=1024, tn=128, tk=d, out_dtype=F32,
                     a_map=plain, w_map=lambda i, j, k, l=l: (l, pick(j), k), w_lead=True, w_t=True,
                     n_valid=N_IN, col_start=lambda j: pick(j) * 128, name="in_proj_small")
        glr = hs[:, OFF_GLR % 128:OFF_GLR % 128 + GLA_RANK]
        iw0 = 128 + OFF_IW % 128
        wt = hs[:, iw0:iw0 + IDX_HEADS].reshape(batch, seq, IDX_HEADS).transpose(0, 2, 1)
        dq, dk, dv, sq, sk, sv, iq, ikd = _split_rope(hm, hs, tab_a, half_a, tab_b, half_b)
        svt = sv.reshape(batch, seq, W_SKV).transpose(0, 2, 1)
        dvt = dv.reshape(batch, seq, W_DIFF).transpose(0, 2, 1)

        mix = jnp.zeros((n, W_MIX), BF16)
        mix = _gla(ha, glr, w_gate_up, b_gate, gla_norm_g, mix, l, batch=batch, seq=seq)
        lambda_init = 0.8 - 0.6 * math.exp(-0.3 * l)
        mix = _diff_attention(dq, dk, dvt, lambda_q1, lambda_k1, lambda_q2, lambda_k2, diff_norm_g, mix, l,
                              batch=batch, seq=seq, lambda_init=lambda_init, tq=1024, heads=1)
        mix = _dsa(iq, sq, ikd, wt, sk, svt, mix, batch=batch, seq=seq, tq=256, tk=512)

        n_gla = GLA_HEADS * GLA_DV // MIX_BLK
        n_skip = MIX_DIFF // MIX_BLK - n_gla
        proj = _matmul(
            mix, w_out, n_out=d, k_steps=d // MIX_BLK, tm=1024, tn=1024, tk=MIX_BLK, out_dtype=F32,
            a_map=lambda i, j, k: (i, jnp.where(k < n_gla, k, k + n_skip)),
            w_map=lambda i, j, k, l=l: (l, k, j), w_lead=True, name="out_proj")
        xf, xb = _add_ln(xf, proj, ln1_g, ln1_b, l, alpha=alpha)

        j = l // 2
        if l % 2 == 0:
            f_dim = ffn_w2.shape[1]
            f_pad = -(-f_dim // 1024) * 1024
            hid = _swiglu_up(xb, ffn_w1, ffn_w3, j, tm=1024, tn=256, tk=d, f_out=f_pad)
            down = _matmul(hid, ffn_w2, n_out=d, k_steps=f_pad // 512, tm=2048, tn=1024, tk=512, out_dtype=F32,
                           a_map=plain, w_map=lambda i, j_, k, j=j: (j, k, j_), w_lead=True, k_valid=f_dim,
                           name="ffn_down")
            xf, xb = _add_ln(xf, down, ln2_g, ln2_b, l, alpha=alpha)
        else:
            xf = _moe_layer(xf, router_w, router_b, moe_w1, moe_w3, moe_w2, ln2_g, ln2_b, j, l, alpha=alpha)
            xb = xf.astype(BF16)
    return xf.reshape(batch, seq, d)
```

```python
import functools
import math

import jax
import jax.numpy as jnp
from jax import lax
from jax.experimental import pallas as pl
from jax.experimental.pallas import tpu as pltpu

F32 = jnp.float32
BF16 = jnp.bfloat16
HIGHEST = lax.Precision.HIGHEST
INT_MIN = -(2 ** 31)

ROPE_THETA = 500000.0
GLA_HEADS, GLA_DK, GLA_DV, GLA_RANK, GLA_TAU, GLA_CHUNK = 4, 192, 384, 16, 16.0, 64
GLA_SUB = 16
DIFF_HEADS, DIFF_D = 10, 64
DSA_HEADS, DSA_KV, DSA_DH = 10, 2, 128
IDX_HEADS, IDX_DIM, IDX_TOPK = 32, 64, 256
MOE_E, MOE_TOPK = 8, 2
LN_EPS = 1e-5

W_GLA = 2 * GLA_HEADS * GLA_DK + 2 * GLA_HEADS * GLA_DV
OFF_GLR = W_GLA
OFF_DQ = OFF_GLR + GLA_RANK
W_DIFF = DIFF_HEADS * 2 * DIFF_D
OFF_DK, OFF_DV = OFF_DQ + W_DIFF, OFF_DQ + 2 * W_DIFF
OFF_SQ = OFF_DQ + 3 * W_DIFF
W_SQ, W_SKV = DSA_HEADS * DSA_DH, DSA_KV * DSA_DH
OFF_SK, OFF_SV = OFF_SQ + W_SQ, OFF_SQ + W_SQ + W_SKV
OFF_IQ = OFF_SV + W_SKV
W_IQ = IDX_HEADS * IDX_DIM
OFF_IK = OFF_IQ + W_IQ
OFF_IW = OFF_IK + IDX_DIM
N_IN = OFF_IW + IDX_HEADS

HM_BASE = W_GLA
HM_TN = 512
HM_TILES = -(-(N_IN - HM_BASE) // HM_TN)

MIX_GLA, MIX_DIFF, MIX_DSA = 0, 2560, 3840
W_MIX = MIX_DSA + DSA_HEADS * DSA_DH

VMEM_LIMIT = 56 * 1024 * 1024


def _cparams(sem, vmem=VMEM_LIMIT):
    return pltpu.CompilerParams(dimension_semantics=sem, vmem_limit_bytes=vmem)


def _mm_kernel(a_ref, w_ref, o_ref, acc_ref=None, *, k_valid, n_valid, col_start, w_t):
    k = pl.program_id(2)
    k_ax, n_ax = (1, 0) if w_t else (0, 1)

    def partial_product():
        w = w_ref[...]
        if k_valid is not None:
            kk = k * w.shape[k_ax] + lax.broadcasted_iota(jnp.int32, w.shape, k_ax)
            w = jnp.where(kk < k_valid, w, 0.0)
        if n_valid is not None:
            col = col_start(pl.program_id(1)) + lax.broadcasted_iota(jnp.int32, w.shape, n_ax)
            w = jnp.where(col < n_valid, w, 0.0)
        return lax.dot_general(a_ref[...], w.astype(BF16), (((1,), (k_ax,)), ((), ())),
                               preferred_element_type=F32)

    if acc_ref is None:
        o_ref[...] = partial_product().astype(o_ref.dtype)
        return

    @pl.when(k == 0)
    def _():
        acc_ref[...] = partial_product()

    @pl.when(k > 0)
    def _():
        acc_ref[...] += partial_product()

    @pl.when(k == pl.num_programs(2) - 1)
    def _():
        o_ref[...] = acc_ref[...].astype(o_ref.dtype)


def _matmul(a, w, *, n_out, k_steps, tm, tn, tk, out_dtype, a_map, w_map, w_lead=False, k_valid=None,
            n_valid=None, col_start=None, w_t=False, name="mm"):
    m = a.shape[0]
    w_block = (tn, tk) if w_t else (tk, tn)
    if w_lead:
        w_block = (None,) + w_block
    return pl.pallas_call(
        functools.partial(_mm_kernel, k_valid=k_valid, n_valid=n_valid, col_start=col_start, w_t=w_t),
        out_shape=jax.ShapeDtypeStruct((m, n_out), out_dtype),
        grid=(m // tm, pl.cdiv(n_out, tn), k_steps),
        in_specs=[pl.BlockSpec((tm, tk), a_map), pl.BlockSpec(w_block, w_map)],
        out_specs=pl.BlockSpec((tm, tn), lambda i, j, k: (i, j)),
        scratch_shapes=[pltpu.VMEM((tm, tn), F32)] if k_steps > 1 else [],
        compiler_params=_cparams(("parallel", "parallel", "arbitrary")),
        name=name,
    )(a, w)


def _swiglu_kernel(a_ref, w1_ref, w3_ref, o_ref, acc1_ref=None, acc3_ref=None, *, n_valid):
    k = pl.program_id(2)

    def partial_product(w_ref):
        return jnp.dot(a_ref[...], w_ref[...].astype(BF16), preferred_element_type=F32)

    def finish(g, u):
        h = g * jax.nn.sigmoid(g) * u
        col = pl.program_id(1) * h.shape[1] + lax.broadcasted_iota(jnp.int32, h.shape, 1)
        o_ref[...] = jnp.where(col < n_valid, h, 0.0).astype(o_ref.dtype)

    if acc1_ref is None:
        finish(partial_product(w1_ref), partial_product(w3_ref))
        return

    @pl.when(k == 0)
    def _():
        acc1_ref[...] = partial_product(w1_ref)
        acc3_ref[...] = partial_product(w3_ref)

    @pl.when(k > 0)
    def _():
        acc1_ref[...] += partial_product(w1_ref)
        acc3_ref[...] += partial_product(w3_ref)

    @pl.when(k == pl.num_programs(2) - 1)
    def _():
        finish(acc1_ref[...], acc3_ref[...])


def _swiglu_up(a, w1, w3, lead, *, tm, tn, tk, f_out):
    m, kdim = a.shape
    f = w1.shape[-1]
    last = (f - 1) // tn
    wspec = pl.BlockSpec((None, tk, tn), lambda i, j, k: (lead, k, jnp.minimum(j, last)))
    k_steps = kdim // tk
    return pl.pallas_call(
        functools.partial(_swiglu_kernel, n_valid=f),
        out_shape=jax.ShapeDtypeStruct((m, f_out), BF16),
        grid=(m // tm, f_out // tn, k_steps),
        in_specs=[pl.BlockSpec((tm, tk), lambda i, j, k: (i, k)), wspec, wspec],
        out_specs=pl.BlockSpec((tm, tn), lambda i, j, k: (i, j)),
        scratch_shapes=[pltpu.VMEM((tm, tn), F32), pltpu.VMEM((tm, tn), F32)] if k_steps > 1 else [],
        compiler_params=_cparams(("parallel", "parallel", "arbitrary")),
        name="swiglu_up",
    )(a, w1, w3)


def _out_proj_kernel(mix_ref, w_ref, o_ref):
    n_gla = GLA_HEADS * GLA_DV
    w = w_ref[...].astype(BF16)
    o_ref[...] = (jnp.dot(mix_ref[:, MIX_GLA:MIX_GLA + n_gla], w[:n_gla], preferred_element_type=F32)
                  + jnp.dot(mix_ref[:, MIX_DIFF:], w[n_gla:], preferred_element_type=F32))


def _out_proj(mix, w_out, lead, *, tm=1024, tn=256):
    m, wm = mix.shape
    d = w_out.shape[-1]
    return pl.pallas_call(
        _out_proj_kernel,
        out_shape=jax.ShapeDtypeStruct((m, d), F32),
        grid=(m // tm, d // tn),
        in_specs=[pl.BlockSpec((tm, wm), lambda i, j: (i, 0)),
                  pl.BlockSpec((None, w_out.shape[1], tn), lambda i, j: (lead, 0, j))],
        out_specs=pl.BlockSpec((tm, tn), lambda i, j: (i, j)),
        compiler_params=_cparams(("parallel", "parallel")),
        name="out_proj",
    )(mix, w_out)


def _add_ln_kernel(x_ref, y_ref, g_ref, b_ref, o_ref, ob_ref, *, alpha):
    z = alpha * x_ref[...] + y_ref[...].astype(F32)
    mu = jnp.mean(z, axis=-1, keepdims=True)
    zc = z - mu
    var = jnp.mean(zc * zc, axis=-1, keepdims=True)
    out = zc * lax.rsqrt(var + LN_EPS) * g_ref[...] + b_ref[...]
    o_ref[...] = out
    ob_ref[...] = out.astype(BF16)


def _add_ln(x, y, g, b, lead, *, alpha, tm=256):
    m, d = x.shape
    row = pl.BlockSpec((tm, d), lambda i: (i, 0))
    par = pl.BlockSpec((None, 1, d), lambda i: (lead, 0, 0))
    g, b = g.reshape(-1, 1, d), b.reshape(-1, 1, d)
    return pl.pallas_call(
        functools.partial(_add_ln_kernel, alpha=alpha),
        out_shape=(jax.ShapeDtypeStruct((m, d), F32), jax.ShapeDtypeStruct((m, d), BF16)),
        grid=(m // tm,),
        in_specs=[row, row, par, par],
        out_specs=(row, row),
        compiler_params=_cparams(("parallel",)),
        name="add_ln",
    )(x, y, g, b)


def _rope_tables(positions, width):
    rot = width // 4
    half = rot // 2
    inv_freq = 1.0 / (ROPE_THETA ** (jnp.arange(half, dtype=F32) * 2.0 / rot))
    ang = positions.astype(F32)[:, None] * inv_freq
    cos, sin = jnp.cos(ang), jnp.sin(ang)
    n = positions.shape[0]
    one = jnp.ones((n, width - rot), F32)
    zero_h = jnp.zeros((n, half), F32)
    zero_r = jnp.zeros((n, width - rot), F32)
    c = jnp.concatenate([cos, cos, one], axis=1)
    s_up = jnp.concatenate([-sin, zero_h, zero_r], axis=1)
    s_dn = jnp.concatenate([zero_h, sin, zero_r], axis=1)
    rep = 128 // width
    return tuple(jnp.tile(t, (1, rep)) for t in (c, s_up, s_dn)), half


def _rope_tile(x, c, s_up, s_dn, half):
    return x * c + pltpu.roll(x, 128 - half, 1) * s_up + pltpu.roll(x, half, 1) * s_dn


def _split_rope_kernel(hm_ref, hc_ref, ca_ref, ua_ref, da_ref, cb_ref, ub_ref, db_ref,
                       dq_o, dk_o, dv_o, sq_o, sk_o, sv_o, iq_o, ik_o, *, shift, ik_shift, half_a, half_b):
    ta = (ca_ref[...], ua_ref[...], da_ref[...])
    tb = (cb_ref[...], ub_ref[...], db_ref[...])
    lane = lax.broadcasted_iota(jnp.int32, ca_ref.shape, 1)

    def shifted(k):
        return pltpu.roll(hm_ref[:, 128 * k:128 * (k + 1)].astype(F32), 128 - shift, 1)

    m = 0
    nxt = shifted(0)
    for dst, tab, half in ((dq_o, ta, half_a), (dk_o, ta, half_a), (dv_o, None, 0), (sq_o, tb, half_b),
                           (sk_o, tb, half_b), (sv_o, None, 0), (iq_o, ta, half_a)):
        for t in range(dst.shape[1] // 128):
            cur, nxt = nxt, shifted(m + 1)
            y = jnp.where(lane < 128 - shift, cur, nxt)
            if tab is not None:
                y = _rope_tile(y, *tab, half)
            dst[:, 128 * t:128 * (t + 1)] = y.astype(BF16)
            m += 1
    ik = _rope_tile(pltpu.roll(hc_ref[...], 128 - ik_shift, 1), *ta, half_a)
    ik_o[...] = jnp.where(lane < IDX_DIM, ik, pltpu.roll(ik, IDX_DIM, 1)).astype(BF16)


def _split_rope(hm, hs, tab_a, half_a, tab_b, half_b, *, tm=256):
    n, wm = hm.shape
    tile = pl.BlockSpec((tm, 128), lambda i: (i, 0))
    widths = (W_DIFF, W_DIFF, W_DIFF, W_SQ, W_SKV, W_SKV, W_IQ, 128)
    return pl.pallas_call(
        functools.partial(_split_rope_kernel, shift=OFF_DQ - HM_BASE, ik_shift=OFF_IK % 128,
                          half_a=half_a, half_b=half_b),
        out_shape=tuple(jax.ShapeDtypeStruct((n, w), BF16) for w in widths),
        grid=(n // tm,),
        in_specs=[pl.BlockSpec((tm, wm), lambda i: (i, 0)), pl.BlockSpec((tm, 128), lambda i: (i, 1))] + [tile] * 6,
        out_specs=tuple(pl.BlockSpec((tm, w), lambda i: (i, 0)) for w in widths),
        compiler_params=_cparams(("parallel",)),
        name="split_rope",
    )(hm, hs, *tab_a, *tab_b)


def _gla_prep_kernel(q_ref, k_ref, glr_ref, wg_ref, bg_ref, qe_o, kd_o, s_o, eb_o, *, chunks):
    C, SUB, DK = GLA_CHUNK, GLA_SUB, GLA_DK
    NT = (((1,), (1,)), ((), ()))

    row = lax.broadcasted_iota(jnp.int32, (C, C), 0)
    col = lax.broadcasted_iota(jnp.int32, (C, C), 1)
    tril = (col <= row).astype(F32)
    gi = lax.broadcasted_iota(jnp.int32, (SUB, SUB * SUB), 0)
    gr = lax.broadcasted_iota(jnp.int32, (SUB, SUB * SUB), 1)
    gsum = (gr // SUB == gi).astype(BF16)
    pr = lax.broadcasted_iota(jnp.int32, (SUB * SUB, C), 0)
    pc = lax.broadcasted_iota(jnp.int32, (SUB * SUB, C), 1)
    srow = lax.broadcasted_iota(jnp.int32, (SUB, C), 0)
    scol = lax.broadcasted_iota(jnp.int32, (SUB, C), 1)

    def chunk_head(ci, hh):
        r0 = pl.multiple_of(ci * C, C)
        rows = pl.ds(r0, C)
        ks = slice(hh * DK, (hh + 1) * DK)
        q = q_ref[rows, ks].astype(F32) * (DK ** -0.5)
        k = k_ref[rows, ks].astype(F32)
        z = jnp.dot(glr_ref[rows, :], wg_ref[:, ks], precision=HIGHEST, preferred_element_type=F32) + bg_ref[:, ks]
        log_a = -(jnp.maximum(-z, 0.0) + jnp.log1p(jnp.exp(-jnp.abs(z)))) * (1.0 / GLA_TAU)
        b = jnp.dot(tril, log_a, precision=HIGHEST, preferred_element_type=F32)
        qe_o[rows, ks] = (q * jnp.exp(b)).astype(BF16)

        s_rows = []
        for blk in range(C // SUB):
            r0 = blk * SUB
            b_i, q_i, k_i = b[r0:r0 + SUB], q[r0:r0 + SUB], k[r0:r0 + SUB]
            pair = (q_i[:, None, :] * k_i[None, :, :]
                    * jnp.exp(jnp.minimum(b_i[:, None, :] - b_i[None, :, :], 0.0)))
            pair_sum = jnp.sum(pair.reshape(SUB * SUB, DK), axis=-1, keepdims=True)
            placed = jnp.where(pc == r0 + pr % SUB, pair_sum, 0.0).astype(BF16)
            s_blk = jnp.dot(gsum, placed, preferred_element_type=F32)
            s_blk = jnp.where(scol <= srow + r0, s_blk, 0.0)
            if blk > 0:
                beta = b[r0:r0 + 1]
                q_t = (q_i * jnp.exp(b_i - beta)).astype(BF16)
                k_t = (k * jnp.exp(jnp.minimum(beta - b, 0.0))).astype(BF16)
                off = lax.dot_general(q_t, k_t, NT, preferred_element_type=F32)
                s_blk = jnp.where(scol < r0, off, s_blk)
            s_rows.append(s_blk)
        s_o[rows, hh * C:(hh + 1) * C] = jnp.concatenate(s_rows, axis=0).astype(BF16)
        b_last = b[C - 1:C]
        kd_o[rows, ks] = (k * jnp.exp(b_last - b)).astype(BF16)
        eb_o[pl.ds(ci, 1), ks] = jnp.exp(b_last)

    def two_chunks(i, carry):
        for ci in (2 * i, 2 * i + 1):
            for hh in range(2):
                chunk_head(ci, hh)
        return carry

    lax.fori_loop(0, chunks // 2, two_chunks, 0)


def _gla_scan_kernel(qe_ref, kd_ref, s_ref, eb_ref, v_ref, r_ref, ng_ref, mix_ref, o_ref, state_ref):
    del mix_ref
    C, DK, DV = GLA_CHUNK, GLA_DK, GLA_DV
    NT = (((1,), (1,)), ((), ()))
    TN = (((0,), (0,)), ((), ()))

    @pl.when(pl.program_id(0) == 0)
    def _():
        state_ref[...] = jnp.zeros_like(state_ref)

    for b in range(state_ref.shape[0]):
        for h in range(GLA_HEADS):
            ks = slice(h * DK, (h + 1) * DK)
            vs = slice(h * DV, (h + 1) * DV)
            st = state_ref[b, h]
            v = v_ref[b, :, vs]
            o = (lax.dot_general(qe_ref[b, :, ks], st.astype(BF16), NT, preferred_element_type=F32)
                 + jnp.dot(s_ref[b, :, h * C:(h + 1) * C], v, preferred_element_type=F32))
            state_ref[b, h] = st * eb_ref[b:b + 1, ks] + lax.dot_general(v, kd_ref[b, :, ks], TN,
                                                                         preferred_element_type=F32)
            ms = jnp.mean(o * o, axis=-1, keepdims=True)
            o_n = o * lax.rsqrt(ms + LN_EPS) * ng_ref[...]
            r = r_ref[b, :, vs].astype(F32)
            o_ref[b, :, vs] = (o_n * (r * jax.nn.sigmoid(r))).astype(BF16)


def _gla(ha, glr, w_gate_up, b_gate, gla_norm_g, mix, lead, *, batch, seq, chunks=8):
    C = GLA_CHUNK
    n = batch * seq
    nc = seq // C
    hp = GLA_HEADS // 2
    wq = 2 * GLA_DK
    wqk, wv = GLA_HEADS * GLA_DK, GLA_HEADS * GLA_DV
    tr = chunks * C

    def rows(width, first):
        return pl.BlockSpec((tr, width), lambda i, p: (i, first + p))

    qe, kd, sc, eb = pl.pallas_call(
        functools.partial(_gla_prep_kernel, chunks=chunks),
        out_shape=(jax.ShapeDtypeStruct((n, wqk), BF16), jax.ShapeDtypeStruct((n, wqk), BF16),
                   jax.ShapeDtypeStruct((n, GLA_HEADS * C), BF16), jax.ShapeDtypeStruct((n // C, wqk), F32)),
        grid=(n // tr, hp),
        in_specs=[rows(wq, 0), rows(wq, hp),
                  pl.BlockSpec((tr, GLA_RANK), lambda i, p: (i, 0)),
                  pl.BlockSpec((None, GLA_RANK, wq), lambda i, p: (lead, 0, p)),
                  pl.BlockSpec((None, 1, wq), lambda i, p: (lead, 0, p))],
        out_specs=(rows(wq, 0), rows(wq, 0), rows(2 * C, 0),
                   pl.BlockSpec((chunks, wq), lambda i, p: (i, p))),
        compiler_params=_cparams(("parallel", "parallel")),
        name="gla_prep",
    )(ha, ha, glr, w_gate_up, b_gate.reshape(-1, 1, wqk))

    def per_chunk(width, first):
        return pl.BlockSpec((batch, C, width), lambda c: (0, c, first))

    eb = eb.reshape(batch, nc, wqk).transpose(1, 0, 2)
    ha3 = ha.reshape(batch, seq, -1)
    out = pl.pallas_call(
        _gla_scan_kernel,
        out_shape=jax.ShapeDtypeStruct((batch, seq, mix.shape[1]), mix.dtype),
        grid=(nc,),
        in_specs=[per_chunk(wqk, 0), per_chunk(wqk, 0), per_chunk(GLA_HEADS * C, 0),
                  pl.BlockSpec((None, batch, wqk), lambda c: (c, 0, 0)),
                  per_chunk(wv, 1), per_chunk(wv, 2),
                  pl.BlockSpec((None, 1, GLA_DV), lambda c: (lead, 0, 0)),
                  pl.BlockSpec(memory_space=pl.ANY)],
        out_specs=per_chunk(wv, MIX_GLA // wv),
        scratch_shapes=[pltpu.VMEM((batch, GLA_HEADS, GLA_DV, GLA_DK), F32)],
        input_output_aliases={7: 0},
        compiler_params=_cparams(("arbitrary",)),
        name="gla_scan",
    )(qe.reshape(batch, seq, wqk), kd.reshape(batch, seq, wqk), sc.reshape(batch, seq, GLA_HEADS * C), eb,
      ha3, ha3, gla_norm_g.reshape(-1, 1, GLA_DV), mix.reshape(batch, seq, -1))
    return out.reshape(mix.shape)


def _diff_kernel(lq1_ref, lk1_ref, lq2_ref, lk2_ref, q_ref, k_ref, vt_ref, g_ref, mix_ref, o_ref,
                 qs_ref, m_ref, l_ref, acc_ref, *, tq, lambda_init):
    del mix_ref
    qi = pl.program_id(2)
    NT = (((1,), (1,)), ((), ()))
    hw = 2 * DIFF_D
    heads = q_ref.shape[1] // hw

    lane = lax.broadcasted_iota(jnp.int32, (tq, hw), 1)
    for hh in range(heads):
        q = q_ref[:, hw * hh:hw * (hh + 1)].astype(F32) * (DIFF_D ** -0.5)
        qs_ref[hh, :tq] = jnp.where(lane < DIFF_D, q, 0.0).astype(BF16)
        qs_ref[hh, tq:] = jnp.where(lane >= DIFF_D, q, 0.0).astype(BF16)
    m_ref[...] = jnp.full_like(m_ref, -jnp.inf)
    l_ref[...] = jnp.zeros_like(l_ref)
    acc_ref[...] = jnp.zeros_like(acc_ref)

    def tile(j, masked):
        r0 = pl.multiple_of(j * tq, tq)
        for hh in range(heads):
            hs = slice(hw * hh, hw * (hh + 1))
            s = lax.dot_general(k_ref[pl.ds(r0, tq), hs], qs_ref[hh], NT, preferred_element_type=F32)
            if masked:
                r = lax.broadcasted_iota(jnp.int32, s.shape, 0)
                c = lax.broadcasted_iota(jnp.int32, s.shape, 1)
                s = jnp.where(r <= jnp.where(c >= tq, c - tq, c), s, -jnp.inf)
            m_prev = m_ref[hh]
            m_new = jnp.maximum(m_prev, jnp.max(s, axis=0, keepdims=True))
            alpha = jnp.exp(m_prev - m_new)
            p = jnp.exp(s - m_new)
            l_ref[hh] = alpha * l_ref[hh] + jnp.sum(p, axis=0, keepdims=True)
            acc_ref[hh] = alpha * acc_ref[hh] + jnp.dot(vt_ref[hs, pl.ds(r0, tq)], p.astype(BF16),
                                                      preferred_element_type=F32)
            m_ref[hh] = m_new

    def full_tile(j, carry):
        tile(j, False)
        return carry

    lax.fori_loop(0, qi, full_tile, 0)
    tile(qi, True)

    lam = (jnp.exp(jnp.sum(lq1_ref[...] * lk1_ref[...], axis=-1, keepdims=True))
           - jnp.exp(jnp.sum(lq2_ref[...] * lk2_ref[...], axis=-1, keepdims=True)) + lambda_init)
    for hh in range(heads):
        o = acc_ref[hh] / l_ref[hh]
        o = o[:, :tq] - lam * o[:, tq:]
        ms = jnp.mean(o * o, axis=0, keepdims=True)
        o = o * lax.rsqrt(ms + LN_EPS) * g_ref[...] * (1.0 - lambda_init)
        o_ref[:, hw * hh:hw * (hh + 1)] = o.T.astype(BF16)


def _diff_attention(dq, dk, dvt, lq1, lk1, lq2, lk2, norm_g, mix, lead, *, batch, seq, lambda_init, tq, heads=2):
    nq = seq // tq
    hw = 2 * DIFF_D
    pw = heads * hw
    lam_spec = pl.BlockSpec((None, 1, DIFF_D), lambda b, p, qi: (lead, 0, 0))
    lq1, lk1, lq2, lk2 = (t.reshape(-1, 1, DIFF_D) for t in (lq1, lk1, lq2, lk2))
    return pl.pallas_call(
        functools.partial(_diff_kernel, tq=tq, lambda_init=lambda_init),
        out_shape=jax.ShapeDtypeStruct(mix.shape, mix.dtype),
        grid=(batch, DIFF_HEADS // heads, nq),
        in_specs=[lam_spec] * 4 + [
            pl.BlockSpec((tq, pw), lambda b, p, qi: (b * nq + qi, p)),
            pl.BlockSpec((seq, pw), lambda b, p, qi: (b, p)),
            pl.BlockSpec((None, pw, seq), lambda b, p, qi: (b, p, 0)),
            pl.BlockSpec((None, hw, 1), lambda b, p, qi: (lead, 0, 0)),
            pl.BlockSpec(memory_space=pl.ANY)],
        out_specs=pl.BlockSpec((tq, pw), lambda b, p, qi: (b * nq + qi, MIX_DIFF // pw + p)),
        scratch_shapes=[pltpu.VMEM((heads, 2 * tq, hw), BF16), pltpu.VMEM((heads, 1, 2 * tq), F32),
                        pltpu.VMEM((heads, 1, 2 * tq), F32), pltpu.VMEM((heads, hw, 2 * tq), F32)],
        input_output_aliases={8: 0},
        compiler_params=_cparams(("parallel", "parallel", "arbitrary")),
        name="diff_attn",
    )(lq1, lk1, lq2, lk2, dq, dk, dvt, norm_g.reshape(-1, hw, 1), mix)


def _dsa_kernel(iq_ref, sq_ref, ikd_ref, wt_ref, sk_ref, svt_ref, mix_ref, o_ref,
                iqm_ref, keys_ref, bias_ref, q5_ref, m_ref, l_ref, acc_ref, *, tq, tk, seq, k_sel):
    del mix_ref
    qi = pl.program_id(1)
    NT = (((1,), (1,)), ((), ()))
    n_tiles = (qi * tq + tq + tk - 1) // tk
    qpos = qi * tq + lax.broadcasted_iota(jnp.int32, (1, tq), 1)
    krow = lax.broadcasted_iota(jnp.int32, (tk, tq), 0)
    idx_scale = (IDX_DIM ** -0.5) * (IDX_HEADS ** -0.5)
    neg_inf = F32(-jnp.inf)

    def tile_start(j):
        return pl.multiple_of(j * tk, tk)

    lane = lax.broadcasted_iota(jnp.int32, (tq, 128), 1)
    for t in range(IDX_HEADS // 2):
        a = iq_ref[:, 128 * t:128 * (t + 1)]
        zero = jnp.zeros_like(a)
        iqm_ref[t, :tq] = jnp.where(lane < IDX_DIM, a, zero)
        iqm_ref[t, tq:] = jnp.where(lane >= IDX_DIM, a, zero)
    wt = wt_ref[...]

    def score_tile(j, carry):
        r0 = tile_start(j)
        kd = ikd_ref[pl.ds(r0, tk), :]
        acc = jnp.zeros((tk, 2 * tq), F32)
        for t in range(IDX_HEADS // 2):
            lg = lax.dot_general(kd, iqm_ref[t], NT, preferred_element_type=F32)
            w2 = jnp.concatenate([wt[2 * t:2 * t + 1], wt[2 * t + 1:2 * t + 2]], axis=1)
            acc = acc + jnp.maximum(lg, 0.0) * w2
        acc = acc[:, :tq] + acc[:, tq:]
        bits = pltpu.bitcast(acc * idx_scale, jnp.int32)
        key = jnp.where(bits < 0, (bits ^ 0x7FFFFFFF) + 1, bits)
        keys_ref[pl.ds(r0, tk), :] = jnp.where(r0 + krow <= qpos, key, INT_MIN)
        return carry

    lax.fori_loop(0, n_tiles, score_tile, 0)

    def count_ones(ones):
        def body(j, cnt):
            r0 = tile_start(j)
            return cnt + jnp.sum(ones(keys_ref[pl.ds(r0, tk), :], r0 + krow), axis=0, keepdims=True)
        return lax.fori_loop(0, n_tiles, body, jnp.zeros((1, tq), jnp.int32))

    def count(pred):
        return count_ones(lambda kk, kp: jnp.where(pred(kk, kp), 1, 0))

    def thr_bit(i, t_u):
        cand_u = t_u | jnp.left_shift(jnp.int32(1), 31 - i)
        cand = cand_u ^ INT_MIN
        return jnp.where(count(lambda kk, kp: kk >= cand) >= k_sel, cand_u, t_u)

    thr = lax.fori_loop(0, 32, thr_bit, jnp.zeros((1, tq), jnp.int32)) ^ INT_MIN
    need = k_sel - count(lambda kk, kp: kk > thr)
    n_eq = count(lambda kk, kp: kk == thr)
    excess = jnp.where(thr == INT_MIN, 0, jnp.where(n_eq > need, 1, 0))

    def tie_search():
        def idx_bit(i, j_lo):
            cand = j_lo | jnp.left_shift(jnp.int32(1), (seq - 1).bit_length() - 1 - i)
            c = count_ones(lambda kk, kp: jnp.where(kk == thr, jnp.where(kp < cand, 1, 0), 0))
            return jnp.where(c < need, cand, j_lo)
        return lax.fori_loop(0, (seq - 1).bit_length(), idx_bit, jnp.zeros((1, tq), jnp.int32))

    j_star = lax.cond(jnp.max(excess) > 0, tie_search, lambda: jnp.full((1, tq), seq, jnp.int32))
    j_star = jnp.where(thr == INT_MIN, -1, jnp.where(excess > 0, j_star, seq))

    def bias_tile(j, carry):
        r0 = tile_start(j)
        kk = keys_ref[pl.ds(r0, tk), :]
        tie = jnp.where(r0 + krow <= j_star, 0.0, neg_inf)
        bias_ref[pl.ds(r0, tk), :] = jnp.where(kk > thr, 0.0, jnp.where(kk == thr, tie, neg_inf))
        return carry

    lax.fori_loop(0, n_tiles, bias_tile, 0)

    scale = DSA_DH ** -0.5
    rep = DSA_HEADS // DSA_KV
    for g in range(DSA_KV):
        for i in range(rep):
            h = g * rep + i
            q5_ref[g, tq * i:tq * (i + 1)] = sq_ref[:, DSA_DH * h:DSA_DH * (h + 1)]
    m_ref[...] = jnp.full_like(m_ref, neg_inf)
    l_ref[...] = jnp.zeros_like(l_ref)
    acc_ref[...] = jnp.zeros_like(acc_ref)

    def attn_tile(j, carry):
        r0 = tile_start(j)
        bias = bias_ref[pl.ds(r0, tk), :]
        bias = jnp.concatenate([bias] * rep, axis=1)
        for g in range(DSA_KV):
            gs = slice(DSA_DH * g, DSA_DH * (g + 1))
            s = lax.dot_general(sk_ref[pl.ds(r0, tk), gs], q5_ref[g], NT, preferred_element_type=F32)
            s = s * scale + bias
            m_prev = m_ref[g]
            m_new = jnp.maximum(m_prev, jnp.max(s, axis=0, keepdims=True))
            m_safe = jnp.where(m_new == neg_inf, 0.0, m_new)
            alpha = jnp.exp(m_prev - m_safe)
            p = jnp.exp(s - m_safe)
            l_ref[g] = alpha * l_ref[g] + jnp.sum(p, axis=0, keepdims=True)
            acc_ref[g] = alpha * acc_ref[g] + jnp.dot(svt_ref[gs, pl.ds(r0, tk)], p.astype(BF16),
                                                    preferred_element_type=F32)
            m_ref[g] = m_new
        return carry

    lax.fori_loop(0, n_tiles, attn_tile, 0)
    for g in range(DSA_KV):
        out_t = acc_ref[g] / l_ref[g]
        for i in range(rep):
            h = g * rep + i
            o_ref[:, DSA_DH * h:DSA_DH * (h + 1)] = out_t[:, tq * i:tq * (i + 1)].T.astype(BF16)


def _dsa(iq, sq, ikd, wt, sk, svt, mix, *, batch, seq, tq, tk):
    nq = seq // tq
    k_sel = min(IDX_TOPK, seq // 4)
    rep = DSA_HEADS // DSA_KV
    return pl.pallas_call(
        functools.partial(_dsa_kernel, tq=tq, tk=tk, seq=seq, k_sel=k_sel),
        out_shape=jax.ShapeDtypeStruct(mix.shape, mix.dtype),
        grid=(batch, nq),
        in_specs=[pl.BlockSpec((tq, W_IQ), lambda b, qi: (b * nq + qi, 0)),
                  pl.BlockSpec((tq, W_SQ), lambda b, qi: (b * nq + qi, 0)),
                  pl.BlockSpec((seq, 128), lambda b, qi: (b, 0)),
                  pl.BlockSpec((None, IDX_HEADS, tq), lambda b, qi: (b, 0, qi)),
                  pl.BlockSpec((seq, W_SKV), lambda b, qi: (b, 0)),
                  pl.BlockSpec((None, W_SKV, seq), lambda b, qi: (b, 0, 0)),
                  pl.BlockSpec(memory_space=pl.ANY)],
        out_specs=pl.BlockSpec((tq, W_SQ), lambda b, qi: (b * nq + qi, MIX_DSA // W_SQ)),
        scratch_shapes=[pltpu.VMEM((IDX_HEADS // 2, 2 * tq, 128), BF16), pltpu.VMEM((seq, tq), jnp.int32),
                        pltpu.VMEM((seq, tq), F32), pltpu.VMEM((DSA_KV, rep * tq, DSA_DH), BF16),
                        pltpu.VMEM((DSA_KV, 1, rep * tq), F32), pltpu.VMEM((DSA_KV, 1, rep * tq), F32),
                        pltpu.VMEM((DSA_KV, DSA_DH, rep * tq), F32)],
        input_output_aliases={6: 0},
        compiler_params=_cparams(("parallel", "arbitrary")),
        name="dsa",
    )(iq, sq, ikd, wt, sk, svt, mix)


def _router_kernel(x_ref, w_ref, b_ref, e_ref, g_ref, xp_ref):
    x = x_ref[...]
    logits = jnp.dot(x, w_ref[...], precision=HIGHEST, preferred_element_type=F32) + b_ref[...]
    lane = lax.broadcasted_iota(jnp.int32, logits.shape, 1)
    neg_inf = F32(-jnp.inf)
    lg = jnp.where(lane < MOE_E, logits, neg_inf)
    m1 = jnp.max(lg, axis=-1, keepdims=True)
    i1 = jnp.min(jnp.where(lg == m1, lane, 128), axis=-1, keepdims=True)
    lg2 = jnp.where(lane == i1, neg_inf, lg)
    m2 = jnp.max(lg2, axis=-1, keepdims=True)
    i2 = jnp.min(jnp.where(lg2 == m2, lane, 128), axis=-1, keepdims=True)
    e21 = jnp.exp(m2 - m1)
    g1 = 1.0 / (1.0 + e21)
    e_ref[...] = jnp.where(lane == 0, i1, jnp.where(lane == 1, i2, 0))
    g_ref[...] = jnp.where(lane == 0, g1, jnp.where(lane == 1, e21 * g1, 0.0))
    half = x.shape[1] // 2
    lo = pltpu.bitcast(x[:, :half].astype(BF16).astype(F32), jnp.uint32)
    hi = pltpu.bitcast(x[:, half:].astype(BF16).astype(F32), jnp.uint32)
    xp_ref[...] = (hi & jnp.uint32(0xFFFF0000)) | (lo >> 16)


def _router(x, router_w, router_b, *, tm=256):
    n, d = x.shape
    row = lambda w: pl.BlockSpec((tm, w), lambda i: (i, 0))
    return pl.pallas_call(
        _router_kernel,
        out_shape=(jax.ShapeDtypeStruct((n, 128), jnp.int32), jax.ShapeDtypeStruct((n, 128), F32),
                   jax.ShapeDtypeStruct((n, d // 2), jnp.uint32)),
        grid=(n // tm,),
        in_specs=[row(d), pl.BlockSpec((d, 128), lambda i: (0, 0)), pl.BlockSpec((1, 128), lambda i: (0, 0))],
        out_specs=(row(128), row(128), row(d // 2)),
        compiler_params=_cparams(("parallel",)),
        name="router",
    )(x, router_w, router_b)


def _dispatch_kernel(pos_ref, xp_ref, buf_ref, o_ref, sem, *, tb):
    del buf_ref
    base = pl.program_id(0) * tb

    def row_copy(i, s):
        return pltpu.make_async_copy(xp_ref.at[pl.ds(i, 1)], o_ref.at[pl.ds(pos_ref[2 * (base + i) + s], 1)], sem)

    def start(i, c):
        row_copy(i, 0).start()
        row_copy(i, 1).start()
        return c

    def wait(i, c):
        row_copy(i, 0).wait()
        row_copy(i, 1).wait()
        return c

    lax.fori_loop(0, tb, start, 0)
    lax.fori_loop(0, tb, wait, 0)


def _dispatch(pos, xp, buf, *, tb=256):
    n, w = xp.shape
    return pl.pallas_call(
        functools.partial(_dispatch_kernel, tb=tb),
        out_shape=jax.ShapeDtypeStruct(buf.shape, buf.dtype),
        grid_spec=pltpu.PrefetchScalarGridSpec(
            num_scalar_prefetch=1,
            grid=(n // tb,),
            in_specs=[pl.BlockSpec((tb, w), lambda i, pos: (i, 0)), pl.BlockSpec(memory_space=pl.ANY)],
            out_specs=pl.BlockSpec(memory_space=pl.ANY),
            scratch_shapes=[pltpu.SemaphoreType.DMA],
        ),
        input_output_aliases={2: 0},
        compiler_params=_cparams(("arbitrary",)),
        name="moe_dispatch",
    )(pos, xp, buf)


def _unpack_rows(word):
    lo = pltpu.bitcast(word << 16, F32).astype(BF16)
    hi = pltpu.bitcast(word & jnp.uint32(0xFFFF0000), F32).astype(BF16)
    return lo, hi


def _moe_up_kernel(be_ref, bf_ref, na_ref, a_ref, w1_ref, w3_ref, o_ref, w1b_ref, w3b_ref):
    del be_ref
    i = pl.program_id(1)

    @pl.when(i < na_ref[0])
    def _():
        @pl.when(bf_ref[i] == 1)
        def _():
            w1b_ref[...] = w1_ref[...].astype(BF16)
            w3b_ref[...] = w3_ref[...].astype(BF16)

        lo, hi = _unpack_rows(a_ref[...])
        half = lo.shape[1]
        g = (jnp.dot(lo, w1b_ref[:half], preferred_element_type=F32)
             + jnp.dot(hi, w1b_ref[half:], preferred_element_type=F32))
        u = (jnp.dot(lo, w3b_ref[:half], preferred_element_type=F32)
             + jnp.dot(hi, w3b_ref[half:], preferred_element_type=F32))
        o_ref[...] = (g * jax.nn.sigmoid(g) * u).astype(BF16)

    @pl.when(i >= na_ref[0])
    def _():
        o_ref[...] = jnp.zeros_like(o_ref)


def _moe_down_kernel(be_ref, bf_ref, na_ref, h_ref, w2_ref, o_ref, w2b_ref):
    del be_ref
    i = pl.program_id(1)

    @pl.when(i < na_ref[0])
    def _():
        @pl.when(bf_ref[i] == 1)
        def _():
            w2b_ref[...] = w2_ref[...].astype(BF16)

        o_ref[...] = jnp.dot(h_ref[...], w2b_ref[...], preferred_element_type=F32)

    @pl.when(i >= na_ref[0])
    def _():
        o_ref[...] = jnp.zeros_like(o_ref)


def _moe_ffn(block_e, block_first, n_active, a_sorted, w1, w3, w2, lead, *, tm, tn):
    r, half = a_sorted.shape
    d = 2 * half
    f = w1.shape[-1]
    nb = r // tm

    def rows(i, na):
        return jnp.minimum(i, na[0] - 1)

    h = pl.pallas_call(
        _moe_up_kernel,
        out_shape=jax.ShapeDtypeStruct((r, f), BF16),
        grid_spec=pltpu.PrefetchScalarGridSpec(
            num_scalar_prefetch=3,
            grid=(f // tn, nb),
            in_specs=[pl.BlockSpec((tm, half), lambda n, i, be, bf, na: (rows(i, na), 0)),
                      pl.BlockSpec((None, None, d, tn), lambda n, i, be, bf, na: (lead, be[i], 0, n)),
                      pl.BlockSpec((None, None, d, tn), lambda n, i, be, bf, na: (lead, be[i], 0, n))],
            out_specs=pl.BlockSpec((tm, tn), lambda n, i, be, bf, na: (i, n)),
            scratch_shapes=[pltpu.VMEM((d, tn), BF16), pltpu.VMEM((d, tn), BF16)],
        ),
        compiler_params=_cparams(("arbitrary", "arbitrary")),
        name="moe_up",
    )(block_e, block_first, n_active, a_sorted, w1, w3)
    return pl.pallas_call(
        _moe_down_kernel,
        out_shape=jax.ShapeDtypeStruct((r, d), F32),
        grid_spec=pltpu.PrefetchScalarGridSpec(
            num_scalar_prefetch=3,
            grid=(d // tn, nb),
            in_specs=[pl.BlockSpec((tm, f), lambda n, i, be, bf, na: (rows(i, na), 0)),
                      pl.BlockSpec((None, None, f, tn), lambda n, i, be, bf, na: (lead, be[i], 0, n))],
            out_specs=pl.BlockSpec((tm, tn), lambda n, i, be, bf, na: (i, n)),
            scratch_shapes=[pltpu.VMEM((f, tn), BF16)],
        ),
        compiler_params=_cparams(("arbitrary", "arbitrary")),
        name="moe_down",
    )(block_e, block_first, n_active, h, w2)


def _combine_ln_kernel(pos_ref, x_ref, gate_ref, g_ref, b_ref, y_ref, o_ref, ybuf_ref, sem, *, tb, alpha):
    step, n_steps = pl.program_id(0), pl.num_programs(0)
    slot = step % 2

    def row_copy(blk, buf, i, s):
        return pltpu.make_async_copy(y_ref.at[pl.ds(pos_ref[2 * (blk * tb + i) + s], 1)],
                                     ybuf_ref.at[buf, s, pl.ds(i, 1)], sem.at[buf])

    def start_block(blk, buf):
        def body(i, c):
            row_copy(blk, buf, i, 0).start()
            row_copy(blk, buf, i, 1).start()
            return c
        lax.fori_loop(0, tb, body, 0)

    def wait_block(blk, buf):
        def body(i, c):
            row_copy(blk, buf, i, 0).wait()
            row_copy(blk, buf, i, 1).wait()
            return c
        lax.fori_loop(0, tb, body, 0)

    @pl.when(step == 0)
    def _():
        start_block(0, 0)

    @pl.when(step + 1 < n_steps)
    def _():
        start_block(step + 1, 1 - slot)

    wait_block(step, slot)
    gate = gate_ref[...]
    f = ybuf_ref[slot, 0] * gate[:, 0:1] + ybuf_ref[slot, 1] * gate[:, 1:2]
    z = alpha * x_ref[...] + f
    mu = jnp.mean(z, axis=-1, keepdims=True)
    zc = z - mu
    var = jnp.mean(zc * zc, axis=-1, keepdims=True)
    o_ref[...] = zc * lax.rsqrt(var + LN_EPS) * g_ref[...] + b_ref[...]


def _combine_ln(pos, x, gates, g, b, y, lead, *, alpha, tb=128):
    n, d = x.shape
    par = pl.BlockSpec((None, 1, d), lambda i, pos: (lead, 0, 0))
    return pl.pallas_call(
        functools.partial(_combine_ln_kernel, tb=tb, alpha=alpha),
        out_shape=jax.ShapeDtypeStruct((n, d), F32),
        grid_spec=pltpu.PrefetchScalarGridSpec(
            num_scalar_prefetch=1,
            grid=(n // tb,),
            in_specs=[pl.BlockSpec((tb, d), lambda i, pos: (i, 0)),
                      pl.BlockSpec((tb, 128), lambda i, pos: (i, 0)),
                      par, par, pl.BlockSpec(memory_space=pl.ANY)],
            out_specs=pl.BlockSpec((tb, d), lambda i, pos: (i, 0)),
            scratch_shapes=[pltpu.VMEM((2, 2, tb, d), F32), pltpu.SemaphoreType.DMA((2,))],
        ),
        compiler_params=_cparams(("arbitrary",)),
        name="moe_combine_ln",
    )(pos, x, gates, g.reshape(-1, 1, d), b.reshape(-1, 1, d), y)


def _moe_layer(x, router_w, router_b, w1, w3, w2, ln_g, ln_b, lead_moe, lead_ln, *, alpha, tm=512, tn=512):
    n, d = x.shape
    rw = jnp.pad(router_w[lead_moe], ((0, 0), (0, 128 - MOE_E)))
    rb = jnp.pad(router_b[lead_moe], (0, 128 - MOE_E)).reshape(1, 128)
    top_e, gates, xp = _router(x, rw, rb)
    flat_e = top_e[:, :MOE_TOPK].reshape(-1)
    onehot = (flat_e[:, None] == jnp.arange(MOE_E, dtype=jnp.int32)[None, :]).astype(jnp.int32)
    rank = jnp.sum((jnp.cumsum(onehot, axis=0) - onehot) * onehot, axis=1)
    counts = jnp.sum(onehot, axis=0)
    padded = (counts + tm - 1) // tm * tm
    end_padded = jnp.cumsum(padded)
    start_padded = end_padded - padded
    pos = (start_padded[flat_e] + rank).astype(jnp.int32)
    nb = -(-(n * MOE_TOPK + MOE_E * (tm - 1)) // tm)
    block_start = jnp.arange(nb, dtype=jnp.int32) * tm
    block_e = jnp.sum((block_start[:, None] >= end_padded[None, :]).astype(jnp.int32), axis=1)
    block_e = jnp.minimum(block_e, MOE_E - 1)
    n_active = (end_padded[-1] // tm).astype(jnp.int32).reshape(1)
    block_first = jnp.concatenate([jnp.ones((1,), jnp.int32), (block_e[1:] != block_e[:-1]).astype(jnp.int32)])
    a_sorted = _dispatch(pos, xp, jnp.zeros((nb * tm, d // 2), jnp.uint32))
    y = _moe_ffn(block_e, block_first, n_active, a_sorted, w1, w3, w2, lead_moe, tm=tm, tn=tn)
    return _combine_ln(pos, x, gates, ln_g, ln_b, y, lead_ln, alpha=alpha)


def kernel(x, positions, w_in, w_gate_up, b_gate, gla_norm_g, lambda_q1, lambda_k1, lambda_q2, lambda_k2,
           diff_norm_g, w_out, ln1_g, ln1_b, ln2_g, ln2_b, ffn_w1, ffn_w3, ffn_w2, router_w, router_b,
           moe_w1, moe_w3, moe_w2):
    batch, seq, d = x.shape
    n = batch * seq
    depth = w_in.shape[0]
    alpha = (2 * depth) ** 0.25
    xf = x.reshape(n, d)
    xb = xf.astype(BF16)
    pos = positions.reshape(n)
    tab_a, half_a = _rope_tables(pos, DIFF_D)
    tab_b, half_b = _rope_tables(pos, DSA_DH)
    plain = lambda i, j, k: (i, k)
    w_in_t = jnp.swapaxes(w_in, 1, 2)
    for l in range(depth):
        ha = _matmul(xb, w_in_t, n_out=W_GLA, k_steps=1, tm=1024, tn=HM_TN, tk=d, out_dtype=BF16,
                     a_map=plain, w_map=lambda i, j, k, l=l: (l, j, k), w_lead=True, w_t=True,
                     name="in_proj_gla")
        hm = _matmul(xb, w_in_t, n_out=HM_TILES * HM_TN, k_steps=1, tm=1024, tn=HM_TN, tk=d,
                     out_dtype=BF16, a_map=plain,
                     w_map=lambda i, j, k, l=l: (l, HM_BASE // HM_TN + j, k), w_lead=True, w_t=True,
                     n_valid=N_IN, col_start=lambda j: HM_BASE + j * HM_TN, name="in_proj_attn")
        small_tiles = (OFF_GLR // 128, OFF_IK // 128)
        pick = lambda j: jnp.where(j == 0, small_tiles[0], small_tiles[1])
        hs = _matmul(xb, w_in_t, n_out=256, k_steps=1, tm=1024, tn=128, tk=d, out_dtype=F32,
                     a_map=plain, w_map=lambda i, j, k, l=l: (l, pick(j), k), w_lead=True, w_t=True,
                     n_valid=N_IN, col_start=lambda j: pick(j) * 128, name="in_proj_small")
        glr = hs[:, OFF_GLR % 128:OFF_GLR % 128 + GLA_RANK]
        iw0 = 128 + OFF_IW % 128
        wt = hs[:, iw0:iw0 + IDX_HEADS].reshape(batch, seq, IDX_HEADS).transpose(0, 2, 1)
        dq, dk, dv, sq, sk, sv, iq, ikd = _split_rope(hm, hs, tab_a, half_a, tab_b, half_b)
        svt = sv.reshape(batch, seq, W_SKV).transpose(0, 2, 1)
        dvt = dv.reshape(batch, seq, W_DIFF).transpose(0, 2, 1)

        mix = jnp.zeros((n, W_MIX), BF16)
        mix = _gla(ha, glr, w_gate_up, b_gate, gla_norm_g, mix, l, batch=batch, seq=seq)
        lambda_init = 0.8 - 0.6 * math.exp(-0.3 * l)
        mix = _diff_attention(dq, dk, dvt, lambda_q1, lambda_k1, lambda_q2, lambda_k2, diff_norm_g, mix, l,
                              batch=batch, seq=seq, lambda_init=lambda_init, tq=1024, heads=1)
        mix = _dsa(iq, sq, ikd, wt, sk, svt, mix, batch=batch, seq=seq, tq=256, tk=512)

        proj = _out_proj(mix, w_out, l)
        xf, xb = _add_ln(xf, proj, ln1_g, ln1_b, l, alpha=alpha)

        j = l // 2
        if l % 2 == 0:
            f_dim = ffn_w2.shape[1]
            f_pad = -(-f_dim // 1024) * 1024
            hid = _swiglu_up(xb, ffn_w1, ffn_w3, j, tm=1024, tn=256, tk=d, f_out=f_pad)
            down = _matmul(hid, ffn_w2, n_out=d, k_steps=8, tm=2048, tn=1024, tk=f_pad // 8, out_dtype=F32,
                           a_map=plain, w_map=lambda i, j_, k, j=j: (j, k, j_), w_lead=True, k_valid=f_dim,
                           name="ffn_down")
            xf, xb = _add_ln(xf, down, ln2_g, ln2_b, l, alpha=alpha)
        else:
            xf = _moe_layer(xf, router_w, router_b, moe_w1, moe_w3, moe_w2, ln2_g, ln2_b, j, l, alpha=alpha)
            xb = xf.astype(BF16)
    return xf.reshape(batch, seq, d)
```

```python
import functools
import math

import jax
import jax.numpy as jnp
from jax import lax
from jax.experimental import pallas as pl
from jax.experimental.pallas import tpu as pltpu

F32 = jnp.float32
BF16 = jnp.bfloat16
HIGHEST = lax.Precision.HIGHEST
INT_MIN = -(2 ** 31)

ROPE_THETA = 500000.0
GLA_HEADS, GLA_DK, GLA_DV, GLA_RANK, GLA_TAU, GLA_CHUNK = 4, 192, 384, 16, 16.0, 64
GLA_SUB = 16
DIFF_HEADS, DIFF_D = 10, 64
DSA_HEADS, DSA_KV, DSA_DH = 10, 2, 128
IDX_HEADS, IDX_DIM, IDX_TOPK = 32, 64, 256
MOE_E, MOE_TOPK = 8, 2
LN_EPS = 1e-5

W_GLA = 2 * GLA_HEADS * GLA_DK + 2 * GLA_HEADS * GLA_DV
OFF_GLR = W_GLA
OFF_DQ = OFF_GLR + GLA_RANK
W_DIFF = DIFF_HEADS * 2 * DIFF_D
OFF_DK, OFF_DV = OFF_DQ + W_DIFF, OFF_DQ + 2 * W_DIFF
OFF_SQ = OFF_DQ + 3 * W_DIFF
W_SQ, W_SKV = DSA_HEADS * DSA_DH, DSA_KV * DSA_DH
OFF_SK, OFF_SV = OFF_SQ + W_SQ, OFF_SQ + W_SQ + W_SKV
OFF_IQ = OFF_SV + W_SKV
W_IQ = IDX_HEADS * IDX_DIM
OFF_IK = OFF_IQ + W_IQ
OFF_IW = OFF_IK + IDX_DIM
N_IN = OFF_IW + IDX_HEADS

HM_BASE = W_GLA
HM_TN = 512
HM_TILES = -(-(N_IN - HM_BASE) // HM_TN)

MIX_GLA, MIX_DIFF, MIX_DSA = 0, 2560, 3840
W_MIX = MIX_DSA + DSA_HEADS * DSA_DH

VMEM_LIMIT = 56 * 1024 * 1024


def _cparams(sem, vmem=VMEM_LIMIT):
    return pltpu.CompilerParams(dimension_semantics=sem, vmem_limit_bytes=vmem)


def _mm_kernel(a_ref, w_ref, o_ref, acc_ref=None, *, k_valid, n_valid, col_start, w_t):
    k = pl.program_id(2)
    k_ax, n_ax = (1, 0) if w_t else (0, 1)

    def partial_product():
        w = w_ref[...]
        if k_valid is not None:
            kk = k * w.shape[k_ax] + lax.broadcasted_iota(jnp.int32, w.shape, k_ax)
            w = jnp.where(kk < k_valid, w, 0.0)
        if n_valid is not None:
            col = col_start(pl.program_id(1)) + lax.broadcasted_iota(jnp.int32, w.shape, n_ax)
            w = jnp.where(col < n_valid, w, 0.0)
        return lax.dot_general(a_ref[...], w.astype(BF16), (((1,), (k_ax,)), ((), ())),
                               preferred_element_type=F32)

    if acc_ref is None:
        o_ref[...] = partial_product().astype(o_ref.dtype)
        return

    @pl.when(k == 0)
    def _():
        acc_ref[...] = partial_product()

    @pl.when(k > 0)
    def _():
        acc_ref[...] += partial_product()

    @pl.when(k == pl.num_programs(2) - 1)
    def _():
        o_ref[...] = acc_ref[...].astype(o_ref.dtype)


def _matmul(a, w, *, n_out, k_steps, tm, tn, tk, out_dtype, a_map, w_map, w_lead=False, k_valid=None,
            n_valid=None, col_start=None, w_t=False, name="mm"):
    m = a.shape[0]
    w_block = (tn, tk) if w_t else (tk, tn)
    if w_lead:
        w_block = (None,) + w_block
    return pl.pallas_call(
        functools.partial(_mm_kernel, k_valid=k_valid, n_valid=n_valid, col_start=col_start, w_t=w_t),
        out_shape=jax.ShapeDtypeStruct((m, n_out), out_dtype),
        grid=(m // tm, pl.cdiv(n_out, tn), k_steps),
        in_specs=[pl.BlockSpec((tm, tk), a_map), pl.BlockSpec(w_block, w_map)],
        out_specs=pl.BlockSpec((tm, tn), lambda i, j, k: (i, j)),
        scratch_shapes=[pltpu.VMEM((tm, tn), F32)] if k_steps > 1 else [],
        compiler_params=_cparams(("parallel", "parallel", "arbitrary")),
        name=name,
    )(a, w)


def _swiglu_kernel(a_ref, w1_ref, w3_ref, o_ref, acc1_ref=None, acc3_ref=None, *, n_valid):
    k = pl.program_id(2)

    def partial_product(w_ref):
        return jnp.dot(a_ref[...], w_ref[...].astype(BF16), preferred_element_type=F32)

    def finish(g, u):
        h = g * jax.nn.sigmoid(g) * u
        col = pl.program_id(1) * h.shape[1] + lax.broadcasted_iota(jnp.int32, h.shape, 1)
        o_ref[...] = jnp.where(col < n_valid, h, 0.0).astype(o_ref.dtype)

    if acc1_ref is None:
        finish(partial_product(w1_ref), partial_product(w3_ref))
        return

    @pl.when(k == 0)
    def _():
        acc1_ref[...] = partial_product(w1_ref)
        acc3_ref[...] = partial_product(w3_ref)

    @pl.when(k > 0)
    def _():
        acc1_ref[...] += partial_product(w1_ref)
        acc3_ref[...] += partial_product(w3_ref)

    @pl.when(k == pl.num_programs(2) - 1)
    def _():
        finish(acc1_ref[...], acc3_ref[...])


def _swiglu_up(a, w1, w3, lead, *, tm, tn, tk, f_out):
    m, kdim = a.shape
    f = w1.shape[-1]
    last = (f - 1) // tn
    wspec = pl.BlockSpec((None, tk, tn), lambda i, j, k: (lead, k, jnp.minimum(j, last)))
    k_steps = kdim // tk
    return pl.pallas_call(
        functools.partial(_swiglu_kernel, n_valid=f),
        out_shape=jax.ShapeDtypeStruct((m, f_out), BF16),
        grid=(m // tm, f_out // tn, k_steps),
        in_specs=[pl.BlockSpec((tm, tk), lambda i, j, k: (i, k)), wspec, wspec],
        out_specs=pl.BlockSpec((tm, tn), lambda i, j, k: (i, j)),
        scratch_shapes=[pltpu.VMEM((tm, tn), F32), pltpu.VMEM((tm, tn), F32)] if k_steps > 1 else [],
        compiler_params=_cparams(("parallel", "parallel", "arbitrary")),
        name="swiglu_up",
    )(a, w1, w3)


def _out_proj_kernel(mix_ref, w_ref, o_ref):
    n_gla = GLA_HEADS * GLA_DV
    w = w_ref[...].astype(BF16)
    o_ref[...] = (jnp.dot(mix_ref[:, MIX_GLA:MIX_GLA + n_gla], w[:n_gla], preferred_element_type=F32)
                  + jnp.dot(mix_ref[:, MIX_DIFF:], w[n_gla:], preferred_element_type=F32))


def _out_proj(mix, w_out, lead, *, tm=1024, tn=256):
    m, wm = mix.shape
    d = w_out.shape[-1]
    return pl.pallas_call(
        _out_proj_kernel,
        out_shape=jax.ShapeDtypeStruct((m, d), F32),
        grid=(m // tm, d // tn),
        in_specs=[pl.BlockSpec((tm, wm), lambda i, j: (i, 0)),
                  pl.BlockSpec((None, w_out.shape[1], tn), lambda i, j: (lead, 0, j))],
        out_specs=pl.BlockSpec((tm, tn), lambda i, j: (i, j)),
        compiler_params=_cparams(("parallel", "parallel")),
        name="out_proj",
    )(mix, w_out)


def _add_ln_kernel(x_ref, y_ref, g_ref, b_ref, o_ref, ob_ref, *, alpha):
    z = alpha * x_ref[...] + y_ref[...].astype(F32)
    mu = jnp.mean(z, axis=-1, keepdims=True)
    zc = z - mu
    var = jnp.mean(zc * zc, axis=-1, keepdims=True)
    out = zc * lax.rsqrt(var + LN_EPS) * g_ref[...] + b_ref[...]
    o_ref[...] = out
    ob_ref[...] = out.astype(BF16)


def _add_ln(x, y, g, b, lead, *, alpha, tm=256):
    m, d = x.shape
    row = pl.BlockSpec((tm, d), lambda i: (i, 0))
    par = pl.BlockSpec((None, 1, d), lambda i: (lead, 0, 0))
    g, b = g.reshape(-1, 1, d), b.reshape(-1, 1, d)
    return pl.pallas_call(
        functools.partial(_add_ln_kernel, alpha=alpha),
        out_shape=(jax.ShapeDtypeStruct((m, d), F32), jax.ShapeDtypeStruct((m, d), BF16)),
        grid=(m // tm,),
        in_specs=[row, row, par, par],
        out_specs=(row, row),
        compiler_params=_cparams(("parallel",)),
        name="add_ln",
    )(x, y, g, b)


def _rope_tables(positions, width):
    rot = width // 4
    half = rot // 2
    inv_freq = 1.0 / (ROPE_THETA ** (jnp.arange(half, dtype=F32) * 2.0 / rot))
    ang = positions.astype(F32)[:, None] * inv_freq
    cos, sin = jnp.cos(ang), jnp.sin(ang)
    n = positions.shape[0]
    one = jnp.ones((n, width - rot), F32)
    zero_h = jnp.zeros((n, half), F32)
    zero_r = jnp.zeros((n, width - rot), F32)
    c = jnp.concatenate([cos, cos, one], axis=1)
    s_up = jnp.concatenate([-sin, zero_h, zero_r], axis=1)
    s_dn = jnp.concatenate([zero_h, sin, zero_r], axis=1)
    rep = 128 // width
    return tuple(jnp.tile(t, (1, rep)) for t in (c, s_up, s_dn)), half


def _rope_tile(x, c, s_up, s_dn, half):
    return x * c + pltpu.roll(x, 128 - half, 1) * s_up + pltpu.roll(x, half, 1) * s_dn


def _split_rope_kernel(hm_ref, hc_ref, ca_ref, ua_ref, da_ref, cb_ref, ub_ref, db_ref,
                       dq_o, dk_o, dv_o, sq_o, sk_o, sv_o, iq_o, ik_o, *, shift, ik_shift, half_a, half_b):
    ta = (ca_ref[...], ua_ref[...], da_ref[...])
    tb = (cb_ref[...], ub_ref[...], db_ref[...])
    lane = lax.broadcasted_iota(jnp.int32, ca_ref.shape, 1)

    def shifted(k):
        return pltpu.roll(hm_ref[:, 128 * k:128 * (k + 1)].astype(F32), 128 - shift, 1)

    m = 0
    nxt = shifted(0)
    for dst, tab, half in ((dq_o, ta, half_a), (dk_o, ta, half_a), (dv_o, None, 0), (sq_o, tb, half_b),
                           (sk_o, tb, half_b), (sv_o, None, 0), (iq_o, ta, half_a)):
        for t in range(dst.shape[1] // 128):
            cur, nxt = nxt, shifted(m + 1)
            y = jnp.where(lane < 128 - shift, cur, nxt)
            if tab is not None:
                y = _rope_tile(y, *tab, half)
            dst[:, 128 * t:128 * (t + 1)] = y.astype(BF16)
            m += 1
    ik = _rope_tile(pltpu.roll(hc_ref[...], 128 - ik_shift, 1), *ta, half_a)
    ik_o[...] = jnp.where(lane < IDX_DIM, ik, pltpu.roll(ik, IDX_DIM, 1)).astype(BF16)


def _split_rope(hm, hs, tab_a, half_a, tab_b, half_b, *, tm=256):
    n, wm = hm.shape
    tile = pl.BlockSpec((tm, 128), lambda i: (i, 0))
    widths = (W_DIFF, W_DIFF, W_DIFF, W_SQ, W_SKV, W_SKV, W_IQ, 128)
    return pl.pallas_call(
        functools.partial(_split_rope_kernel, shift=OFF_DQ - HM_BASE, ik_shift=OFF_IK % 128,
                          half_a=half_a, half_b=half_b),
        out_shape=tuple(jax.ShapeDtypeStruct((n, w), BF16) for w in widths),
        grid=(n // tm,),
        in_specs=[pl.BlockSpec((tm, wm), lambda i: (i, 0)), pl.BlockSpec((tm, 128), lambda i: (i, 1))] + [tile] * 6,
        out_specs=tuple(pl.BlockSpec((tm, w), lambda i: (i, 0)) for w in widths),
        compiler_params=_cparams(("parallel",)),
        name="split_rope",
    )(hm, hs, *tab_a, *tab_b)


def _gla_prep_kernel(q_ref, k_ref, glr_ref, wg_ref, bg_ref, qe_o, kd_o, s_o, eb_o, *, chunks):
    C, SUB, DK = GLA_CHUNK, GLA_SUB, GLA_DK
    NT = (((1,), (1,)), ((), ()))

    row = lax.broadcasted_iota(jnp.int32, (C, C), 0)
    col = lax.broadcasted_iota(jnp.int32, (C, C), 1)
    tril = (col <= row).astype(F32)
    gi = lax.broadcasted_iota(jnp.int32, (SUB, SUB * SUB), 0)
    gr = lax.broadcasted_iota(jnp.int32, (SUB, SUB * SUB), 1)
    gsum = (gr // SUB == gi).astype(BF16)
    pr = lax.broadcasted_iota(jnp.int32, (SUB * SUB, C), 0)
    pc = lax.broadcasted_iota(jnp.int32, (SUB * SUB, C), 1)
    srow = lax.broadcasted_iota(jnp.int32, (SUB, C), 0)
    scol = lax.broadcasted_iota(jnp.int32, (SUB, C), 1)

    def chunk_head(ci, hh):
        r0 = pl.multiple_of(ci * C, C)
        rows = pl.ds(r0, C)
        ks = slice(hh * DK, (hh + 1) * DK)
        q = q_ref[rows, ks].astype(F32) * (DK ** -0.5)
        k = k_ref[rows, ks].astype(F32)
        z = jnp.dot(glr_ref[rows, :], wg_ref[:, ks], precision=HIGHEST, preferred_element_type=F32) + bg_ref[:, ks]
        log_a = -(jnp.maximum(-z, 0.0) + jnp.log1p(jnp.exp(-jnp.abs(z)))) * (1.0 / GLA_TAU)
        b = jnp.dot(tril, log_a, precision=HIGHEST, preferred_element_type=F32)
        qe_o[rows, ks] = (q * jnp.exp(b)).astype(BF16)

        s_rows = []
        for blk in range(C // SUB):
            r0 = blk * SUB
            b_i, q_i, k_i = b[r0:r0 + SUB], q[r0:r0 + SUB], k[r0:r0 + SUB]
            pair = (q_i[:, None, :] * k_i[None, :, :]
                    * jnp.exp(jnp.minimum(b_i[:, None, :] - b_i[None, :, :], 0.0)))
            pair_sum = jnp.sum(pair.reshape(SUB * SUB, DK), axis=-1, keepdims=True)
            placed = jnp.where(pc == r0 + pr % SUB, pair_sum, 0.0).astype(BF16)
            s_blk = jnp.dot(gsum, placed, preferred_element_type=F32)
            s_blk = jnp.where(scol <= srow + r0, s_blk, 0.0)
            if blk > 0:
                beta = b[r0:r0 + 1]
                q_t = (q_i * jnp.exp(b_i - beta)).astype(BF16)
                k_t = (k * jnp.exp(jnp.minimum(beta - b, 0.0))).astype(BF16)
                off = lax.dot_general(q_t, k_t, NT, preferred_element_type=F32)
                s_blk = jnp.where(scol < r0, off, s_blk)
            s_rows.append(s_blk)
        s_o[rows, hh * C:(hh + 1) * C] = jnp.concatenate(s_rows, axis=0).astype(BF16)
        b_last = b[C - 1:C]
        kd_o[rows, ks] = (k * jnp.exp(b_last - b)).astype(BF16)
        eb_o[pl.ds(ci, 1), ks] = jnp.exp(b_last)

    def two_chunks(i, carry):
        for ci in (2 * i, 2 * i + 1):
            for hh in range(2):
                chunk_head(ci, hh)
        return carry

    lax.fori_loop(0, chunks // 2, two_chunks, 0)


def _gla_scan_kernel(qe_ref, kd_ref, s_ref, eb_ref, v_ref, r_ref, ng_ref, mix_ref, o_ref, state_ref):
    del mix_ref
    C, DK, DV = GLA_CHUNK, GLA_DK, GLA_DV
    NT = (((1,), (1,)), ((), ()))
    TN = (((0,), (0,)), ((), ()))

    @pl.when(pl.program_id(0) == 0)
    def _():
        state_ref[...] = jnp.zeros_like(state_ref)

    for b in range(state_ref.shape[0]):
        for h in range(GLA_HEADS):
            ks = slice(h * DK, (h + 1) * DK)
            vs = slice(h * DV, (h + 1) * DV)
            st = state_ref[b, h]
            v = v_ref[b, :, vs]
            o = (lax.dot_general(qe_ref[b, :, ks], st.astype(BF16), NT, preferred_element_type=F32)
                 + jnp.dot(s_ref[b, :, h * C:(h + 1) * C], v, preferred_element_type=F32))
            state_ref[b, h] = st * eb_ref[b:b + 1, ks] + lax.dot_general(v, kd_ref[b, :, ks], TN,
                                                                         preferred_element_type=F32)
            ms = jnp.mean(o * o, axis=-1, keepdims=True)
            o_n = o * lax.rsqrt(ms + LN_EPS) * ng_ref[...]
            r = r_ref[b, :, vs].astype(F32)
            o_ref[b, :, vs] = (o_n * (r * jax.nn.sigmoid(r))).astype(BF16)


def _gla(ha, glr, w_gate_up, b_gate, gla_norm_g, mix, lead, *, batch, seq, chunks=8):
    C = GLA_CHUNK
    n = batch * seq
    nc = seq // C
    hp = GLA_HEADS // 2
    wq = 2 * GLA_DK
    wqk, wv = GLA_HEADS * GLA_DK, GLA_HEADS * GLA_DV
    tr = chunks * C

    def rows(width, first):
        return pl.BlockSpec((tr, width), lambda i, p: (i, first + p))

    qe, kd, sc, eb = pl.pallas_call(
        functools.partial(_gla_prep_kernel, chunks=chunks),
        out_shape=(jax.ShapeDtypeStruct((n, wqk), BF16), jax.ShapeDtypeStruct((n, wqk), BF16),
                   jax.ShapeDtypeStruct((n, GLA_HEADS * C), BF16), jax.ShapeDtypeStruct((n // C, wqk), F32)),
        grid=(n // tr, hp),
        in_specs=[rows(wq, 0), rows(wq, hp),
                  pl.BlockSpec((tr, GLA_RANK), lambda i, p: (i, 0)),
                  pl.BlockSpec((None, GLA_RANK, wq), lambda i, p: (lead, 0, p)),
                  pl.BlockSpec((None, 1, wq), lambda i, p: (lead, 0, p))],
        out_specs=(rows(wq, 0), rows(wq, 0), rows(2 * C, 0),
                   pl.BlockSpec((chunks, wq), lambda i, p: (i, p))),
        compiler_params=_cparams(("parallel", "parallel")),
        name="gla_prep",
    )(ha, ha, glr, w_gate_up, b_gate.reshape(-1, 1, wqk))

    def per_chunk(width, first):
        return pl.BlockSpec((batch, C, width), lambda c: (0, c, first))

    eb = eb.reshape(batch, nc, wqk).transpose(1, 0, 2)
    ha3 = ha.reshape(batch, seq, -1)
    out = pl.pallas_call(
        _gla_scan_kernel,
        out_shape=jax.ShapeDtypeStruct((batch, seq, mix.shape[1]), mix.dtype),
        grid=(nc,),
        in_specs=[per_chunk(wqk, 0), per_chunk(wqk, 0), per_chunk(GLA_HEADS * C, 0),
                  pl.BlockSpec((None, batch, wqk), lambda c: (c, 0, 0)),
                  per_chunk(wv, 1), per_chunk(wv, 2),
                  pl.BlockSpec((None, 1, GLA_DV), lambda c: (lead, 0, 0)),
                  pl.BlockSpec(memory_space=pl.ANY)],
        out_specs=per_chunk(wv, MIX_GLA // wv),
        scratch_shapes=[pltpu.VMEM((batch, GLA_HEADS, GLA_DV, GLA_DK), F32)],
        input_output_aliases={7: 0},
        compiler_params=_cparams(("arbitrary",)),
        name="gla_scan",
    )(qe.reshape(batch, seq, wqk), kd.reshape(batch, seq, wqk), sc.reshape(batch, seq, GLA_HEADS * C), eb,
      ha3, ha3, gla_norm_g.reshape(-1, 1, GLA_DV), mix.reshape(batch, seq, -1))
    return out.reshape(mix.shape)


def _diff_kernel(lq1_ref, lk1_ref, lq2_ref, lk2_ref, q_ref, k_ref, vt_ref, g_ref, mix_ref, o_ref,
                 qs_ref, m_ref, l_ref, acc_ref, *, tq, lambda_init):
    del mix_ref
    qi = pl.program_id(2)
    NT = (((1,), (1,)), ((), ()))
    hw = 2 * DIFF_D
    heads = q_ref.shape[1] // hw

    lane = lax.broadcasted_iota(jnp.int32, (tq, hw), 1)
    for hh in range(heads):
        q = q_ref[:, hw * hh:hw * (hh + 1)].astype(F32) * (DIFF_D ** -0.5)
        qs_ref[hh, :tq] = jnp.where(lane < DIFF_D, q, 0.0).astype(BF16)
        qs_ref[hh, tq:] = jnp.where(lane >= DIFF_D, q, 0.0).astype(BF16)
    m_ref[...] = jnp.full_like(m_ref, -jnp.inf)
    l_ref[...] = jnp.zeros_like(l_ref)
    acc_ref[...] = jnp.zeros_like(acc_ref)

    def tile(j, masked):
        r0 = pl.multiple_of(j * tq, tq)
        for hh in range(heads):
            hs = slice(hw * hh, hw * (hh + 1))
            s = lax.dot_general(k_ref[pl.ds(r0, tq), hs], qs_ref[hh], NT, preferred_element_type=F32)
            if masked:
                r = lax.broadcasted_iota(jnp.int32, s.shape, 0)
                c = lax.broadcasted_iota(jnp.int32, s.shape, 1)
                s = jnp.where(r <= jnp.where(c >= tq, c - tq, c), s, -jnp.inf)
            m_prev = m_ref[hh]
            m_new = jnp.maximum(m_prev, jnp.max(s, axis=0, keepdims=True))
            alpha = jnp.exp(m_prev - m_new)
            p = jnp.exp(s - m_new)
            l_ref[hh] = alpha * l_ref[hh] + jnp.sum(p, axis=0, keepdims=True)
            acc_ref[hh] = alpha * acc_ref[hh] + jnp.dot(vt_ref[hs, pl.ds(r0, tq)], p.astype(BF16),
                                                      preferred_element_type=F32)
            m_ref[hh] = m_new

    def full_tile(j, carry):
        tile(j, False)
        return carry

    lax.fori_loop(0, qi, full_tile, 0)
    tile(qi, True)

    lam = (jnp.exp(jnp.sum(lq1_ref[...] * lk1_ref[...], axis=-1, keepdims=True))
           - jnp.exp(jnp.sum(lq2_ref[...] * lk2_ref[...], axis=-1, keepdims=True)) + lambda_init)
    for hh in range(heads):
        o = acc_ref[hh] / l_ref[hh]
        o = o[:, :tq] - lam * o[:, tq:]
        ms = jnp.mean(o * o, axis=0, keepdims=True)
        o = o * lax.rsqrt(ms + LN_EPS) * g_ref[...] * (1.0 - lambda_init)
        o_ref[:, hw * hh:hw * (hh + 1)] = o.T.astype(BF16)


def _diff_attention(dq, dk, dvt, lq1, lk1, lq2, lk2, norm_g, mix, lead, *, batch, seq, lambda_init, tq, heads=2):
    nq = seq // tq
    hw = 2 * DIFF_D
    pw = heads * hw
    lam_spec = pl.BlockSpec((None, 1, DIFF_D), lambda b, p, qi: (lead, 0, 0))
    lq1, lk1, lq2, lk2 = (t.reshape(-1, 1, DIFF_D) for t in (lq1, lk1, lq2, lk2))
    return pl.pallas_call(
        functools.partial(_diff_kernel, tq=tq, lambda_init=lambda_init),
        out_shape=jax.ShapeDtypeStruct(mix.shape, mix.dtype),
        grid=(batch, DIFF_HEADS // heads, nq),
        in_specs=[lam_spec] * 4 + [
            pl.BlockSpec((tq, pw), lambda b, p, qi: (b * nq + qi, p)),
            pl.BlockSpec((seq, pw), lambda b, p, qi: (b, p)),
            pl.BlockSpec((None, pw, seq), lambda b, p, qi: (b, p, 0)),
            pl.BlockSpec((None, hw, 1), lambda b, p, qi: (lead, 0, 0)),
            pl.BlockSpec(memory_space=pl.ANY)],
        out_specs=pl.BlockSpec((tq, pw), lambda b, p, qi: (b * nq + qi, MIX_DIFF // pw + p)),
        scratch_shapes=[pltpu.VMEM((heads, 2 * tq, hw), BF16), pltpu.VMEM((heads, 1, 2 * tq), F32),
                        pltpu.VMEM((heads, 1, 2 * tq), F32), pltpu.VMEM((heads, hw, 2 * tq), F32)],
        input_output_aliases={8: 0},
        compiler_params=_cparams(("parallel", "parallel", "arbitrary")),
        name="diff_attn",
    )(lq1, lk1, lq2, lk2, dq, dk, dvt, norm_g.reshape(-1, hw, 1), mix)


def _dsa_kernel(iq_ref, sq_ref, ikd_ref, wt_ref, sk_ref, svt_ref, mix_ref, o_ref,
                iqm_ref, keys_ref, bias_ref, q5_ref, m_ref, l_ref, acc_ref, *, tq, tk, seq, k_sel):
    del mix_ref
    qi = pl.program_id(1)
    NT = (((1,), (1,)), ((), ()))
    n_tiles = (qi * tq + tq + tk - 1) // tk
    qpos = qi * tq + lax.broadcasted_iota(jnp.int32, (1, tq), 1)
    krow = lax.broadcasted_iota(jnp.int32, (tk, tq), 0)
    idx_scale = (IDX_DIM ** -0.5) * (IDX_HEADS ** -0.5)
    neg_inf = F32(-jnp.inf)

    def tile_start(j):
        return pl.multiple_of(j * tk, tk)

    lane = lax.broadcasted_iota(jnp.int32, (tq, 128), 1)
    for t in range(IDX_HEADS // 2):
        a = iq_ref[:, 128 * t:128 * (t + 1)]
        zero = jnp.zeros_like(a)
        iqm_ref[t, :tq] = jnp.where(lane < IDX_DIM, a, zero)
        iqm_ref[t, tq:] = jnp.where(lane >= IDX_DIM, a, zero)
    wt = wt_ref[...]

    def score_tile(j, carry):
        r0 = tile_start(j)
        kd = ikd_ref[pl.ds(r0, tk), :]
        acc = jnp.zeros((tk, 2 * tq), F32)
        for t in range(IDX_HEADS // 2):
            lg = lax.dot_general(kd, iqm_ref[t], NT, preferred_element_type=F32)
            w2 = jnp.concatenate([wt[2 * t:2 * t + 1], wt[2 * t + 1:2 * t + 2]], axis=1)
            acc = acc + jnp.maximum(lg, 0.0) * w2
        acc = acc[:, :tq] + acc[:, tq:]
        bits = pltpu.bitcast(acc * idx_scale, jnp.int32)
        key = jnp.where(bits < 0, (bits ^ 0x7FFFFFFF) + 1, bits)
        keys_ref[pl.ds(r0, tk), :] = jnp.where(r0 + krow <= qpos, key, INT_MIN)
        return carry

    lax.fori_loop(0, n_tiles, score_tile, 0)

    def count_ones(ones):
        def body(j, cnt):
            r0 = tile_start(j)
            return cnt + jnp.sum(ones(keys_ref[pl.ds(r0, tk), :], r0 + krow), axis=0, keepdims=True)
        return lax.fori_loop(0, n_tiles, body, jnp.zeros((1, tq), jnp.int32))

    def count(pred):
        return count_ones(lambda kk, kp: jnp.where(pred(kk, kp), 1, 0))

    def thr_bit(i, t_u):
        cand_u = t_u | jnp.left_shift(jnp.int32(1), 31 - i)
        cand = cand_u ^ INT_MIN
        return jnp.where(count(lambda kk, kp: kk >= cand) >= k_sel, cand_u, t_u)

    thr = lax.fori_loop(0, 32, thr_bit, jnp.zeros((1, tq), jnp.int32)) ^ INT_MIN
    need = k_sel - count(lambda kk, kp: kk > thr)
    n_eq = count(lambda kk, kp: kk == thr)
    excess = jnp.where(thr == INT_MIN, 0, jnp.where(n_eq > need, 1, 0))

    def tie_search():
        def idx_bit(i, j_lo):
            cand = j_lo | jnp.left_shift(jnp.int32(1), (seq - 1).bit_length() - 1 - i)
            c = count_ones(lambda kk, kp: jnp.where(kk == thr, jnp.where(kp < cand, 1, 0), 0))
            return jnp.where(c < need, cand, j_lo)
        return lax.fori_loop(0, (seq - 1).bit_length(), idx_bit, jnp.zeros((1, tq), jnp.int32))

    j_star = lax.cond(jnp.max(excess) > 0, tie_search, lambda: jnp.full((1, tq), seq, jnp.int32))
    j_star = jnp.where(thr == INT_MIN, -1, jnp.where(excess > 0, j_star, seq))

    def bias_tile(j, carry):
        r0 = tile_start(j)
        kk = keys_ref[pl.ds(r0, tk), :]
        tie = jnp.where(r0 + krow <= j_star, 0.0, neg_inf)
        bias_ref[pl.ds(r0, tk), :] = jnp.where(kk > thr, 0.0, jnp.where(kk == thr, tie, neg_inf))
        return carry

    lax.fori_loop(0, n_tiles, bias_tile, 0)

    scale = DSA_DH ** -0.5
    rep = DSA_HEADS // DSA_KV
    for g in range(DSA_KV):
        for i in range(rep):
            h = g * rep + i
            q5_ref[g, tq * i:tq * (i + 1)] = sq_ref[:, DSA_DH * h:DSA_DH * (h + 1)]
    m_ref[...] = jnp.full_like(m_ref, neg_inf)
    l_ref[...] = jnp.zeros_like(l_ref)
    acc_ref[...] = jnp.zeros_like(acc_ref)

    def attn_tile(j, carry):
        r0 = tile_start(j)
        bias = bias_ref[pl.ds(r0, tk), :]
        bias = jnp.concatenate([bias] * rep, axis=1)
        for g in range(DSA_KV):
            gs = slice(DSA_DH * g, DSA_DH * (g + 1))
            s = lax.dot_general(sk_ref[pl.ds(r0, tk), gs], q5_ref[g], NT, preferred_element_type=F32)
            s = s * scale + bias
            m_prev = m_ref[g]
            m_new = jnp.maximum(m_prev, jnp.max(s, axis=0, keepdims=True))
            m_safe = jnp.where(m_new == neg_inf, 0.0, m_new)
            alpha = jnp.exp(m_prev - m_safe)
            p = jnp.exp(s - m_safe)
            l_ref[g] = alpha * l_ref[g] + jnp.sum(p, axis=0, keepdims=True)
            acc_ref[g] = alpha * acc_ref[g] + jnp.dot(svt_ref[gs, pl.ds(r0, tk)], p.astype(BF16),
                                                    preferred_element_type=F32)
            m_ref[g] = m_new
        return carry

    lax.fori_loop(0, n_tiles, attn_tile, 0)
    for g in range(DSA_KV):
        out_t = acc_ref[g] / l_ref[g]
        for i in range(rep):
            h = g * rep + i
            o_ref[:, DSA_DH * h:DSA_DH * (h + 1)] = out_t[:, tq * i:tq * (i + 1)].T.astype(BF16)


def _dsa(iq, sq, ikd, wt, sk, svt, mix, *, batch, seq, tq, tk):
    nq = seq // tq
    k_sel = min(IDX_TOPK, seq // 4)
    rep = DSA_HEADS // DSA_KV
    return pl.pallas_call(
        functools.partial(_dsa_kernel, tq=tq, tk=tk, seq=seq, k_sel=k_sel),
        out_shape=jax.ShapeDtypeStruct(mix.shape, mix.dtype),
        grid=(batch, nq),
        in_specs=[pl.BlockSpec((tq, W_IQ), lambda b, qi: (b * nq + qi, 0)),
                  pl.BlockSpec((tq, W_SQ), lambda b, qi: (b * nq + qi, 0)),
                  pl.BlockSpec((seq, 128), lambda b, qi: (b, 0)),
                  pl.BlockSpec((None, IDX_HEADS, tq), lambda b, qi: (b, 0, qi)),
                  pl.BlockSpec((seq, W_SKV), lambda b, qi: (b, 0)),
                  pl.BlockSpec((None, W_SKV, seq), lambda b, qi: (b, 0, 0)),
                  pl.BlockSpec(memory_space=pl.ANY)],
        out_specs=pl.BlockSpec((tq, W_SQ), lambda b, qi: (b * nq + qi, MIX_DSA // W_SQ)),
        scratch_shapes=[pltpu.VMEM((IDX_HEADS // 2, 2 * tq, 128), BF16), pltpu.VMEM((seq, tq), jnp.int32),
                        pltpu.VMEM((seq, tq), F32), pltpu.VMEM((DSA_KV, rep * tq, DSA_DH), BF16),
                        pltpu.VMEM((DSA_KV, 1, rep * tq), F32), pltpu.VMEM((DSA_KV, 1, rep * tq), F32),
                        pltpu.VMEM((DSA_KV, DSA_DH, rep * tq), F32)],
        input_output_aliases={6: 0},
        compiler_params=_cparams(("parallel", "arbitrary")),
        name="dsa",
    )(iq, sq, ikd, wt, sk, svt, mix)


def _router_kernel(x_ref, w_ref, b_ref, e_ref, g_ref, xp_ref):
    x = x_ref[...]
    logits = jnp.dot(x, w_ref[...], precision=HIGHEST, preferred_element_type=F32) + b_ref[...]
    lane = lax.broadcasted_iota(jnp.int32, logits.shape, 1)
    neg_inf = F32(-jnp.inf)
    lg = jnp.where(lane < MOE_E, logits, neg_inf)
    m1 = jnp.max(lg, axis=-1, keepdims=True)
    i1 = jnp.min(jnp.where(lg == m1, lane, 128), axis=-1, keepdims=True)
    lg2 = jnp.where(lane == i1, neg_inf, lg)
    m2 = jnp.max(lg2, axis=-1, keepdims=True)
    i2 = jnp.min(jnp.where(lg2 == m2, lane, 128), axis=-1, keepdims=True)
    e21 = jnp.exp(m2 - m1)
    g1 = 1.0 / (1.0 + e21)
    e_ref[...] = jnp.where(lane == 0, i1, jnp.where(lane == 1, i2, 0))
    g_ref[...] = jnp.where(lane == 0, g1, jnp.where(lane == 1, e21 * g1, 0.0))
    half = x.shape[1] // 2
    lo = pltpu.bitcast(x[:, :half].astype(BF16).astype(F32), jnp.uint32)
    hi = pltpu.bitcast(x[:, half:].astype(BF16).astype(F32), jnp.uint32)
    xp_ref[...] = (hi & jnp.uint32(0xFFFF0000)) | (lo >> 16)


def _router(x, router_w, router_b, *, tm=256):
    n, d = x.shape
    row = lambda w: pl.BlockSpec((tm, w), lambda i: (i, 0))
    return pl.pallas_call(
        _router_kernel,
        out_shape=(jax.ShapeDtypeStruct((n, 128), jnp.int32), jax.ShapeDtypeStruct((n, 128), F32),
                   jax.ShapeDtypeStruct((n, d // 2), jnp.uint32)),
        grid=(n // tm,),
        in_specs=[row(d), pl.BlockSpec((d, 128), lambda i: (0, 0)), pl.BlockSpec((1, 128), lambda i: (0, 0))],
        out_specs=(row(128), row(128), row(d // 2)),
        compiler_params=_cparams(("parallel",)),
        name="router",
    )(x, router_w, router_b)


def _dispatch_kernel(pos_ref, xp_ref, buf_ref, o_ref, sem, *, tb):
    del buf_ref
    base = pl.program_id(0) * tb

    def row_copy(i, s):
        return pltpu.make_async_copy(xp_ref.at[pl.ds(i, 1)], o_ref.at[pl.ds(pos_ref[2 * (base + i) + s], 1)], sem)

    def start(i, c):
        row_copy(i, 0).start()
        row_copy(i, 1).start()
        return c

    def wait(i, c):
        row_copy(i, 0).wait()
        row_copy(i, 1).wait()
        return c

    lax.fori_loop(0, tb, start, 0)
    lax.fori_loop(0, tb, wait, 0)


def _dispatch(pos, xp, buf, *, tb=256):
    n, w = xp.shape
    return pl.pallas_call(
        functools.partial(_dispatch_kernel, tb=tb),
        out_shape=jax.ShapeDtypeStruct(buf.shape, buf.dtype),
        grid_spec=pltpu.PrefetchScalarGridSpec(
            num_scalar_prefetch=1,
            grid=(n // tb,),
            in_specs=[pl.BlockSpec((tb, w), lambda i, pos: (i, 0)), pl.BlockSpec(memory_space=pl.ANY)],
            out_specs=pl.BlockSpec(memory_space=pl.ANY),
            scratch_shapes=[pltpu.SemaphoreType.DMA],
        ),
        input_output_aliases={2: 0},
        compiler_params=_cparams(("arbitrary",)),
        name="moe_dispatch",
    )(pos, xp, buf)


def _unpack_rows(word):
    lo = pltpu.bitcast(word << 16, F32).astype(BF16)
    hi = pltpu.bitcast(word & jnp.uint32(0xFFFF0000), F32).astype(BF16)
    return lo, hi


def _expert_tile_refresh(be_ref, bf_ref, nxt_ref, w_hbms, stage_ref, wb_refs, sem, *, lead, tn):
    n, i = pl.program_id(0), pl.program_id(1)

    def fetch(e, col_tile):
        cols = pl.ds(pl.multiple_of(col_tile * tn, tn), tn)
        return [pltpu.make_async_copy(w.at[lead, e, :, cols], stage_ref.at[t], sem)
                for t, w in enumerate(w_hbms)]

    @pl.when(bf_ref[i] == 1)
    def _():
        @pl.when(jnp.logical_and(n == 0, i == 0))
        def _():
            for c in fetch(be_ref[0], 0):
                c.start()

        for c in fetch(be_ref[i], n):
            c.wait()
        for t, wb in enumerate(wb_refs):
            wb[...] = stage_ref[t].astype(BF16)
        last_run = nxt_ref[i] < 0
        e_next = jnp.where(last_run, be_ref[0], nxt_ref[i])
        n_next = jnp.where(last_run, n + 1, n)

        @pl.when(n_next < pl.num_programs(0))
        def _():
            for c in fetch(e_next, n_next):
                c.start()


def _moe_up_kernel(be_ref, bf_ref, na_ref, nxt_ref, a_ref, w1_hbm, w3_hbm, o_ref,
                   stage_ref, w1b_ref, w3b_ref, sem, *, lead, tn):
    i = pl.program_id(1)

    @pl.when(i < na_ref[0])
    def _():
        _expert_tile_refresh(be_ref, bf_ref, nxt_ref, (w1_hbm, w3_hbm), stage_ref, (w1b_ref, w3b_ref), sem,
                             lead=lead, tn=tn)
        lo, hi = _unpack_rows(a_ref[...])
        half = lo.shape[1]
        g = (jnp.dot(lo, w1b_ref[:half], preferred_element_type=F32)
             + jnp.dot(hi, w1b_ref[half:], preferred_element_type=F32))
        u = (jnp.dot(lo, w3b_ref[:half], preferred_element_type=F32)
             + jnp.dot(hi, w3b_ref[half:], preferred_element_type=F32))
        o_ref[...] = (g * jax.nn.sigmoid(g) * u).astype(BF16)

    @pl.when(i >= na_ref[0])
    def _():
        o_ref[...] = jnp.zeros_like(o_ref)


def _moe_down_kernel(be_ref, bf_ref, na_ref, nxt_ref, h_ref, w2_hbm, o_ref, stage_ref, w2b_ref, sem, *, lead, tn):
    i = pl.program_id(1)

    @pl.when(i < na_ref[0])
    def _():
        _expert_tile_refresh(be_ref, bf_ref, nxt_ref, (w2_hbm,), stage_ref, (w2b_ref,), sem, lead=lead, tn=tn)
        o_ref[...] = jnp.dot(h_ref[...], w2b_ref[...], preferred_element_type=F32)

    @pl.when(i >= na_ref[0])
    def _():
        o_ref[...] = jnp.zeros_like(o_ref)


def _moe_ffn(block_e, block_first, n_active, next_e, a_sorted, w1, w3, w2, lead, *, tm, tn):
    r, half = a_sorted.shape
    d = 2 * half
    f = w1.shape[-1]
    nb = r // tm
    hbm = pl.BlockSpec(memory_space=pl.ANY)

    def rows(i, na):
        return jnp.minimum(i, na[0] - 1)

    h = pl.pallas_call(
        functools.partial(_moe_up_kernel, lead=lead, tn=tn),
        out_shape=jax.ShapeDtypeStruct((r, f), BF16),
        grid_spec=pltpu.PrefetchScalarGridSpec(
            num_scalar_prefetch=4,
            grid=(f // tn, nb),
            in_specs=[pl.BlockSpec((tm, half), lambda n, i, be, bf, na, nx: (rows(i, na), 0)), hbm, hbm],
            out_specs=pl.BlockSpec((tm, tn), lambda n, i, be, bf, na, nx: (i, n)),
            scratch_shapes=[pltpu.VMEM((2, d, tn), F32), pltpu.VMEM((d, tn), BF16), pltpu.VMEM((d, tn), BF16),
                            pltpu.SemaphoreType.DMA],
        ),
        compiler_params=_cparams(("arbitrary", "arbitrary")),
        name="moe_up",
    )(block_e, block_first, n_active, next_e, a_sorted, w1, w3)
    return pl.pallas_call(
        functools.partial(_moe_down_kernel, lead=lead, tn=tn),
        out_shape=jax.ShapeDtypeStruct((r, d), F32),
        grid_spec=pltpu.PrefetchScalarGridSpec(
            num_scalar_prefetch=4,
            grid=(d // tn, nb),
            in_specs=[pl.BlockSpec((tm, f), lambda n, i, be, bf, na, nx: (rows(i, na), 0)), hbm],
            out_specs=pl.BlockSpec((tm, tn), lambda n, i, be, bf, na, nx: (i, n)),
            scratch_shapes=[pltpu.VMEM((1, f, tn), F32), pltpu.VMEM((f, tn), BF16), pltpu.SemaphoreType.DMA],
        ),
        compiler_params=_cparams(("arbitrary", "arbitrary")),
        name="moe_down",
    )(block_e, block_first, n_active, next_e, h, w2)


def _combine_ln_kernel(pos_ref, x_ref, gate_ref, g_ref, b_ref, y_ref, o_ref, ybuf_ref, sem, *, tb, alpha):
    step, n_steps = pl.program_id(0), pl.num_programs(0)
    slot = step % 2

    def row_copy(blk, buf, i, s):
        return pltpu.make_async_copy(y_ref.at[pl.ds(pos_ref[2 * (blk * tb + i) + s], 1)],
                                     ybuf_ref.at[buf, s, pl.ds(i, 1)], sem.at[buf])

    def start_block(blk, buf):
        def body(i, c):
            row_copy(blk, buf, i, 0).start()
            row_copy(blk, buf, i, 1).start()
            return c
        lax.fori_loop(0, tb, body, 0)

    def wait_block(blk, buf):
        def body(i, c):
            row_copy(blk, buf, i, 0).wait()
            row_copy(blk, buf, i, 1).wait()
            return c
        lax.fori_loop(0, tb, body, 0)

    @pl.when(step == 0)
    def _():
        start_block(0, 0)

    @pl.when(step + 1 < n_steps)
    def _():
        start_block(step + 1, 1 - slot)

    wait_block(step, slot)
    gate = gate_ref[...]
    f = ybuf_ref[slot, 0] * gate[:, 0:1] + ybuf_ref[slot, 1] * gate[:, 1:2]
    z = alpha * x_ref[...] + f
    mu = jnp.mean(z, axis=-1, keepdims=True)
    zc = z - mu
    var = jnp.mean(zc * zc, axis=-1, keepdims=True)
    o_ref[...] = zc * lax.rsqrt(var + LN_EPS) * g_ref[...] + b_ref[...]


def _combine_ln(pos, x, gates, g, b, y, lead, *, alpha, tb=128):
    n, d = x.shape
    par = pl.BlockSpec((None, 1, d), lambda i, pos: (lead, 0, 0))
    return pl.pallas_call(
        functools.partial(_combine_ln_kernel, tb=tb, alpha=alpha),
        out_shape=jax.ShapeDtypeStruct((n, d), F32),
        grid_spec=pltpu.PrefetchScalarGridSpec(
            num_scalar_prefetch=1,
            grid=(n // tb,),
            in_specs=[pl.BlockSpec((tb, d), lambda i, pos: (i, 0)),
                      pl.BlockSpec((tb, 128), lambda i, pos: (i, 0)),
                      par, par, pl.BlockSpec(memory_space=pl.ANY)],
            out_specs=pl.BlockSpec((tb, d), lambda i, pos: (i, 0)),
            scratch_shapes=[pltpu.VMEM((2, 2, tb, d), F32), pltpu.SemaphoreType.DMA((2,))],
        ),
        compiler_params=_cparams(("arbitrary",)),
        name="moe_combine_ln",
    )(pos, x, gates, g.reshape(-1, 1, d), b.reshape(-1, 1, d), y)


def _moe_layer(x, router_w, router_b, w1, w3, w2, ln_g, ln_b, lead_moe, lead_ln, *, alpha, tm=512, tn=512):
    n, d = x.shape
    rw = jnp.pad(router_w[lead_moe], ((0, 0), (0, 128 - MOE_E)))
    rb = jnp.pad(router_b[lead_moe], (0, 128 - MOE_E)).reshape(1, 128)
    top_e, gates, xp = _router(x, rw, rb)
    flat_e = top_e[:, :MOE_TOPK].reshape(-1)
    onehot = (flat_e[:, None] == jnp.arange(MOE_E, dtype=jnp.int32)[None, :]).astype(jnp.int32)
    rank = jnp.sum((jnp.cumsum(onehot, axis=0) - onehot) * onehot, axis=1)
    counts = jnp.sum(onehot, axis=0)
    padded = (counts + tm - 1) // tm * tm
    end_padded = jnp.cumsum(padded)
    start_padded = end_padded - padded
    pos = (start_padded[flat_e] + rank).astype(jnp.int32)
    nb = -(-(n * MOE_TOPK + MOE_E * (tm - 1)) // tm)
    block_start = jnp.arange(nb, dtype=jnp.int32) * tm
    block_e = jnp.sum((block_start[:, None] >= end_padded[None, :]).astype(jnp.int32), axis=1)
    block_e = jnp.minimum(block_e, MOE_E - 1)
    n_active = (end_padded[-1] // tm).astype(jnp.int32).reshape(1)
    block_first = jnp.concatenate([jnp.ones((1,), jnp.int32), (block_e[1:] != block_e[:-1]).astype(jnp.int32)])
    blk = jnp.arange(nb, dtype=jnp.int32)
    run_start = (block_first == 1) & (blk < n_active[0])
    later_start = jnp.where((blk[None, :] > blk[:, None]) & run_start[None, :], blk[None, :], nb)
    next_start = jnp.min(later_start, axis=1)
    next_e = jnp.where(next_start < nb, block_e[jnp.minimum(next_start, nb - 1)], -1).astype(jnp.int32)
    a_sorted = _dispatch(pos, xp, jnp.zeros((nb * tm, d // 2), jnp.uint32))
    y = _moe_ffn(block_e, block_first, n_active, next_e, a_sorted, w1, w3, w2, lead_moe, tm=tm, tn=tn)
    return _combine_ln(pos, x, gates, ln_g, ln_b, y, lead_ln, alpha=alpha)


def kernel(x, positions, w_in, w_gate_up, b_gate, gla_norm_g, lambda_q1, lambda_k1, lambda_q2, lambda_k2,
           diff_norm_g, w_out, ln1_g, ln1_b, ln2_g, ln2_b, ffn_w1, ffn_w3, ffn_w2, router_w, router_b,
           moe_w1, moe_w3, moe_w2):
    batch, seq, d = x.shape
    n = batch * seq
    depth = w_in.shape[0]
    alpha = (2 * depth) ** 0.25
    xf = x.reshape(n, d)
    xb = xf.astype(BF16)
    pos = positions.reshape(n)
    tab_a, half_a = _rope_tables(pos, DIFF_D)
    tab_b, half_b = _rope_tables(pos, DSA_DH)
    plain = lambda i, j, k: (i, k)
    w_in_t = jnp.swapaxes(w_in, 1, 2)
    for l in range(depth):
        ha = _matmul(xb, w_in_t, n_out=W_GLA, k_steps=1, tm=1024, tn=HM_TN, tk=d, out_dtype=BF16,
                     a_map=plain, w_map=lambda i, j, k, l=l: (l, j, k), w_lead=True, w_t=True,
                     name="in_proj_gla")
        hm = _matmul(xb, w_in_t, n_out=HM_TILES * HM_TN, k_steps=1, tm=1024, tn=HM_TN, tk=d,
                     out_dtype=BF16, a_map=plain,
                     w_map=lambda i, j, k, l=l: (l, HM_BASE // HM_TN + j, k), w_lead=True, w_t=True,
                     n_valid=N_IN, col_start=lambda j: HM_BASE + j * HM_TN, name="in_proj_attn")
        small_tiles = (OFF_GLR // 128, OFF_IK // 128)
        pick = lambda j: jnp.where(j == 0, small_tiles[0], small_tiles[1])
        hs = _matmul(xb, w_in_t, n_out=256, k_steps=1, tm=1024, tn=128, tk=d, out_dtype=F32,
                     a_map=plain, w_map=lambda i, j, k, l=l: (l, pick(j), k), w_lead=True, w_t=True,
                     n_valid=N_IN, col_start=lambda j: pick(j) * 128, name="in_proj_small")
        glr = hs[:, OFF_GLR % 128:OFF_GLR % 128 + GLA_RANK]
        iw0 = 128 + OFF_IW % 128
        wt = hs[:, iw0:iw0 + IDX_HEADS].reshape(batch, seq, IDX_HEADS).transpose(0, 2, 1)
        dq, dk, dv, sq, sk, sv, iq, ikd = _split_rope(hm, hs, tab_a, half_a, tab_b, half_b)
        svt = sv.reshape(batch, seq, W_SKV).transpose(0, 2, 1)
        dvt = dv.reshape(batch, seq, W_DIFF).transpose(0, 2, 1)

        mix = jnp.zeros((n, W_MIX), BF16)
        mix = _gla(ha, glr, w_gate_up, b_gate, gla_norm_g, mix, l, batch=batch, seq=seq)
        lambda_init = 0.8 - 0.6 * math.exp(-0.3 * l)
        mix = _diff_attention(dq, dk, dvt, lambda_q1, lambda_k1, lambda_q2, lambda_k2, diff_norm_g, mix, l,
                              batch=batch, seq=seq, lambda_init=lambda_init, tq=1024, heads=1)
        mix = _dsa(iq, sq, ikd, wt, sk, svt, mix, batch=batch, seq=seq, tq=256, tk=512)

        proj = _out_proj(mix, w_out, l)
        xf, xb = _add_ln(xf, proj, ln1_g, ln1_b, l, alpha=alpha)

        j = l // 2
        if l % 2 == 0:
            f_dim = ffn_w2.shape[1]
            f_pad = -(-f_dim // 1024) * 1024
            hid = _swiglu_up(xb, ffn_w1, ffn_w3, j, tm=1024, tn=256, tk=d, f_out=f_pad)
            down = _matmul(hid, ffn_w2, n_out=d, k_steps=8, tm=2048, tn=1024, tk=f_pad // 8, out_dtype=F32,
                           a_map=plain, w_map=lambda i, j_, k, j=j: (j, k, j_), w_lead=True, k_valid=f_dim,
                           name="ffn_down")
            xf, xb = _add_ln(xf, down, ln2_g, ln2_b, l, alpha=alpha)
        else:
            xf = _moe_layer(xf, router_w, router_b, moe_w1, moe_w3, moe_w2, ln2_g, ln2_b, j, l, alpha=alpha)
            xb = xf.astype(BF16)
    return xf.reshape(batch, seq, d)
```

```python
import functools
import math

import jax
import jax.numpy as jnp
from jax import lax
from jax.experimental import pallas as pl
from jax.experimental.pallas import tpu as pltpu

F32 = jnp.float32
BF16 = jnp.bfloat16
HIGHEST = lax.Precision.HIGHEST
INT_MIN = -(2 ** 31)

ROPE_THETA = 500000.0
GLA_HEADS, GLA_DK, GLA_DV, GLA_RANK, GLA_TAU, GLA_CHUNK = 4, 192, 384, 16, 16.0, 64
GLA_SUB = 16
GLA_SAFE_DECAY = 40.0
GLA_SAFE_K = 1e18
DIFF_HEADS, DIFF_D = 10, 64
DSA_HEADS, DSA_KV, DSA_DH = 10, 2, 128
IDX_HEADS, IDX_DIM, IDX_TOPK = 32, 64, 256
MOE_E, MOE_TOPK = 8, 2
LN_EPS = 1e-5

W_GLA = 2 * GLA_HEADS * GLA_DK + 2 * GLA_HEADS * GLA_DV
OFF_GLR = W_GLA
OFF_DQ = OFF_GLR + GLA_RANK
W_DIFF = DIFF_HEADS * 2 * DIFF_D
OFF_DK, OFF_DV = OFF_DQ + W_DIFF, OFF_DQ + 2 * W_DIFF
OFF_SQ = OFF_DQ + 3 * W_DIFF
W_SQ, W_SKV = DSA_HEADS * DSA_DH, DSA_KV * DSA_DH
OFF_SK, OFF_SV = OFF_SQ + W_SQ, OFF_SQ + W_SQ + W_SKV
OFF_IQ = OFF_SV + W_SKV
W_IQ = IDX_HEADS * IDX_DIM
OFF_IK = OFF_IQ + W_IQ
OFF_IW = OFF_IK + IDX_DIM
N_IN = OFF_IW + IDX_HEADS

HM_BASE = W_GLA
HM_TN = 512
HM_TILES = -(-(N_IN - HM_BASE) // HM_TN)

MIX_GLA, MIX_DIFF, MIX_DSA = 0, 2560, 3840
W_MIX = MIX_DSA + DSA_HEADS * DSA_DH

VMEM_LIMIT = 56 * 1024 * 1024


def _cparams(sem, vmem=VMEM_LIMIT):
    return pltpu.CompilerParams(dimension_semantics=sem, vmem_limit_bytes=vmem)


def _mm_kernel(a_ref, w_ref, o_ref, acc_ref=None, *, k_valid, n_valid, col_start, w_t):
    k = pl.program_id(2)
    k_ax, n_ax = (1, 0) if w_t else (0, 1)

    def partial_product():
        w = w_ref[...]
        if k_valid is not None:
            kk = k * w.shape[k_ax] + lax.broadcasted_iota(jnp.int32, w.shape, k_ax)
            w = jnp.where(kk < k_valid, w, 0.0)
        if n_valid is not None:
            col = col_start(pl.program_id(1)) + lax.broadcasted_iota(jnp.int32, w.shape, n_ax)
            w = jnp.where(col < n_valid, w, 0.0)
        return lax.dot_general(a_ref[...], w.astype(BF16), (((1,), (k_ax,)), ((), ())),
                               preferred_element_type=F32)

    if acc_ref is None:
        o_ref[...] = partial_product().astype(o_ref.dtype)
        return

    @pl.when(k == 0)
    def _():
        acc_ref[...] = partial_product()

    @pl.when(k > 0)
    def _():
        acc_ref[...] += partial_product()

    @pl.when(k == pl.num_programs(2) - 1)
    def _():
        o_ref[...] = acc_ref[...].astype(o_ref.dtype)


def _matmul(a, w, *, n_out, k_steps, tm, tn, tk, out_dtype, a_map, w_map, w_lead=False, k_valid=None,
            n_valid=None, col_start=None, w_t=False, name="mm"):
    m = a.shape[0]
    w_block = (tn, tk) if w_t else (tk, tn)
    if w_lead:
        w_block = (None,) + w_block
    return pl.pallas_call(
        functools.partial(_mm_kernel, k_valid=k_valid, n_valid=n_valid, col_start=col_start, w_t=w_t),
        out_shape=jax.ShapeDtypeStruct((m, n_out), out_dtype),
        grid=(m // tm, pl.cdiv(n_out, tn), k_steps),
        in_specs=[pl.BlockSpec((tm, tk), a_map), pl.BlockSpec(w_block, w_map)],
        out_specs=pl.BlockSpec((tm, tn), lambda i, j, k: (i, j)),
        scratch_shapes=[pltpu.VMEM((tm, tn), F32)] if k_steps > 1 else [],
        compiler_params=_cparams(("parallel", "parallel", "arbitrary")),
        name=name,
    )(a, w)


def _swiglu_kernel(a_ref, w1_ref, w3_ref, o_ref, acc1_ref=None, acc3_ref=None, *, n_valid):
    k = pl.program_id(2)

    def partial_product(w_ref):
        return jnp.dot(a_ref[...], w_ref[...].astype(BF16), preferred_element_type=F32)

    def finish(g, u):
        h = g * jax.nn.sigmoid(g) * u
        col = pl.program_id(1) * h.shape[1] + lax.broadcasted_iota(jnp.int32, h.shape, 1)
        o_ref[...] = jnp.where(col < n_valid, h, 0.0).astype(o_ref.dtype)

    if acc1_ref is None:
        finish(partial_product(w1_ref), partial_product(w3_ref))
        return

    @pl.when(k == 0)
    def _():
        acc1_ref[...] = partial_product(w1_ref)
        acc3_ref[...] = partial_product(w3_ref)

    @pl.when(k > 0)
    def _():
        acc1_ref[...] += partial_product(w1_ref)
        acc3_ref[...] += partial_product(w3_ref)

    @pl.when(k == pl.num_programs(2) - 1)
    def _():
        finish(acc1_ref[...], acc3_ref[...])


def _swiglu_up(a, w1, w3, lead, *, tm, tn, tk, f_out):
    m, kdim = a.shape
    f = w1.shape[-1]
    last = (f - 1) // tn
    wspec = pl.BlockSpec((None, tk, tn), lambda i, j, k: (lead, k, jnp.minimum(j, last)))
    k_steps = kdim // tk
    return pl.pallas_call(
        functools.partial(_swiglu_kernel, n_valid=f),
        out_shape=jax.ShapeDtypeStruct((m, f_out), BF16),
        grid=(m // tm, f_out // tn, k_steps),
        in_specs=[pl.BlockSpec((tm, tk), lambda i, j, k: (i, k)), wspec, wspec],
        out_specs=pl.BlockSpec((tm, tn), lambda i, j, k: (i, j)),
        scratch_shapes=[pltpu.VMEM((tm, tn), F32), pltpu.VMEM((tm, tn), F32)] if k_steps > 1 else [],
        compiler_params=_cparams(("parallel", "parallel", "arbitrary")),
        name="swiglu_up",
    )(a, w1, w3)


def _out_proj_kernel(mix_ref, w_ref, o_ref):
    n_gla = GLA_HEADS * GLA_DV
    w = w_ref[...].astype(BF16)
    o_ref[...] = (jnp.dot(mix_ref[:, MIX_GLA:MIX_GLA + n_gla], w[:n_gla], preferred_element_type=F32)
                  + jnp.dot(mix_ref[:, MIX_DIFF:], w[n_gla:], preferred_element_type=F32))


def _out_proj(mix, w_out, lead, *, tm=1024, tn=256):
    m, wm = mix.shape
    d = w_out.shape[-1]
    return pl.pallas_call(
        _out_proj_kernel,
        out_shape=jax.ShapeDtypeStruct((m, d), F32),
        grid=(m // tm, d // tn),
        in_specs=[pl.BlockSpec((tm, wm), lambda i, j: (i, 0)),
                  pl.BlockSpec((None, w_out.shape[1], tn), lambda i, j: (lead, 0, j))],
        out_specs=pl.BlockSpec((tm, tn), lambda i, j: (i, j)),
        compiler_params=_cparams(("parallel", "parallel")),
        name="out_proj",
    )(mix, w_out)


def _add_ln_kernel(x_ref, y_ref, g_ref, b_ref, o_ref, ob_ref, *, alpha):
    z = alpha * x_ref[...] + y_ref[...].astype(F32)
    mu = jnp.mean(z, axis=-1, keepdims=True)
    zc = z - mu
    var = jnp.mean(zc * zc, axis=-1, keepdims=True)
    out = zc * lax.rsqrt(var + LN_EPS) * g_ref[...] + b_ref[...]
    o_ref[...] = out
    ob_ref[...] = out.astype(BF16)


def _add_ln(x, y, g, b, lead, *, alpha, tm=256):
    m, d = x.shape
    row = pl.BlockSpec((tm, d), lambda i: (i, 0))
    par = pl.BlockSpec((None, 1, d), lambda i: (lead, 0, 0))
    g, b = g.reshape(-1, 1, d), b.reshape(-1, 1, d)
    return pl.pallas_call(
        functools.partial(_add_ln_kernel, alpha=alpha),
        out_shape=(jax.ShapeDtypeStruct((m, d), F32), jax.ShapeDtypeStruct((m, d), BF16)),
        grid=(m // tm,),
        in_specs=[row, row, par, par],
        out_specs=(row, row),
        compiler_params=_cparams(("parallel",)),
        name="add_ln",
    )(x, y, g, b)


def _rope_tables(positions, width):
    rot = width // 4
    half = rot // 2
    inv_freq = 1.0 / (ROPE_THETA ** (jnp.arange(half, dtype=F32) * 2.0 / rot))
    ang = positions.astype(F32)[:, None] * inv_freq
    cos, sin = jnp.cos(ang), jnp.sin(ang)
    n = positions.shape[0]
    one = jnp.ones((n, width - rot), F32)
    zero_h = jnp.zeros((n, half), F32)
    zero_r = jnp.zeros((n, width - rot), F32)
    c = jnp.concatenate([cos, cos, one], axis=1)
    s_up = jnp.concatenate([-sin, zero_h, zero_r], axis=1)
    s_dn = jnp.concatenate([zero_h, sin, zero_r], axis=1)
    rep = 128 // width
    return tuple(jnp.tile(t, (1, rep)) for t in (c, s_up, s_dn)), half


def _rope_tile(x, c, s_up, s_dn, half):
    return x * c + pltpu.roll(x, 128 - half, 1) * s_up + pltpu.roll(x, half, 1) * s_dn


def _split_rope_kernel(hm_ref, hc_ref, ca_ref, ua_ref, da_ref, cb_ref, ub_ref, db_ref,
                       dq_o, dk_o, dv_o, sq_o, sk_o, sv_o, iq_o, ik_o, *, shift, ik_shift, half_a, half_b):
    ta = (ca_ref[...], ua_ref[...], da_ref[...])
    tb = (cb_ref[...], ub_ref[...], db_ref[...])
    lane = lax.broadcasted_iota(jnp.int32, ca_ref.shape, 1)

    def shifted(k):
        return pltpu.roll(hm_ref[:, 128 * k:128 * (k + 1)].astype(F32), 128 - shift, 1)

    m = 0
    nxt = shifted(0)
    for dst, tab, half in ((dq_o, ta, half_a), (dk_o, ta, half_a), (dv_o, None, 0), (sq_o, tb, half_b),
                           (sk_o, tb, half_b), (sv_o, None, 0), (iq_o, ta, half_a)):
        for t in range(dst.shape[1] // 128):
            cur, nxt = nxt, shifted(m + 1)
            y = jnp.where(lane < 128 - shift, cur, nxt)
            if tab is not None:
                y = _rope_tile(y, *tab, half)
            dst[:, 128 * t:128 * (t + 1)] = y.astype(BF16)
            m += 1
    ik = _rope_tile(pltpu.roll(hc_ref[...], 128 - ik_shift, 1), *ta, half_a)
    ik_o[...] = jnp.where(lane < IDX_DIM, ik, pltpu.roll(ik, IDX_DIM, 1)).astype(BF16)


def _split_rope(hm, hs, tab_a, half_a, tab_b, half_b, *, tm=256):
    n, wm = hm.shape
    tile = pl.BlockSpec((tm, 128), lambda i: (i, 0))
    widths = (W_DIFF, W_DIFF, W_DIFF, W_SQ, W_SKV, W_SKV, W_IQ, 128)
    return pl.pallas_call(
        functools.partial(_split_rope_kernel, shift=OFF_DQ - HM_BASE, ik_shift=OFF_IK % 128,
                          half_a=half_a, half_b=half_b),
        out_shape=tuple(jax.ShapeDtypeStruct((n, w), BF16) for w in widths),
        grid=(n // tm,),
        in_specs=[pl.BlockSpec((tm, wm), lambda i: (i, 0)), pl.BlockSpec((tm, 128), lambda i: (i, 1))] + [tile] * 6,
        out_specs=tuple(pl.BlockSpec((tm, w), lambda i: (i, 0)) for w in widths),
        compiler_params=_cparams(("parallel",)),
        name="split_rope",
    )(hm, hs, *tab_a, *tab_b)


def _gla_prep_kernel(q_ref, k_ref, glr_ref, wg_ref, bg_ref, qe_o, kd_o, s_o, eb_o, la_ref, *, chunks):
    C, SUB, DK = GLA_CHUNK, GLA_SUB, GLA_DK
    NT = (((1,), (1,)), ((), ()))

    row = lax.broadcasted_iota(jnp.int32, (C, C), 0)
    col = lax.broadcasted_iota(jnp.int32, (C, C), 1)
    tril = (col <= row).astype(F32)
    gi = lax.broadcasted_iota(jnp.int32, (SUB, SUB * SUB), 0)
    gr = lax.broadcasted_iota(jnp.int32, (SUB, SUB * SUB), 1)
    gsum = (gr // SUB == gi).astype(BF16)
    pr = lax.broadcasted_iota(jnp.int32, (SUB * SUB, C), 0)
    pc = lax.broadcasted_iota(jnp.int32, (SUB * SUB, C), 1)
    srow = lax.broadcasted_iota(jnp.int32, (SUB, C), 0)
    scol = lax.broadcasted_iota(jnp.int32, (SUB, C), 1)

    z = jnp.dot(glr_ref[...], wg_ref[...], precision=HIGHEST, preferred_element_type=F32) + bg_ref[...]
    la_all = -(jnp.maximum(-z, 0.0) + jnp.log1p(jnp.exp(-jnp.abs(z)))) * (1.0 / GLA_TAU)
    la_ref[...] = la_all
    sub_decay = -jnp.sum(la_all.reshape(chunks * C // SUB, SUB, la_all.shape[1]), axis=1)
    k_mag = jnp.max(jnp.abs(k_ref[...].astype(F32)))
    safe = jnp.logical_and(jnp.max(sub_decay) <= GLA_SAFE_DECAY, k_mag <= GLA_SAFE_K)

    def chunk_head(ci, hh, fast, b_pair):
        r0 = pl.multiple_of(ci * C, C)
        rows = pl.ds(r0, C)
        ks = slice(hh * DK, (hh + 1) * DK)
        q = q_ref[rows, ks].astype(F32) * (DK ** -0.5)
        k = k_ref[rows, ks].astype(F32)
        b = b_pair[:, ks]
        qe_o[rows, ks] = (q * jnp.exp(b)).astype(BF16)

        s_rows = []
        for blk in range(C // SUB):
            r0 = blk * SUB
            b_i, q_i, k_i = b[r0:r0 + SUB], q[r0:r0 + SUB], k[r0:r0 + SUB]
            beta = b[r0:r0 + 1]
            q_t = (q_i * jnp.exp(b_i - beta)).astype(BF16)
            if fast:
                k_t = (k * jnp.exp(jnp.minimum(beta - b, GLA_SAFE_DECAY))).astype(BF16)
                s_blk = lax.dot_general(q_t, k_t, NT, preferred_element_type=F32)
                s_rows.append(jnp.where(scol <= srow + r0, s_blk, 0.0))
                continue
            pair = (q_i[:, None, :] * k_i[None, :, :]
                    * jnp.exp(jnp.minimum(b_i[:, None, :] - b_i[None, :, :], 0.0)))
            pair_sum = jnp.sum(pair.reshape(SUB * SUB, DK), axis=-1, keepdims=True)
            placed = jnp.where(pc == r0 + pr % SUB, pair_sum, 0.0).astype(BF16)
            s_blk = jnp.dot(gsum, placed, preferred_element_type=F32)
            s_blk = jnp.where(scol <= srow + r0, s_blk, 0.0)
            if blk > 0:
                k_t = (k * jnp.exp(jnp.minimum(beta - b, 0.0))).astype(BF16)
                off = lax.dot_general(q_t, k_t, NT, preferred_element_type=F32)
                s_blk = jnp.where(scol < r0, off, s_blk)
            s_rows.append(s_blk)
        s_o[rows, hh * C:(hh + 1) * C] = jnp.concatenate(s_rows, axis=0).astype(BF16)
        b_last = b[C - 1:C]
        kd_o[rows, ks] = (k * jnp.exp(b_last - b)).astype(BF16)
        eb_o[pl.ds(ci, 1), ks] = jnp.exp(b_last)

    def all_chunks(fast):
        per = 4 if fast else 2

        def some_chunks(i, carry):
            for c in range(per):
                ci = per * i + c
                la = la_ref[pl.ds(pl.multiple_of(ci * C, C), C), :]
                b_pair = jnp.dot(tril, la, precision=HIGHEST, preferred_element_type=F32)
                for hh in range(2):
                    chunk_head(ci, hh, fast, b_pair)
            return carry
        lax.fori_loop(0, chunks // per, some_chunks, 0)

    @pl.when(safe)
    def _():
        all_chunks(True)

    @pl.when(jnp.logical_not(safe))
    def _():
        all_chunks(False)


def _gla_scan_kernel(qe_ref, kd_ref, s_ref, eb_ref, v_ref, r_ref, ng_ref, mix_ref, o_ref, state_ref):
    del mix_ref
    C, DK, DV = GLA_CHUNK, GLA_DK, GLA_DV
    NT = (((1,), (1,)), ((), ()))
    TN = (((0,), (0,)), ((), ()))

    @pl.when(pl.program_id(0) == 0)
    def _():
        state_ref[...] = jnp.zeros_like(state_ref)

    for b in range(state_ref.shape[0]):
        for h in range(GLA_HEADS):
            ks = slice(h * DK, (h + 1) * DK)
            vs = slice(h * DV, (h + 1) * DV)
            st = state_ref[b, h]
            v = v_ref[b, :, vs]
            o = (lax.dot_general(qe_ref[b, :, ks], st.astype(BF16), NT, preferred_element_type=F32)
                 + jnp.dot(s_ref[b, :, h * C:(h + 1) * C], v, preferred_element_type=F32))
            state_ref[b, h] = st * eb_ref[b:b + 1, ks] + lax.dot_general(v, kd_ref[b, :, ks], TN,
                                                                         preferred_element_type=F32)
            ms = jnp.mean(o * o, axis=-1, keepdims=True)
            o_n = o * lax.rsqrt(ms + LN_EPS) * ng_ref[...]
            r = r_ref[b, :, vs].astype(F32)
            o_ref[b, :, vs] = (o_n * (r * jax.nn.sigmoid(r))).astype(BF16)


def _gla(ha, glr, w_gate_up, b_gate, gla_norm_g, mix, lead, *, batch, seq, chunks=8):
    C = GLA_CHUNK
    n = batch * seq
    nc = seq // C
    hp = GLA_HEADS // 2
    wq = 2 * GLA_DK
    wqk, wv = GLA_HEADS * GLA_DK, GLA_HEADS * GLA_DV
    tr = chunks * C

    def rows(width, first):
        return pl.BlockSpec((tr, width), lambda i, p: (i, first + p))

    qe, kd, sc, eb = pl.pallas_call(
        functools.partial(_gla_prep_kernel, chunks=chunks),
        out_shape=(jax.ShapeDtypeStruct((n, wqk), BF16), jax.ShapeDtypeStruct((n, wqk), BF16),
                   jax.ShapeDtypeStruct((n, GLA_HEADS * C), BF16), jax.ShapeDtypeStruct((n // C, wqk), F32)),
        grid=(n // tr, hp),
        in_specs=[rows(wq, 0), rows(wq, hp),
                  pl.BlockSpec((tr, GLA_RANK), lambda i, p: (i, 0)),
                  pl.BlockSpec((None, GLA_RANK, wq), lambda i, p: (lead, 0, p)),
                  pl.BlockSpec((None, 1, wq), lambda i, p: (lead, 0, p))],
        out_specs=(rows(wq, 0), rows(wq, 0), rows(2 * C, 0),
                   pl.BlockSpec((chunks, wq), lambda i, p: (i, p))),
        scratch_shapes=[pltpu.VMEM((tr, wq), F32)],
        compiler_params=_cparams(("parallel", "parallel")),
        name="gla_prep",
    )(ha, ha, glr, w_gate_up, b_gate.reshape(-1, 1, wqk))

    def per_chunk(width, first):
        return pl.BlockSpec((batch, C, width), lambda c: (0, c, first))

    eb = eb.reshape(batch, nc, wqk).transpose(1, 0, 2)
    ha3 = ha.reshape(batch, seq, -1)
    out = pl.pallas_call(
        _gla_scan_kernel,
        out_shape=jax.ShapeDtypeStruct((batch, seq, mix.shape[1]), mix.dtype),
        grid=(nc,),
        in_specs=[per_chunk(wqk, 0), per_chunk(wqk, 0), per_chunk(GLA_HEADS * C, 0),
                  pl.BlockSpec((None, batch, wqk), lambda c: (c, 0, 0)),
                  per_chunk(wv, 1), per_chunk(wv, 2),
                  pl.BlockSpec((None, 1, GLA_DV), lambda c: (lead, 0, 0)),
                  pl.BlockSpec(memory_space=pl.ANY)],
        out_specs=per_chunk(wv, MIX_GLA // wv),
        scratch_shapes=[pltpu.VMEM((batch, GLA_HEADS, GLA_DV, GLA_DK), F32)],
        input_output_aliases={7: 0},
        compiler_params=_cparams(("arbitrary",)),
        name="gla_scan",
    )(qe.reshape(batch, seq, wqk), kd.reshape(batch, seq, wqk), sc.reshape(batch, seq, GLA_HEADS * C), eb,
      ha3, ha3, gla_norm_g.reshape(-1, 1, GLA_DV), mix.reshape(batch, seq, -1))
    return out.reshape(mix.shape)


def _diff_kernel(lq1_ref, lk1_ref, lq2_ref, lk2_ref, q_ref, k_ref, vt_ref, g_ref, mix_ref, o_ref,
                 qs_ref, m_ref, l_ref, acc_ref, *, tq, lambda_init):
    del mix_ref
    qi = pl.program_id(2)
    NT = (((1,), (1,)), ((), ()))
    hw = 2 * DIFF_D
    heads = q_ref.shape[1] // hw

    lane = lax.broadcasted_iota(jnp.int32, (tq, hw), 1)
    for hh in range(heads):
        q = q_ref[:, hw * hh:hw * (hh + 1)].astype(F32) * (DIFF_D ** -0.5)
        qs_ref[hh, :tq] = jnp.where(lane < DIFF_D, q, 0.0).astype(BF16)
        qs_ref[hh, tq:] = jnp.where(lane >= DIFF_D, q, 0.0).astype(BF16)
    m_ref[...] = jnp.full_like(m_ref, -jnp.inf)
    l_ref[...] = jnp.zeros_like(l_ref)
    acc_ref[...] = jnp.zeros_like(acc_ref)

    def tile(j, masked):
        r0 = pl.multiple_of(j * tq, tq)
        for hh in range(heads):
            hs = slice(hw * hh, hw * (hh + 1))
            s = lax.dot_general(k_ref[pl.ds(r0, tq), hs], qs_ref[hh], NT, preferred_element_type=F32)
            if masked:
                r = lax.broadcasted_iota(jnp.int32, s.shape, 0)
                c = lax.broadcasted_iota(jnp.int32, s.shape, 1)
                s = jnp.where(r <= jnp.where(c >= tq, c - tq, c), s, -jnp.inf)
            m_prev = m_ref[hh]
            m_new = jnp.maximum(m_prev, jnp.max(s, axis=0, keepdims=True))
            alpha = jnp.exp(m_prev - m_new)
            p = jnp.exp(s - m_new)
            l_ref[hh] = alpha * l_ref[hh] + jnp.sum(p, axis=0, keepdims=True)
            acc_ref[hh] = alpha * acc_ref[hh] + jnp.dot(vt_ref[hs, pl.ds(r0, tq)], p.astype(BF16),
                                                      preferred_element_type=F32)
            m_ref[hh] = m_new

    def full_tile(j, carry):
        tile(j, False)
        return carry

    lax.fori_loop(0, qi, full_tile, 0)
    tile(qi, True)

    lam = (jnp.exp(jnp.sum(lq1_ref[...] * lk1_ref[...], axis=-1, keepdims=True))
           - jnp.exp(jnp.sum(lq2_ref[...] * lk2_ref[...], axis=-1, keepdims=True)) + lambda_init)
    for hh in range(heads):
        o = acc_ref[hh] / l_ref[hh]
        o = o[:, :tq] - lam * o[:, tq:]
        ms = jnp.mean(o * o, axis=0, keepdims=True)
        o = o * lax.rsqrt(ms + LN_EPS) * g_ref[...] * (1.0 - lambda_init)
        o_ref[:, hw * hh:hw * (hh + 1)] = o.T.astype(BF16)


def _diff_attention(dq, dk, dvt, lq1, lk1, lq2, lk2, norm_g, mix, lead, *, batch, seq, lambda_init, tq, heads=2):
    nq = seq // tq
    hw = 2 * DIFF_D
    pw = heads * hw
    lam_spec = pl.BlockSpec((None, 1, DIFF_D), lambda b, p, qi: (lead, 0, 0))
    lq1, lk1, lq2, lk2 = (t.reshape(-1, 1, DIFF_D) for t in (lq1, lk1, lq2, lk2))
    return pl.pallas_call(
        functools.partial(_diff_kernel, tq=tq, lambda_init=lambda_init),
        out_shape=jax.ShapeDtypeStruct(mix.shape, mix.dtype),
        grid=(batch, DIFF_HEADS // heads, nq),
        in_specs=[lam_spec] * 4 + [
            pl.BlockSpec((tq, pw), lambda b, p, qi: (b * nq + qi, p)),
            pl.BlockSpec((seq, pw), lambda b, p, qi: (b, p)),
            pl.BlockSpec((None, pw, seq), lambda b, p, qi: (b, p, 0)),
            pl.BlockSpec((None, hw, 1), lambda b, p, qi: (lead, 0, 0)),
            pl.BlockSpec(memory_space=pl.ANY)],
        out_specs=pl.BlockSpec((tq, pw), lambda b, p, qi: (b * nq + qi, MIX_DIFF // pw + p)),
        scratch_shapes=[pltpu.VMEM((heads, 2 * tq, hw), BF16), pltpu.VMEM((heads, 1, 2 * tq), F32),
                        pltpu.VMEM((heads, 1, 2 * tq), F32), pltpu.VMEM((heads, hw, 2 * tq), F32)],
        input_output_aliases={8: 0},
        compiler_params=_cparams(("parallel", "parallel", "arbitrary")),
        name="diff_attn",
    )(lq1, lk1, lq2, lk2, dq, dk, dvt, norm_g.reshape(-1, hw, 1), mix)


def _dsa_kernel(iq_ref, sq_ref, ikd_ref, wt_ref, sk_ref, svt_ref, mix_ref, o_ref,
                iqm_ref, keys_ref, bias_ref, q5_ref, m_ref, l_ref, acc_ref, *, tq, tk, seq, k_sel):
    del mix_ref
    qi = pl.program_id(1)
    NT = (((1,), (1,)), ((), ()))
    n_tiles = (qi * tq + tq + tk - 1) // tk
    qpos = qi * tq + lax.broadcasted_iota(jnp.int32, (1, tq), 1)
    krow = lax.broadcasted_iota(jnp.int32, (tk, tq), 0)
    idx_scale = (IDX_DIM ** -0.5) * (IDX_HEADS ** -0.5)
    neg_inf = F32(-jnp.inf)

    def tile_start(j):
        return pl.multiple_of(j * tk, tk)

    lane = lax.broadcasted_iota(jnp.int32, (tq, 128), 1)
    for t in range(IDX_HEADS // 2):
        a = iq_ref[:, 128 * t:128 * (t + 1)]
        zero = jnp.zeros_like(a)
        iqm_ref[t, :tq] = jnp.where(lane < IDX_DIM, a, zero)
        iqm_ref[t, tq:] = jnp.where(lane >= IDX_DIM, a, zero)
    wt = wt_ref[...]

    def score_tile(j, carry):
        r0 = tile_start(j)
        kd = ikd_ref[pl.ds(r0, tk), :]
        acc = jnp.zeros((tk, 2 * tq), F32)
        for t in range(IDX_HEADS // 2):
            lg = lax.dot_general(kd, iqm_ref[t], NT, preferred_element_type=F32)
            w2 = jnp.concatenate([wt[2 * t:2 * t + 1], wt[2 * t + 1:2 * t + 2]], axis=1)
            acc = acc + jnp.maximum(lg, 0.0) * w2
        acc = acc[:, :tq] + acc[:, tq:]
        bits = pltpu.bitcast(acc * idx_scale, jnp.int32)
        key = jnp.where(bits < 0, (bits ^ 0x7FFFFFFF) + 1, bits)
        keys_ref[pl.ds(r0, tk), :] = jnp.where(r0 + krow <= qpos, key, INT_MIN)
        return carry

    lax.fori_loop(0, n_tiles, score_tile, 0)

    def count_ones(ones):
        def body(j, cnt):
            r0 = tile_start(j)
            return cnt + jnp.sum(ones(keys_ref[pl.ds(r0, tk), :], r0 + krow), axis=0, keepdims=True)
        return lax.fori_loop(0, n_tiles, body, jnp.zeros((1, tq), jnp.int32))

    def count(pred):
        return count_ones(lambda kk, kp: jnp.where(pred(kk, kp), 1, 0))

    def thr_bit(i, t_u):
        cand_u = t_u | jnp.left_shift(jnp.int32(1), 31 - i)
        cand = cand_u ^ INT_MIN
        return jnp.where(count(lambda kk, kp: kk >= cand) >= k_sel, cand_u, t_u)

    thr = lax.fori_loop(0, 32, thr_bit, jnp.zeros((1, tq), jnp.int32)) ^ INT_MIN
    need = k_sel - count(lambda kk, kp: kk > thr)
    n_eq = count(lambda kk, kp: kk == thr)
    excess = jnp.where(thr == INT_MIN, 0, jnp.where(n_eq > need, 1, 0))

    def tie_search():
        def idx_bit(i, j_lo):
            cand = j_lo | jnp.left_shift(jnp.int32(1), (seq - 1).bit_length() - 1 - i)
            c = count_ones(lambda kk, kp: jnp.where(kk == thr, jnp.where(kp < cand, 1, 0), 0))
            return jnp.where(c < need, cand, j_lo)
        return lax.fori_loop(0, (seq - 1).bit_length(), idx_bit, jnp.zeros((1, tq), jnp.int32))

    j_star = lax.cond(jnp.max(excess) > 0, tie_search, lambda: jnp.full((1, tq), seq, jnp.int32))
    j_star = jnp.where(thr == INT_MIN, -1, jnp.where(excess > 0, j_star, seq))

    def bias_tile(j, carry):
        r0 = tile_start(j)
        kk = keys_ref[pl.ds(r0, tk), :]
        tie = jnp.where(r0 + krow <= j_star, 0.0, neg_inf)
        bias_ref[pl.ds(r0, tk), :] = jnp.where(kk > thr, 0.0, jnp.where(kk == thr, tie, neg_inf))
        return carry

    lax.fori_loop(0, n_tiles, bias_tile, 0)

    scale = DSA_DH ** -0.5
    rep = DSA_HEADS // DSA_KV
    for g in range(DSA_KV):
        for i in range(rep):
            h = g * rep + i
            q5_ref[g, tq * i:tq * (i + 1)] = sq_ref[:, DSA_DH * h:DSA_DH * (h + 1)]
    m_ref[...] = jnp.full_like(m_ref, neg_inf)
    l_ref[...] = jnp.zeros_like(l_ref)
    acc_ref[...] = jnp.zeros_like(acc_ref)

    def attn_tile(j, carry):
        r0 = tile_start(j)
        bias = bias_ref[pl.ds(r0, tk), :]
        bias = jnp.concatenate([bias] * rep, axis=1)
        for g in range(DSA_KV):
            gs = slice(DSA_DH * g, DSA_DH * (g + 1))
            s = lax.dot_general(sk_ref[pl.ds(r0, tk), gs], q5_ref[g], NT, preferred_element_type=F32)
            s = s * scale + bias
            m_prev = m_ref[g]
            m_new = jnp.maximum(m_prev, jnp.max(s, axis=0, keepdims=True))
            m_safe = jnp.where(m_new == neg_inf, 0.0, m_new)
            alpha = jnp.exp(m_prev - m_safe)
            p = jnp.exp(s - m_safe)
            l_ref[g] = alpha * l_ref[g] + jnp.sum(p, axis=0, keepdims=True)
            acc_ref[g] = alpha * acc_ref[g] + jnp.dot(svt_ref[gs, pl.ds(r0, tk)], p.astype(BF16),
                                                    preferred_element_type=F32)
            m_ref[g] = m_new
        return carry

    lax.fori_loop(0, n_tiles, attn_tile, 0)
    for g in range(DSA_KV):
        out_t = acc_ref[g] / l_ref[g]
        for i in range(rep):
            h = g * rep + i
            o_ref[:, DSA_DH * h:DSA_DH * (h + 1)] = out_t[:, tq * i:tq * (i + 1)].T.astype(BF16)


def _dsa(iq, sq, ikd, wt, sk, svt, mix, *, batch, seq, tq, tk):
    nq = seq // tq
    k_sel = min(IDX_TOPK, seq // 4)
    rep = DSA_HEADS // DSA_KV
    return pl.pallas_call(
        functools.partial(_dsa_kernel, tq=tq, tk=tk, seq=seq, k_sel=k_sel),
        out_shape=jax.ShapeDtypeStruct(mix.shape, mix.dtype),
        grid=(batch, nq),
        in_specs=[pl.BlockSpec((tq, W_IQ), lambda b, qi: (b * nq + qi, 0)),
                  pl.BlockSpec((tq, W_SQ), lambda b, qi: (b * nq + qi, 0)),
                  pl.BlockSpec((seq, 128), lambda b, qi: (b, 0)),
                  pl.BlockSpec((None, IDX_HEADS, tq), lambda b, qi: (b, 0, qi)),
                  pl.BlockSpec((seq, W_SKV), lambda b, qi: (b, 0)),
                  pl.BlockSpec((None, W_SKV, seq), lambda b, qi: (b, 0, 0)),
                  pl.BlockSpec(memory_space=pl.ANY)],
        out_specs=pl.BlockSpec((tq, W_SQ), lambda b, qi: (b * nq + qi, MIX_DSA // W_SQ)),
        scratch_shapes=[pltpu.VMEM((IDX_HEADS // 2, 2 * tq, 128), BF16), pltpu.VMEM((seq, tq), jnp.int32),
                        pltpu.VMEM((seq, tq), F32), pltpu.VMEM((DSA_KV, rep * tq, DSA_DH), BF16),
                        pltpu.VMEM((DSA_KV, 1, rep * tq), F32), pltpu.VMEM((DSA_KV, 1, rep * tq), F32),
                        pltpu.VMEM((DSA_KV, DSA_DH, rep * tq), F32)],
        input_output_aliases={6: 0},
        compiler_params=_cparams(("parallel", "arbitrary")),
        name="dsa",
    )(iq, sq, ikd, wt, sk, svt, mix)


def _router_kernel(x_ref, w_ref, b_ref, e_ref, g_ref, xp_ref):
    x = x_ref[...]
    logits = jnp.dot(x, w_ref[...], precision=HIGHEST, preferred_element_type=F32) + b_ref[...]
    lane = lax.broadcasted_iota(jnp.int32, logits.shape, 1)
    neg_inf = F32(-jnp.inf)
    lg = jnp.where(lane < MOE_E, logits, neg_inf)
    m1 = jnp.max(lg, axis=-1, keepdims=True)
    i1 = jnp.min(jnp.where(lg == m1, lane, 128), axis=-1, keepdims=True)
    lg2 = jnp.where(lane == i1, neg_inf, lg)
    m2 = jnp.max(lg2, axis=-1, keepdims=True)
    i2 = jnp.min(jnp.where(lg2 == m2, lane, 128), axis=-1, keepdims=True)
    e21 = jnp.exp(m2 - m1)
    g1 = 1.0 / (1.0 + e21)
    e_ref[...] = jnp.where(lane == 0, i1, jnp.where(lane == 1, i2, 0))
    g_ref[...] = jnp.where(lane == 0, g1, jnp.where(lane == 1, e21 * g1, 0.0))
    half = x.shape[1] // 2
    lo = pltpu.bitcast(x[:, :half].astype(BF16).astype(F32), jnp.uint32)
    hi = pltpu.bitcast(x[:, half:].astype(BF16).astype(F32), jnp.uint32)
    xp_ref[...] = (hi & jnp.uint32(0xFFFF0000)) | (lo >> 16)


def _router(x, router_w, router_b, *, tm=256):
    n, d = x.shape
    row = lambda w: pl.BlockSpec((tm, w), lambda i: (i, 0))
    return pl.pallas_call(
        _router_kernel,
        out_shape=(jax.ShapeDtypeStruct((n, 128), jnp.int32), jax.ShapeDtypeStruct((n, 128), F32),
                   jax.ShapeDtypeStruct((n, d // 2), jnp.uint32)),
        grid=(n // tm,),
        in_specs=[row(d), pl.BlockSpec((d, 128), lambda i: (0, 0)), pl.BlockSpec((1, 128), lambda i: (0, 0))],
        out_specs=(row(128), row(128), row(d // 2)),
        compiler_params=_cparams(("parallel",)),
        name="router",
    )(x, router_w, router_b)


def _dispatch_kernel(pos_ref, xp_ref, buf_ref, o_ref, sem, *, tb):
    del buf_ref
    base = pl.program_id(0) * tb

    def row_copy(i, s):
        return pltpu.make_async_copy(xp_ref.at[pl.ds(i, 1)], o_ref.at[pl.ds(pos_ref[2 * (base + i) + s], 1)], sem)

    def start(i, c):
        row_copy(i, 0).start()
        row_copy(i, 1).start()
        return c

    def wait(i, c):
        row_copy(i, 0).wait()
        row_copy(i, 1).wait()
        return c

    lax.fori_loop(0, tb, start, 0)
    lax.fori_loop(0, tb, wait, 0)


def _dispatch(pos, xp, buf, *, tb=256):
    n, w = xp.shape
    return pl.pallas_call(
        functools.partial(_dispatch_kernel, tb=tb),
        out_shape=jax.ShapeDtypeStruct(buf.shape, buf.dtype),
        grid_spec=pltpu.PrefetchScalarGridSpec(
            num_scalar_prefetch=1,
            grid=(n // tb,),
            in_specs=[pl.BlockSpec((tb, w), lambda i, pos: (i, 0)), pl.BlockSpec(memory_space=pl.ANY)],
            out_specs=pl.BlockSpec(memory_space=pl.ANY),
            scratch_shapes=[pltpu.SemaphoreType.DMA],
        ),
        input_output_aliases={2: 0},
        compiler_params=_cparams(("arbitrary",)),
        name="moe_dispatch",
    )(pos, xp, buf)


def _unpack_rows(word):
    lo = pltpu.bitcast(word << 16, F32).astype(BF16)
    hi = pltpu.bitcast(word & jnp.uint32(0xFFFF0000), F32).astype(BF16)
    return lo, hi


def _expert_tile_refresh(be_ref, bf_ref, nxt_ref, w_hbms, stage_ref, wb_refs, sem, *, lead, tn):
    n, i = pl.program_id(0), pl.program_id(1)

    def fetch(e, col_tile):
        cols = pl.ds(pl.multiple_of(col_tile * tn, tn), tn)
        return [pltpu.make_async_copy(w.at[lead, e, :, cols], stage_ref.at[t], sem)
                for t, w in enumerate(w_hbms)]

    @pl.when(bf_ref[i] == 1)
    def _():
        @pl.when(jnp.logical_and(n == 0, i == 0))
        def _():
            for c in fetch(be_ref[0], 0):
                c.start()

        for c in fetch(be_ref[i], n):
            c.wait()
        for t, wb in enumerate(wb_refs):
            wb[...] = stage_ref[t].astype(BF16)
        last_run = nxt_ref[i] < 0
        e_next = jnp.where(last_run, be_ref[0], nxt_ref[i])
        n_next = jnp.where(last_run, n + 1, n)

        @pl.when(n_next < pl.num_programs(0))
        def _():
            for c in fetch(e_next, n_next):
                c.start()


def _moe_up_kernel(be_ref, bf_ref, na_ref, nxt_ref, a_ref, w1_hbm, w3_hbm, o_ref,
                   stage_ref, w1b_ref, w3b_ref, sem, *, lead, tn):
    i = pl.program_id(1)

    @pl.when(i < na_ref[0])
    def _():
        _expert_tile_refresh(be_ref, bf_ref, nxt_ref, (w1_hbm, w3_hbm), stage_ref, (w1b_ref, w3b_ref), sem,
                             lead=lead, tn=tn)
        lo, hi = _unpack_rows(a_ref[...])
        half = lo.shape[1]
        g = (jnp.dot(lo, w1b_ref[:half], preferred_element_type=F32)
             + jnp.dot(hi, w1b_ref[half:], preferred_element_type=F32))
        u = (jnp.dot(lo, w3b_ref[:half], preferred_element_type=F32)
             + jnp.dot(hi, w3b_ref[half:], preferred_element_type=F32))
        o_ref[...] = (g * jax.nn.sigmoid(g) * u).astype(BF16)

    @pl.when(i >= na_ref[0])
    def _():
        o_ref[...] = jnp.zeros_like(o_ref)


def _moe_down_kernel(be_ref, bf_ref, na_ref, nxt_ref, h_ref, w2_hbm, o_ref, stage_ref, w2b_ref, sem, *, lead, tn):
    i = pl.program_id(1)

    @pl.when(i < na_ref[0])
    def _():
        _expert_tile_refresh(be_ref, bf_ref, nxt_ref, (w2_hbm,), stage_ref, (w2b_ref,), sem, lead=lead, tn=tn)
        o_ref[...] = jnp.dot(h_ref[...], w2b_ref[...], preferred_element_type=F32)

    @pl.when(i >= na_ref[0])
    def _():
        o_ref[...] = jnp.zeros_like(o_ref)


def _moe_ffn(block_e, block_first, n_active, next_e, a_sorted, w1, w3, w2, lead, *, tm, tn):
    r, half = a_sorted.shape
    d = 2 * half
    f = w1.shape[-1]
    nb = r // tm
    hbm = pl.BlockSpec(memory_space=pl.ANY)

    def rows(i, na):
        return jnp.minimum(i, na[0] - 1)

    h = pl.pallas_call(
        functools.partial(_moe_up_kernel, lead=lead, tn=tn),
        out_shape=jax.ShapeDtypeStruct((r, f), BF16),
        grid_spec=pltpu.PrefetchScalarGridSpec(
            num_scalar_prefetch=4,
            grid=(f // tn, nb),
            in_specs=[pl.BlockSpec((tm, half), lambda n, i, be, bf, na, nx: (rows(i, na), 0)), hbm, hbm],
            out_specs=pl.BlockSpec((tm, tn), lambda n, i, be, bf, na, nx: (i, n)),
            scratch_shapes=[pltpu.VMEM((2, d, tn), F32), pltpu.VMEM((d, tn), BF16), pltpu.VMEM((d, tn), BF16),
                            pltpu.SemaphoreType.DMA],
        ),
        compiler_params=_cparams(("arbitrary", "arbitrary")),
        name="moe_up",
    )(block_e, block_first, n_active, next_e, a_sorted, w1, w3)
    return pl.pallas_call(
        functools.partial(_moe_down_kernel, lead=lead, tn=tn),
        out_shape=jax.ShapeDtypeStruct((r, d), F32),
        grid_spec=pltpu.PrefetchScalarGridSpec(
            num_scalar_prefetch=4,
            grid=(d // tn, nb),
            in_specs=[pl.BlockSpec((tm, f), lambda n, i, be, bf, na, nx: (rows(i, na), 0)), hbm],
            out_specs=pl.BlockSpec((tm, tn), lambda n, i, be, bf, na, nx: (i, n)),
            scratch_shapes=[pltpu.VMEM((1, f, tn), F32), pltpu.VMEM((f, tn), BF16), pltpu.SemaphoreType.DMA],
        ),
        compiler_params=_cparams(("arbitrary", "arbitrary")),
        name="moe_down",
    )(block_e, block_first, n_active, next_e, h, w2)


def _combine_ln_kernel(pos_ref, x_ref, gate_ref, g_ref, b_ref, y_ref, o_ref, ybuf_ref, sem, *, tb, alpha):
    step, n_steps = pl.program_id(0), pl.num_programs(0)
    slot = step % 2

    def row_copy(blk, buf, i, s):
        return pltpu.make_async_copy(y_ref.at[pl.ds(pos_ref[2 * (blk * tb + i) + s], 1)],
                                     ybuf_ref.at[buf, s, pl.ds(i, 1)], sem.at[buf])

    def start_block(blk, buf):
        def body(i, c):
            row_copy(blk, buf, i, 0).start()
            row_copy(blk, buf, i, 1).start()
            return c
        lax.fori_loop(0, tb, body, 0)

    def wait_block(blk, buf):
        def body(i, c):
            row_copy(blk, buf, i, 0).wait()
            row_copy(blk, buf, i, 1).wait()
            return c
        lax.fori_loop(0, tb, body, 0)

    @pl.when(step == 0)
    def _():
        start_block(0, 0)

    @pl.when(step + 1 < n_steps)
    def _():
        start_block(step + 1, 1 - slot)

    wait_block(step, slot)
    gate = gate_ref[...]
    f = ybuf_ref[slot, 0] * gate[:, 0:1] + ybuf_ref[slot, 1] * gate[:, 1:2]
    z = alpha * x_ref[...] + f
    mu = jnp.mean(z, axis=-1, keepdims=True)
    zc = z - mu
    var = jnp.mean(zc * zc, axis=-1, keepdims=True)
    o_ref[...] = zc * lax.rsqrt(var + LN_EPS) * g_ref[...] + b_ref[...]


def _combine_ln(pos, x, gates, g, b, y, lead, *, alpha, tb=128):
    n, d = x.shape
    par = pl.BlockSpec((None, 1, d), lambda i, pos: (lead, 0, 0))
    return pl.pallas_call(
        functools.partial(_combine_ln_kernel, tb=tb, alpha=alpha),
        out_shape=jax.ShapeDtypeStruct((n, d), F32),
        grid_spec=pltpu.PrefetchScalarGridSpec(
            num_scalar_prefetch=1,
            grid=(n // tb,),
            in_specs=[pl.BlockSpec((tb, d), lambda i, pos: (i, 0)),
                      pl.BlockSpec((tb, 128), lambda i, pos: (i, 0)),
                      par, par, pl.BlockSpec(memory_space=pl.ANY)],
            out_specs=pl.BlockSpec((tb, d), lambda i, pos: (i, 0)),
            scratch_shapes=[pltpu.VMEM((2, 2, tb, d), F32), pltpu.SemaphoreType.DMA((2,))],
        ),
        compiler_params=_cparams(("arbitrary",)),
        name="moe_combine_ln",
    )(pos, x, gates, g.reshape(-1, 1, d), b.reshape(-1, 1, d), y)


def _moe_layer(x, router_w, router_b, w1, w3, w2, ln_g, ln_b, lead_moe, lead_ln, *, alpha, tm=512, tn=512):
    n, d = x.shape
    rw = jnp.pad(router_w[lead_moe], ((0, 0), (0, 128 - MOE_E)))
    rb = jnp.pad(router_b[lead_moe], (0, 128 - MOE_E)).reshape(1, 128)
    top_e, gates, xp = _router(x, rw, rb)
    flat_e = top_e[:, :MOE_TOPK].reshape(-1)
    onehot = (flat_e[:, None] == jnp.arange(MOE_E, dtype=jnp.int32)[None, :]).astype(jnp.int32)
    rank = jnp.sum((jnp.cumsum(onehot, axis=0) - onehot) * onehot, axis=1)
    counts = jnp.sum(onehot, axis=0)
    padded = (counts + tm - 1) // tm * tm
    end_padded = jnp.cumsum(padded)
    start_padded = end_padded - padded
    pos = (start_padded[flat_e] + rank).astype(jnp.int32)
    nb = -(-(n * MOE_TOPK + MOE_E * (tm - 1)) // tm)
    block_start = jnp.arange(nb, dtype=jnp.int32) * tm
    block_e = jnp.sum((block_start[:, None] >= end_padded[None, :]).astype(jnp.int32), axis=1)
    block_e = jnp.minimum(block_e, MOE_E - 1)
    n_active = (end_padded[-1] // tm).astype(jnp.int32).reshape(1)
    block_first = jnp.concatenate([jnp.ones((1,), jnp.int32), (block_e[1:] != block_e[:-1]).astype(jnp.int32)])
    blk = jnp.arange(nb, dtype=jnp.int32)
    run_start = (block_first == 1) & (blk < n_active[0])
    later_start = jnp.where((blk[None, :] > blk[:, None]) & run_start[None, :], blk[None, :], nb)
    next_start = jnp.min(later_start, axis=1)
    next_e = jnp.where(next_start < nb, block_e[jnp.minimum(next_start, nb - 1)], -1).astype(jnp.int32)
    a_sorted = _dispatch(pos, xp, jnp.zeros((nb * tm, d // 2), jnp.uint32))
    y = _moe_ffn(block_e, block_first, n_active, next_e, a_sorted, w1, w3, w2, lead_moe, tm=tm, tn=tn)
    return _combine_ln(pos, x, gates, ln_g, ln_b, y, lead_ln, alpha=alpha)


def kernel(x, positions, w_in, w_gate_up, b_gate, gla_norm_g, lambda_q1, lambda_k1, lambda_q2, lambda_k2,
           diff_norm_g, w_out, ln1_g, ln1_b, ln2_g, ln2_b, ffn_w1, ffn_w3, ffn_w2, router_w, router_b,
           moe_w1, moe_w3, moe_w2):
    batch, seq, d = x.shape
    n = batch * seq
    depth = w_in.shape[0]
    alpha = (2 * depth) ** 0.25
    xf = x.reshape(n, d)
    xb = xf.astype(BF16)
    pos = positions.reshape(n)
    tab_a, half_a = _rope_tables(pos, DIFF_D)
    tab_b, half_b = _rope_tables(pos, DSA_DH)
    plain = lambda i, j, k: (i, k)
    w_in_t = jnp.swapaxes(w_in, 1, 2)
    for l in range(depth):
        ha = _matmul(xb, w_in_t, n_out=W_GLA, k_steps=1, tm=1024, tn=HM_TN, tk=d, out_dtype=BF16,
                     a_map=plain, w_map=lambda i, j, k, l=l: (l, j, k), w_lead=True, w_t=True,
                     name="in_proj_gla")
        hm = _matmul(xb, w_in_t, n_out=HM_TILES * HM_TN, k_steps=1, tm=1024, tn=HM_TN, tk=d,
                     out_dtype=BF16, a_map=plain,
                     w_map=lambda i, j, k, l=l: (l, HM_BASE // HM_TN + j, k), w_lead=True, w_t=True,
                     n_valid=N_IN, col_start=lambda j: HM_BASE + j * HM_TN, name="in_proj_attn")
        small_tiles = (OFF_GLR // 128, OFF_IK // 128)
        pick = lambda j: jnp.where(j == 0, small_tiles[0], small_tiles[1])
        hs = _matmul(xb, w_in_t, n_out=256, k_steps=1, tm=1024, tn=128, tk=d, out_dtype=F32,
                     a_map=plain, w_map=lambda i, j, k, l=l: (l, pick(j), k), w_lead=True, w_t=True,
                     n_valid=N_IN, col_start=lambda j: pick(j) * 128, name="in_proj_small")
        glr = hs[:, OFF_GLR % 128:OFF_GLR % 128 + GLA_RANK]
        iw0 = 128 + OFF_IW % 128
        wt = hs[:, iw0:iw0 + IDX_HEADS].reshape(batch, seq, IDX_HEADS).transpose(0, 2, 1)
        dq, dk, dv, sq, sk, sv, iq, ikd = _split_rope(hm, hs, tab_a, half_a, tab_b, half_b)
        svt = sv.reshape(batch, seq, W_SKV).transpose(0, 2, 1)
        dvt = dv.reshape(batch, seq, W_DIFF).transpose(0, 2, 1)

        mix = jnp.zeros((n, W_MIX), BF16)
        mix = _gla(ha, glr, w_gate_up, b_gate, gla_norm_g, mix, l, batch=batch, seq=seq)
        lambda_init = 0.8 - 0.6 * math.exp(-0.3 * l)
        mix = _diff_attention(dq, dk, dvt, lambda_q1, lambda_k1, lambda_q2, lambda_k2, diff_norm_g, mix, l,
                              batch=batch, seq=seq, lambda_init=lambda_init, tq=1024, heads=1)
        mix = _dsa(iq, sq, ikd, wt, sk, svt, mix, batch=batch, seq=seq, tq=256, tk=512)

        proj = _out_proj(mix, w_out, l)
        xf, xb = _add_ln(xf, proj, ln1_g, ln1_b, l, alpha=alpha)

        j = l // 2
        if l % 2 == 0:
            f_dim = ffn_w2.shape[1]
            f_pad = -(-f_dim // 1024) * 1024
            hid = _swiglu_up(xb, ffn_w1, ffn_w3, j, tm=1024, tn=256, tk=d, f_out=f_pad)
            down = _matmul(hid, ffn_w2, n_out=d, k_steps=8, tm=2048, tn=1024, tk=f_pad // 8, out_dtype=F32,
                           a_map=plain, w_map=lambda i, j_, k, j=j: (j, k, j_), w_lead=True, k_valid=f_dim,
                           name="ffn_down")
            xf, xb = _add_ln(xf, down, ln2_g, ln2_b, l, alpha=alpha)
        else:
            xf = _moe_layer(xf, router_w, router_b, moe_w1, moe_w3, moe_w2, ln2_g, ln2_b, j, l, alpha=alpha)
            xb = xf.astype(BF16)
    return xf.reshape(batch, seq, d)
```

```python
import functools
import math

import jax
import jax.numpy as jnp
from jax import lax
from jax.experimental import pallas as pl
from jax.experimental.pallas import tpu as pltpu

F32 = jnp.float32
BF16 = jnp.bfloat16
HIGHEST = lax.Precision.HIGHEST
INT_MIN = -(2 ** 31)

ROPE_THETA = 500000.0
GLA_HEADS, GLA_DK, GLA_DV, GLA_RANK, GLA_TAU, GLA_CHUNK = 4, 192, 384, 16, 16.0, 64
GLA_SUB = 16
GLA_SAFE_DECAY = 40.0
GLA_SAFE_K = 1e18
DIFF_HEADS, DIFF_D = 10, 64
DSA_HEADS, DSA_KV, DSA_DH = 10, 2, 128
IDX_HEADS, IDX_DIM, IDX_TOPK = 32, 64, 256
MOE_E, MOE_TOPK = 8, 2
LN_EPS = 1e-5
SOFTMAX_SAFE_LOGIT = 60.0

W_GLA = 2 * GLA_HEADS * GLA_DK + 2 * GLA_HEADS * GLA_DV
OFF_GLR = W_GLA
OFF_DQ = OFF_GLR + GLA_RANK
W_DIFF = DIFF_HEADS * 2 * DIFF_D
OFF_DK, OFF_DV = OFF_DQ + W_DIFF, OFF_DQ + 2 * W_DIFF
OFF_SQ = OFF_DQ + 3 * W_DIFF
W_SQ, W_SKV = DSA_HEADS * DSA_DH, DSA_KV * DSA_DH
OFF_SK, OFF_SV = OFF_SQ + W_SQ, OFF_SQ + W_SQ + W_SKV
OFF_IQ = OFF_SV + W_SKV
W_IQ = IDX_HEADS * IDX_DIM
OFF_IK = OFF_IQ + W_IQ
OFF_IW = OFF_IK + IDX_DIM
N_IN = OFF_IW + IDX_HEADS

HM_BASE = W_GLA
HM_TN = 512
HM_TILES = -(-(N_IN - HM_BASE) // HM_TN)

MIX_GLA, MIX_DIFF, MIX_DSA = 0, 2560, 3840
W_MIX = MIX_DSA + DSA_HEADS * DSA_DH

VMEM_LIMIT = 56 * 1024 * 1024


def _cparams(sem, vmem=VMEM_LIMIT):
    return pltpu.CompilerParams(dimension_semantics=sem, vmem_limit_bytes=vmem)


def _mm_kernel(a_ref, w_ref, o_ref, acc_ref=None, *, k_valid, n_valid, col_start, w_t):
    k = pl.program_id(2)
    k_ax, n_ax = (1, 0) if w_t else (0, 1)

    def partial_product():
        w = w_ref[...]
        if k_valid is not None:
            kk = k * w.shape[k_ax] + lax.broadcasted_iota(jnp.int32, w.shape, k_ax)
            w = jnp.where(kk < k_valid, w, 0.0)
        if n_valid is not None:
            col = col_start(pl.program_id(1)) + lax.broadcasted_iota(jnp.int32, w.shape, n_ax)
            w = jnp.where(col < n_valid, w, 0.0)
        return lax.dot_general(a_ref[...], w.astype(BF16), (((1,), (k_ax,)), ((), ())),
                               preferred_element_type=F32)

    if acc_ref is None:
        o_ref[...] = partial_product().astype(o_ref.dtype)
        return

    @pl.when(k == 0)
    def _():
        acc_ref[...] = partial_product()

    @pl.when(k > 0)
    def _():
        acc_ref[...] += partial_product()

    @pl.when(k == pl.num_programs(2) - 1)
    def _():
        o_ref[...] = acc_ref[...].astype(o_ref.dtype)


def _matmul(a, w, *, n_out, k_steps, tm, tn, tk, out_dtype, a_map, w_map, w_lead=False, k_valid=None,
            n_valid=None, col_start=None, w_t=False, name="mm"):
    m = a.shape[0]
    w_block = (tn, tk) if w_t else (tk, tn)
    if w_lead:
        w_block = (None,) + w_block
    return pl.pallas_call(
        functools.partial(_mm_kernel, k_valid=k_valid, n_valid=n_valid, col_start=col_start, w_t=w_t),
        out_shape=jax.ShapeDtypeStruct((m, n_out), out_dtype),
        grid=(m // tm, pl.cdiv(n_out, tn), k_steps),
        in_specs=[pl.BlockSpec((tm, tk), a_map), pl.BlockSpec(w_block, w_map)],
        out_specs=pl.BlockSpec((tm, tn), lambda i, j, k: (i, j)),
        scratch_shapes=[pltpu.VMEM((tm, tn), F32)] if k_steps > 1 else [],
        compiler_params=_cparams(("parallel", "parallel", "arbitrary")),
        name=name,
    )(a, w)


def _swiglu_kernel(a_ref, w1_ref, w3_ref, o_ref, acc1_ref=None, acc3_ref=None, *, n_valid):
    k = pl.program_id(2)

    def partial_product(w_ref):
        return jnp.dot(a_ref[...], w_ref[...].astype(BF16), preferred_element_type=F32)

    def finish(g, u):
        h = g * jax.nn.sigmoid(g) * u
        col = pl.program_id(1) * h.shape[1] + lax.broadcasted_iota(jnp.int32, h.shape, 1)
        o_ref[...] = jnp.where(col < n_valid, h, 0.0).astype(o_ref.dtype)

    if acc1_ref is None:
        finish(partial_product(w1_ref), partial_product(w3_ref))
        return

    @pl.when(k == 0)
    def _():
        acc1_ref[...] = partial_product(w1_ref)
        acc3_ref[...] = partial_product(w3_ref)

    @pl.when(k > 0)
    def _():
        acc1_ref[...] += partial_product(w1_ref)
        acc3_ref[...] += partial_product(w3_ref)

    @pl.when(k == pl.num_programs(2) - 1)
    def _():
        finish(acc1_ref[...], acc3_ref[...])


def _swiglu_up(a, w1, w3, lead, *, tm, tn, tk, f_out):
    m, kdim = a.shape
    f = w1.shape[-1]
    last = (f - 1) // tn
    wspec = pl.BlockSpec((None, tk, tn), lambda i, j, k: (lead, k, jnp.minimum(j, last)))
    k_steps = kdim // tk
    return pl.pallas_call(
        functools.partial(_swiglu_kernel, n_valid=f),
        out_shape=jax.ShapeDtypeStruct((m, f_out), BF16),
        grid=(m // tm, f_out // tn, k_steps),
        in_specs=[pl.BlockSpec((tm, tk), lambda i, j, k: (i, k)), wspec, wspec],
        out_specs=pl.BlockSpec((tm, tn), lambda i, j, k: (i, j)),
        scratch_shapes=[pltpu.VMEM((tm, tn), F32), pltpu.VMEM((tm, tn), F32)] if k_steps > 1 else [],
        compiler_params=_cparams(("parallel", "parallel", "arbitrary")),
        name="swiglu_up",
    )(a, w1, w3)


def _out_proj_kernel(mix_ref, w_ref, o_ref):
    n_gla = GLA_HEADS * GLA_DV
    w = w_ref[...].astype(BF16)
    o_ref[...] = (jnp.dot(mix_ref[:, MIX_GLA:MIX_GLA + n_gla], w[:n_gla], preferred_element_type=F32)
                  + jnp.dot(mix_ref[:, MIX_DIFF:], w[n_gla:], preferred_element_type=F32))


def _out_proj(mix, w_out, lead, *, tm=1024, tn=256):
    m, wm = mix.shape
    d = w_out.shape[-1]
    return pl.pallas_call(
        _out_proj_kernel,
        out_shape=jax.ShapeDtypeStruct((m, d), F32),
        grid=(m // tm, d // tn),
        in_specs=[pl.BlockSpec((tm, wm), lambda i, j: (i, 0)),
                  pl.BlockSpec((None, w_out.shape[1], tn), lambda i, j: (lead, 0, j))],
        out_specs=pl.BlockSpec((tm, tn), lambda i, j: (i, j)),
        compiler_params=_cparams(("parallel", "parallel")),
        name="out_proj",
    )(mix, w_out)


def _add_ln_kernel(x_ref, y_ref, g_ref, b_ref, o_ref, ob_ref, *, alpha):
    z = alpha * x_ref[...] + y_ref[...].astype(F32)
    mu = jnp.mean(z, axis=-1, keepdims=True)
    zc = z - mu
    var = jnp.mean(zc * zc, axis=-1, keepdims=True)
    out = zc * lax.rsqrt(var + LN_EPS) * g_ref[...] + b_ref[...]
    o_ref[...] = out
    ob_ref[...] = out.astype(BF16)


def _add_ln(x, y, g, b, lead, *, alpha, tm=256):
    m, d = x.shape
    row = pl.BlockSpec((tm, d), lambda i: (i, 0))
    par = pl.BlockSpec((None, 1, d), lambda i: (lead, 0, 0))
    g, b = g.reshape(-1, 1, d), b.reshape(-1, 1, d)
    return pl.pallas_call(
        functools.partial(_add_ln_kernel, alpha=alpha),
        out_shape=(jax.ShapeDtypeStruct((m, d), F32), jax.ShapeDtypeStruct((m, d), BF16)),
        grid=(m // tm,),
        in_specs=[row, row, par, par],
        out_specs=(row, row),
        compiler_params=_cparams(("parallel",)),
        name="add_ln",
    )(x, y, g, b)


def _rope_tables(positions, width):
    rot = width // 4
    half = rot // 2
    inv_freq = 1.0 / (ROPE_THETA ** (jnp.arange(half, dtype=F32) * 2.0 / rot))
    ang = positions.astype(F32)[:, None] * inv_freq
    cos, sin = jnp.cos(ang), jnp.sin(ang)
    n = positions.shape[0]
    one = jnp.ones((n, width - rot), F32)
    zero_h = jnp.zeros((n, half), F32)
    zero_r = jnp.zeros((n, width - rot), F32)
    c = jnp.concatenate([cos, cos, one], axis=1)
    s_up = jnp.concatenate([-sin, zero_h, zero_r], axis=1)
    s_dn = jnp.concatenate([zero_h, sin, zero_r], axis=1)
    rep = 128 // width
    return tuple(jnp.tile(t, (1, rep)) for t in (c, s_up, s_dn)), half


def _rope_tile(x, c, s_up, s_dn, half):
    return x * c + pltpu.roll(x, 128 - half, 1) * s_up + pltpu.roll(x, half, 1) * s_dn


def _split_rope_kernel(hm_ref, hc_ref, ca_ref, ua_ref, da_ref, cb_ref, ub_ref, db_ref,
                       dq_o, dk_o, dv_o, sq_o, sk_o, sv_o, iq_o, ik_o, *, shift, ik_shift, half_a, half_b):
    ta = (ca_ref[...], ua_ref[...], da_ref[...])
    tb = (cb_ref[...], ub_ref[...], db_ref[...])
    lane = lax.broadcasted_iota(jnp.int32, ca_ref.shape, 1)

    def shifted(k):
        return pltpu.roll(hm_ref[:, 128 * k:128 * (k + 1)].astype(F32), 128 - shift, 1)

    m = 0
    nxt = shifted(0)
    for dst, tab, half in ((dq_o, ta, half_a), (dk_o, ta, half_a), (dv_o, None, 0), (sq_o, tb, half_b),
                           (sk_o, tb, half_b), (sv_o, None, 0), (iq_o, ta, half_a)):
        for t in range(dst.shape[1] // 128):
            cur, nxt = nxt, shifted(m + 1)
            y = jnp.where(lane < 128 - shift, cur, nxt)
            if tab is not None:
                y = _rope_tile(y, *tab, half)
            dst[:, 128 * t:128 * (t + 1)] = y.astype(BF16)
            m += 1
    ik = _rope_tile(pltpu.roll(hc_ref[...], 128 - ik_shift, 1), *ta, half_a)
    ik_o[...] = jnp.where(lane < IDX_DIM, ik, pltpu.roll(ik, IDX_DIM, 1)).astype(BF16)


def _split_rope(hm, hs, tab_a, half_a, tab_b, half_b, *, tm=256):
    n, wm = hm.shape
    tile = pl.BlockSpec((tm, 128), lambda i: (i, 0))
    widths = (W_DIFF, W_DIFF, W_DIFF, W_SQ, W_SKV, W_SKV, W_IQ, 128)
    return pl.pallas_call(
        functools.partial(_split_rope_kernel, shift=OFF_DQ - HM_BASE, ik_shift=OFF_IK % 128,
                          half_a=half_a, half_b=half_b),
        out_shape=tuple(jax.ShapeDtypeStruct((n, w), BF16) for w in widths),
        grid=(n // tm,),
        in_specs=[pl.BlockSpec((tm, wm), lambda i: (i, 0)), pl.BlockSpec((tm, 128), lambda i: (i, 1))] + [tile] * 6,
        out_specs=tuple(pl.BlockSpec((tm, w), lambda i: (i, 0)) for w in widths),
        compiler_params=_cparams(("parallel",)),
        name="split_rope",
    )(hm, hs, *tab_a, *tab_b)


def _gla_prep_kernel(q_ref, k_ref, glr_ref, wg_ref, bg_ref, qe_o, kd_o, s_o, eb_o, la_ref, *, chunks):
    C, SUB, DK = GLA_CHUNK, GLA_SUB, GLA_DK
    NT = (((1,), (1,)), ((), ()))

    row = lax.broadcasted_iota(jnp.int32, (C, C), 0)
    col = lax.broadcasted_iota(jnp.int32, (C, C), 1)
    tril = (col <= row).astype(F32)
    gi = lax.broadcasted_iota(jnp.int32, (SUB, SUB * SUB), 0)
    gr = lax.broadcasted_iota(jnp.int32, (SUB, SUB * SUB), 1)
    gsum = (gr // SUB == gi).astype(BF16)
    pr = lax.broadcasted_iota(jnp.int32, (SUB * SUB, C), 0)
    pc = lax.broadcasted_iota(jnp.int32, (SUB * SUB, C), 1)
    srow = lax.broadcasted_iota(jnp.int32, (SUB, C), 0)
    scol = lax.broadcasted_iota(jnp.int32, (SUB, C), 1)

    z = jnp.dot(glr_ref[...], wg_ref[...], precision=HIGHEST, preferred_element_type=F32) + bg_ref[...]
    la_all = -(jnp.maximum(-z, 0.0) + jnp.log1p(jnp.exp(-jnp.abs(z)))) * (1.0 / GLA_TAU)
    la_ref[...] = la_all
    sub_decay = -jnp.sum(la_all.reshape(chunks * C // SUB, SUB, la_all.shape[1]), axis=1)
    k_mag = jnp.max(jnp.abs(k_ref[...].astype(F32)))
    safe = jnp.logical_and(jnp.max(sub_decay) <= GLA_SAFE_DECAY, k_mag <= GLA_SAFE_K)

    def chunk_head(ci, hh, fast, b_pair):
        r0 = pl.multiple_of(ci * C, C)
        rows = pl.ds(r0, C)
        ks = slice(hh * DK, (hh + 1) * DK)
        q = q_ref[rows, ks].astype(F32) * (DK ** -0.5)
        k = k_ref[rows, ks].astype(F32)
        b = b_pair[:, ks]
        qe_o[rows, ks] = (q * jnp.exp(b)).astype(BF16)

        s_rows = []
        for blk in range(C // SUB):
            r0 = blk * SUB
            b_i, q_i, k_i = b[r0:r0 + SUB], q[r0:r0 + SUB], k[r0:r0 + SUB]
            beta = b[r0:r0 + 1]
            q_t = (q_i * jnp.exp(b_i - beta)).astype(BF16)
            if fast:
                k_t = (k * jnp.exp(jnp.minimum(beta - b, GLA_SAFE_DECAY))).astype(BF16)
                s_blk = lax.dot_general(q_t, k_t, NT, preferred_element_type=F32)
                s_rows.append(jnp.where(scol <= srow + r0, s_blk, 0.0))
                continue
            pair = (q_i[:, None, :] * k_i[None, :, :]
                    * jnp.exp(jnp.minimum(b_i[:, None, :] - b_i[None, :, :], 0.0)))
            pair_sum = jnp.sum(pair.reshape(SUB * SUB, DK), axis=-1, keepdims=True)
            placed = jnp.where(pc == r0 + pr % SUB, pair_sum, 0.0).astype(BF16)
            s_blk = jnp.dot(gsum, placed, preferred_element_type=F32)
            s_blk = jnp.where(scol <= srow + r0, s_blk, 0.0)
            if blk > 0:
                k_t = (k * jnp.exp(jnp.minimum(beta - b, 0.0))).astype(BF16)
                off = lax.dot_general(q_t, k_t, NT, preferred_element_type=F32)
                s_blk = jnp.where(scol < r0, off, s_blk)
            s_rows.append(s_blk)
        s_o[rows, hh * C:(hh + 1) * C] = jnp.concatenate(s_rows, axis=0).astype(BF16)
        b_last = b[C - 1:C]
        kd_o[rows, ks] = (k * jnp.exp(b_last - b)).astype(BF16)
        eb_o[pl.ds(ci, 1), ks] = jnp.exp(b_last)

    def all_chunks(fast):
        per = 4 if fast else 2

        def some_chunks(i, carry):
            for c in range(per):
                ci = per * i + c
                la = la_ref[pl.ds(pl.multiple_of(ci * C, C), C), :]
                b_pair = jnp.dot(tril, la, precision=HIGHEST, preferred_element_type=F32)
                for hh in range(2):
                    chunk_head(ci, hh, fast, b_pair)
            return carry
        lax.fori_loop(0, chunks // per, some_chunks, 0)

    @pl.when(safe)
    def _():
        all_chunks(True)

    @pl.when(jnp.logical_not(safe))
    def _():
        all_chunks(False)


def _gla_scan_kernel(qe_ref, kd_ref, s_ref, eb_ref, v_ref, r_ref, ng_ref, mix_ref, o_ref, state_ref):
    del mix_ref
    C, DK, DV = GLA_CHUNK, GLA_DK, GLA_DV
    NT = (((1,), (1,)), ((), ()))
    TN = (((0,), (0,)), ((), ()))

    @pl.when(pl.program_id(0) == 0)
    def _():
        state_ref[...] = jnp.zeros_like(state_ref)

    for b in range(state_ref.shape[0]):
        for h in range(GLA_HEADS):
            ks = slice(h * DK, (h + 1) * DK)
            vs = slice(h * DV, (h + 1) * DV)
            st = state_ref[b, h]
            v = v_ref[b, :, vs]
            o = (lax.dot_general(qe_ref[b, :, ks], st.astype(BF16), NT, preferred_element_type=F32)
                 + jnp.dot(s_ref[b, :, h * C:(h + 1) * C], v, preferred_element_type=F32))
            state_ref[b, h] = st * eb_ref[b:b + 1, ks] + lax.dot_general(v, kd_ref[b, :, ks], TN,
                                                                         preferred_element_type=F32)
            ms = jnp.mean(o * o, axis=-1, keepdims=True)
            o_n = o * lax.rsqrt(ms + LN_EPS) * ng_ref[...]
            r = r_ref[b, :, vs].astype(F32)
            o_ref[b, :, vs] = (o_n * (r * jax.nn.sigmoid(r))).astype(BF16)


def _gla(ha, glr, w_gate_up, b_gate, gla_norm_g, mix, lead, *, batch, seq, chunks=8):
    C = GLA_CHUNK
    n = batch * seq
    nc = seq // C
    hp = GLA_HEADS // 2
    wq = 2 * GLA_DK
    wqk, wv = GLA_HEADS * GLA_DK, GLA_HEADS * GLA_DV
    tr = chunks * C

    def rows(width, first):
        return pl.BlockSpec((tr, width), lambda i, p: (i, first + p))

    qe, kd, sc, eb = pl.pallas_call(
        functools.partial(_gla_prep_kernel, chunks=chunks),
        out_shape=(jax.ShapeDtypeStruct((n, wqk), BF16), jax.ShapeDtypeStruct((n, wqk), BF16),
                   jax.ShapeDtypeStruct((n, GLA_HEADS * C), BF16), jax.ShapeDtypeStruct((n // C, wqk), F32)),
        grid=(n // tr, hp),
        in_specs=[rows(wq, 0), rows(wq, hp),
                  pl.BlockSpec((tr, GLA_RANK), lambda i, p: (i, 0)),
                  pl.BlockSpec((None, GLA_RANK, wq), lambda i, p: (lead, 0, p)),
                  pl.BlockSpec((None, 1, wq), lambda i, p: (lead, 0, p))],
        out_specs=(rows(wq, 0), rows(wq, 0), rows(2 * C, 0),
                   pl.BlockSpec((chunks, wq), lambda i, p: (i, p))),
        scratch_shapes=[pltpu.VMEM((tr, wq), F32)],
        compiler_params=_cparams(("parallel", "parallel")),
        name="gla_prep",
    )(ha, ha, glr, w_gate_up, b_gate.reshape(-1, 1, wqk))

    def per_chunk(width, first):
        return pl.BlockSpec((batch, C, width), lambda c: (0, c, first))

    eb = eb.reshape(batch, nc, wqk).transpose(1, 0, 2)
    ha3 = ha.reshape(batch, seq, -1)
    out = pl.pallas_call(
        _gla_scan_kernel,
        out_shape=jax.ShapeDtypeStruct((batch, seq, mix.shape[1]), mix.dtype),
        grid=(nc,),
        in_specs=[per_chunk(wqk, 0), per_chunk(wqk, 0), per_chunk(GLA_HEADS * C, 0),
                  pl.BlockSpec((None, batch, wqk), lambda c: (c, 0, 0)),
                  per_chunk(wv, 1), per_chunk(wv, 2),
                  pl.BlockSpec((None, 1, GLA_DV), lambda c: (lead, 0, 0)),
                  pl.BlockSpec(memory_space=pl.ANY)],
        out_specs=per_chunk(wv, MIX_GLA // wv),
        scratch_shapes=[pltpu.VMEM((batch, GLA_HEADS, GLA_DV, GLA_DK), F32)],
        input_output_aliases={7: 0},
        compiler_params=_cparams(("arbitrary",)),
        name="gla_scan",
    )(qe.reshape(batch, seq, wqk), kd.reshape(batch, seq, wqk), sc.reshape(batch, seq, GLA_HEADS * C), eb,
      ha3, ha3, gla_norm_g.reshape(-1, 1, GLA_DV), mix.reshape(batch, seq, -1))
    return out.reshape(mix.shape)


def _diff_kernel(lq1_ref, lk1_ref, lq2_ref, lk2_ref, q_ref, k_ref, vt_ref, g_ref, mix_ref, o_ref,
                 qs_ref, m_ref, l_ref, acc_ref, ksq_ref, *, tq, lambda_init):
    del mix_ref
    qi = pl.program_id(2)
    NT = (((1,), (1,)), ((), ()))
    hw = 2 * DIFF_D
    heads = q_ref.shape[1] // hw

    lane = lax.broadcasted_iota(jnp.int32, (tq, hw), 1)
    logit_bound_sq = F32(0.0)
    for hh in range(heads):
        hs = slice(hw * hh, hw * (hh + 1))
        q = q_ref[:, hs].astype(F32) * (DIFF_D ** -0.5)
        qs_ref[hh, :tq] = jnp.where(lane < DIFF_D, q, 0.0).astype(BF16)
        qs_ref[hh, tq:] = jnp.where(lane >= DIFF_D, q, 0.0).astype(BF16)

        @pl.when(qi == 0)
        def _():
            k = k_ref[:, hs].astype(F32)
            ksq_ref[hh] = jnp.max(jnp.sum(k * k, axis=-1, keepdims=True))

        logit_bound_sq = jnp.maximum(logit_bound_sq,
                                     jnp.max(jnp.sum(q * q, axis=-1, keepdims=True)) * ksq_ref[hh])
    m_ref[...] = jnp.full_like(m_ref, -jnp.inf)
    l_ref[...] = jnp.zeros_like(l_ref)
    acc_ref[...] = jnp.zeros_like(acc_ref)
    bounded = logit_bound_sq <= SOFTMAX_SAFE_LOGIT ** 2

    def tile(j, masked, use_max):
        r0 = pl.multiple_of(j * tq, tq)
        for hh in range(heads):
            hs = slice(hw * hh, hw * (hh + 1))
            s = lax.dot_general(k_ref[pl.ds(r0, tq), hs], qs_ref[hh], NT, preferred_element_type=F32)
            if masked:
                r = lax.broadcasted_iota(jnp.int32, s.shape, 0)
                c = lax.broadcasted_iota(jnp.int32, s.shape, 1)
                s = jnp.where(r <= jnp.where(c >= tq, c - tq, c), s, -jnp.inf)
            if use_max:
                m_prev = m_ref[hh]
                m_new = jnp.maximum(m_prev, jnp.max(s, axis=0, keepdims=True))
                alpha = jnp.exp(m_prev - m_new)
                p = jnp.exp(s - m_new)
                l_ref[hh] = alpha * l_ref[hh] + jnp.sum(p, axis=0, keepdims=True)
                acc_ref[hh] = alpha * acc_ref[hh] + jnp.dot(vt_ref[hs, pl.ds(r0, tq)], p.astype(BF16),
                                                          preferred_element_type=F32)
                m_ref[hh] = m_new
            else:
                p = jnp.exp(s)
                l_ref[hh] += jnp.sum(p, axis=0, keepdims=True)
                acc_ref[hh] += jnp.dot(vt_ref[hs, pl.ds(r0, tq)], p.astype(BF16), preferred_element_type=F32)

    def all_tiles(use_max):
        def full_tile(j, carry):
            tile(j, False, use_max)
            return carry
        lax.fori_loop(0, qi, full_tile, 0)
        tile(qi, True, use_max)

    @pl.when(bounded)
    def _():
        all_tiles(False)

    @pl.when(jnp.logical_not(bounded))
    def _():
        all_tiles(True)

    lam = (jnp.exp(jnp.sum(lq1_ref[...] * lk1_ref[...], axis=-1, keepdims=True))
           - jnp.exp(jnp.sum(lq2_ref[...] * lk2_ref[...], axis=-1, keepdims=True)) + lambda_init)
    for hh in range(heads):
        o = acc_ref[hh] / l_ref[hh]
        o = o[:, :tq] - lam * o[:, tq:]
        ms = jnp.mean(o * o, axis=0, keepdims=True)
        o = o * lax.rsqrt(ms + LN_EPS) * g_ref[...] * (1.0 - lambda_init)
        o_ref[:, hw * hh:hw * (hh + 1)] = o.T.astype(BF16)


def _diff_attention(dq, dk, dvt, lq1, lk1, lq2, lk2, norm_g, mix, lead, *, batch, seq, lambda_init, tq, heads=2):
    nq = seq // tq
    hw = 2 * DIFF_D
    pw = heads * hw
    lam_spec = pl.BlockSpec((None, 1, DIFF_D), lambda b, p, qi: (lead, 0, 0))
    lq1, lk1, lq2, lk2 = (t.reshape(-1, 1, DIFF_D) for t in (lq1, lk1, lq2, lk2))
    return pl.pallas_call(
        functools.partial(_diff_kernel, tq=tq, lambda_init=lambda_init),
        out_shape=jax.ShapeDtypeStruct(mix.shape, mix.dtype),
        grid=(batch, DIFF_HEADS // heads, nq),
        in_specs=[lam_spec] * 4 + [
            pl.BlockSpec((tq, pw), lambda b, p, qi: (b * nq + qi, p)),
            pl.BlockSpec((seq, pw), lambda b, p, qi: (b, p)),
            pl.BlockSpec((None, pw, seq), lambda b, p, qi: (b, p, 0)),
            pl.BlockSpec((None, hw, 1), lambda b, p, qi: (lead, 0, 0)),
            pl.BlockSpec(memory_space=pl.ANY)],
        out_specs=pl.BlockSpec((tq, pw), lambda b, p, qi: (b * nq + qi, MIX_DIFF // pw + p)),
        scratch_shapes=[pltpu.VMEM((heads, 2 * tq, hw), BF16), pltpu.VMEM((heads, 1, 2 * tq), F32),
                        pltpu.VMEM((heads, 1, 2 * tq), F32), pltpu.VMEM((heads, hw, 2 * tq), F32),
                        pltpu.SMEM((heads,), F32)],
        input_output_aliases={8: 0},
        compiler_params=_cparams(("parallel", "parallel", "arbitrary")),
        name="diff_attn",
    )(lq1, lk1, lq2, lk2, dq, dk, dvt, norm_g.reshape(-1, hw, 1), mix)


def _dsa_kernel(iq_ref, sq_ref, ikd_ref, wt_ref, sk_ref, svt_ref, mix_ref, o_ref,
                iqm_ref, keys_ref, bias_ref, q5_ref, m_ref, l_ref, acc_ref, ksq_ref, *, tq, tk, seq, k_sel):
    del mix_ref
    qi = pl.program_id(1)
    NT = (((1,), (1,)), ((), ()))
    n_tiles = (qi * tq + tq + tk - 1) // tk
    qpos = qi * tq + lax.broadcasted_iota(jnp.int32, (1, tq), 1)
    krow = lax.broadcasted_iota(jnp.int32, (tk, tq), 0)
    idx_scale = (IDX_DIM ** -0.5) * (IDX_HEADS ** -0.5)
    neg_inf = F32(-jnp.inf)

    def tile_start(j):
        return pl.multiple_of(j * tk, tk)

    lane = lax.broadcasted_iota(jnp.int32, (tq, 128), 1)
    for t in range(IDX_HEADS // 2):
        a = iq_ref[:, 128 * t:128 * (t + 1)]
        zero = jnp.zeros_like(a)
        iqm_ref[t, :tq] = jnp.where(lane < IDX_DIM, a, zero)
        iqm_ref[t, tq:] = jnp.where(lane >= IDX_DIM, a, zero)
    wt = wt_ref[...]

    def score_tile(j, carry):
        r0 = tile_start(j)
        kd = ikd_ref[pl.ds(r0, tk), :]
        acc = jnp.zeros((tk, 2 * tq), F32)
        for t in range(IDX_HEADS // 2):
            lg = lax.dot_general(kd, iqm_ref[t], NT, preferred_element_type=F32)
            w2 = jnp.concatenate([wt[2 * t:2 * t + 1], wt[2 * t + 1:2 * t + 2]], axis=1)
            acc = acc + jnp.maximum(lg, 0.0) * w2
        acc = acc[:, :tq] + acc[:, tq:]
        bits = pltpu.bitcast(acc * idx_scale, jnp.int32)
        key = jnp.where(bits < 0, (bits ^ 0x7FFFFFFF) + 1, bits)
        keys_ref[pl.ds(r0, tk), :] = jnp.where(r0 + krow <= qpos, key, INT_MIN)
        return carry

    lax.fori_loop(0, n_tiles, score_tile, 0)

    def count_ones(ones):
        def body(j, cnt):
            r0 = tile_start(j)
            return cnt + jnp.sum(ones(keys_ref[pl.ds(r0, tk), :], r0 + krow), axis=0, keepdims=True)
        return lax.fori_loop(0, n_tiles, body, jnp.zeros((1, tq), jnp.int32))

    def count(pred):
        return count_ones(lambda kk, kp: jnp.where(pred(kk, kp), 1, 0))

    def thr_bit(i, t_u):
        cand_u = t_u | jnp.left_shift(jnp.int32(1), 31 - i)
        cand = cand_u ^ INT_MIN
        return jnp.where(count(lambda kk, kp: kk >= cand) >= k_sel, cand_u, t_u)

    thr = lax.fori_loop(0, 32, thr_bit, jnp.zeros((1, tq), jnp.int32)) ^ INT_MIN
    need = k_sel - count(lambda kk, kp: kk > thr)
    n_eq = count(lambda kk, kp: kk == thr)
    excess = jnp.where(thr == INT_MIN, 0, jnp.where(n_eq > need, 1, 0))

    def tie_search():
        def idx_bit(i, j_lo):
            cand = j_lo | jnp.left_shift(jnp.int32(1), (seq - 1).bit_length() - 1 - i)
            c = count_ones(lambda kk, kp: jnp.where(kk == thr, jnp.where(kp < cand, 1, 0), 0))
            return jnp.where(c < need, cand, j_lo)
        return lax.fori_loop(0, (seq - 1).bit_length(), idx_bit, jnp.zeros((1, tq), jnp.int32))

    j_star = lax.cond(jnp.max(excess) > 0, tie_search, lambda: jnp.full((1, tq), seq, jnp.int32))
    j_star = jnp.where(thr == INT_MIN, -1, jnp.where(excess > 0, j_star, seq))

    def bias_tile(j, carry):
        r0 = tile_start(j)
        kk = keys_ref[pl.ds(r0, tk), :]
        tie = jnp.where(r0 + krow <= j_star, 0.0, neg_inf)
        bias_ref[pl.ds(r0, tk), :] = jnp.where(kk > thr, 0.0, jnp.where(kk == thr, tie, neg_inf))
        return carry

    lax.fori_loop(0, n_tiles, bias_tile, 0)

    scale = DSA_DH ** -0.5
    rep = DSA_HEADS // DSA_KV
    logit_bound_sq = F32(0.0)
    for g in range(DSA_KV):
        q_sq = F32(0.0)
        for i in range(rep):
            h = g * rep + i
            qh = sq_ref[:, DSA_DH * h:DSA_DH * (h + 1)]
            q5_ref[g, tq * i:tq * (i + 1)] = qh
            qf = qh.astype(F32)
            q_sq = jnp.maximum(q_sq, jnp.max(jnp.sum(qf * qf, axis=-1, keepdims=True)))

        @pl.when(qi == 0)
        def _():
            kf = sk_ref[:, DSA_DH * g:DSA_DH * (g + 1)].astype(F32)
            ksq_ref[g] = jnp.max(jnp.sum(kf * kf, axis=-1, keepdims=True))

        logit_bound_sq = jnp.maximum(logit_bound_sq, q_sq * ksq_ref[g] * (scale * scale))
    m_ref[...] = jnp.full_like(m_ref, neg_inf)
    l_ref[...] = jnp.zeros_like(l_ref)
    acc_ref[...] = jnp.zeros_like(acc_ref)
    bounded = logit_bound_sq <= SOFTMAX_SAFE_LOGIT ** 2

    def attn_tiles(use_max):
        def attn_tile(j, carry):
            r0 = tile_start(j)
            bias = bias_ref[pl.ds(r0, tk), :]
            bias = jnp.concatenate([bias] * rep, axis=1)
            for g in range(DSA_KV):
                gs = slice(DSA_DH * g, DSA_DH * (g + 1))
                s = lax.dot_general(sk_ref[pl.ds(r0, tk), gs], q5_ref[g], NT, preferred_element_type=F32)
                s = s * scale + bias
                vt = svt_ref[gs, pl.ds(r0, tk)]
                if use_max:
                    m_prev = m_ref[g]
                    m_new = jnp.maximum(m_prev, jnp.max(s, axis=0, keepdims=True))
                    m_safe = jnp.where(m_new == neg_inf, 0.0, m_new)
                    alpha = jnp.exp(m_prev - m_safe)
                    p = jnp.exp(s - m_safe)
                    l_ref[g] = alpha * l_ref[g] + jnp.sum(p, axis=0, keepdims=True)
                    acc_ref[g] = alpha * acc_ref[g] + jnp.dot(vt, p.astype(BF16), preferred_element_type=F32)
                    m_ref[g] = m_new
                else:
                    p = jnp.exp(s)
                    l_ref[g] += jnp.sum(p, axis=0, keepdims=True)
                    acc_ref[g] += jnp.dot(vt, p.astype(BF16), preferred_element_type=F32)
            return carry

        lax.fori_loop(0, n_tiles, attn_tile, 0)

    @pl.when(bounded)
    def _():
        attn_tiles(False)

    @pl.when(jnp.logical_not(bounded))
    def _():
        attn_tiles(True)
    for g in range(DSA_KV):
        out_t = acc_ref[g] / l_ref[g]
        for i in range(rep):
            h = g * rep + i
            o_ref[:, DSA_DH * h:DSA_DH * (h + 1)] = out_t[:, tq * i:tq * (i + 1)].T.astype(BF16)


def _dsa(iq, sq, ikd, wt, sk, svt, mix, *, batch, seq, tq, tk):
    nq = seq // tq
    k_sel = min(IDX_TOPK, seq // 4)
    rep = DSA_HEADS // DSA_KV
    return pl.pallas_call(
        functools.partial(_dsa_kernel, tq=tq, tk=tk, seq=seq, k_sel=k_sel),
        out_shape=jax.ShapeDtypeStruct(mix.shape, mix.dtype),
        grid=(batch, nq),
        in_specs=[pl.BlockSpec((tq, W_IQ), lambda b, qi: (b * nq + qi, 0)),
                  pl.BlockSpec((tq, W_SQ), lambda b, qi: (b * nq + qi, 0)),
                  pl.BlockSpec((seq, 128), lambda b, qi: (b, 0)),
                  pl.BlockSpec((None, IDX_HEADS, tq), lambda b, qi: (b, 0, qi)),
                  pl.BlockSpec((seq, W_SKV), lambda b, qi: (b, 0)),
                  pl.BlockSpec((None, W_SKV, seq), lambda b, qi: (b, 0, 0)),
                  pl.BlockSpec(memory_space=pl.ANY)],
        out_specs=pl.BlockSpec((tq, W_SQ), lambda b, qi: (b * nq + qi, MIX_DSA // W_SQ)),
        scratch_shapes=[pltpu.VMEM((IDX_HEADS // 2, 2 * tq, 128), BF16), pltpu.VMEM((seq, tq), jnp.int32),
                        pltpu.VMEM((seq, tq), F32), pltpu.VMEM((DSA_KV, rep * tq, DSA_DH), BF16),
                        pltpu.VMEM((DSA_KV, 1, rep * tq), F32), pltpu.VMEM((DSA_KV, 1, rep * tq), F32),
                        pltpu.VMEM((DSA_KV, DSA_DH, rep * tq), F32), pltpu.SMEM((DSA_KV,), F32)],
        input_output_aliases={6: 0},
        compiler_params=_cparams(("parallel", "arbitrary")),
        name="dsa",
    )(iq, sq, ikd, wt, sk, svt, mix)


def _router_kernel(x_ref, w_ref, b_ref, e_ref, g_ref, xp_ref):
    x = x_ref[...]
    logits = jnp.dot(x, w_ref[...], precision=HIGHEST, preferred_element_type=F32) + b_ref[...]
    lane = lax.broadcasted_iota(jnp.int32, logits.shape, 1)
    neg_inf = F32(-jnp.inf)
    lg = jnp.where(lane < MOE_E, logits, neg_inf)
    m1 = jnp.max(lg, axis=-1, keepdims=True)
    i1 = jnp.min(jnp.where(lg == m1, lane, 128), axis=-1, keepdims=True)
    lg2 = jnp.where(lane == i1, neg_inf, lg)
    m2 = jnp.max(lg2, axis=-1, keepdims=True)
    i2 = jnp.min(jnp.where(lg2 == m2, lane, 128), axis=-1, keepdims=True)
    e21 = jnp.exp(m2 - m1)
    g1 = 1.0 / (1.0 + e21)
    e_ref[...] = jnp.where(lane == 0, i1, jnp.where(lane == 1, i2, 0))
    g_ref[...] = jnp.where(lane == 0, g1, jnp.where(lane == 1, e21 * g1, 0.0))
    half = x.shape[1] // 2
    lo = pltpu.bitcast(x[:, :half].astype(BF16).astype(F32), jnp.uint32)
    hi = pltpu.bitcast(x[:, half:].astype(BF16).astype(F32), jnp.uint32)
    xp_ref[...] = (hi & jnp.uint32(0xFFFF0000)) | (lo >> 16)


def _router(x, router_w, router_b, *, tm=256):
    n, d = x.shape
    row = lambda w: pl.BlockSpec((tm, w), lambda i: (i, 0))
    return pl.pallas_call(
        _router_kernel,
        out_shape=(jax.ShapeDtypeStruct((n, 128), jnp.int32), jax.ShapeDtypeStruct((n, 128), F32),
                   jax.ShapeDtypeStruct((n, d // 2), jnp.uint32)),
        grid=(n // tm,),
        in_specs=[row(d), pl.BlockSpec((d, 128), lambda i: (0, 0)), pl.BlockSpec((1, 128), lambda i: (0, 0))],
        out_specs=(row(128), row(128), row(d // 2)),
        compiler_params=_cparams(("parallel",)),
        name="router",
    )(x, router_w, router_b)


def _dispatch_kernel(pos_ref, xp_ref, buf_ref, o_ref, sem, *, tb):
    del buf_ref
    base = pl.program_id(0) * tb

    def row_copy(i, s):
        return pltpu.make_async_copy(xp_ref.at[pl.ds(i, 1)], o_ref.at[pl.ds(pos_ref[2 * (base + i) + s], 1)], sem)

    def start(i, c):
        row_copy(i, 0).start()
        row_copy(i, 1).start()
        return c

    def wait(i, c):
        row_copy(i, 0).wait()
        row_copy(i, 1).wait()
        return c

    lax.fori_loop(0, tb, start, 0)
    lax.fori_loop(0, tb, wait, 0)


def _dispatch(pos, xp, buf, *, tb=256):
    n, w = xp.shape
    return pl.pallas_call(
        functools.partial(_dispatch_kernel, tb=tb),
        out_shape=jax.ShapeDtypeStruct(buf.shape, buf.dtype),
        grid_spec=pltpu.PrefetchScalarGridSpec(
            num_scalar_prefetch=1,
            grid=(n // tb,),
            in_specs=[pl.BlockSpec((tb, w), lambda i, pos: (i, 0)), pl.BlockSpec(memory_space=pl.ANY)],
            out_specs=pl.BlockSpec(memory_space=pl.ANY),
            scratch_shapes=[pltpu.SemaphoreType.DMA],
        ),
        input_output_aliases={2: 0},
        compiler_params=_cparams(("arbitrary",)),
        name="moe_dispatch",
    )(pos, xp, buf)


def _unpack_rows(word):
    lo = pltpu.bitcast(word << 16, F32).astype(BF16)
    hi = pltpu.bitcast(word & jnp.uint32(0xFFFF0000), F32).astype(BF16)
    return lo, hi


def _expert_tile_refresh(be_ref, bf_ref, nxt_ref, w_hbms, stage_ref, wb_refs, sem, *, lead, tn):
    n, i = pl.program_id(0), pl.program_id(1)

    def fetch(e, col_tile):
        cols = pl.ds(pl.multiple_of(col_tile * tn, tn), tn)
        return [pltpu.make_async_copy(w.at[lead, e, :, cols], stage_ref.at[t], sem)
                for t, w in enumerate(w_hbms)]

    @pl.when(bf_ref[i] == 1)
    def _():
        @pl.when(jnp.logical_and(n == 0, i == 0))
        def _():
            for c in fetch(be_ref[0], 0):
                c.start()

        for c in fetch(be_ref[i], n):
            c.wait()
        for t, wb in enumerate(wb_refs):
            wb[...] = stage_ref[t].astype(BF16)
        last_run = nxt_ref[i] < 0
        e_next = jnp.where(last_run, be_ref[0], nxt_ref[i])
        n_next = jnp.where(last_run, n + 1, n)

        @pl.when(n_next < pl.num_programs(0))
        def _():
            for c in fetch(e_next, n_next):
                c.start()


def _moe_up_kernel(be_ref, bf_ref, na_ref, nxt_ref, a_ref, w1_hbm, w3_hbm, o_ref,
                   stage_ref, w1b_ref, w3b_ref, sem, *, lead, tn):
    i = pl.program_id(1)

    @pl.when(i < na_ref[0])
    def _():
        _expert_tile_refresh(be_ref, bf_ref, nxt_ref, (w1_hbm, w3_hbm), stage_ref, (w1b_ref, w3b_ref), sem,
                             lead=lead, tn=tn)
        lo, hi = _unpack_rows(a_ref[...])
        half = lo.shape[1]
        g = (jnp.dot(lo, w1b_ref[:half], preferred_element_type=F32)
             + jnp.dot(hi, w1b_ref[half:], preferred_element_type=F32))
        u = (jnp.dot(lo, w3b_ref[:half], preferred_element_type=F32)
             + jnp.dot(hi, w3b_ref[half:], preferred_element_type=F32))
        o_ref[...] = (g * jax.nn.sigmoid(g) * u).astype(BF16)

    @pl.when(i >= na_ref[0])
    def _():
        o_ref[...] = jnp.zeros_like(o_ref)


def _moe_down_kernel(be_ref, bf_ref, na_ref, nxt_ref, h_ref, w2_hbm, o_ref, stage_ref, w2b_ref, sem, *, lead, tn):
    i = pl.program_id(1)

    @pl.when(i < na_ref[0])
    def _():
        _expert_tile_refresh(be_ref, bf_ref, nxt_ref, (w2_hbm,), stage_ref, (w2b_ref,), sem, lead=lead, tn=tn)
        o_ref[...] = jnp.dot(h_ref[...], w2b_ref[...], preferred_element_type=F32)

    @pl.when(i >= na_ref[0])
    def _():
        o_ref[...] = jnp.zeros_like(o_ref)


def _moe_ffn(block_e, block_first, n_active, next_e, a_sorted, w1, w3, w2, lead, *, tm, tn):
    r, half = a_sorted.shape
    d = 2 * half
    f = w1.shape[-1]
    nb = r // tm
    hbm = pl.BlockSpec(memory_space=pl.ANY)

    def rows(i, na):
        return jnp.minimum(i, na[0] - 1)

    h = pl.pallas_call(
        functools.partial(_moe_up_kernel, lead=lead, tn=tn),
        out_shape=jax.ShapeDtypeStruct((r, f), BF16),
        grid_spec=pltpu.PrefetchScalarGridSpec(
            num_scalar_prefetch=4,
            grid=(f // tn, nb),
            in_specs=[pl.BlockSpec((tm, half), lambda n, i, be, bf, na, nx: (rows(i, na), 0)), hbm, hbm],
            out_specs=pl.BlockSpec((tm, tn), lambda n, i, be, bf, na, nx: (i, n)),
            scratch_shapes=[pltpu.VMEM((2, d, tn), F32), pltpu.VMEM((d, tn), BF16), pltpu.VMEM((d, tn), BF16),
                            pltpu.SemaphoreType.DMA],
        ),
        compiler_params=_cparams(("arbitrary", "arbitrary")),
        name="moe_up",
    )(block_e, block_first, n_active, next_e, a_sorted, w1, w3)
    return pl.pallas_call(
        functools.partial(_moe_down_kernel, lead=lead, tn=tn),
        out_shape=jax.ShapeDtypeStruct((r, d), F32),
        grid_spec=pltpu.PrefetchScalarGridSpec(
            num_scalar_prefetch=4,
            grid=(d // tn, nb),
            in_specs=[pl.BlockSpec((tm, f), lambda n, i, be, bf, na, nx: (rows(i, na), 0)), hbm],
            out_specs=pl.BlockSpec((tm, tn), lambda n, i, be, bf, na, nx: (i, n)),
            scratch_shapes=[pltpu.VMEM((1, f, tn), F32), pltpu.VMEM((f, tn), BF16), pltpu.SemaphoreType.DMA],
        ),
        compiler_params=_cparams(("arbitrary", "arbitrary")),
        name="moe_down",
    )(block_e, block_first, n_active, next_e, h, w2)


def _combine_ln_kernel(pos_ref, x_ref, gate_ref, g_ref, b_ref, y_ref, o_ref, ybuf_ref, sem, *, tb, alpha):
    step, n_steps = pl.program_id(0), pl.num_programs(0)
    slot = step % 2

    def row_copy(blk, buf, i, s):
        return pltpu.make_async_copy(y_ref.at[pl.ds(pos_ref[2 * (blk * tb + i) + s], 1)],
                                     ybuf_ref.at[buf, s, pl.ds(i, 1)], sem.at[buf])

    def start_block(blk, buf):
        def body(i, c):
            row_copy(blk, buf, i, 0).start()
            row_copy(blk, buf, i, 1).start()
            return c
        lax.fori_loop(0, tb, body, 0)

    def wait_block(blk, buf):
        def body(i, c):
            row_copy(blk, buf, i, 0).wait()
            row_copy(blk, buf, i, 1).wait()
            return c
        lax.fori_loop(0, tb, body, 0)

    @pl.when(step == 0)
    def _():
        start_block(0, 0)

    @pl.when(step + 1 < n_steps)
    def _():
        start_block(step + 1, 1 - slot)

    wait_block(step, slot)
    gate = gate_ref[...]
    f = ybuf_ref[slot, 0] * gate[:, 0:1] + ybuf_ref[slot, 1] * gate[:, 1:2]
    z = alpha * x_ref[...] + f
    mu = jnp.mean(z, axis=-1, keepdims=True)
    zc = z - mu
    var = jnp.mean(zc * zc, axis=-1, keepdims=True)
    o_ref[...] = zc * lax.rsqrt(var + LN_EPS) * g_ref[...] + b_ref[...]


def _combine_ln(pos, x, gates, g, b, y, lead, *, alpha, tb=128):
    n, d = x.shape
    par = pl.BlockSpec((None, 1, d), lambda i, pos: (lead, 0, 0))
    return pl.pallas_call(
        functools.partial(_combine_ln_kernel, tb=tb, alpha=alpha),
        out_shape=jax.ShapeDtypeStruct((n, d), F32),
        grid_spec=pltpu.PrefetchScalarGridSpec(
            num_scalar_prefetch=1,
            grid=(n // tb,),
            in_specs=[pl.BlockSpec((tb, d), lambda i, pos: (i, 0)),
                      pl.BlockSpec((tb, 128), lambda i, pos: (i, 0)),
                      par, par, pl.BlockSpec(memory_space=pl.ANY)],
            out_specs=pl.BlockSpec((tb, d), lambda i, pos: (i, 0)),
            scratch_shapes=[pltpu.VMEM((2, 2, tb, d), F32), pltpu.SemaphoreType.DMA((2,))],
        ),
        compiler_params=_cparams(("arbitrary",)),
        name="moe_combine_ln",
    )(pos, x, gates, g.reshape(-1, 1, d), b.reshape(-1, 1, d), y)


def _moe_layer(x, router_w, router_b, w1, w3, w2, ln_g, ln_b, lead_moe, lead_ln, *, alpha, tm=512, tn=512):
    n, d = x.shape
    rw = jnp.pad(router_w[lead_moe], ((0, 0), (0, 128 - MOE_E)))
    rb = jnp.pad(router_b[lead_moe], (0, 128 - MOE_E)).reshape(1, 128)
    top_e, gates, xp = _router(x, rw, rb)
    flat_e = top_e[:, :MOE_TOPK].reshape(-1)
    onehot = (flat_e[:, None] == jnp.arange(MOE_E, dtype=jnp.int32)[None, :]).astype(jnp.int32)
    rank = jnp.sum((jnp.cumsum(onehot, axis=0) - onehot) * onehot, axis=1)
    counts = jnp.sum(onehot, axis=0)
    padded = (counts + tm - 1) // tm * tm
    end_padded = jnp.cumsum(padded)
    start_padded = end_padded - padded
    pos = (start_padded[flat_e] + rank).astype(jnp.int32)
    nb = -(-(n * MOE_TOPK + MOE_E * (tm - 1)) // tm)
    block_start = jnp.arange(nb, dtype=jnp.int32) * tm
    block_e = jnp.sum((block_start[:, None] >= end_padded[None, :]).astype(jnp.int32), axis=1)
    block_e = jnp.minimum(block_e, MOE_E - 1)
    n_active = (end_padded[-1] // tm).astype(jnp.int32).reshape(1)
    block_first = jnp.concatenate([jnp.ones((1,), jnp.int32), (block_e[1:] != block_e[:-1]).astype(jnp.int32)])
    blk = jnp.arange(nb, dtype=jnp.int32)
    run_start = (block_first == 1) & (blk < n_active[0])
    later_start = jnp.where((blk[None, :] > blk[:, None]) & run_start[None, :], blk[None, :], nb)
    next_start = jnp.min(later_start, axis=1)
    next_e = jnp.where(next_start < nb, block_e[jnp.minimum(next_start, nb - 1)], -1).astype(jnp.int32)
    a_sorted = _dispatch(pos, xp, jnp.zeros((nb * tm, d // 2), jnp.uint32))
    y = _moe_ffn(block_e, block_first, n_active, next_e, a_sorted, w1, w3, w2, lead_moe, tm=tm, tn=tn)
    return _combine_ln(pos, x, gates, ln_g, ln_b, y, lead_ln, alpha=alpha)


def kernel(x, positions, w_in, w_gate_up, b_gate, gla_norm_g, lambda_q1, lambda_k1, lambda_q2, lambda_k2,
           diff_norm_g, w_out, ln1_g, ln1_b, ln2_g, ln2_b, ffn_w1, ffn_w3, ffn_w2, router_w, router_b,
           moe_w1, moe_w3, moe_w2):
    batch, seq, d = x.shape
    n = batch * seq
    depth = w_in.shape[0]
    alpha = (2 * depth) ** 0.25
    xf = x.reshape(n, d)
    xb = xf.astype(BF16)
    pos = positions.reshape(n)
    tab_a, half_a = _rope_tables(pos, DIFF_D)
    tab_b, half_b = _rope_tables(pos, DSA_DH)
    plain = lambda i, j, k: (i, k)
    w_in_t = jnp.swapaxes(w_in, 1, 2)
    for l in range(depth):
        ha = _matmul(xb, w_in_t, n_out=W_GLA, k_steps=1, tm=1024, tn=HM_TN, tk=d, out_dtype=BF16,
                     a_map=plain, w_map=lambda i, j, k, l=l: (l, j, k), w_lead=True, w_t=True,
                     name="in_proj_gla")
        hm = _matmul(xb, w_in_t, n_out=HM_TILES * HM_TN, k_steps=1, tm=1024, tn=HM_TN, tk=d,
                     out_dtype=BF16, a_map=plain,
                     w_map=lambda i, j, k, l=l: (l, HM_BASE // HM_TN + j, k), w_lead=True, w_t=True,
                     n_valid=N_IN, col_start=lambda j: HM_BASE + j * HM_TN, name="in_proj_attn")
        small_tiles = (OFF_GLR // 128, OFF_IK // 128)
        pick = lambda j: jnp.where(j == 0, small_tiles[0], small_tiles[1])
        hs = _matmul(xb, w_in_t, n_out=256, k_steps=1, tm=1024, tn=128, tk=d, out_dtype=F32,
                     a_map=plain, w_map=lambda i, j, k, l=l: (l, pick(j), k), w_lead=True, w_t=True,
                     n_valid=N_IN, col_start=lambda j: pick(j) * 128, name="in_proj_small")
        glr = hs[:, OFF_GLR % 128:OFF_GLR % 128 + GLA_RANK]
        iw0 = 128 + OFF_IW % 128
        wt = hs[:, iw0:iw0 + IDX_HEADS].reshape(batch, seq, IDX_HEADS).transpose(0, 2, 1)
        dq, dk, dv, sq, sk, sv, iq, ikd = _split_rope(hm, hs, tab_a, half_a, tab_b, half_b)
        svt = sv.reshape(batch, seq, W_SKV).transpose(0, 2, 1)
        dvt = dv.reshape(batch, seq, W_DIFF).transpose(0, 2, 1)

        mix = jnp.zeros((n, W_MIX), BF16)
        mix = _gla(ha, glr, w_gate_up, b_gate, gla_norm_g, mix, l, batch=batch, seq=seq)
        lambda_init = 0.8 - 0.6 * math.exp(-0.3 * l)
        mix = _diff_attention(dq, dk, dvt, lambda_q1, lambda_k1, lambda_q2, lambda_k2, diff_norm_g, mix, l,
                              batch=batch, seq=seq, lambda_init=lambda_init, tq=1024, heads=1)
        mix = _dsa(iq, sq, ikd, wt, sk, svt, mix, batch=batch, seq=seq, tq=256, tk=512)

        proj = _out_proj(mix, w_out, l)
        xf, xb = _add_ln(xf, proj, ln1_g, ln1_b, l, alpha=alpha)

        j = l // 2
        if l % 2 == 0:
            f_dim = ffn_w2.shape[1]
            f_pad = -(-f_dim // 1024) * 1024
            hid = _swiglu_up(xb, ffn_w1, ffn_w3, j, tm=1024, tn=256, tk=d, f_out=f_pad)
            down = _matmul(hid, ffn_w2, n_out=d, k_steps=8, tm=2048, tn=1024, tk=f_pad // 8, out_dtype=F32,
                           a_map=plain, w_map=lambda i, j_, k, j=j: (j, k, j_), w_lead=True, k_valid=f_dim,
                           name="ffn_down")
            xf, xb = _add_ln(xf, down, ln2_g, ln2_b, l, alpha=alpha)
        else:
            xf = _moe_layer(xf, router_w, router_b, moe_w1, moe_w3, moe_w2, ln2_g, ln2_b, j, l, alpha=alpha)
            xb = xf.astype(BF16)
    return xf.reshape(batch, seq, d)
```

```python
import functools
import math

import jax
import jax.numpy as jnp
from jax import lax
from jax.experimental import pallas as pl
from jax.experimental.pallas import tpu as pltpu

F32 = jnp.float32
BF16 = jnp.bfloat16
HIGHEST = lax.Precision.HIGHEST
INT_MIN = -(2 ** 31)

ROPE_THETA = 500000.0
GLA_HEADS, GLA_DK, GLA_DV, GLA_RANK, GLA_TAU, GLA_CHUNK = 4, 192, 384, 16, 16.0, 64
GLA_SUB = 16
GLA_SAFE_DECAY = 40.0
GLA_SAFE_K = 1e18
DIFF_HEADS, DIFF_D = 10, 64
DSA_HEADS, DSA_KV, DSA_DH = 10, 2, 128
IDX_HEADS, IDX_DIM, IDX_TOPK = 32, 64, 256
MOE_E, MOE_TOPK = 8, 2
LN_EPS = 1e-5
SOFTMAX_SAFE_LOGIT = 60.0

W_GLA = 2 * GLA_HEADS * GLA_DK + 2 * GLA_HEADS * GLA_DV
OFF_GLR = W_GLA
OFF_DQ = OFF_GLR + GLA_RANK
W_DIFF = DIFF_HEADS * 2 * DIFF_D
OFF_DK, OFF_DV = OFF_DQ + W_DIFF, OFF_DQ + 2 * W_DIFF
OFF_SQ = OFF_DQ + 3 * W_DIFF
W_SQ, W_SKV = DSA_HEADS * DSA_DH, DSA_KV * DSA_DH
OFF_SK, OFF_SV = OFF_SQ + W_SQ, OFF_SQ + W_SQ + W_SKV
OFF_IQ = OFF_SV + W_SKV
W_IQ = IDX_HEADS * IDX_DIM
OFF_IK = OFF_IQ + W_IQ
OFF_IW = OFF_IK + IDX_DIM
N_IN = OFF_IW + IDX_HEADS

HM_BASE = W_GLA
HM_TN = 512
HM_TILES = -(-(N_IN - HM_BASE) // HM_TN)

MIX_GLA, MIX_DIFF, MIX_DSA = 0, 2560, 3840
W_MIX = MIX_DSA + DSA_HEADS * DSA_DH

VMEM_LIMIT = 56 * 1024 * 1024


def _cparams(sem, vmem=VMEM_LIMIT):
    return pltpu.CompilerParams(dimension_semantics=sem, vmem_limit_bytes=vmem)


def _mm_kernel(a_ref, w_ref, o_ref, acc_ref=None, *, k_valid, n_valid, col_start, w_t):
    k = pl.program_id(2)
    k_ax, n_ax = (1, 0) if w_t else (0, 1)

    def partial_product():
        w = w_ref[...]
        if k_valid is not None:
            kk = k * w.shape[k_ax] + lax.broadcasted_iota(jnp.int32, w.shape, k_ax)
            w = jnp.where(kk < k_valid, w, 0.0)
        if n_valid is not None:
            col = col_start(pl.program_id(1)) + lax.broadcasted_iota(jnp.int32, w.shape, n_ax)
            w = jnp.where(col < n_valid, w, 0.0)
        return lax.dot_general(a_ref[...], w.astype(BF16), (((1,), (k_ax,)), ((), ())),
                               preferred_element_type=F32)

    if acc_ref is None:
        o_ref[...] = partial_product().astype(o_ref.dtype)
        return

    @pl.when(k == 0)
    def _():
        acc_ref[...] = partial_product()

    @pl.when(k > 0)
    def _():
        acc_ref[...] += partial_product()

    @pl.when(k == pl.num_programs(2) - 1)
    def _():
        o_ref[...] = acc_ref[...].astype(o_ref.dtype)


def _matmul(a, w, *, n_out, k_steps, tm, tn, tk, out_dtype, a_map, w_map, w_lead=False, k_valid=None,
            n_valid=None, col_start=None, w_t=False, name="mm"):
    m = a.shape[0]
    w_block = (tn, tk) if w_t else (tk, tn)
    if w_lead:
        w_block = (None,) + w_block
    return pl.pallas_call(
        functools.partial(_mm_kernel, k_valid=k_valid, n_valid=n_valid, col_start=col_start, w_t=w_t),
        out_shape=jax.ShapeDtypeStruct((m, n_out), out_dtype),
        grid=(m // tm, pl.cdiv(n_out, tn), k_steps),
        in_specs=[pl.BlockSpec((tm, tk), a_map), pl.BlockSpec(w_block, w_map)],
        out_specs=pl.BlockSpec((tm, tn), lambda i, j, k: (i, j)),
        scratch_shapes=[pltpu.VMEM((tm, tn), F32)] if k_steps > 1 else [],
        compiler_params=_cparams(("parallel", "parallel", "arbitrary")),
        name=name,
    )(a, w)


def _swiglu_kernel(a_ref, w1_ref, w3_ref, o_ref, acc1_ref=None, acc3_ref=None, *, n_valid):
    k = pl.program_id(2)

    def partial_product(w_ref):
        return jnp.dot(a_ref[...], w_ref[...].astype(BF16), preferred_element_type=F32)

    def finish(g, u):
        h = g * jax.nn.sigmoid(g) * u
        col = pl.program_id(1) * h.shape[1] + lax.broadcasted_iota(jnp.int32, h.shape, 1)
        o_ref[...] = jnp.where(col < n_valid, h, 0.0).astype(o_ref.dtype)

    if acc1_ref is None:
        finish(partial_product(w1_ref), partial_product(w3_ref))
        return

    @pl.when(k == 0)
    def _():
        acc1_ref[...] = partial_product(w1_ref)
        acc3_ref[...] = partial_product(w3_ref)

    @pl.when(k > 0)
    def _():
        acc1_ref[...] += partial_product(w1_ref)
        acc3_ref[...] += partial_product(w3_ref)

    @pl.when(k == pl.num_programs(2) - 1)
    def _():
        finish(acc1_ref[...], acc3_ref[...])


def _swiglu_up(a, w1, w3, lead, *, tm, tn, tk, f_out):
    m, kdim = a.shape
    f = w1.shape[-1]
    last = (f - 1) // tn
    wspec = pl.BlockSpec((None, tk, tn), lambda i, j, k: (lead, k, jnp.minimum(j, last)))
    k_steps = kdim // tk
    return pl.pallas_call(
        functools.partial(_swiglu_kernel, n_valid=f),
        out_shape=jax.ShapeDtypeStruct((m, f_out), BF16),
        grid=(m // tm, f_out // tn, k_steps),
        in_specs=[pl.BlockSpec((tm, tk), lambda i, j, k: (i, k)), wspec, wspec],
        out_specs=pl.BlockSpec((tm, tn), lambda i, j, k: (i, j)),
        scratch_shapes=[pltpu.VMEM((tm, tn), F32), pltpu.VMEM((tm, tn), F32)] if k_steps > 1 else [],
        compiler_params=_cparams(("parallel", "parallel", "arbitrary")),
        name="swiglu_up",
    )(a, w1, w3)


def _out_proj_kernel(mix_ref, w_ref, o_ref):
    n_gla = GLA_HEADS * GLA_DV
    w = w_ref[...].astype(BF16)
    o_ref[...] = (jnp.dot(mix_ref[:, MIX_GLA:MIX_GLA + n_gla], w[:n_gla], preferred_element_type=F32)
                  + jnp.dot(mix_ref[:, MIX_DIFF:], w[n_gla:], preferred_element_type=F32))


def _out_proj(mix, w_out, lead, *, tm=1024, tn=256):
    m, wm = mix.shape
    d = w_out.shape[-1]
    return pl.pallas_call(
        _out_proj_kernel,
        out_shape=jax.ShapeDtypeStruct((m, d), F32),
        grid=(m // tm, d // tn),
        in_specs=[pl.BlockSpec((tm, wm), lambda i, j: (i, 0)),
                  pl.BlockSpec((None, w_out.shape[1], tn), lambda i, j: (lead, 0, j))],
        out_specs=pl.BlockSpec((tm, tn), lambda i, j: (i, j)),
        compiler_params=_cparams(("parallel", "parallel")),
        name="out_proj",
    )(mix, w_out)


def _add_ln_kernel(x_ref, y_ref, g_ref, b_ref, o_ref, ob_ref, *, alpha):
    z = alpha * x_ref[...] + y_ref[...].astype(F32)
    mu = jnp.mean(z, axis=-1, keepdims=True)
    zc = z - mu
    var = jnp.mean(zc * zc, axis=-1, keepdims=True)
    out = zc * lax.rsqrt(var + LN_EPS) * g_ref[...] + b_ref[...]
    o_ref[...] = out
    ob_ref[...] = out.astype(BF16)


def _add_ln(x, y, g, b, lead, *, alpha, tm=256):
    m, d = x.shape
    row = pl.BlockSpec((tm, d), lambda i: (i, 0))
    par = pl.BlockSpec((None, 1, d), lambda i: (lead, 0, 0))
    g, b = g.reshape(-1, 1, d), b.reshape(-1, 1, d)
    return pl.pallas_call(
        functools.partial(_add_ln_kernel, alpha=alpha),
        out_shape=(jax.ShapeDtypeStruct((m, d), F32), jax.ShapeDtypeStruct((m, d), BF16)),
        grid=(m // tm,),
        in_specs=[row, row, par, par],
        out_specs=(row, row),
        compiler_params=_cparams(("parallel",)),
        name="add_ln",
    )(x, y, g, b)


def _rope_tables(positions, width):
    rot = width // 4
    half = rot // 2
    inv_freq = 1.0 / (ROPE_THETA ** (jnp.arange(half, dtype=F32) * 2.0 / rot))
    ang = positions.astype(F32)[:, None] * inv_freq
    cos, sin = jnp.cos(ang), jnp.sin(ang)
    n = positions.shape[0]
    one = jnp.ones((n, width - rot), F32)
    zero_h = jnp.zeros((n, half), F32)
    zero_r = jnp.zeros((n, width - rot), F32)
    c = jnp.concatenate([cos, cos, one], axis=1)
    s_up = jnp.concatenate([-sin, zero_h, zero_r], axis=1)
    s_dn = jnp.concatenate([zero_h, sin, zero_r], axis=1)
    rep = 128 // width
    return tuple(jnp.tile(t, (1, rep)) for t in (c, s_up, s_dn)), half


def _rope_tile(x, c, s_up, s_dn, half):
    return x * c + pltpu.roll(x, 128 - half, 1) * s_up + pltpu.roll(x, half, 1) * s_dn


def _split_rope_kernel(hm_ref, hc_ref, ca_ref, ua_ref, da_ref, cb_ref, ub_ref, db_ref,
                       dq_o, dk_o, dv_o, sq_o, sk_o, sv_o, iq_o, ik_o, *, shift, ik_shift, half_a, half_b):
    ta = (ca_ref[...], ua_ref[...], da_ref[...])
    tb = (cb_ref[...], ub_ref[...], db_ref[...])
    lane = lax.broadcasted_iota(jnp.int32, ca_ref.shape, 1)

    def shifted(k):
        return pltpu.roll(hm_ref[:, 128 * k:128 * (k + 1)].astype(F32), 128 - shift, 1)

    m = 0
    nxt = shifted(0)
    for dst, tab, half in ((dq_o, ta, half_a), (dk_o, ta, half_a), (dv_o, None, 0), (sq_o, tb, half_b),
                           (sk_o, tb, half_b), (sv_o, None, 0), (iq_o, ta, half_a)):
        for t in range(dst.shape[1] // 128):
            cur, nxt = nxt, shifted(m + 1)
            y = jnp.where(lane < 128 - shift, cur, nxt)
            if tab is not None:
                y = _rope_tile(y, *tab, half)
            dst[:, 128 * t:128 * (t + 1)] = y.astype(BF16)
            m += 1
    ik = _rope_tile(pltpu.roll(hc_ref[...], 128 - ik_shift, 1), *ta, half_a)
    ik_o[...] = jnp.where(lane < IDX_DIM, ik, pltpu.roll(ik, IDX_DIM, 1)).astype(BF16)


def _split_rope(hm, hs, tab_a, half_a, tab_b, half_b, *, tm=256):
    n, wm = hm.shape
    tile = pl.BlockSpec((tm, 128), lambda i: (i, 0))
    widths = (W_DIFF, W_DIFF, W_DIFF, W_SQ, W_SKV, W_SKV, W_IQ, 128)
    return pl.pallas_call(
        functools.partial(_split_rope_kernel, shift=OFF_DQ - HM_BASE, ik_shift=OFF_IK % 128,
                          half_a=half_a, half_b=half_b),
        out_shape=tuple(jax.ShapeDtypeStruct((n, w), BF16) for w in widths),
        grid=(n // tm,),
        in_specs=[pl.BlockSpec((tm, wm), lambda i: (i, 0)), pl.BlockSpec((tm, 128), lambda i: (i, 1))] + [tile] * 6,
        out_specs=tuple(pl.BlockSpec((tm, w), lambda i: (i, 0)) for w in widths),
        compiler_params=_cparams(("parallel",)),
        name="split_rope",
    )(hm, hs, *tab_a, *tab_b)


def _gla_prep_kernel(q_ref, k_ref, glr_ref, wg_ref, bg_ref, qe_o, kd_o, s_o, eb_o, la_ref, *, chunks):
    C, SUB, DK = GLA_CHUNK, GLA_SUB, GLA_DK
    NT = (((1,), (1,)), ((), ()))

    row = lax.broadcasted_iota(jnp.int32, (C, C), 0)
    col = lax.broadcasted_iota(jnp.int32, (C, C), 1)
    tril = (col <= row).astype(F32)
    gi = lax.broadcasted_iota(jnp.int32, (SUB, SUB * SUB), 0)
    gr = lax.broadcasted_iota(jnp.int32, (SUB, SUB * SUB), 1)
    gsum = (gr // SUB == gi).astype(BF16)
    pr = lax.broadcasted_iota(jnp.int32, (SUB * SUB, C), 0)
    pc = lax.broadcasted_iota(jnp.int32, (SUB * SUB, C), 1)
    srow = lax.broadcasted_iota(jnp.int32, (SUB, C), 0)
    scol = lax.broadcasted_iota(jnp.int32, (SUB, C), 1)

    z = jnp.dot(glr_ref[...], wg_ref[...], precision=HIGHEST, preferred_element_type=F32) + bg_ref[...]
    la_all = -(jnp.maximum(-z, 0.0) + jnp.log1p(jnp.exp(-jnp.abs(z)))) * (1.0 / GLA_TAU)
    la_ref[...] = la_all
    sub_decay = -jnp.sum(la_all.reshape(chunks * C // SUB, SUB, la_all.shape[1]), axis=1)
    k_mag = jnp.max(jnp.abs(k_ref[...].astype(F32)))
    safe = jnp.logical_and(jnp.max(sub_decay) <= GLA_SAFE_DECAY, k_mag <= GLA_SAFE_K)

    def chunk_head(ci, hh, fast, b_pair):
        r0 = pl.multiple_of(ci * C, C)
        rows = pl.ds(r0, C)
        ks = slice(hh * DK, (hh + 1) * DK)
        q = q_ref[rows, ks].astype(F32) * (DK ** -0.5)
        k = k_ref[rows, ks].astype(F32)
        b = b_pair[:, ks]
        qe_o[rows, ks] = (q * jnp.exp(b)).astype(BF16)

        s_rows = []
        for blk in range(C // SUB):
            r0 = blk * SUB
            b_i, q_i, k_i = b[r0:r0 + SUB], q[r0:r0 + SUB], k[r0:r0 + SUB]
            beta = b[r0:r0 + 1]
            q_t = (q_i * jnp.exp(b_i - beta)).astype(BF16)
            if fast:
                k_t = (k * jnp.exp(jnp.minimum(beta - b, GLA_SAFE_DECAY))).astype(BF16)
                s_blk = lax.dot_general(q_t, k_t, NT, preferred_element_type=F32)
                s_rows.append(jnp.where(scol <= srow + r0, s_blk, 0.0))
                continue
            pair = (q_i[:, None, :] * k_i[None, :, :]
                    * jnp.exp(jnp.minimum(b_i[:, None, :] - b_i[None, :, :], 0.0)))
            pair_sum = jnp.sum(pair.reshape(SUB * SUB, DK), axis=-1, keepdims=True)
            placed = jnp.where(pc == r0 + pr % SUB, pair_sum, 0.0).astype(BF16)
            s_blk = jnp.dot(gsum, placed, preferred_element_type=F32)
            s_blk = jnp.where(scol <= srow + r0, s_blk, 0.0)
            if blk > 0:
                k_t = (k * jnp.exp(jnp.minimum(beta - b, 0.0))).astype(BF16)
                off = lax.dot_general(q_t, k_t, NT, preferred_element_type=F32)
                s_blk = jnp.where(scol < r0, off, s_blk)
            s_rows.append(s_blk)
        s_o[rows, hh * C:(hh + 1) * C] = jnp.concatenate(s_rows, axis=0).astype(BF16)
        b_last = b[C - 1:C]
        kd_o[rows, ks] = (k * jnp.exp(b_last - b)).astype(BF16)
        eb_o[pl.ds(ci, 1), ks] = jnp.exp(b_last)

    def all_chunks(fast):
        per = 4 if fast else 2

        def some_chunks(i, carry):
            for c in range(per):
                ci = per * i + c
                la = la_ref[pl.ds(pl.multiple_of(ci * C, C), C), :]
                b_pair = jnp.dot(tril, la, precision=HIGHEST, preferred_element_type=F32)
                for hh in range(2):
                    chunk_head(ci, hh, fast, b_pair)
            return carry
        lax.fori_loop(0, chunks // per, some_chunks, 0)

    @pl.when(safe)
    def _():
        all_chunks(True)

    @pl.when(jnp.logical_not(safe))
    def _():
        all_chunks(False)


def _gla_scan_kernel(qe_ref, kd_ref, s_ref, eb_ref, v_ref, r_ref, ng_ref, mix_ref, o_ref, state_ref):
    del mix_ref
    C, DK, DV = GLA_CHUNK, GLA_DK, GLA_DV
    NT = (((1,), (1,)), ((), ()))
    TN = (((0,), (0,)), ((), ()))

    @pl.when(pl.program_id(0) == 0)
    def _():
        state_ref[...] = jnp.zeros_like(state_ref)

    for b in range(state_ref.shape[0]):
        for h in range(GLA_HEADS):
            ks = slice(h * DK, (h + 1) * DK)
            vs = slice(h * DV, (h + 1) * DV)
            st = state_ref[b, h]
            v = v_ref[b, :, vs]
            o = (lax.dot_general(qe_ref[b, :, ks], st.astype(BF16), NT, preferred_element_type=F32)
                 + jnp.dot(s_ref[b, :, h * C:(h + 1) * C], v, preferred_element_type=F32))
            state_ref[b, h] = st * eb_ref[b:b + 1, ks] + lax.dot_general(v, kd_ref[b, :, ks], TN,
                                                                         preferred_element_type=F32)
            ms = jnp.mean(o * o, axis=-1, keepdims=True)
            o_n = o * lax.rsqrt(ms + LN_EPS) * ng_ref[...]
            r = r_ref[b, :, vs].astype(F32)
            o_ref[b, :, vs] = (o_n * (r * jax.nn.sigmoid(r))).astype(BF16)


def _gla(ha, glr, w_gate_up, b_gate, gla_norm_g, mix, lead, *, batch, seq, chunks=8):
    C = GLA_CHUNK
    n = batch * seq
    nc = seq // C
    hp = GLA_HEADS // 2
    wq = 2 * GLA_DK
    wqk, wv = GLA_HEADS * GLA_DK, GLA_HEADS * GLA_DV
    tr = chunks * C

    def rows(width, first):
        return pl.BlockSpec((tr, width), lambda i, p: (i, first + p))

    qe, kd, sc, eb = pl.pallas_call(
        functools.partial(_gla_prep_kernel, chunks=chunks),
        out_shape=(jax.ShapeDtypeStruct((n, wqk), BF16), jax.ShapeDtypeStruct((n, wqk), BF16),
                   jax.ShapeDtypeStruct((n, GLA_HEADS * C), BF16), jax.ShapeDtypeStruct((n // C, wqk), F32)),
        grid=(n // tr, hp),
        in_specs=[rows(wq, 0), rows(wq, hp),
                  pl.BlockSpec((tr, GLA_RANK), lambda i, p: (i, 0)),
                  pl.BlockSpec((None, GLA_RANK, wq), lambda i, p: (lead, 0, p)),
                  pl.BlockSpec((None, 1, wq), lambda i, p: (lead, 0, p))],
        out_specs=(rows(wq, 0), rows(wq, 0), rows(2 * C, 0),
                   pl.BlockSpec((chunks, wq), lambda i, p: (i, p))),
        scratch_shapes=[pltpu.VMEM((tr, wq), F32)],
        compiler_params=_cparams(("parallel", "parallel")),
        name="gla_prep",
    )(ha, ha, glr, w_gate_up, b_gate.reshape(-1, 1, wqk))

    def per_chunk(width, first):
        return pl.BlockSpec((batch, C, width), lambda c: (0, c, first))

    eb = eb.reshape(batch, nc, wqk).transpose(1, 0, 2)
    ha3 = ha.reshape(batch, seq, -1)
    out = pl.pallas_call(
        _gla_scan_kernel,
        out_shape=jax.ShapeDtypeStruct((batch, seq, mix.shape[1]), mix.dtype),
        grid=(nc,),
        in_specs=[per_chunk(wqk, 0), per_chunk(wqk, 0), per_chunk(GLA_HEADS * C, 0),
                  pl.BlockSpec((None, batch, wqk), lambda c: (c, 0, 0)),
                  per_chunk(wv, 1), per_chunk(wv, 2),
                  pl.BlockSpec((None, 1, GLA_DV), lambda c: (lead, 0, 0)),
                  pl.BlockSpec(memory_space=pl.ANY)],
        out_specs=per_chunk(wv, MIX_GLA // wv),
        scratch_shapes=[pltpu.VMEM((batch, GLA_HEADS, GLA_DV, GLA_DK), F32)],
        input_output_aliases={7: 0},
        compiler_params=_cparams(("arbitrary",)),
        name="gla_scan",
    )(qe.reshape(batch, seq, wqk), kd.reshape(batch, seq, wqk), sc.reshape(batch, seq, GLA_HEADS * C), eb,
      ha3, ha3, gla_norm_g.reshape(-1, 1, GLA_DV), mix.reshape(batch, seq, -1))
    return out.reshape(mix.shape)


def _diff_kernel(lq1_ref, lk1_ref, lq2_ref, lk2_ref, q_ref, k_ref, vt_ref, g_ref, mix_ref, o_ref,
                 qs_ref, m_ref, l_ref, acc_ref, ksq_ref, *, tq, lambda_init):
    del mix_ref
    qi = pl.program_id(2)
    NT = (((1,), (1,)), ((), ()))
    hw = 2 * DIFF_D
    heads = q_ref.shape[1] // hw

    lane = lax.broadcasted_iota(jnp.int32, (tq, hw), 1)
    logit_bound_sq = F32(0.0)
    for hh in range(heads):
        hs = slice(hw * hh, hw * (hh + 1))
        q = q_ref[:, hs].astype(F32) * (DIFF_D ** -0.5)
        qs_ref[hh, :tq] = jnp.where(lane < DIFF_D, q, 0.0).astype(BF16)
        qs_ref[hh, tq:] = jnp.where(lane >= DIFF_D, q, 0.0).astype(BF16)

        @pl.when(qi == 0)
        def _():
            k = k_ref[:, hs].astype(F32)
            ksq_ref[hh] = jnp.max(jnp.sum(k * k, axis=-1, keepdims=True))

        logit_bound_sq = jnp.maximum(logit_bound_sq,
                                     jnp.max(jnp.sum(q * q, axis=-1, keepdims=True)) * ksq_ref[hh])
    m_ref[...] = jnp.full_like(m_ref, -jnp.inf)
    l_ref[...] = jnp.zeros_like(l_ref)
    acc_ref[...] = jnp.zeros_like(acc_ref)
    bounded = logit_bound_sq <= SOFTMAX_SAFE_LOGIT ** 2

    def tile(j, masked, use_max):
        r0 = pl.multiple_of(j * tq, tq)
        for hh in range(heads):
            hs = slice(hw * hh, hw * (hh + 1))
            s = lax.dot_general(k_ref[pl.ds(r0, tq), hs], qs_ref[hh], NT, preferred_element_type=F32)
            if masked:
                r = lax.broadcasted_iota(jnp.int32, s.shape, 0)
                c = lax.broadcasted_iota(jnp.int32, s.shape, 1)
                s = jnp.where(r <= jnp.where(c >= tq, c - tq, c), s, -jnp.inf)
            if use_max:
                m_prev = m_ref[hh]
                m_new = jnp.maximum(m_prev, jnp.max(s, axis=0, keepdims=True))
                alpha = jnp.exp(m_prev - m_new)
                p = jnp.exp(s - m_new)
                l_ref[hh] = alpha * l_ref[hh] + jnp.sum(p, axis=0, keepdims=True)
                acc_ref[hh] = alpha * acc_ref[hh] + jnp.dot(vt_ref[hs, pl.ds(r0, tq)], p.astype(BF16),
                                                          preferred_element_type=F32)
                m_ref[hh] = m_new
            else:
                p = jnp.exp(s)
                l_ref[hh] += jnp.sum(p, axis=0, keepdims=True)
                acc_ref[hh] += jnp.dot(vt_ref[hs, pl.ds(r0, tq)], p.astype(BF16), preferred_element_type=F32)

    def all_tiles(use_max):
        def full_tile(j, carry):
            tile(j, False, use_max)
            return carry
        lax.fori_loop(0, qi, full_tile, 0)
        tile(qi, True, use_max)

    @pl.when(bounded)
    def _():
        all_tiles(False)

    @pl.when(jnp.logical_not(bounded))
    def _():
        all_tiles(True)

    lam = (jnp.exp(jnp.sum(lq1_ref[...] * lk1_ref[...], axis=-1, keepdims=True))
           - jnp.exp(jnp.sum(lq2_ref[...] * lk2_ref[...], axis=-1, keepdims=True)) + lambda_init)
    for hh in range(heads):
        o = acc_ref[hh] / l_ref[hh]
        o = o[:, :tq] - lam * o[:, tq:]
        ms = jnp.mean(o * o, axis=0, keepdims=True)
        o = o * lax.rsqrt(ms + LN_EPS) * g_ref[...] * (1.0 - lambda_init)
        o_ref[:, hw * hh:hw * (hh + 1)] = o.T.astype(BF16)


def _diff_attention(dq, dk, dvt, lq1, lk1, lq2, lk2, norm_g, mix, lead, *, batch, seq, lambda_init, tq, heads=2):
    nq = seq // tq
    hw = 2 * DIFF_D
    pw = heads * hw
    lam_spec = pl.BlockSpec((None, 1, DIFF_D), lambda b, p, qi: (lead, 0, 0))
    lq1, lk1, lq2, lk2 = (t.reshape(-1, 1, DIFF_D) for t in (lq1, lk1, lq2, lk2))
    return pl.pallas_call(
        functools.partial(_diff_kernel, tq=tq, lambda_init=lambda_init),
        out_shape=jax.ShapeDtypeStruct(mix.shape, mix.dtype),
        grid=(batch, DIFF_HEADS // heads, nq),
        in_specs=[lam_spec] * 4 + [
            pl.BlockSpec((tq, pw), lambda b, p, qi: (b * nq + qi, p)),
            pl.BlockSpec((seq, pw), lambda b, p, qi: (b, p)),
            pl.BlockSpec((None, pw, seq), lambda b, p, qi: (b, p, 0)),
            pl.BlockSpec((None, hw, 1), lambda b, p, qi: (lead, 0, 0)),
            pl.BlockSpec(memory_space=pl.ANY)],
        out_specs=pl.BlockSpec((tq, pw), lambda b, p, qi: (b * nq + qi, MIX_DIFF // pw + p)),
        scratch_shapes=[pltpu.VMEM((heads, 2 * tq, hw), BF16), pltpu.VMEM((heads, 1, 2 * tq), F32),
                        pltpu.VMEM((heads, 1, 2 * tq), F32), pltpu.VMEM((heads, hw, 2 * tq), F32),
                        pltpu.SMEM((heads,), F32)],
        input_output_aliases={8: 0},
        compiler_params=_cparams(("parallel", "parallel", "arbitrary")),
        name="diff_attn",
    )(lq1, lk1, lq2, lk2, dq, dk, dvt, norm_g.reshape(-1, hw, 1), mix)


def _dsa_kernel(iq_ref, sq_ref, ikd_ref, wt_ref, sk_ref, svt_ref, mix_ref, o_ref,
                iqm_ref, keys_ref, bias_ref, q5_ref, m_ref, l_ref, acc_ref, ksq_ref, hi_ref, lo_ref,
                *, tq, tk, seq, k_sel):
    del mix_ref
    qi = pl.program_id(1)
    NT = (((1,), (1,)), ((), ()))
    n_tiles = (qi * tq + tq + tk - 1) // tk
    qpos = qi * tq + lax.broadcasted_iota(jnp.int32, (1, tq), 1)
    krow = lax.broadcasted_iota(jnp.int32, (tk, tq), 0)
    idx_scale = (IDX_DIM ** -0.5) * (IDX_HEADS ** -0.5)
    neg_inf = F32(-jnp.inf)

    def tile_start(j):
        return pl.multiple_of(j * tk, tk)

    lane = lax.broadcasted_iota(jnp.int32, (tq, 128), 1)
    for t in range(IDX_HEADS // 2):
        a = iq_ref[:, 128 * t:128 * (t + 1)]
        zero = jnp.zeros_like(a)
        iqm_ref[t, :tq] = jnp.where(lane < IDX_DIM, a, zero)
        iqm_ref[t, tq:] = jnp.where(lane >= IDX_DIM, a, zero)
    wt = wt_ref[...]

    def score_tile(j, carry):
        r0 = tile_start(j)
        kd = ikd_ref[pl.ds(r0, tk), :]
        acc = jnp.zeros((tk, 2 * tq), F32)
        for t in range(IDX_HEADS // 2):
            lg = lax.dot_general(kd, iqm_ref[t], NT, preferred_element_type=F32)
            w2 = jnp.concatenate([wt[2 * t:2 * t + 1], wt[2 * t + 1:2 * t + 2]], axis=1)
            acc = acc + jnp.maximum(lg, 0.0) * w2
        acc = acc[:, :tq] + acc[:, tq:]
        bits = pltpu.bitcast(acc * idx_scale, jnp.int32)
        key = jnp.where(bits < 0, (bits ^ 0x7FFFFFFF) + 1, bits)
        key = jnp.where(r0 + krow <= qpos, key, INT_MIN)
        keys_ref[pl.ds(r0, tk), :] = key
        hi_ref[pl.ds(r0, tk), :] = (key >> 16).astype(jnp.int16)
        return carry

    lax.fori_loop(0, n_tiles, score_tile, 0)

    def count_ones(ones):
        def body(j, cnt):
            r0 = tile_start(j)
            return cnt + jnp.sum(ones(keys_ref[pl.ds(r0, tk), :], r0 + krow), axis=0, keepdims=True)
        return lax.fori_loop(0, n_tiles, body, jnp.zeros((1, tq), jnp.int32))

    def count(pred):
        return count_ones(lambda kk, kp: jnp.where(pred(kk, kp), 1, 0))

    def count16(ref, pred):
        def body(j, cnt):
            ones = jnp.where(pred(ref[pl.ds(tile_start(j), tk), :]), jnp.int16(1), jnp.int16(0))
            ones = ones.reshape(tk // 16, 16, tq)
            part = ones[0]
            for r in range(1, tk // 16):
                part = part + ones[r]
            return cnt + jnp.sum(part.astype(jnp.int32), axis=0, keepdims=True)
        return lax.fori_loop(0, n_tiles, body, jnp.zeros((1, tq), jnp.int32))

    def as_ordered_i16(u):
        return jnp.right_shift(jnp.left_shift(u ^ 0x8000, 16), 16).astype(jnp.int16)

    def search16(ref, count_above):
        def bit(i, t_u):
            cand_u = t_u | jnp.left_shift(jnp.int32(1), 15 - i)
            cand = as_ordered_i16(cand_u)
            c = count_above + count16(ref, lambda v: v >= cand)
            return jnp.where(c >= k_sel, cand_u, t_u)
        return lax.fori_loop(0, 16, bit, jnp.zeros((1, tq), jnp.int32))

    thr_hi = as_ordered_i16(search16(hi_ref, 0))
    above = count16(hi_ref, lambda v: v > thr_hi)

    def low_half_tile(j, carry):
        rows = pl.ds(tile_start(j), tk)
        lo = as_ordered_i16(keys_ref[rows, :] & 0xFFFF)
        lo_ref[rows, :] = jnp.where(hi_ref[rows, :] == thr_hi, lo, jnp.int16(-32768))
        return carry

    lax.fori_loop(0, n_tiles, low_half_tile, 0)
    thr = jnp.left_shift(thr_hi.astype(jnp.int32), 16) | search16(lo_ref, above)
    need = k_sel - count(lambda kk, kp: kk > thr)
    n_eq = count(lambda kk, kp: kk == thr)
    excess = jnp.where(thr == INT_MIN, 0, jnp.where(n_eq > need, 1, 0))

    def tie_search():
        def idx_bit(i, j_lo):
            cand = j_lo | jnp.left_shift(jnp.int32(1), (seq - 1).bit_length() - 1 - i)
            c = count_ones(lambda kk, kp: jnp.where(kk == thr, jnp.where(kp < cand, 1, 0), 0))
            return jnp.where(c < need, cand, j_lo)
        return lax.fori_loop(0, (seq - 1).bit_length(), idx_bit, jnp.zeros((1, tq), jnp.int32))

    j_star = lax.cond(jnp.max(excess) > 0, tie_search, lambda: jnp.full((1, tq), seq, jnp.int32))
    j_star = jnp.where(thr == INT_MIN, -1, jnp.where(excess > 0, j_star, seq))

    def bias_tile(j, carry):
        r0 = tile_start(j)
        kk = keys_ref[pl.ds(r0, tk), :]
        tie = jnp.where(r0 + krow <= j_star, 0.0, neg_inf)
        bias_ref[pl.ds(r0, tk), :] = jnp.where(kk > thr, 0.0, jnp.where(kk == thr, tie, neg_inf))
        return carry

    lax.fori_loop(0, n_tiles, bias_tile, 0)

    scale = DSA_DH ** -0.5
    rep = DSA_HEADS // DSA_KV
    logit_bound_sq = F32(0.0)
    for g in range(DSA_KV):
        q_sq = F32(0.0)
        for i in range(rep):
            h = g * rep + i
            qh = sq_ref[:, DSA_DH * h:DSA_DH * (h + 1)]
            q5_ref[g, tq * i:tq * (i + 1)] = qh
            qf = qh.astype(F32)
            q_sq = jnp.maximum(q_sq, jnp.max(jnp.sum(qf * qf, axis=-1, keepdims=True)))

        @pl.when(qi == 0)
        def _():
            kf = sk_ref[:, DSA_DH * g:DSA_DH * (g + 1)].astype(F32)
            ksq_ref[g] = jnp.max(jnp.sum(kf * kf, axis=-1, keepdims=True))

        logit_bound_sq = jnp.maximum(logit_bound_sq, q_sq * ksq_ref[g] * (scale * scale))
    m_ref[...] = jnp.full_like(m_ref, neg_inf)
    l_ref[...] = jnp.zeros_like(l_ref)
    acc_ref[...] = jnp.zeros_like(acc_ref)
    bounded = logit_bound_sq <= SOFTMAX_SAFE_LOGIT ** 2

    def attn_tiles(use_max):
        def attn_tile(j, carry):
            r0 = tile_start(j)
            bias = bias_ref[pl.ds(r0, tk), :]
            bias = jnp.concatenate([bias] * rep, axis=1)
            for g in range(DSA_KV):
                gs = slice(DSA_DH * g, DSA_DH * (g + 1))
                s = lax.dot_general(sk_ref[pl.ds(r0, tk), gs], q5_ref[g], NT, preferred_element_type=F32)
                s = s * scale + bias
                vt = svt_ref[gs, pl.ds(r0, tk)]
                if use_max:
                    m_prev = m_ref[g]
                    m_new = jnp.maximum(m_prev, jnp.max(s, axis=0, keepdims=True))
                    m_safe = jnp.where(m_new == neg_inf, 0.0, m_new)
                    alpha = jnp.exp(m_prev - m_safe)
                    p = jnp.exp(s - m_safe)
                    l_ref[g] = alpha * l_ref[g] + jnp.sum(p, axis=0, keepdims=True)
                    acc_ref[g] = alpha * acc_ref[g] + jnp.dot(vt, p.astype(BF16), preferred_element_type=F32)
                    m_ref[g] = m_new
                else:
                    p = jnp.exp(s)
                    l_ref[g] += jnp.sum(p, axis=0, keepdims=True)
                    acc_ref[g] += jnp.dot(vt, p.astype(BF16), preferred_element_type=F32)
            return carry

        lax.fori_loop(0, n_tiles, attn_tile, 0)

    @pl.when(bounded)
    def _():
        attn_tiles(False)

    @pl.when(jnp.logical_not(bounded))
    def _():
        attn_tiles(True)
    for g in range(DSA_KV):
        out_t = acc_ref[g] / l_ref[g]
        for i in range(rep):
            h = g * rep + i
            o_ref[:, DSA_DH * h:DSA_DH * (h + 1)] = out_t[:, tq * i:tq * (i + 1)].T.astype(BF16)


def _dsa(iq, sq, ikd, wt, sk, svt, mix, *, batch, seq, tq, tk):
    nq = seq // tq
    k_sel = min(IDX_TOPK, seq // 4)
    rep = DSA_HEADS // DSA_KV
    return pl.pallas_call(
        functools.partial(_dsa_kernel, tq=tq, tk=tk, seq=seq, k_sel=k_sel),
        out_shape=jax.ShapeDtypeStruct(mix.shape, mix.dtype),
        grid=(batch, nq),
        in_specs=[pl.BlockSpec((tq, W_IQ), lambda b, qi: (b * nq + qi, 0)),
                  pl.BlockSpec((tq, W_SQ), lambda b, qi: (b * nq + qi, 0)),
                  pl.BlockSpec((seq, 128), lambda b, qi: (b, 0)),
                  pl.BlockSpec((None, IDX_HEADS, tq), lambda b, qi: (b, 0, qi)),
                  pl.BlockSpec((seq, W_SKV), lambda b, qi: (b, 0)),
                  pl.BlockSpec((None, W_SKV, seq), lambda b, qi: (b, 0, 0)),
                  pl.BlockSpec(memory_space=pl.ANY)],
        out_specs=pl.BlockSpec((tq, W_SQ), lambda b, qi: (b * nq + qi, MIX_DSA // W_SQ)),
        scratch_shapes=[pltpu.VMEM((IDX_HEADS // 2, 2 * tq, 128), BF16), pltpu.VMEM((seq, tq), jnp.int32),
                        pltpu.VMEM((seq, tq), F32), pltpu.VMEM((DSA_KV, rep * tq, DSA_DH), BF16),
                        pltpu.VMEM((DSA_KV, 1, rep * tq), F32), pltpu.VMEM((DSA_KV, 1, rep * tq), F32),
                        pltpu.VMEM((DSA_KV, DSA_DH, rep * tq), F32), pltpu.SMEM((DSA_KV,), F32),
                        pltpu.VMEM((seq, tq), jnp.int16), pltpu.VMEM((seq, tq), jnp.int16)],
        input_output_aliases={6: 0},
        compiler_params=_cparams(("parallel", "arbitrary")),
        name="dsa",
    )(iq, sq, ikd, wt, sk, svt, mix)


def _router_kernel(x_ref, w_ref, b_ref, e_ref, g_ref, xp_ref):
    x = x_ref[...]
    logits = jnp.dot(x, w_ref[...], precision=HIGHEST, preferred_element_type=F32) + b_ref[...]
    lane = lax.broadcasted_iota(jnp.int32, logits.shape, 1)
    neg_inf = F32(-jnp.inf)
    lg = jnp.where(lane < MOE_E, logits, neg_inf)
    m1 = jnp.max(lg, axis=-1, keepdims=True)
    i1 = jnp.min(jnp.where(lg == m1, lane, 128), axis=-1, keepdims=True)
    lg2 = jnp.where(lane == i1, neg_inf, lg)
    m2 = jnp.max(lg2, axis=-1, keepdims=True)
    i2 = jnp.min(jnp.where(lg2 == m2, lane, 128), axis=-1, keepdims=True)
    e21 = jnp.exp(m2 - m1)
    g1 = 1.0 / (1.0 + e21)
    e_ref[...] = jnp.where(lane == 0, i1, jnp.where(lane == 1, i2, 0))
    g_ref[...] = jnp.where(lane == 0, g1, jnp.where(lane == 1, e21 * g1, 0.0))
    half = x.shape[1] // 2
    lo = pltpu.bitcast(x[:, :half].astype(BF16).astype(F32), jnp.uint32)
    hi = pltpu.bitcast(x[:, half:].astype(BF16).astype(F32), jnp.uint32)
    xp_ref[...] = (hi & jnp.uint32(0xFFFF0000)) | (lo >> 16)


def _router(x, router_w, router_b, *, tm=256):
    n, d = x.shape
    row = lambda w: pl.BlockSpec((tm, w), lambda i: (i, 0))
    return pl.pallas_call(
        _router_kernel,
        out_shape=(jax.ShapeDtypeStruct((n, 128), jnp.int32), jax.ShapeDtypeStruct((n, 128), F32),
                   jax.ShapeDtypeStruct((n, d // 2), jnp.uint32)),
        grid=(n // tm,),
        in_specs=[row(d), pl.BlockSpec((d, 128), lambda i: (0, 0)), pl.BlockSpec((1, 128), lambda i: (0, 0))],
        out_specs=(row(128), row(128), row(d // 2)),
        compiler_params=_cparams(("parallel",)),
        name="router",
    )(x, router_w, router_b)


def _dispatch_kernel(pos_ref, xp_ref, buf_ref, o_ref, sem, *, tb):
    del buf_ref
    base = pl.program_id(0) * tb

    def row_copy(i, s):
        return pltpu.make_async_copy(xp_ref.at[pl.ds(i, 1)], o_ref.at[pl.ds(pos_ref[2 * (base + i) + s], 1)], sem)

    def start(i, c):
        row_copy(i, 0).start()
        row_copy(i, 1).start()
        return c

    def wait(i, c):
        row_copy(i, 0).wait()
        row_copy(i, 1).wait()
        return c

    lax.fori_loop(0, tb, start, 0)
    lax.fori_loop(0, tb, wait, 0)


def _dispatch(pos, xp, buf, *, tb=256):
    n, w = xp.shape
    return pl.pallas_call(
        functools.partial(_dispatch_kernel, tb=tb),
        out_shape=jax.ShapeDtypeStruct(buf.shape, buf.dtype),
        grid_spec=pltpu.PrefetchScalarGridSpec(
            num_scalar_prefetch=1,
            grid=(n // tb,),
            in_specs=[pl.BlockSpec((tb, w), lambda i, pos: (i, 0)), pl.BlockSpec(memory_space=pl.ANY)],
            out_specs=pl.BlockSpec(memory_space=pl.ANY),
            scratch_shapes=[pltpu.SemaphoreType.DMA],
        ),
        input_output_aliases={2: 0},
        compiler_params=_cparams(("arbitrary",)),
        name="moe_dispatch",
    )(pos, xp, buf)


def _unpack_rows(word):
    lo = pltpu.bitcast(word << 16, F32).astype(BF16)
    hi = pltpu.bitcast(word & jnp.uint32(0xFFFF0000), F32).astype(BF16)
    return lo, hi


def _expert_tile_refresh(be_ref, bf_ref, nxt_ref, w_hbms, stage_ref, wb_refs, sem, *, lead, tn):
    n, i = pl.program_id(0), pl.program_id(1)

    def fetch(e, col_tile):
        cols = pl.ds(pl.multiple_of(col_tile * tn, tn), tn)
        return [pltpu.make_async_copy(w.at[lead, e, :, cols], stage_ref.at[t], sem)
                for t, w in enumerate(w_hbms)]

    @pl.when(bf_ref[i] == 1)
    def _():
        @pl.when(jnp.logical_and(n == 0, i == 0))
        def _():
            for c in fetch(be_ref[0], 0):
                c.start()

        for c in fetch(be_ref[i], n):
            c.wait()
        for t, wb in enumerate(wb_refs):
            wb[...] = stage_ref[t].astype(BF16)
        last_run = nxt_ref[i] < 0
        e_next = jnp.where(last_run, be_ref[0], nxt_ref[i])
        n_next = jnp.where(last_run, n + 1, n)

        @pl.when(n_next < pl.num_programs(0))
        def _():
            for c in fetch(e_next, n_next):
                c.start()


def _moe_up_kernel(be_ref, bf_ref, na_ref, nxt_ref, a_ref, w1_hbm, w3_hbm, o_ref,
                   stage_ref, w1b_ref, w3b_ref, sem, *, lead, tn):
    i = pl.program_id(1)

    @pl.when(i < na_ref[0])
    def _():
        _expert_tile_refresh(be_ref, bf_ref, nxt_ref, (w1_hbm, w3_hbm), stage_ref, (w1b_ref, w3b_ref), sem,
                             lead=lead, tn=tn)
        lo, hi = _unpack_rows(a_ref[...])
        half = lo.shape[1]
        g = (jnp.dot(lo, w1b_ref[:half], preferred_element_type=F32)
             + jnp.dot(hi, w1b_ref[half:], preferred_element_type=F32))
        u = (jnp.dot(lo, w3b_ref[:half], preferred_element_type=F32)
             + jnp.dot(hi, w3b_ref[half:], preferred_element_type=F32))
        o_ref[...] = (g * jax.nn.sigmoid(g) * u).astype(BF16)

    @pl.when(i >= na_ref[0])
    def _():
        o_ref[...] = jnp.zeros_like(o_ref)


def _moe_down_kernel(be_ref, bf_ref, na_ref, nxt_ref, h_ref, w2_hbm, o_ref, stage_ref, w2b_ref, sem, *, lead, tn):
    i = pl.program_id(1)

    @pl.when(i < na_ref[0])
    def _():
        _expert_tile_refresh(be_ref, bf_ref, nxt_ref, (w2_hbm,), stage_ref, (w2b_ref,), sem, lead=lead, tn=tn)
        o_ref[...] = jnp.dot(h_ref[...], w2b_ref[...], preferred_element_type=F32)

    @pl.when(i >= na_ref[0])
    def _():
        o_ref[...] = jnp.zeros_like(o_ref)


def _moe_ffn(block_e, block_first, n_active, next_e, a_sorted, w1, w3, w2, lead, *, tm, tn):
    r, half = a_sorted.shape
    d = 2 * half
    f = w1.shape[-1]
    nb = r // tm
    hbm = pl.BlockSpec(memory_space=pl.ANY)

    def rows(i, na):
        return jnp.minimum(i, na[0] - 1)

    h = pl.pallas_call(
        functools.partial(_moe_up_kernel, lead=lead, tn=tn),
        out_shape=jax.ShapeDtypeStruct((r, f), BF16),
        grid_spec=pltpu.PrefetchScalarGridSpec(
            num_scalar_prefetch=4,
            grid=(f // tn, nb),
            in_specs=[pl.BlockSpec((tm, half), lambda n, i, be, bf, na, nx: (rows(i, na), 0)), hbm, hbm],
            out_specs=pl.BlockSpec((tm, tn), lambda n, i, be, bf, na, nx: (i, n)),
            scratch_shapes=[pltpu.VMEM((2, d, tn), F32), pltpu.VMEM((d, tn), BF16), pltpu.VMEM((d, tn), BF16),
                            pltpu.SemaphoreType.DMA],
        ),
        compiler_params=_cparams(("arbitrary", "arbitrary")),
        name="moe_up",
    )(block_e, block_first, n_active, next_e, a_sorted, w1, w3)
    return pl.pallas_call(
        functools.partial(_moe_down_kernel, lead=lead, tn=tn),
        out_shape=jax.ShapeDtypeStruct((r, d), F32),
        grid_spec=pltpu.PrefetchScalarGridSpec(
            num_scalar_prefetch=4,
            grid=(d // tn, nb),
            in_specs=[pl.BlockSpec((tm, f), lambda n, i, be, bf, na, nx: (rows(i, na), 0)), hbm],
            out_specs=pl.BlockSpec((tm, tn), lambda n, i, be, bf, na, nx: (i, n)),
            scratch_shapes=[pltpu.VMEM((1, f, tn), F32), pltpu.VMEM((f, tn), BF16), pltpu.SemaphoreType.DMA],
        ),
        compiler_params=_cparams(("arbitrary", "arbitrary")),
        name="moe_down",
    )(block_e, block_first, n_active, next_e, h, w2)


def _combine_ln_kernel(pos_ref, x_ref, gate_ref, g_ref, b_ref, y_ref, o_ref, ybuf_ref, sem, *, tb, alpha):
    step, n_steps = pl.program_id(0), pl.num_programs(0)
    slot = step % 2

    def row_copy(blk, buf, i, s):
        return pltpu.make_async_copy(y_ref.at[pl.ds(pos_ref[2 * (blk * tb + i) + s], 1)],
                                     ybuf_ref.at[buf, s, pl.ds(i, 1)], sem.at[buf])

    def start_block(blk, buf):
        def body(i, c):
            row_copy(blk, buf, i, 0).start()
            row_copy(blk, buf, i, 1).start()
            return c
        lax.fori_loop(0, tb, body, 0)

    def wait_block(blk, buf):
        def body(i, c):
            row_copy(blk, buf, i, 0).wait()
            row_copy(blk, buf, i, 1).wait()
            return c
        lax.fori_loop(0, tb, body, 0)

    @pl.when(step == 0)
    def _():
        start_block(0, 0)

    @pl.when(step + 1 < n_steps)
    def _():
        start_block(step + 1, 1 - slot)

    wait_block(step, slot)
    gate = gate_ref[...]
    f = ybuf_ref[slot, 0] * gate[:, 0:1] + ybuf_ref[slot, 1] * gate[:, 1:2]
    z = alpha * x_ref[...] + f
    mu = jnp.mean(z, axis=-1, keepdims=True)
    zc = z - mu
    var = jnp.mean(zc * zc, axis=-1, keepdims=True)
    o_ref[...] = zc * lax.rsqrt(var + LN_EPS) * g_ref[...] + b_ref[...]


def _combine_ln(pos, x, gates, g, b, y, lead, *, alpha, tb=128):
    n, d = x.shape
    par = pl.BlockSpec((None, 1, d), lambda i, pos: (lead, 0, 0))
    return pl.pallas_call(
        functools.partial(_combine_ln_kernel, tb=tb, alpha=alpha),
        out_shape=jax.ShapeDtypeStruct((n, d), F32),
        grid_spec=pltpu.PrefetchScalarGridSpec(
            num_scalar_prefetch=1,
            grid=(n // tb,),
            in_specs=[pl.BlockSpec((tb, d), lambda i, pos: (i, 0)),
                      pl.BlockSpec((tb, 128), lambda i, pos: (i, 0)),
                      par, par, pl.BlockSpec(memory_space=pl.ANY)],
            out_specs=pl.BlockSpec((tb, d), lambda i, pos: (i, 0)),
            scratch_shapes=[pltpu.VMEM((2, 2, tb, d), F32), pltpu.SemaphoreType.DMA((2,))],
        ),
        compiler_params=_cparams(("arbitrary",)),
        name="moe_combine_ln",
    )(pos, x, gates, g.reshape(-1, 1, d), b.reshape(-1, 1, d), y)


def _moe_layer(x, router_w, router_b, w1, w3, w2, ln_g, ln_b, lead_moe, lead_ln, *, alpha, tm=512, tn=512):
    n, d = x.shape
    rw = jnp.pad(router_w[lead_moe], ((0, 0), (0, 128 - MOE_E)))
    rb = jnp.pad(router_b[lead_moe], (0, 128 - MOE_E)).reshape(1, 128)
    top_e, gates, xp = _router(x, rw, rb)
    flat_e = top_e[:, :MOE_TOPK].reshape(-1)
    onehot = (flat_e[:, None] == jnp.arange(MOE_E, dtype=jnp.int32)[None, :]).astype(jnp.int32)
    rank = jnp.sum((jnp.cumsum(onehot, axis=0) - onehot) * onehot, axis=1)
    counts = jnp.sum(onehot, axis=0)
    padded = (counts + tm - 1) // tm * tm
    end_padded = jnp.cumsum(padded)
    start_padded = end_padded - padded
    pos = (start_padded[flat_e] + rank).astype(jnp.int32)
    nb = -(-(n * MOE_TOPK + MOE_E * (tm - 1)) // tm)
    block_start = jnp.arange(nb, dtype=jnp.int32) * tm
    block_e = jnp.sum((block_start[:, None] >= end_padded[None, :]).astype(jnp.int32), axis=1)
    block_e = jnp.minimum(block_e, MOE_E - 1)
    n_active = (end_padded[-1] // tm).astype(jnp.int32).reshape(1)
    block_first = jnp.concatenate([jnp.ones((1,), jnp.int32), (block_e[1:] != block_e[:-1]).astype(jnp.int32)])
    blk = jnp.arange(nb, dtype=jnp.int32)
    run_start = (block_first == 1) & (blk < n_active[0])
    later_start = jnp.where((blk[None, :] > blk[:, None]) & run_start[None, :], blk[None, :], nb)
    next_start = jnp.min(later_start, axis=1)
    next_e = jnp.where(next_start < nb, block_e[jnp.minimum(next_start, nb - 1)], -1).astype(jnp.int32)
    a_sorted = _dispatch(pos, xp, jnp.zeros((nb * tm, d // 2), jnp.uint32))
    y = _moe_ffn(block_e, block_first, n_active, next_e, a_sorted, w1, w3, w2, lead_moe, tm=tm, tn=tn)
    return _combine_ln(pos, x, gates, ln_g, ln_b, y, lead_ln, alpha=alpha)


def kernel(x, positions, w_in, w_gate_up, b_gate, gla_norm_g, lambda_q1, lambda_k1, lambda_q2, lambda_k2,
           diff_norm_g, w_out, ln1_g, ln1_b, ln2_g, ln2_b, ffn_w1, ffn_w3, ffn_w2, router_w, router_b,
           moe_w1, moe_w3, moe_w2):
    batch, seq, d = x.shape
    n = batch * seq
    depth = w_in.shape[0]
    alpha = (2 * depth) ** 0.25
    xf = x.reshape(n, d)
    xb = xf.astype(BF16)
    pos = positions.reshape(n)
    tab_a, half_a = _rope_tables(pos, DIFF_D)
    tab_b, half_b = _rope_tables(pos, DSA_DH)
    plain = lambda i, j, k: (i, k)
    w_in_t = jnp.swapaxes(w_in, 1, 2)
    for l in range(depth):
        ha = _matmul(xb, w_in_t, n_out=W_GLA, k_steps=1, tm=1024, tn=HM_TN, tk=d, out_dtype=BF16,
                     a_map=plain, w_map=lambda i, j, k, l=l: (l, j, k), w_lead=True, w_t=True,
                     name="in_proj_gla")
        hm = _matmul(xb, w_in_t, n_out=HM_TILES * HM_TN, k_steps=1, tm=1024, tn=HM_TN, tk=d,
                     out_dtype=BF16, a_map=plain,
                     w_map=lambda i, j, k, l=l: (l, HM_BASE // HM_TN + j, k), w_lead=True, w_t=True,
                     n_valid=N_IN, col_start=lambda j: HM_BASE + j * HM_TN, name="in_proj_attn")
        small_tiles = (OFF_GLR // 128, OFF_IK // 128)
        pick = lambda j: jnp.where(j == 0, small_tiles[0], small_tiles[1])
        hs = _matmul(xb, w_in_t, n_out=256, k_steps=1, tm=1024, tn=128, tk=d, out_dtype=F32,
                     a_map=plain, w_map=lambda i, j, k, l=l: (l, pick(j), k), w_lead=True, w_t=True,
                     n_valid=N_IN, col_start=lambda j: pick(j) * 128, name="in_proj_small")
        glr = hs[:, OFF_GLR % 128:OFF_GLR % 128 + GLA_RANK]
        iw0 = 128 + OFF_IW % 128
        wt = hs[:, iw0:iw0 + IDX_HEADS].reshape(batch, seq, IDX_HEADS).transpose(0, 2, 1)
        dq, dk, dv, sq, sk, sv, iq, ikd = _split_rope(hm, hs, tab_a, half_a, tab_b, half_b)
        svt = sv.reshape(batch, seq, W_SKV).transpose(0, 2, 1)
        dvt = dv.reshape(batch, seq, W_DIFF).transpose(0, 2, 1)

        mix = jnp.zeros((n, W_MIX), BF16)
        mix = _gla(ha, glr, w_gate_up, b_gate, gla_norm_g, mix, l, batch=batch, seq=seq)
        lambda_init = 0.8 - 0.6 * math.exp(-0.3 * l)
        mix = _diff_attention(dq, dk, dvt, lambda_q1, lambda_k1, lambda_q2, lambda_k2, diff_norm_g, mix, l,
                              batch=batch, seq=seq, lambda_init=lambda_init, tq=1024, heads=1)
        mix = _dsa(iq, sq, ikd, wt, sk, svt, mix, batch=batch, seq=seq, tq=256, tk=512)

        proj = _out_proj(mix, w_out, l)
        xf, xb = _add_ln(xf, proj, ln1_g, ln1_b, l, alpha=alpha)

        j = l // 2
        if l % 2 == 0:
            f_dim = ffn_w2.shape[1]
            f_pad = -(-f_dim // 1024) * 1024
            hid = _swiglu_up(xb, ffn_w1, ffn_w3, j, tm=1024, tn=256, tk=d, f_out=f_pad)
            down = _matmul(hid, ffn_w2, n_out=d, k_steps=8, tm=2048, tn=1024, tk=f_pad // 8, out_dtype=F32,
                           a_map=plain, w_map=lambda i, j_, k, j=j: (j, k, j_), w_lead=True, k_valid=f_dim,
                           name="ffn_down")
            xf, xb = _add_ln(xf, down, ln2_g, ln2_b, l, alpha=alpha)
        else:
            xf = _moe_layer(xf, router_w, router_b, moe_w1, moe_w3, moe_w2, ln2_g, ln2_b, j, l, alpha=alpha)
            xb = xf.astype(BF16)
    return xf.reshape(batch, seq, d)
```

```python
import functools
import math

import jax
import jax.numpy as jnp
from jax import lax
from jax.experimental import pallas as pl
from jax.experimental.pallas import tpu as pltpu

F32 = jnp.float32
BF16 = jnp.bfloat16
HIGHEST = lax.Precision.HIGHEST
INT_MIN = -(2 ** 31)

ROPE_THETA = 500000.0
GLA_HEADS, GLA_DK, GLA_DV, GLA_RANK, GLA_TAU, GLA_CHUNK = 4, 192, 384, 16, 16.0, 64
GLA_SUB = 16
GLA_SAFE_DECAY = 40.0
GLA_SAFE_K = 1e18
DIFF_HEADS, DIFF_D = 10, 64
DSA_HEADS, DSA_KV, DSA_DH = 10, 2, 128
IDX_HEADS, IDX_DIM, IDX_TOPK = 32, 64, 256
MOE_E, MOE_TOPK = 8, 2
LN_EPS = 1e-5
SOFTMAX_SAFE_LOGIT = 60.0

W_GLA = 2 * GLA_HEADS * GLA_DK + 2 * GLA_HEADS * GLA_DV
OFF_GLR = W_GLA
OFF_DQ = OFF_GLR + GLA_RANK
W_DIFF = DIFF_HEADS * 2 * DIFF_D
OFF_DK, OFF_DV = OFF_DQ + W_DIFF, OFF_DQ + 2 * W_DIFF
OFF_SQ = OFF_DQ + 3 * W_DIFF
W_SQ, W_SKV = DSA_HEADS * DSA_DH, DSA_KV * DSA_DH
OFF_SK, OFF_SV = OFF_SQ + W_SQ, OFF_SQ + W_SQ + W_SKV
OFF_IQ = OFF_SV + W_SKV
W_IQ = IDX_HEADS * IDX_DIM
OFF_IK = OFF_IQ + W_IQ
OFF_IW = OFF_IK + IDX_DIM
N_IN = OFF_IW + IDX_HEADS

HM_BASE = W_GLA
HM_TN = 512
HM_TILES = -(-(N_IN - HM_BASE) // HM_TN)

MIX_GLA, MIX_DIFF, MIX_DSA = 0, 2560, 3840
W_MIX = MIX_DSA + DSA_HEADS * DSA_DH

VMEM_LIMIT = 56 * 1024 * 1024


def _cparams(sem, vmem=VMEM_LIMIT):
    return pltpu.CompilerParams(dimension_semantics=sem, vmem_limit_bytes=vmem)


def _mm_kernel(a_ref, w_ref, o_ref, acc_ref=None, *, k_valid, n_valid, col_start, w_t):
    k = pl.program_id(2)
    k_ax, n_ax = (1, 0) if w_t else (0, 1)

    def partial_product():
        w = w_ref[...]
        if k_valid is not None:
            kk = k * w.shape[k_ax] + lax.broadcasted_iota(jnp.int32, w.shape, k_ax)
            w = jnp.where(kk < k_valid, w, 0.0)
        if n_valid is not None:
            col = col_start(pl.program_id(1)) + lax.broadcasted_iota(jnp.int32, w.shape, n_ax)
            w = jnp.where(col < n_valid, w, 0.0)
        return lax.dot_general(a_ref[...], w.astype(BF16), (((1,), (k_ax,)), ((), ())),
                               preferred_element_type=F32)

    if acc_ref is None:
        o_ref[...] = partial_product().astype(o_ref.dtype)
        return

    @pl.when(k == 0)
    def _():
        acc_ref[...] = partial_product()

    @pl.when(k > 0)
    def _():
        acc_ref[...] += partial_product()

    @pl.when(k == pl.num_programs(2) - 1)
    def _():
        o_ref[...] = acc_ref[...].astype(o_ref.dtype)


def _matmul(a, w, *, n_out, k_steps, tm, tn, tk, out_dtype, a_map, w_map, w_lead=False, k_valid=None,
            n_valid=None, col_start=None, w_t=False, name="mm"):
    m = a.shape[0]
    w_block = (tn, tk) if w_t else (tk, tn)
    if w_lead:
        w_block = (None,) + w_block
    return pl.pallas_call(
        functools.partial(_mm_kernel, k_valid=k_valid, n_valid=n_valid, col_start=col_start, w_t=w_t),
        out_shape=jax.ShapeDtypeStruct((m, n_out), out_dtype),
        grid=(m // tm, pl.cdiv(n_out, tn), k_steps),
        in_specs=[pl.BlockSpec((tm, tk), a_map), pl.BlockSpec(w_block, w_map)],
        out_specs=pl.BlockSpec((tm, tn), lambda i, j, k: (i, j)),
        scratch_shapes=[pltpu.VMEM((tm, tn), F32)] if k_steps > 1 else [],
        compiler_params=_cparams(("parallel", "parallel", "arbitrary")),
        name=name,
    )(a, w)


def _swiglu_kernel(a_ref, w1_ref, w3_ref, o_ref, acc1_ref=None, acc3_ref=None, *, n_valid):
    k = pl.program_id(2)

    def partial_product(w_ref):
        return jnp.dot(a_ref[...], w_ref[...].astype(BF16), preferred_element_type=F32)

    def finish(g, u):
        h = g * jax.nn.sigmoid(g) * u
        col = pl.program_id(1) * h.shape[1] + lax.broadcasted_iota(jnp.int32, h.shape, 1)
        o_ref[...] = jnp.where(col < n_valid, h, 0.0).astype(o_ref.dtype)

    if acc1_ref is None:
        finish(partial_product(w1_ref), partial_product(w3_ref))
        return

    @pl.when(k == 0)
    def _():
        acc1_ref[...] = partial_product(w1_ref)
        acc3_ref[...] = partial_product(w3_ref)

    @pl.when(k > 0)
    def _():
        acc1_ref[...] += partial_product(w1_ref)
        acc3_ref[...] += partial_product(w3_ref)

    @pl.when(k == pl.num_programs(2) - 1)
    def _():
        finish(acc1_ref[...], acc3_ref[...])


def _swiglu_up(a, w1, w3, lead, *, tm, tn, tk, f_out):
    m, kdim = a.shape
    f = w1.shape[-1]
    last = (f - 1) // tn
    wspec = pl.BlockSpec((None, tk, tn), lambda i, j, k: (lead, k, jnp.minimum(j, last)))
    k_steps = kdim // tk
    return pl.pallas_call(
        functools.partial(_swiglu_kernel, n_valid=f),
        out_shape=jax.ShapeDtypeStruct((m, f_out), BF16),
        grid=(m // tm, f_out // tn, k_steps),
        in_specs=[pl.BlockSpec((tm, tk), lambda i, j, k: (i, k)), wspec, wspec],
        out_specs=pl.BlockSpec((tm, tn), lambda i, j, k: (i, j)),
        scratch_shapes=[pltpu.VMEM((tm, tn), F32), pltpu.VMEM((tm, tn), F32)] if k_steps > 1 else [],
        compiler_params=_cparams(("parallel", "parallel", "arbitrary")),
        name="swiglu_up",
    )(a, w1, w3)


def _out_proj_kernel(mix_ref, w_ref, o_ref):
    n_gla = GLA_HEADS * GLA_DV
    w = w_ref[...].astype(BF16)
    o_ref[...] = (jnp.dot(mix_ref[:, MIX_GLA:MIX_GLA + n_gla], w[:n_gla], preferred_element_type=F32)
                  + jnp.dot(mix_ref[:, MIX_DIFF:], w[n_gla:], preferred_element_type=F32))


def _out_proj(mix, w_out, lead, *, tm=1024, tn=256):
    m, wm = mix.shape
    d = w_out.shape[-1]
    return pl.pallas_call(
        _out_proj_kernel,
        out_shape=jax.ShapeDtypeStruct((m, d), F32),
        grid=(m // tm, d // tn),
        in_specs=[pl.BlockSpec((tm, wm), lambda i, j: (i, 0)),
                  pl.BlockSpec((None, w_out.shape[1], tn), lambda i, j: (lead, 0, j))],
        out_specs=pl.BlockSpec((tm, tn), lambda i, j: (i, j)),
        compiler_params=_cparams(("parallel", "parallel")),
        name="out_proj",
    )(mix, w_out)


def _add_ln_kernel(x_ref, y_ref, g_ref, b_ref, o_ref, ob_ref, *, alpha):
    z = alpha * x_ref[...] + y_ref[...].astype(F32)
    mu = jnp.mean(z, axis=-1, keepdims=True)
    zc = z - mu
    var = jnp.mean(zc * zc, axis=-1, keepdims=True)
    out = zc * lax.rsqrt(var + LN_EPS) * g_ref[...] + b_ref[...]
    o_ref[...] = out
    ob_ref[...] = out.astype(BF16)


def _add_ln(x, y, g, b, lead, *, alpha, tm=256):
    m, d = x.shape
    row = pl.BlockSpec((tm, d), lambda i: (i, 0))
    par = pl.BlockSpec((None, 1, d), lambda i: (lead, 0, 0))
    g, b = g.reshape(-1, 1, d), b.reshape(-1, 1, d)
    return pl.pallas_call(
        functools.partial(_add_ln_kernel, alpha=alpha),
        out_shape=(jax.ShapeDtypeStruct((m, d), F32), jax.ShapeDtypeStruct((m, d), BF16)),
        grid=(m // tm,),
        in_specs=[row, row, par, par],
        out_specs=(row, row),
        compiler_params=_cparams(("parallel",)),
        name="add_ln",
    )(x, y, g, b)


def _rope_tables(positions, width):
    rot = width // 4
    half = rot // 2
    inv_freq = 1.0 / (ROPE_THETA ** (jnp.arange(half, dtype=F32) * 2.0 / rot))
    ang = positions.astype(F32)[:, None] * inv_freq
    cos, sin = jnp.cos(ang), jnp.sin(ang)
    n = positions.shape[0]
    one = jnp.ones((n, width - rot), F32)
    zero_h = jnp.zeros((n, half), F32)
    zero_r = jnp.zeros((n, width - rot), F32)
    c = jnp.concatenate([cos, cos, one], axis=1)
    s_up = jnp.concatenate([-sin, zero_h, zero_r], axis=1)
    s_dn = jnp.concatenate([zero_h, sin, zero_r], axis=1)
    rep = 128 // width
    return tuple(jnp.tile(t, (1, rep)) for t in (c, s_up, s_dn)), half


def _rope_tile(x, c, s_up, s_dn, half):
    return x * c + pltpu.roll(x, 128 - half, 1) * s_up + pltpu.roll(x, half, 1) * s_dn


def _split_rope_kernel(hm_ref, hc_ref, ca_ref, ua_ref, da_ref, cb_ref, ub_ref, db_ref,
                       dq_o, dk_o, dv_o, sq_o, sk_o, sv_o, iq_o, ik_o, *, shift, ik_shift, half_a, half_b):
    ta = (ca_ref[...], ua_ref[...], da_ref[...])
    tb = (cb_ref[...], ub_ref[...], db_ref[...])
    lane = lax.broadcasted_iota(jnp.int32, ca_ref.shape, 1)

    def shifted(k):
        return pltpu.roll(hm_ref[:, 128 * k:128 * (k + 1)].astype(F32), 128 - shift, 1)

    m = 0
    nxt = shifted(0)
    for dst, tab, half in ((dq_o, ta, half_a), (dk_o, ta, half_a), (dv_o, None, 0), (sq_o, tb, half_b),
                           (sk_o, tb, half_b), (sv_o, None, 0), (iq_o, ta, half_a)):
        for t in range(dst.shape[1] // 128):
            cur, nxt = nxt, shifted(m + 1)
            y = jnp.where(lane < 128 - shift, cur, nxt)
            if tab is not None:
                y = _rope_tile(y, *tab, half)
            dst[:, 128 * t:128 * (t + 1)] = y.astype(BF16)
            m += 1
    ik = _rope_tile(pltpu.roll(hc_ref[...], 128 - ik_shift, 1), *ta, half_a)
    ik_o[...] = jnp.where(lane < IDX_DIM, ik, pltpu.roll(ik, IDX_DIM, 1)).astype(BF16)


def _split_rope(hm, hs, tab_a, half_a, tab_b, half_b, *, tm=256):
    n, wm = hm.shape
    tile = pl.BlockSpec((tm, 128), lambda i: (i, 0))
    widths = (W_DIFF, W_DIFF, W_DIFF, W_SQ, W_SKV, W_SKV, W_IQ, 128)
    return pl.pallas_call(
        functools.partial(_split_rope_kernel, shift=OFF_DQ - HM_BASE, ik_shift=OFF_IK % 128,
                          half_a=half_a, half_b=half_b),
        out_shape=tuple(jax.ShapeDtypeStruct((n, w), BF16) for w in widths),
        grid=(n // tm,),
        in_specs=[pl.BlockSpec((tm, wm), lambda i: (i, 0)), pl.BlockSpec((tm, 128), lambda i: (i, 1))] + [tile] * 6,
        out_specs=tuple(pl.BlockSpec((tm, w), lambda i: (i, 0)) for w in widths),
        compiler_params=_cparams(("parallel",)),
        name="split_rope",
    )(hm, hs, *tab_a, *tab_b)


def _gla_prep_kernel(q_ref, k_ref, glr_ref, wg_ref, bg_ref, qe_o, kd_o, s_o, eb_o, la_ref, *, chunks):
    C, SUB, DK = GLA_CHUNK, GLA_SUB, GLA_DK
    NT = (((1,), (1,)), ((), ()))

    row = lax.broadcasted_iota(jnp.int32, (C, C), 0)
    col = lax.broadcasted_iota(jnp.int32, (C, C), 1)
    tril = (col <= row).astype(F32)
    gi = lax.broadcasted_iota(jnp.int32, (SUB, SUB * SUB), 0)
    gr = lax.broadcasted_iota(jnp.int32, (SUB, SUB * SUB), 1)
    gsum = (gr // SUB == gi).astype(BF16)
    pr = lax.broadcasted_iota(jnp.int32, (SUB * SUB, C), 0)
    pc = lax.broadcasted_iota(jnp.int32, (SUB * SUB, C), 1)
    srow = lax.broadcasted_iota(jnp.int32, (SUB, C), 0)
    scol = lax.broadcasted_iota(jnp.int32, (SUB, C), 1)

    z = jnp.dot(glr_ref[...], wg_ref[...], precision=HIGHEST, preferred_element_type=F32) + bg_ref[...]
    la_all = -(jnp.maximum(-z, 0.0) + jnp.log1p(jnp.exp(-jnp.abs(z)))) * (1.0 / GLA_TAU)
    la_ref[...] = la_all
    sub_decay = -jnp.sum(la_all.reshape(chunks * C // SUB, SUB, la_all.shape[1]), axis=1)
    k_mag = jnp.max(jnp.abs(k_ref[...].astype(F32)))
    safe = jnp.logical_and(jnp.max(sub_decay) <= GLA_SAFE_DECAY, k_mag <= GLA_SAFE_K)

    def chunk_head(ci, hh, fast, b_pair):
        r0 = pl.multiple_of(ci * C, C)
        rows = pl.ds(r0, C)
        ks = slice(hh * DK, (hh + 1) * DK)
        q = q_ref[rows, ks].astype(F32) * (DK ** -0.5)
        k = k_ref[rows, ks].astype(F32)
        b = b_pair[:, ks]
        qe_o[rows, ks] = (q * jnp.exp(b)).astype(BF16)

        s_rows = []
        for blk in range(C // SUB):
            r0 = blk * SUB
            b_i, q_i, k_i = b[r0:r0 + SUB], q[r0:r0 + SUB], k[r0:r0 + SUB]
            beta = b[r0:r0 + 1]
            q_t = (q_i * jnp.exp(b_i - beta)).astype(BF16)
            if fast:
                k_t = (k * jnp.exp(jnp.minimum(beta - b, GLA_SAFE_DECAY))).astype(BF16)
                s_blk = lax.dot_general(q_t, k_t, NT, preferred_element_type=F32)
                s_rows.append(jnp.where(scol <= srow + r0, s_blk, 0.0))
                continue
            pair = (q_i[:, None, :] * k_i[None, :, :]
                    * jnp.exp(jnp.minimum(b_i[:, None, :] - b_i[None, :, :], 0.0)))
            pair_sum = jnp.sum(pair.reshape(SUB * SUB, DK), axis=-1, keepdims=True)
            placed = jnp.where(pc == r0 + pr % SUB, pair_sum, 0.0).astype(BF16)
            s_blk = jnp.dot(gsum, placed, preferred_element_type=F32)
            s_blk = jnp.where(scol <= srow + r0, s_blk, 0.0)
            if blk > 0:
                k_t = (k * jnp.exp(jnp.minimum(beta - b, 0.0))).astype(BF16)
                off = lax.dot_general(q_t, k_t, NT, preferred_element_type=F32)
                s_blk = jnp.where(scol < r0, off, s_blk)
            s_rows.append(s_blk)
        s_o[rows, hh * C:(hh + 1) * C] = jnp.concatenate(s_rows, axis=0).astype(BF16)
        b_last = b[C - 1:C]
        kd_o[rows, ks] = (k * jnp.exp(b_last - b)).astype(BF16)
        eb_o[pl.ds(ci, 1), ks] = jnp.exp(b_last)

    def all_chunks(fast):
        per = 4 if fast else 2

        def some_chunks(i, carry):
            for c in range(per):
                ci = per * i + c
                la = la_ref[pl.ds(pl.multiple_of(ci * C, C), C), :]
                b_pair = jnp.dot(tril, la, precision=HIGHEST, preferred_element_type=F32)
                for hh in range(2):
                    chunk_head(ci, hh, fast, b_pair)
            return carry
        lax.fori_loop(0, chunks // per, some_chunks, 0)

    @pl.when(safe)
    def _():
        all_chunks(True)

    @pl.when(jnp.logical_not(safe))
    def _():
        all_chunks(False)


def _gla_scan_kernel(qe_ref, kd_ref, s_ref, eb_ref, v_ref, r_ref, ng_ref, mix_ref, o_ref, state_ref):
    del mix_ref
    C, DK, DV = GLA_CHUNK, GLA_DK, GLA_DV
    NT = (((1,), (1,)), ((), ()))
    TN = (((0,), (0,)), ((), ()))

    @pl.when(pl.program_id(0) == 0)
    def _():
        state_ref[...] = jnp.zeros_like(state_ref)

    for b in range(state_ref.shape[0]):
        for h in range(GLA_HEADS):
            ks = slice(h * DK, (h + 1) * DK)
            vs = slice(h * DV, (h + 1) * DV)
            st = state_ref[b, h]
            v = v_ref[b, :, vs]
            o = (lax.dot_general(qe_ref[b, :, ks], st.astype(BF16), NT, preferred_element_type=F32)
                 + jnp.dot(s_ref[b, :, h * C:(h + 1) * C], v, preferred_element_type=F32))
            state_ref[b, h] = st * eb_ref[b:b + 1, ks] + lax.dot_general(v, kd_ref[b, :, ks], TN,
                                                                         preferred_element_type=F32)
            ms = jnp.mean(o * o, axis=-1, keepdims=True)
            o_n = o * lax.rsqrt(ms + LN_EPS) * ng_ref[...]
            r = r_ref[b, :, vs].astype(F32)
            o_ref[b, :, vs] = (o_n * (r * jax.nn.sigmoid(r))).astype(BF16)


def _gla(ha, glr, w_gate_up, b_gate, gla_norm_g, mix, lead, *, batch, seq, chunks=8):
    C = GLA_CHUNK
    n = batch * seq
    nc = seq // C
    hp = GLA_HEADS // 2
    wq = 2 * GLA_DK
    wqk, wv = GLA_HEADS * GLA_DK, GLA_HEADS * GLA_DV
    tr = chunks * C

    def rows(width, first):
        return pl.BlockSpec((tr, width), lambda i, p: (i, first + p))

    qe, kd, sc, eb = pl.pallas_call(
        functools.partial(_gla_prep_kernel, chunks=chunks),
        out_shape=(jax.ShapeDtypeStruct((n, wqk), BF16), jax.ShapeDtypeStruct((n, wqk), BF16),
                   jax.ShapeDtypeStruct((n, GLA_HEADS * C), BF16), jax.ShapeDtypeStruct((n // C, wqk), F32)),
        grid=(n // tr, hp),
        in_specs=[rows(wq, 0), rows(wq, hp),
                  pl.BlockSpec((tr, GLA_RANK), lambda i, p: (i, 0)),
                  pl.BlockSpec((None, GLA_RANK, wq), lambda i, p: (lead, 0, p)),
                  pl.BlockSpec((None, 1, wq), lambda i, p: (lead, 0, p))],
        out_specs=(rows(wq, 0), rows(wq, 0), rows(2 * C, 0),
                   pl.BlockSpec((chunks, wq), lambda i, p: (i, p))),
        scratch_shapes=[pltpu.VMEM((tr, wq), F32)],
        compiler_params=_cparams(("parallel", "parallel")),
        name="gla_prep",
    )(ha, ha, glr, w_gate_up, b_gate.reshape(-1, 1, wqk))

    def per_chunk(width, first):
        return pl.BlockSpec((batch, C, width), lambda c: (0, c, first))

    eb = eb.reshape(batch, nc, wqk).transpose(1, 0, 2)
    ha3 = ha.reshape(batch, seq, -1)
    out = pl.pallas_call(
        _gla_scan_kernel,
        out_shape=jax.ShapeDtypeStruct((batch, seq, mix.shape[1]), mix.dtype),
        grid=(nc,),
        in_specs=[per_chunk(wqk, 0), per_chunk(wqk, 0), per_chunk(GLA_HEADS * C, 0),
                  pl.BlockSpec((None, batch, wqk), lambda c: (c, 0, 0)),
                  per_chunk(wv, 1), per_chunk(wv, 2),
                  pl.BlockSpec((None, 1, GLA_DV), lambda c: (lead, 0, 0)),
                  pl.BlockSpec(memory_space=pl.ANY)],
        out_specs=per_chunk(wv, MIX_GLA // wv),
        scratch_shapes=[pltpu.VMEM((batch, GLA_HEADS, GLA_DV, GLA_DK), F32)],
        input_output_aliases={7: 0},
        compiler_params=_cparams(("arbitrary",)),
        name="gla_scan",
    )(qe.reshape(batch, seq, wqk), kd.reshape(batch, seq, wqk), sc.reshape(batch, seq, GLA_HEADS * C), eb,
      ha3, ha3, gla_norm_g.reshape(-1, 1, GLA_DV), mix.reshape(batch, seq, -1))
    return out.reshape(mix.shape)


def _diff_kernel(lq1_ref, lk1_ref, lq2_ref, lk2_ref, q_ref, k_ref, vt_ref, g_ref, mix_ref, o_ref,
                 qs_ref, m_ref, l_ref, acc_ref, ksq_ref, *, tq, lambda_init):
    del mix_ref
    qi = pl.program_id(2)
    NT = (((1,), (1,)), ((), ()))
    hw = 2 * DIFF_D
    heads = q_ref.shape[1] // hw

    lane = lax.broadcasted_iota(jnp.int32, (tq, hw), 1)
    logit_bound_sq = F32(0.0)
    for hh in range(heads):
        hs = slice(hw * hh, hw * (hh + 1))
        q = q_ref[:, hs].astype(F32) * (DIFF_D ** -0.5)
        qs_ref[hh, :tq] = jnp.where(lane < DIFF_D, q, 0.0).astype(BF16)
        qs_ref[hh, tq:] = jnp.where(lane >= DIFF_D, q, 0.0).astype(BF16)

        @pl.when(qi == 0)
        def _():
            k = k_ref[:, hs].astype(F32)
            ksq_ref[hh] = jnp.max(jnp.sum(k * k, axis=-1, keepdims=True))

        logit_bound_sq = jnp.maximum(logit_bound_sq,
                                     jnp.max(jnp.sum(q * q, axis=-1, keepdims=True)) * ksq_ref[hh])
    m_ref[...] = jnp.full_like(m_ref, -jnp.inf)
    l_ref[...] = jnp.zeros_like(l_ref)
    acc_ref[...] = jnp.zeros_like(acc_ref)
    bounded = logit_bound_sq <= SOFTMAX_SAFE_LOGIT ** 2

    def tile(j, masked, use_max):
        r0 = pl.multiple_of(j * tq, tq)
        for hh in range(heads):
            hs = slice(hw * hh, hw * (hh + 1))
            s = lax.dot_general(k_ref[pl.ds(r0, tq), hs], qs_ref[hh], NT, preferred_element_type=F32)
            if masked:
                r = lax.broadcasted_iota(jnp.int32, s.shape, 0)
                c = lax.broadcasted_iota(jnp.int32, s.shape, 1)
                s = jnp.where(r <= jnp.where(c >= tq, c - tq, c), s, -jnp.inf)
            if use_max:
                m_prev = m_ref[hh]
                m_new = jnp.maximum(m_prev, jnp.max(s, axis=0, keepdims=True))
                alpha = jnp.exp(m_prev - m_new)
                p = jnp.exp(s - m_new)
                l_ref[hh] = alpha * l_ref[hh] + jnp.sum(p, axis=0, keepdims=True)
                acc_ref[hh] = alpha * acc_ref[hh] + jnp.dot(vt_ref[hs, pl.ds(r0, tq)], p.astype(BF16),
                                                          preferred_element_type=F32)
                m_ref[hh] = m_new
            else:
                p = jnp.exp(s)
                l_ref[hh] += jnp.sum(p, axis=0, keepdims=True)
                acc_ref[hh] += jnp.dot(vt_ref[hs, pl.ds(r0, tq)], p.astype(BF16), preferred_element_type=F32)

    def all_tiles(use_max):
        def full_tile(j, carry):
            tile(j, False, use_max)
            return carry
        lax.fori_loop(0, qi, full_tile, 0)
        tile(qi, True, use_max)

    @pl.when(bounded)
    def _():
        all_tiles(False)

    @pl.when(jnp.logical_not(bounded))
    def _():
        all_tiles(True)

    lam = (jnp.exp(jnp.sum(lq1_ref[...] * lk1_ref[...], axis=-1, keepdims=True))
           - jnp.exp(jnp.sum(lq2_ref[...] * lk2_ref[...], axis=-1, keepdims=True)) + lambda_init)
    for hh in range(heads):
        o = acc_ref[hh] / l_ref[hh]
        o = o[:, :tq] - lam * o[:, tq:]
        ms = jnp.mean(o * o, axis=0, keepdims=True)
        o = o * lax.rsqrt(ms + LN_EPS) * g_ref[...] * (1.0 - lambda_init)
        o_ref[:, hw * hh:hw * (hh + 1)] = o.T.astype(BF16)


def _diff_attention(dq, dk, dvt, lq1, lk1, lq2, lk2, norm_g, mix, lead, *, batch, seq, lambda_init, tq, heads=2):
    nq = seq // tq
    hw = 2 * DIFF_D
    pw = heads * hw
    lam_spec = pl.BlockSpec((None, 1, DIFF_D), lambda b, p, qi: (lead, 0, 0))
    lq1, lk1, lq2, lk2 = (t.reshape(-1, 1, DIFF_D) for t in (lq1, lk1, lq2, lk2))
    return pl.pallas_call(
        functools.partial(_diff_kernel, tq=tq, lambda_init=lambda_init),
        out_shape=jax.ShapeDtypeStruct(mix.shape, mix.dtype),
        grid=(batch, DIFF_HEADS // heads, nq),
        in_specs=[lam_spec] * 4 + [
            pl.BlockSpec((tq, pw), lambda b, p, qi: (b * nq + qi, p)),
            pl.BlockSpec((seq, pw), lambda b, p, qi: (b, p)),
            pl.BlockSpec((None, pw, seq), lambda b, p, qi: (b, p, 0)),
            pl.BlockSpec((None, hw, 1), lambda b, p, qi: (lead, 0, 0)),
            pl.BlockSpec(memory_space=pl.ANY)],
        out_specs=pl.BlockSpec((tq, pw), lambda b, p, qi: (b * nq + qi, MIX_DIFF // pw + p)),
        scratch_shapes=[pltpu.VMEM((heads, 2 * tq, hw), BF16), pltpu.VMEM((heads, 1, 2 * tq), F32),
                        pltpu.VMEM((heads, 1, 2 * tq), F32), pltpu.VMEM((heads, hw, 2 * tq), F32),
                        pltpu.SMEM((heads,), F32)],
        input_output_aliases={8: 0},
        compiler_params=_cparams(("parallel", "parallel", "arbitrary")),
        name="diff_attn",
    )(lq1, lk1, lq2, lk2, dq, dk, dvt, norm_g.reshape(-1, hw, 1), mix)


def _dsa_kernel(iq_ref, sq_ref, ikd_ref, wt_ref, sk_ref, svt_ref, mix_ref, o_ref,
                iqm_ref, keys_ref, bias_ref, q5_ref, m_ref, l_ref, acc_ref, ksq_ref, hi_ref, lo_ref,
                *, tq, tk, seq, k_sel):
    del mix_ref
    qi = pl.program_id(1)
    NT = (((1,), (1,)), ((), ()))
    n_tiles = (qi * tq + tq + tk - 1) // tk
    qpos = qi * tq + lax.broadcasted_iota(jnp.int32, (1, tq), 1)
    krow = lax.broadcasted_iota(jnp.int32, (tk, tq), 0)
    idx_scale = (IDX_DIM ** -0.5) * (IDX_HEADS ** -0.5)
    neg_inf = F32(-jnp.inf)

    def tile_start(j):
        return pl.multiple_of(j * tk, tk)

    lane = lax.broadcasted_iota(jnp.int32, (tq, 128), 1)
    for t in range(IDX_HEADS // 2):
        a = iq_ref[:, 128 * t:128 * (t + 1)]
        zero = jnp.zeros_like(a)
        iqm_ref[t, :tq] = jnp.where(lane < IDX_DIM, a, zero)
        iqm_ref[t, tq:] = jnp.where(lane >= IDX_DIM, a, zero)
    wt = wt_ref[...]

    def score_tile(j, carry):
        r0 = tile_start(j)
        kd = ikd_ref[pl.ds(r0, tk), :]
        acc = jnp.zeros((tk, 2 * tq), F32)
        for t in range(IDX_HEADS // 2):
            lg = lax.dot_general(kd, iqm_ref[t], NT, preferred_element_type=F32)
            w2 = jnp.concatenate([wt[2 * t:2 * t + 1], wt[2 * t + 1:2 * t + 2]], axis=1)
            acc = acc + jnp.maximum(lg, 0.0) * w2
        acc = acc[:, :tq] + acc[:, tq:]
        bits = pltpu.bitcast(acc * idx_scale, jnp.int32)
        key = jnp.where(bits < 0, (bits ^ 0x7FFFFFFF) + 1, bits)
        key = jnp.where(r0 + krow <= qpos, key, INT_MIN)
        keys_ref[pl.ds(r0, tk), :] = key
        hi_ref[pl.ds(r0, tk), :] = (key >> 16).astype(jnp.int16)
        return carry

    lax.fori_loop(0, n_tiles, score_tile, 0)

    def count_ones(ones):
        def body(j, cnt):
            r0 = tile_start(j)
            return cnt + jnp.sum(ones(keys_ref[pl.ds(r0, tk), :], r0 + krow), axis=0, keepdims=True)
        return lax.fori_loop(0, n_tiles, body, jnp.zeros((1, tq), jnp.int32))

    def count(pred):
        return count_ones(lambda kk, kp: jnp.where(pred(kk, kp), 1, 0))

    def count16(ref, pred):
        def body(j, cnt):
            ones = jnp.where(pred(ref[pl.ds(tile_start(j), tk), :]), jnp.int16(1), jnp.int16(0))
            ones = ones.reshape(tk // 16, 16, tq)
            part = ones[0]
            for r in range(1, tk // 16):
                part = part + ones[r]
            return cnt + jnp.sum(part.astype(jnp.int32), axis=0, keepdims=True)
        return lax.fori_loop(0, n_tiles, body, jnp.zeros((1, tq), jnp.int32))

    def as_ordered_i16(u):
        return jnp.right_shift(jnp.left_shift(u ^ 0x8000, 16), 16).astype(jnp.int16)

    def search16(ref, count_above):
        def bit(i, t_u):
            cand_u = t_u | jnp.left_shift(jnp.int32(1), 15 - i)
            cand = as_ordered_i16(cand_u)
            c = count_above + count16(ref, lambda v: v >= cand)
            return jnp.where(c >= k_sel, cand_u, t_u)
        return lax.fori_loop(0, 16, bit, jnp.zeros((1, tq), jnp.int32))

    thr_hi = as_ordered_i16(search16(hi_ref, 0))
    above = count16(hi_ref, lambda v: v > thr_hi)

    def low_half_tile(j, carry):
        rows = pl.ds(tile_start(j), tk)
        lo = as_ordered_i16(keys_ref[rows, :] & 0xFFFF)
        lo_ref[rows, :] = jnp.where(hi_ref[rows, :] == thr_hi, lo, jnp.int16(-32768))
        return carry

    lax.fori_loop(0, n_tiles, low_half_tile, 0)
    thr = jnp.left_shift(thr_hi.astype(jnp.int32), 16) | search16(lo_ref, above)
    need = k_sel - count(lambda kk, kp: kk > thr)
    n_eq = count(lambda kk, kp: kk == thr)
    excess = jnp.where(thr == INT_MIN, 0, jnp.where(n_eq > need, 1, 0))

    def tie_search():
        def idx_bit(i, j_lo):
            cand = j_lo | jnp.left_shift(jnp.int32(1), (seq - 1).bit_length() - 1 - i)
            c = count_ones(lambda kk, kp: jnp.where(kk == thr, jnp.where(kp < cand, 1, 0), 0))
            return jnp.where(c < need, cand, j_lo)
        return lax.fori_loop(0, (seq - 1).bit_length(), idx_bit, jnp.zeros((1, tq), jnp.int32))

    j_star = lax.cond(jnp.max(excess) > 0, tie_search, lambda: jnp.full((1, tq), seq, jnp.int32))
    j_star = jnp.where(thr == INT_MIN, -1, jnp.where(excess > 0, j_star, seq))

    def bias_tile(j, carry):
        r0 = tile_start(j)
        kk = keys_ref[pl.ds(r0, tk), :]
        tie = jnp.where(r0 + krow <= j_star, 0.0, neg_inf)
        bias_ref[pl.ds(r0, tk), :] = jnp.where(kk > thr, 0.0, jnp.where(kk == thr, tie, neg_inf))
        return carry

    lax.fori_loop(0, n_tiles, bias_tile, 0)

    scale = DSA_DH ** -0.5
    rep = DSA_HEADS // DSA_KV
    logit_bound_sq = F32(0.0)
    for g in range(DSA_KV):
        q_sq = F32(0.0)
        for i in range(rep):
            h = g * rep + i
            qh = sq_ref[:, DSA_DH * h:DSA_DH * (h + 1)]
            q5_ref[g, tq * i:tq * (i + 1)] = qh
            qf = qh.astype(F32)
            q_sq = jnp.maximum(q_sq, jnp.max(jnp.sum(qf * qf, axis=-1, keepdims=True)))

        @pl.when(qi == 0)
        def _():
            kf = sk_ref[:, DSA_DH * g:DSA_DH * (g + 1)].astype(F32)
            ksq_ref[g] = jnp.max(jnp.sum(kf * kf, axis=-1, keepdims=True))

        logit_bound_sq = jnp.maximum(logit_bound_sq, q_sq * ksq_ref[g] * (scale * scale))
    m_ref[...] = jnp.full_like(m_ref, neg_inf)
    l_ref[...] = jnp.zeros_like(l_ref)
    acc_ref[...] = jnp.zeros_like(acc_ref)
    bounded = logit_bound_sq <= SOFTMAX_SAFE_LOGIT ** 2

    def attn_tiles(use_max):
        def attn_tile(j, carry):
            r0 = tile_start(j)
            bias = bias_ref[pl.ds(r0, tk), :]
            bias = jnp.concatenate([bias] * rep, axis=1)
            for g in range(DSA_KV):
                gs = slice(DSA_DH * g, DSA_DH * (g + 1))
                s = lax.dot_general(sk_ref[pl.ds(r0, tk), gs], q5_ref[g], NT, preferred_element_type=F32)
                s = s * scale + bias
                vt = svt_ref[gs, pl.ds(r0, tk)]
                if use_max:
                    m_prev = m_ref[g]
                    m_new = jnp.maximum(m_prev, jnp.max(s, axis=0, keepdims=True))
                    m_safe = jnp.where(m_new == neg_inf, 0.0, m_new)
                    alpha = jnp.exp(m_prev - m_safe)
                    p = jnp.exp(s - m_safe)
                    l_ref[g] = alpha * l_ref[g] + jnp.sum(p, axis=0, keepdims=True)
                    acc_ref[g] = alpha * acc_ref[g] + jnp.dot(vt, p.astype(BF16), preferred_element_type=F32)
                    m_ref[g] = m_new
                else:
                    p = jnp.exp(s)
                    l_ref[g] += jnp.sum(p, axis=0, keepdims=True)
                    acc_ref[g] += jnp.dot(vt, p.astype(BF16), preferred_element_type=F32)
            return carry

        lax.fori_loop(0, n_tiles, attn_tile, 0)

    @pl.when(bounded)
    def _():
        attn_tiles(False)

    @pl.when(jnp.logical_not(bounded))
    def _():
        attn_tiles(True)
    for g in range(DSA_KV):
        out_t = acc_ref[g] / l_ref[g]
        for i in range(rep):
            h = g * rep + i
            o_ref[:, DSA_DH * h:DSA_DH * (h + 1)] = out_t[:, tq * i:tq * (i + 1)].T.astype(BF16)


def _dsa(iq, sq, ikd, wt, sk, svt, mix, *, batch, seq, tq, tk):
    nq = seq // tq
    k_sel = min(IDX_TOPK, seq // 4)
    rep = DSA_HEADS // DSA_KV
    return pl.pallas_call(
        functools.partial(_dsa_kernel, tq=tq, tk=tk, seq=seq, k_sel=k_sel),
        out_shape=jax.ShapeDtypeStruct(mix.shape, mix.dtype),
        grid=(batch, nq),
        in_specs=[pl.BlockSpec((tq, W_IQ), lambda b, qi: (b * nq + qi, 0)),
                  pl.BlockSpec((tq, W_SQ), lambda b, qi: (b * nq + qi, 0)),
                  pl.BlockSpec((seq, 128), lambda b, qi: (b, 0)),
                  pl.BlockSpec((None, IDX_HEADS, tq), lambda b, qi: (b, 0, qi)),
                  pl.BlockSpec((seq, W_SKV), lambda b, qi: (b, 0)),
                  pl.BlockSpec((None, W_SKV, seq), lambda b, qi: (b, 0, 0)),
                  pl.BlockSpec(memory_space=pl.ANY)],
        out_specs=pl.BlockSpec((tq, W_SQ), lambda b, qi: (b * nq + qi, MIX_DSA // W_SQ)),
        scratch_shapes=[pltpu.VMEM((IDX_HEADS // 2, 2 * tq, 128), BF16), pltpu.VMEM((seq, tq), jnp.int32),
                        pltpu.VMEM((seq, tq), F32), pltpu.VMEM((DSA_KV, rep * tq, DSA_DH), BF16),
                        pltpu.VMEM((DSA_KV, 1, rep * tq), F32), pltpu.VMEM((DSA_KV, 1, rep * tq), F32),
                        pltpu.VMEM((DSA_KV, DSA_DH, rep * tq), F32), pltpu.SMEM((DSA_KV,), F32),
                        pltpu.VMEM((seq, tq), jnp.int16), pltpu.VMEM((seq, tq), jnp.int16)],
        input_output_aliases={6: 0},
        compiler_params=_cparams(("parallel", "arbitrary")),
        name="dsa",
    )(iq, sq, ikd, wt, sk, svt, mix)


def _router_kernel(x_ref, w_ref, b_ref, e_ref, g_ref, xp_ref):
    x = x_ref[...]
    logits = jnp.dot(x, w_ref[...], precision=HIGHEST, preferred_element_type=F32) + b_ref[...]
    lane = lax.broadcasted_iota(jnp.int32, logits.shape, 1)
    neg_inf = F32(-jnp.inf)
    lg = jnp.where(lane < MOE_E, logits, neg_inf)
    m1 = jnp.max(lg, axis=-1, keepdims=True)
    i1 = jnp.min(jnp.where(lg == m1, lane, 128), axis=-1, keepdims=True)
    lg2 = jnp.where(lane == i1, neg_inf, lg)
    m2 = jnp.max(lg2, axis=-1, keepdims=True)
    i2 = jnp.min(jnp.where(lg2 == m2, lane, 128), axis=-1, keepdims=True)
    e21 = jnp.exp(m2 - m1)
    g1 = 1.0 / (1.0 + e21)
    e_ref[...] = jnp.where(lane == 0, i1, jnp.where(lane == 1, i2, 0))
    g_ref[...] = jnp.where(lane == 0, g1, jnp.where(lane == 1, e21 * g1, 0.0))
    half = x.shape[1] // 2
    lo = pltpu.bitcast(x[:, :half].astype(BF16).astype(F32), jnp.uint32)
    hi = pltpu.bitcast(x[:, half:].astype(BF16).astype(F32), jnp.uint32)
    xp_ref[...] = (hi & jnp.uint32(0xFFFF0000)) | (lo >> 16)


def _router(x, router_w, router_b, *, tm=256):
    n, d = x.shape
    row = lambda w: pl.BlockSpec((tm, w), lambda i: (i, 0))
    return pl.pallas_call(
        _router_kernel,
        out_shape=(jax.ShapeDtypeStruct((n, 128), jnp.int32), jax.ShapeDtypeStruct((n, 128), F32),
                   jax.ShapeDtypeStruct((n, d // 2), jnp.uint32)),
        grid=(n // tm,),
        in_specs=[row(d), pl.BlockSpec((d, 128), lambda i: (0, 0)), pl.BlockSpec((1, 128), lambda i: (0, 0))],
        out_specs=(row(128), row(128), row(d // 2)),
        compiler_params=_cparams(("parallel",)),
        name="router",
    )(x, router_w, router_b)


def _dispatch_kernel(pos_ref, xp_ref, buf_ref, o_ref, sem, *, tb):
    del buf_ref
    base = pl.program_id(0) * tb

    def row_copy(i, s):
        return pltpu.make_async_copy(xp_ref.at[pl.ds(i, 1)], o_ref.at[pl.ds(pos_ref[2 * (base + i) + s], 1)], sem)

    def start(i, c):
        row_copy(i, 0).start()
        row_copy(i, 1).start()
        return c

    def wait(i, c):
        row_copy(i, 0).wait()
        row_copy(i, 1).wait()
        return c

    lax.fori_loop(0, tb, start, 0)
    lax.fori_loop(0, tb, wait, 0)


def _dispatch(pos, xp, buf, *, tb=256):
    n, w = xp.shape
    return pl.pallas_call(
        functools.partial(_dispatch_kernel, tb=tb),
        out_shape=jax.ShapeDtypeStruct(buf.shape, buf.dtype),
        grid_spec=pltpu.PrefetchScalarGridSpec(
            num_scalar_prefetch=1,
            grid=(n // tb,),
            in_specs=[pl.BlockSpec((tb, w), lambda i, pos: (i, 0)), pl.BlockSpec(memory_space=pl.ANY)],
            out_specs=pl.BlockSpec(memory_space=pl.ANY),
            scratch_shapes=[pltpu.SemaphoreType.DMA],
        ),
        input_output_aliases={2: 0},
        compiler_params=_cparams(("arbitrary",)),
        name="moe_dispatch",
    )(pos, xp, buf)


def _unpack_rows(word):
    lo = pltpu.bitcast(word << 16, F32).astype(BF16)
    hi = pltpu.bitcast(word & jnp.uint32(0xFFFF0000), F32).astype(BF16)
    return lo, hi


def _expert_tile_refresh(be_ref, bf_ref, nxt_ref, w_hbms, stage_ref, wb_refs, sem, *, lead, tn):
    n, i = pl.program_id(0), pl.program_id(1)

    def fetch(e, col_tile):
        cols = pl.ds(pl.multiple_of(col_tile * tn, tn), tn)
        return [pltpu.make_async_copy(w.at[lead, e, :, cols], stage_ref.at[t], sem)
                for t, w in enumerate(w_hbms)]

    @pl.when(bf_ref[i] == 1)
    def _():
        @pl.when(jnp.logical_and(n == 0, i == 0))
        def _():
            for c in fetch(be_ref[0], 0):
                c.start()

        for c in fetch(be_ref[i], n):
            c.wait()
        for t, wb in enumerate(wb_refs):
            wb[...] = stage_ref[t].astype(BF16)
        last_run = nxt_ref[i] < 0
        e_next = jnp.where(last_run, be_ref[0], nxt_ref[i])
        n_next = jnp.where(last_run, n + 1, n)

        @pl.when(n_next < pl.num_programs(0))
        def _():
            for c in fetch(e_next, n_next):
                c.start()


def _moe_up_kernel(be_ref, bf_ref, na_ref, nxt_ref, a_ref, w1_hbm, w3_hbm, o_ref,
                   stage_ref, w1b_ref, w3b_ref, sem, *, lead, tn):
    i = pl.program_id(1)

    @pl.when(i < na_ref[0])
    def _():
        _expert_tile_refresh(be_ref, bf_ref, nxt_ref, (w1_hbm, w3_hbm), stage_ref, (w1b_ref, w3b_ref), sem,
                             lead=lead, tn=tn)
        lo, hi = _unpack_rows(a_ref[...])
        half = lo.shape[1]
        g = (jnp.dot(lo, w1b_ref[:half], preferred_element_type=F32)
             + jnp.dot(hi, w1b_ref[half:], preferred_element_type=F32))
        u = (jnp.dot(lo, w3b_ref[:half], preferred_element_type=F32)
             + jnp.dot(hi, w3b_ref[half:], preferred_element_type=F32))
        o_ref[...] = (g * jax.nn.sigmoid(g) * u).astype(BF16)

    @pl.when(i >= na_ref[0])
    def _():
        o_ref[...] = jnp.zeros_like(o_ref)


def _moe_down_kernel(be_ref, bf_ref, na_ref, nxt_ref, h_ref, w2_hbm, o_ref, stage_ref, w2b_ref, sem, *, lead, tn):
    i = pl.program_id(1)

    @pl.when(i < na_ref[0])
    def _():
        _expert_tile_refresh(be_ref, bf_ref, nxt_ref, (w2_hbm,), stage_ref, (w2b_ref,), sem, lead=lead, tn=tn)
        o_ref[...] = jnp.dot(h_ref[...], w2b_ref[...], preferred_element_type=F32)

    @pl.when(i >= na_ref[0])
    def _():
        o_ref[...] = jnp.zeros_like(o_ref)


def _moe_ffn(block_e, block_first, n_active, next_e, a_sorted, w1, w3, w2, lead, *, tm, tn):
    r, half = a_sorted.shape
    d = 2 * half
    f = w1.shape[-1]
    nb = r // tm
    hbm = pl.BlockSpec(memory_space=pl.ANY)

    def rows(i, na):
        return jnp.minimum(i, na[0] - 1)

    h = pl.pallas_call(
        functools.partial(_moe_up_kernel, lead=lead, tn=tn),
        out_shape=jax.ShapeDtypeStruct((r, f), BF16),
        grid_spec=pltpu.PrefetchScalarGridSpec(
            num_scalar_prefetch=4,
            grid=(f // tn, nb),
            in_specs=[pl.BlockSpec((tm, half), lambda n, i, be, bf, na, nx: (rows(i, na), 0)), hbm, hbm],
            out_specs=pl.BlockSpec((tm, tn), lambda n, i, be, bf, na, nx: (i, n)),
            scratch_shapes=[pltpu.VMEM((2, d, tn), F32), pltpu.VMEM((d, tn), BF16), pltpu.VMEM((d, tn), BF16),
                            pltpu.SemaphoreType.DMA],
        ),
        compiler_params=_cparams(("arbitrary", "arbitrary")),
        name="moe_up",
    )(block_e, block_first, n_active, next_e, a_sorted, w1, w3)
    return pl.pallas_call(
        functools.partial(_moe_down_kernel, lead=lead, tn=tn),
        out_shape=jax.ShapeDtypeStruct((r, d), F32),
        grid_spec=pltpu.PrefetchScalarGridSpec(
            num_scalar_prefetch=4,
            grid=(d // tn, nb),
            in_specs=[pl.BlockSpec((tm, f), lambda n, i, be, bf, na, nx: (rows(i, na), 0)), hbm],
            out_specs=pl.BlockSpec((tm, tn), lambda n, i, be, bf, na, nx: (i, n)),
            scratch_shapes=[pltpu.VMEM((1, f, tn), F32), pltpu.VMEM((f, tn), BF16), pltpu.SemaphoreType.DMA],
        ),
        compiler_params=_cparams(("arbitrary", "arbitrary")),
        name="moe_down",
    )(block_e, block_first, n_active, next_e, h, w2)


def _combine_ln_kernel(pos_ref, x_ref, gate_ref, g_ref, b_ref, y_ref, o_ref, ybuf_ref, sem, *, tb, alpha):
    step, n_steps = pl.program_id(0), pl.num_programs(0)
    slot = step % 2

    def row_copy(blk, buf, i, s):
        return pltpu.make_async_copy(y_ref.at[pl.ds(pos_ref[2 * (blk * tb + i) + s], 1)],
                                     ybuf_ref.at[buf, s, pl.ds(i, 1)], sem.at[buf])

    def start_block(blk, buf):
        def body(i, c):
            row_copy(blk, buf, i, 0).start()
            row_copy(blk, buf, i, 1).start()
            return c
        lax.fori_loop(0, tb, body, 0)

    def wait_block(blk, buf):
        def body(i, c):
            row_copy(blk, buf, i, 0).wait()
            row_copy(blk, buf, i, 1).wait()
            return c
        lax.fori_loop(0, tb, body, 0)

    @pl.when(step == 0)
    def _():
        start_block(0, 0)

    @pl.when(step + 1 < n_steps)
    def _():
        start_block(step + 1, 1 - slot)

    wait_block(step, slot)
    gate = gate_ref[...]
    f = ybuf_ref[slot, 0] * gate[:, 0:1] + ybuf_ref[slot, 1] * gate[:, 1:2]
    z = alpha * x_ref[...] + f
    mu = jnp.mean(z, axis=-1, keepdims=True)
    zc = z - mu
    var = jnp.mean(zc * zc, axis=-1, keepdims=True)
    o_ref[...] = zc * lax.rsqrt(var + LN_EPS) * g_ref[...] + b_ref[...]


def _combine_ln(pos, x, gates, g, b, y, lead, *, alpha, tb=128):
    n, d = x.shape
    par = pl.BlockSpec((None, 1, d), lambda i, pos: (lead, 0, 0))
    return pl.pallas_call(
        functools.partial(_combine_ln_kernel, tb=tb, alpha=alpha),
        out_shape=jax.ShapeDtypeStruct((n, d), F32),
        grid_spec=pltpu.PrefetchScalarGridSpec(
            num_scalar_prefetch=1,
            grid=(n // tb,),
            in_specs=[pl.BlockSpec((tb, d), lambda i, pos: (i, 0)),
                      pl.BlockSpec((tb, 128), lambda i, pos: (i, 0)),
                      par, par, pl.BlockSpec(memory_space=pl.ANY)],
            out_specs=pl.BlockSpec((tb, d), lambda i, pos: (i, 0)),
            scratch_shapes=[pltpu.VMEM((2, 2, tb, d), F32), pltpu.SemaphoreType.DMA((2,))],
        ),
        compiler_params=_cparams(("arbitrary",)),
        name="moe_combine_ln",
    )(pos, x, gates, g.reshape(-1, 1, d), b.reshape(-1, 1, d), y)


def _moe_layer(x, router_w, router_b, w1, w3, w2, ln_g, ln_b, lead_moe, lead_ln, *, alpha, tm=512, tn=512):
    n, d = x.shape
    rw = jnp.pad(router_w[lead_moe], ((0, 0), (0, 128 - MOE_E)))
    rb = jnp.pad(router_b[lead_moe], (0, 128 - MOE_E)).reshape(1, 128)
    top_e, gates, xp = _router(x, rw, rb)
    flat_e = top_e[:, :MOE_TOPK].reshape(-1)
    onehot = (flat_e[:, None] == jnp.arange(MOE_E, dtype=jnp.int32)[None, :]).astype(jnp.int32)
    rank = jnp.sum((jnp.cumsum(onehot, axis=0) - onehot) * onehot, axis=1)
    counts = jnp.sum(onehot, axis=0)
    padded = (counts + tm - 1) // tm * tm
    end_padded = jnp.cumsum(padded)
    start_padded = end_padded - padded
    pos = (start_padded[flat_e] + rank).astype(jnp.int32)
    nb = -(-(n * MOE_TOPK + MOE_E * (tm - 1)) // tm)
    block_start = jnp.arange(nb, dtype=jnp.int32) * tm
    block_e = jnp.sum((block_start[:, None] >= end_padded[None, :]).astype(jnp.int32), axis=1)
    block_e = jnp.minimum(block_e, MOE_E - 1)
    n_active = (end_padded[-1] // tm).astype(jnp.int32).reshape(1)
    block_first = jnp.concatenate([jnp.ones((1,), jnp.int32), (block_e[1:] != block_e[:-1]).astype(jnp.int32)])
    blk = jnp.arange(nb, dtype=jnp.int32)
    run_start = (block_first == 1) & (blk < n_active[0])
    later_start = jnp.where((blk[None, :] > blk[:, None]) & run_start[None, :], blk[None, :], nb)
    next_start = jnp.min(later_start, axis=1)
    next_e = jnp.where(next_start < nb, block_e[jnp.minimum(next_start, nb - 1)], -1).astype(jnp.int32)
    a_sorted = _dispatch(pos, xp, jnp.zeros((nb * tm, d // 2), jnp.uint32))
    y = _moe_ffn(block_e, block_first, n_active, next_e, a_sorted, w1, w3, w2, lead_moe, tm=tm, tn=tn)
    return _combine_ln(pos, x, gates, ln_g, ln_b, y, lead_ln, alpha=alpha)


def kernel(x, positions, w_in, w_gate_up, b_gate, gla_norm_g, lambda_q1, lambda_k1, lambda_q2, lambda_k2,
           diff_norm_g, w_out, ln1_g, ln1_b, ln2_g, ln2_b, ffn_w1, ffn_w3, ffn_w2, router_w, router_b,
           moe_w1, moe_w3, moe_w2):
    batch, seq, d = x.shape
    n = batch * seq
    depth = w_in.shape[0]
    alpha = (2 * depth) ** 0.25
    xf = x.reshape(n, d)
    xb = xf.astype(BF16)
    pos = positions.reshape(n)
    tab_a, half_a = _rope_tables(pos, DIFF_D)
    tab_b, half_b = _rope_tables(pos, DSA_DH)
    plain = lambda i, j, k: (i, k)
    w_in_t = jnp.swapaxes(w_in, 1, 2)
    for l in range(depth):
        ha = _matmul(xb, w_in_t, n_out=W_GLA, k_steps=1, tm=1024, tn=HM_TN, tk=d, out_dtype=BF16,
                     a_map=plain, w_map=lambda i, j, k, l=l: (l, j, k), w_lead=True, w_t=True,
                     name="in_proj_gla")
        hm = _matmul(xb, w_in_t, n_out=HM_TILES * HM_TN, k_steps=1, tm=1024, tn=HM_TN, tk=d,
                     out_dtype=BF16, a_map=plain,
                     w_map=lambda i, j, k, l=l: (l, HM_BASE // HM_TN + j, k), w_lead=True, w_t=True,
                     n_valid=N_IN, col_start=lambda j: HM_BASE + j * HM_TN, name="in_proj_attn")
        small_tiles = (OFF_GLR // 128, OFF_IK // 128)
        pick = lambda j: jnp.where(j == 0, small_tiles[0], small_tiles[1])
        hs = _matmul(xb, w_in_t, n_out=256, k_steps=1, tm=1024, tn=128, tk=d, out_dtype=F32,
                     a_map=plain, w_map=lambda i, j, k, l=l: (l, pick(j), k), w_lead=True, w_t=True,
                     n_valid=N_IN, col_start=lambda j: pick(j) * 128, name="in_proj_small")
        glr = hs[:, OFF_GLR % 128:OFF_GLR % 128 + GLA_RANK]
        iw0 = 128 + OFF_IW % 128
        wt = hs[:, iw0:iw0 + IDX_HEADS].reshape(batch, seq, IDX_HEADS).transpose(0, 2, 1)
        dq, dk, dv, sq, sk, sv, iq, ikd = _split_rope(hm, hs, tab_a, half_a, tab_b, half_b)
        svt = sv.reshape(batch, seq, W_SKV).transpose(0, 2, 1)
        dvt = dv.reshape(batch, seq, W_DIFF).transpose(0, 2, 1)

        mix = jnp.zeros((n, W_MIX), BF16)
        mix = _gla(ha, glr, w_gate_up, b_gate, gla_norm_g, mix, l, batch=batch, seq=seq)
        lambda_init = 0.8 - 0.6 * math.exp(-0.3 * l)
        mix = _diff_attention(dq, dk, dvt, lambda_q1, lambda_k1, lambda_q2, lambda_k2, diff_norm_g, mix, l,
                              batch=batch, seq=seq, lambda_init=lambda_init, tq=1024, heads=1)
        mix = _dsa(iq, sq, ikd, wt, sk, svt, mix, batch=batch, seq=seq, tq=256, tk=512)

        proj = _out_proj(mix, w_out, l)
        xf, xb = _add_ln(xf, proj, ln1_g, ln1_b, l, alpha=alpha)

        j = l // 2
        if l % 2 == 0:
            f_dim = ffn_w2.shape[1]
            f_pad = -(-f_dim // 1024) * 1024
            hid = _swiglu_up(xb, ffn_w1, ffn_w3, j, tm=2048, tn=256, tk=d, f_out=f_pad)
            down = _matmul(hid, ffn_w2, n_out=d, k_steps=8, tm=2048, tn=1024, tk=f_pad // 8, out_dtype=F32,
                           a_map=plain, w_map=lambda i, j_, k, j=j: (j, k, j_), w_lead=True, k_valid=f_dim,
                           name="ffn_down")
            xf, xb = _add_ln(xf, down, ln2_g, ln2_b, l, alpha=alpha)
        else:
            xf = _moe_layer(xf, router_w, router_b, moe_w1, moe_w3, moe_w2, ln2_g, ln2_b, j, l, alpha=alpha)
            xb = xf.astype(BF16)
    return xf.reshape(batch, seq, d)
```

```python
import functools
import math

import jax
import jax.numpy as jnp
from jax import lax
from jax.experimental import pallas as pl
from jax.experimental.pallas import tpu as pltpu

F32 = jnp.float32
BF16 = jnp.bfloat16
HIGHEST = lax.Precision.HIGHEST
INT_MIN = -(2 ** 31)

ROPE_THETA = 500000.0
GLA_HEADS, GLA_DK, GLA_DV, GLA_RANK, GLA_TAU, GLA_CHUNK = 4, 192, 384, 16, 16.0, 64
GLA_SUB = 16
GLA_SAFE_DECAY = 40.0
GLA_SAFE_K = 1e18
DIFF_HEADS, DIFF_D = 10, 64
DSA_HEADS, DSA_KV, DSA_DH = 10, 2, 128
IDX_HEADS, IDX_DIM, IDX_TOPK = 32, 64, 256
MOE_E, MOE_TOPK = 8, 2
LN_EPS = 1e-5
SOFTMAX_SAFE_LOGIT = 60.0

W_GLA = 2 * GLA_HEADS * GLA_DK + 2 * GLA_HEADS * GLA_DV
OFF_GLR = W_GLA
OFF_DQ = OFF_GLR + GLA_RANK
W_DIFF = DIFF_HEADS * 2 * DIFF_D
OFF_DK, OFF_DV = OFF_DQ + W_DIFF, OFF_DQ + 2 * W_DIFF
OFF_SQ = OFF_DQ + 3 * W_DIFF
W_SQ, W_SKV = DSA_HEADS * DSA_DH, DSA_KV * DSA_DH
OFF_SK, OFF_SV = OFF_SQ + W_SQ, OFF_SQ + W_SQ + W_SKV
OFF_IQ = OFF_SV + W_SKV
W_IQ = IDX_HEADS * IDX_DIM
OFF_IK = OFF_IQ + W_IQ
OFF_IW = OFF_IK + IDX_DIM
N_IN = OFF_IW + IDX_HEADS

HM_BASE = W_GLA
HM_TN = 512
HM_TILES = -(-(N_IN - HM_BASE) // HM_TN)

MIX_GLA, MIX_DIFF, MIX_DSA = 0, 2560, 3840
W_MIX = MIX_DSA + DSA_HEADS * DSA_DH

VMEM_LIMIT = 56 * 1024 * 1024


def _cparams(sem, vmem=VMEM_LIMIT):
    return pltpu.CompilerParams(dimension_semantics=sem, vmem_limit_bytes=vmem)


def _mm_kernel(a_ref, w_ref, o_ref, acc_ref=None, *, k_valid, n_valid, col_start, w_t):
    k = pl.program_id(2)
    k_ax, n_ax = (1, 0) if w_t else (0, 1)

    def partial_product():
        w = w_ref[...]
        if k_valid is not None:
            kk = k * w.shape[k_ax] + lax.broadcasted_iota(jnp.int32, w.shape, k_ax)
            w = jnp.where(kk < k_valid, w, 0.0)
        if n_valid is not None:
            col = col_start(pl.program_id(1)) + lax.broadcasted_iota(jnp.int32, w.shape, n_ax)
            w = jnp.where(col < n_valid, w, 0.0)
        return lax.dot_general(a_ref[...], w.astype(BF16), (((1,), (k_ax,)), ((), ())),
                               preferred_element_type=F32)

    if acc_ref is None:
        o_ref[...] = partial_product().astype(o_ref.dtype)
        return

    @pl.when(k == 0)
    def _():
        acc_ref[...] = partial_product()

    @pl.when(k > 0)
    def _():
        acc_ref[...] += partial_product()

    @pl.when(k == pl.num_programs(2) - 1)
    def _():
        o_ref[...] = acc_ref[...].astype(o_ref.dtype)


def _matmul(a, w, *, n_out, k_steps, tm, tn, tk, out_dtype, a_map, w_map, w_lead=False, k_valid=None,
            n_valid=None, col_start=None, w_t=False, name="mm"):
    m = a.shape[0]
    w_block = (tn, tk) if w_t else (tk, tn)
    if w_lead:
        w_block = (None,) + w_block
    return pl.pallas_call(
        functools.partial(_mm_kernel, k_valid=k_valid, n_valid=n_valid, col_start=col_start, w_t=w_t),
        out_shape=jax.ShapeDtypeStruct((m, n_out), out_dtype),
        grid=(m // tm, pl.cdiv(n_out, tn), k_steps),
        in_specs=[pl.BlockSpec((tm, tk), a_map), pl.BlockSpec(w_block, w_map)],
        out_specs=pl.BlockSpec((tm, tn), lambda i, j, k: (i, j)),
        scratch_shapes=[pltpu.VMEM((tm, tn), F32)] if k_steps > 1 else [],
        compiler_params=_cparams(("parallel", "parallel", "arbitrary")),
        name=name,
    )(a, w)


def _swiglu_kernel(a_ref, w1_ref, w3_ref, o_ref, acc1_ref=None, acc3_ref=None, *, n_valid):
    k = pl.program_id(2)

    def partial_product(w_ref):
        return jnp.dot(a_ref[...], w_ref[...].astype(BF16), preferred_element_type=F32)

    def finish(g, u):
        h = g * jax.nn.sigmoid(g) * u
        col = pl.program_id(1) * h.shape[1] + lax.broadcasted_iota(jnp.int32, h.shape, 1)
        o_ref[...] = jnp.where(col < n_valid, h, 0.0).astype(o_ref.dtype)

    if acc1_ref is None:
        finish(partial_product(w1_ref), partial_product(w3_ref))
        return

    @pl.when(k == 0)
    def _():
        acc1_ref[...] = partial_product(w1_ref)
        acc3_ref[...] = partial_product(w3_ref)

    @pl.when(k > 0)
    def _():
        acc1_ref[...] += partial_product(w1_ref)
        acc3_ref[...] += partial_product(w3_ref)

    @pl.when(k == pl.num_programs(2) - 1)
    def _():
        finish(acc1_ref[...], acc3_ref[...])


def _swiglu_up(a, w1, w3, lead, *, tm, tn, tk, f_out):
    m, kdim = a.shape
    f = w1.shape[-1]
    last = (f - 1) // tn
    wspec = pl.BlockSpec((None, tk, tn), lambda i, j, k: (lead, k, jnp.minimum(j, last)))
    k_steps = kdim // tk
    return pl.pallas_call(
        functools.partial(_swiglu_kernel, n_valid=f),
        out_shape=jax.ShapeDtypeStruct((m, f_out), BF16),
        grid=(m // tm, f_out // tn, k_steps),
        in_specs=[pl.BlockSpec((tm, tk), lambda i, j, k: (i, k)), wspec, wspec],
        out_specs=pl.BlockSpec((tm, tn), lambda i, j, k: (i, j)),
        scratch_shapes=[pltpu.VMEM((tm, tn), F32), pltpu.VMEM((tm, tn), F32)] if k_steps > 1 else [],
        compiler_params=_cparams(("parallel", "parallel", "arbitrary")),
        name="swiglu_up",
    )(a, w1, w3)


def _out_proj_kernel(mix_ref, w_ref, o_ref):
    n_gla = GLA_HEADS * GLA_DV
    w = w_ref[...].astype(BF16)
    o_ref[...] = (jnp.dot(mix_ref[:, MIX_GLA:MIX_GLA + n_gla], w[:n_gla], preferred_element_type=F32)
                  + jnp.dot(mix_ref[:, MIX_DIFF:], w[n_gla:], preferred_element_type=F32))


def _out_proj(mix, w_out, lead, *, tm=1024, tn=256):
    m, wm = mix.shape
    d = w_out.shape[-1]
    return pl.pallas_call(
        _out_proj_kernel,
        out_shape=jax.ShapeDtypeStruct((m, d), F32),
        grid=(m // tm, d // tn),
        in_specs=[pl.BlockSpec((tm, wm), lambda i, j: (i, 0)),
                  pl.BlockSpec((None, w_out.shape[1], tn), lambda i, j: (lead, 0, j))],
        out_specs=pl.BlockSpec((tm, tn), lambda i, j: (i, j)),
        compiler_params=_cparams(("parallel", "parallel")),
        name="out_proj",
    )(mix, w_out)


def _add_ln_kernel(x_ref, y_ref, g_ref, b_ref, o_ref, ob_ref, *, alpha):
    z = alpha * x_ref[...] + y_ref[...].astype(F32)
    mu = jnp.mean(z, axis=-1, keepdims=True)
    zc = z - mu
    var = jnp.mean(zc * zc, axis=-1, keepdims=True)
    out = zc * lax.rsqrt(var + LN_EPS) * g_ref[...] + b_ref[...]
    o_ref[...] = out
    ob_ref[...] = out.astype(BF16)


def _add_ln(x, y, g, b, lead, *, alpha, tm=256):
    m, d = x.shape
    row = pl.BlockSpec((tm, d), lambda i: (i, 0))
    par = pl.BlockSpec((None, 1, d), lambda i: (lead, 0, 0))
    g, b = g.reshape(-1, 1, d), b.reshape(-1, 1, d)
    return pl.pallas_call(
        functools.partial(_add_ln_kernel, alpha=alpha),
        out_shape=(jax.ShapeDtypeStruct((m, d), F32), jax.ShapeDtypeStruct((m, d), BF16)),
        grid=(m // tm,),
        in_specs=[row, row, par, par],
        out_specs=(row, row),
        compiler_params=_cparams(("parallel",)),
        name="add_ln",
    )(x, y, g, b)


def _rope_tables(positions, width):
    rot = width // 4
    half = rot // 2
    inv_freq = 1.0 / (ROPE_THETA ** (jnp.arange(half, dtype=F32) * 2.0 / rot))
    ang = positions.astype(F32)[:, None] * inv_freq
    cos, sin = jnp.cos(ang), jnp.sin(ang)
    n = positions.shape[0]
    one = jnp.ones((n, width - rot), F32)
    zero_h = jnp.zeros((n, half), F32)
    zero_r = jnp.zeros((n, width - rot), F32)
    c = jnp.concatenate([cos, cos, one], axis=1)
    s_up = jnp.concatenate([-sin, zero_h, zero_r], axis=1)
    s_dn = jnp.concatenate([zero_h, sin, zero_r], axis=1)
    rep = 128 // width
    return tuple(jnp.tile(t, (1, rep)) for t in (c, s_up, s_dn)), half


def _rope_tile(x, c, s_up, s_dn, half):
    return x * c + pltpu.roll(x, 128 - half, 1) * s_up + pltpu.roll(x, half, 1) * s_dn


def _split_rope_kernel(hm_ref, hc_ref, ca_ref, ua_ref, da_ref, cb_ref, ub_ref, db_ref,
                       dq_o, dk_o, dv_o, sq_o, sk_o, sv_o, iq_o, ik_o, *, shift, ik_shift, half_a, half_b):
    ta = (ca_ref[...], ua_ref[...], da_ref[...])
    tb = (cb_ref[...], ub_ref[...], db_ref[...])
    lane = lax.broadcasted_iota(jnp.int32, ca_ref.shape, 1)

    def shifted(k):
        return pltpu.roll(hm_ref[:, 128 * k:128 * (k + 1)].astype(F32), 128 - shift, 1)

    m = 0
    nxt = shifted(0)
    for dst, tab, half in ((dq_o, ta, half_a), (dk_o, ta, half_a), (dv_o, None, 0), (sq_o, tb, half_b),
                           (sk_o, tb, half_b), (sv_o, None, 0), (iq_o, ta, half_a)):
        for t in range(dst.shape[1] // 128):
            cur, nxt = nxt, shifted(m + 1)
            y = jnp.where(lane < 128 - shift, cur, nxt)
            if tab is not None:
                y = _rope_tile(y, *tab, half)
            dst[:, 128 * t:128 * (t + 1)] = y.astype(BF16)
            m += 1
    ik = _rope_tile(pltpu.roll(hc_ref[...], 128 - ik_shift, 1), *ta, half_a)
    ik_o[...] = jnp.where(lane < IDX_DIM, ik, pltpu.roll(ik, IDX_DIM, 1)).astype(BF16)


def _split_rope(hm, hs, tab_a, half_a, tab_b, half_b, *, tm=256):
    n, wm = hm.shape
    tile = pl.BlockSpec((tm, 128), lambda i: (i, 0))
    widths = (W_DIFF, W_DIFF, W_DIFF, W_SQ, W_SKV, W_SKV, W_IQ, 128)
    return pl.pallas_call(
        functools.partial(_split_rope_kernel, shift=OFF_DQ - HM_BASE, ik_shift=OFF_IK % 128,
                          half_a=half_a, half_b=half_b),
        out_shape=tuple(jax.ShapeDtypeStruct((n, w), BF16) for w in widths),
        grid=(n // tm,),
        in_specs=[pl.BlockSpec((tm, wm), lambda i: (i, 0)), pl.BlockSpec((tm, 128), lambda i: (i, 1))] + [tile] * 6,
        out_specs=tuple(pl.BlockSpec((tm, w), lambda i: (i, 0)) for w in widths),
        compiler_params=_cparams(("parallel",)),
        name="split_rope",
    )(hm, hs, *tab_a, *tab_b)


def _gla_prep_kernel(q_ref, k_ref, glr_ref, wg_ref, bg_ref, qe_o, kd_o, s_o, eb_o, la_ref, *, chunks):
    C, SUB, DK = GLA_CHUNK, GLA_SUB, GLA_DK
    NT = (((1,), (1,)), ((), ()))

    row = lax.broadcasted_iota(jnp.int32, (C, C), 0)
    col = lax.broadcasted_iota(jnp.int32, (C, C), 1)
    tril = (col <= row).astype(F32)
    gi = lax.broadcasted_iota(jnp.int32, (SUB, SUB * SUB), 0)
    gr = lax.broadcasted_iota(jnp.int32, (SUB, SUB * SUB), 1)
    gsum = (gr // SUB == gi).astype(BF16)
    pr = lax.broadcasted_iota(jnp.int32, (SUB * SUB, C), 0)
    pc = lax.broadcasted_iota(jnp.int32, (SUB * SUB, C), 1)
    srow = lax.broadcasted_iota(jnp.int32, (SUB, C), 0)
    scol = lax.broadcasted_iota(jnp.int32, (SUB, C), 1)

    z = jnp.dot(glr_ref[...], wg_ref[...], precision=HIGHEST, preferred_element_type=F32) + bg_ref[...]
    la_all = -(jnp.maximum(-z, 0.0) + jnp.log1p(jnp.exp(-jnp.abs(z)))) * (1.0 / GLA_TAU)
    la_ref[...] = la_all
    sub_decay = -jnp.sum(la_all.reshape(chunks * C // SUB, SUB, la_all.shape[1]), axis=1)
    k_mag = jnp.max(jnp.abs(k_ref[...].astype(F32)))
    safe = jnp.logical_and(jnp.max(sub_decay) <= GLA_SAFE_DECAY, k_mag <= GLA_SAFE_K)

    def chunk_head(ci, hh, fast, b_pair):
        r0 = pl.multiple_of(ci * C, C)
        rows = pl.ds(r0, C)
        ks = slice(hh * DK, (hh + 1) * DK)
        q = q_ref[rows, ks].astype(F32) * (DK ** -0.5)
        k = k_ref[rows, ks].astype(F32)
        b = b_pair[:, ks]
        qe_o[rows, ks] = (q * jnp.exp(b)).astype(BF16)

        s_rows = []
        for blk in range(C // SUB):
            r0 = blk * SUB
            b_i, q_i, k_i = b[r0:r0 + SUB], q[r0:r0 + SUB], k[r0:r0 + SUB]
            beta = b[r0:r0 + 1]
            q_t = (q_i * jnp.exp(b_i - beta)).astype(BF16)
            if fast:
                k_t = (k * jnp.exp(jnp.minimum(beta - b, GLA_SAFE_DECAY))).astype(BF16)
                s_blk = lax.dot_general(q_t, k_t, NT, preferred_element_type=F32)
                s_rows.append(jnp.where(scol <= srow + r0, s_blk, 0.0))
                continue
            pair = (q_i[:, None, :] * k_i[None, :, :]
                    * jnp.exp(jnp.minimum(b_i[:, None, :] - b_i[None, :, :], 0.0)))
            pair_sum = jnp.sum(pair.reshape(SUB * SUB, DK), axis=-1, keepdims=True)
            placed = jnp.where(pc == r0 + pr % SUB, pair_sum, 0.0).astype(BF16)
            s_blk = jnp.dot(gsum, placed, preferred_element_type=F32)
            s_blk = jnp.where(scol <= srow + r0, s_blk, 0.0)
            if blk > 0:
                k_t = (k * jnp.exp(jnp.minimum(beta - b, 0.0))).astype(BF16)
                off = lax.dot_general(q_t, k_t, NT, preferred_element_type=F32)
                s_blk = jnp.where(scol < r0, off, s_blk)
            s_rows.append(s_blk)
        s_o[rows, hh * C:(hh + 1) * C] = jnp.concatenate(s_rows, axis=0).astype(BF16)
        b_last = b[C - 1:C]
        kd_o[rows, ks] = (k * jnp.exp(b_last - b)).astype(BF16)
        eb_o[pl.ds(ci, 1), ks] = jnp.exp(b_last)

    def all_chunks(fast):
        per = 4 if fast else 2

        def some_chunks(i, carry):
            for c in range(per):
                ci = per * i + c
                la = la_ref[pl.ds(pl.multiple_of(ci * C, C), C), :]
                b_pair = jnp.dot(tril, la, precision=HIGHEST, preferred_element_type=F32)
                for hh in range(2):
                    chunk_head(ci, hh, fast, b_pair)
            return carry
        lax.fori_loop(0, chunks // per, some_chunks, 0)

    @pl.when(safe)
    def _():
        all_chunks(True)

    @pl.when(jnp.logical_not(safe))
    def _():
        all_chunks(False)


def _gla_scan_kernel(qe_ref, kd_ref, s_ref, eb_ref, v_ref, r_ref, ng_ref, mix_ref, o_ref, state_ref):
    del mix_ref
    C, DK, DV = GLA_CHUNK, GLA_DK, GLA_DV
    NT = (((1,), (1,)), ((), ()))
    TN = (((0,), (0,)), ((), ()))

    @pl.when(pl.program_id(0) == 0)
    def _():
        state_ref[...] = jnp.zeros_like(state_ref)

    for b in range(state_ref.shape[0]):
        for h in range(GLA_HEADS):
            ks = slice(h * DK, (h + 1) * DK)
            vs = slice(h * DV, (h + 1) * DV)
            st = state_ref[b, h]
            v = v_ref[b, :, vs]
            o = (lax.dot_general(qe_ref[b, :, ks], st.astype(BF16), NT, preferred_element_type=F32)
                 + jnp.dot(s_ref[b, :, h * C:(h + 1) * C], v, preferred_element_type=F32))
            state_ref[b, h] = st * eb_ref[b:b + 1, ks] + lax.dot_general(v, kd_ref[b, :, ks], TN,
                                                                         preferred_element_type=F32)
            ms = jnp.mean(o * o, axis=-1, keepdims=True)
            o_n = o * lax.rsqrt(ms + LN_EPS) * ng_ref[...]
            r = r_ref[b, :, vs].astype(F32)
            o_ref[b, :, vs] = (o_n * (r * jax.nn.sigmoid(r))).astype(BF16)


def _gla(ha, glr, w_gate_up, b_gate, gla_norm_g, mix, lead, *, batch, seq, chunks=8):
    C = GLA_CHUNK
    n = batch * seq
    nc = seq // C
    hp = GLA_HEADS // 2
    wq = 2 * GLA_DK
    wqk, wv = GLA_HEADS * GLA_DK, GLA_HEADS * GLA_DV
    tr = chunks * C

    def rows(width, first):
        return pl.BlockSpec((tr, width), lambda i, p: (i, first + p))

    qe, kd, sc, eb = pl.pallas_call(
        functools.partial(_gla_prep_kernel, chunks=chunks),
        out_shape=(jax.ShapeDtypeStruct((n, wqk), BF16), jax.ShapeDtypeStruct((n, wqk), BF16),
                   jax.ShapeDtypeStruct((n, GLA_HEADS * C), BF16), jax.ShapeDtypeStruct((n // C, wqk), F32)),
        grid=(n // tr, hp),
        in_specs=[rows(wq, 0), rows(wq, hp),
                  pl.BlockSpec((tr, GLA_RANK), lambda i, p: (i, 0)),
                  pl.BlockSpec((None, GLA_RANK, wq), lambda i, p: (lead, 0, p)),
                  pl.BlockSpec((None, 1, wq), lambda i, p: (lead, 0, p))],
        out_specs=(rows(wq, 0), rows(wq, 0), rows(2 * C, 0),
                   pl.BlockSpec((chunks, wq), lambda i, p: (i, p))),
        scratch_shapes=[pltpu.VMEM((tr, wq), F32)],
        compiler_params=_cparams(("parallel", "parallel")),
        name="gla_prep",
    )(ha, ha, glr, w_gate_up, b_gate.reshape(-1, 1, wqk))

    def per_chunk(width, first):
        return pl.BlockSpec((batch, C, width), lambda c: (0, c, first))

    eb = eb.reshape(batch, nc, wqk).transpose(1, 0, 2)
    ha3 = ha.reshape(batch, seq, -1)
    out = pl.pallas_call(
        _gla_scan_kernel,
        out_shape=jax.ShapeDtypeStruct((batch, seq, mix.shape[1]), mix.dtype),
        grid=(nc,),
        in_specs=[per_chunk(wqk, 0), per_chunk(wqk, 0), per_chunk(GLA_HEADS * C, 0),
                  pl.BlockSpec((None, batch, wqk), lambda c: (c, 0, 0)),
                  per_chunk(wv, 1), per_chunk(wv, 2),
                  pl.BlockSpec((None, 1, GLA_DV), lambda c: (lead, 0, 0)),
                  pl.BlockSpec(memory_space=pl.ANY)],
        out_specs=per_chunk(wv, MIX_GLA // wv),
        scratch_shapes=[pltpu.VMEM((batch, GLA_HEADS, GLA_DV, GLA_DK), F32)],
        input_output_aliases={7: 0},
        compiler_params=_cparams(("arbitrary",)),
        name="gla_scan",
    )(qe.reshape(batch, seq, wqk), kd.reshape(batch, seq, wqk), sc.reshape(batch, seq, GLA_HEADS * C), eb,
      ha3, ha3, gla_norm_g.reshape(-1, 1, GLA_DV), mix.reshape(batch, seq, -1))
    return out.reshape(mix.shape)


def _diff_kernel(lq1_ref, lk1_ref, lq2_ref, lk2_ref, q_ref, k_ref, vt_ref, g_ref, mix_ref, o_ref,
                 qs_ref, m_ref, l_ref, acc_ref, ksq_ref, *, tq, lambda_init):
    del mix_ref
    qi = pl.program_id(2)
    NT = (((1,), (1,)), ((), ()))
    hw = 2 * DIFF_D
    heads = q_ref.shape[1] // hw

    lane = lax.broadcasted_iota(jnp.int32, (tq, hw), 1)
    logit_bound_sq = F32(0.0)
    for hh in range(heads):
        hs = slice(hw * hh, hw * (hh + 1))
        q = q_ref[:, hs].astype(F32) * (DIFF_D ** -0.5)
        qs_ref[hh, :tq] = jnp.where(lane < DIFF_D, q, 0.0).astype(BF16)
        qs_ref[hh, tq:] = jnp.where(lane >= DIFF_D, q, 0.0).astype(BF16)

        @pl.when(qi == 0)
        def _():
            k = k_ref[:, hs].astype(F32)
            ksq_ref[hh] = jnp.max(jnp.sum(k * k, axis=-1, keepdims=True))

        logit_bound_sq = jnp.maximum(logit_bound_sq,
                                     jnp.max(jnp.sum(q * q, axis=-1, keepdims=True)) * ksq_ref[hh])
    m_ref[...] = jnp.full_like(m_ref, -jnp.inf)
    l_ref[...] = jnp.zeros_like(l_ref)
    acc_ref[...] = jnp.zeros_like(acc_ref)
    bounded = logit_bound_sq <= SOFTMAX_SAFE_LOGIT ** 2

    def tile(j, masked, use_max):
        r0 = pl.multiple_of(j * tq, tq)
        for hh in range(heads):
            hs = slice(hw * hh, hw * (hh + 1))
            s = lax.dot_general(k_ref[pl.ds(r0, tq), hs], qs_ref[hh], NT, preferred_element_type=F32)
            if masked:
                r = lax.broadcasted_iota(jnp.int32, s.shape, 0)
                c = lax.broadcasted_iota(jnp.int32, s.shape, 1)
                s = jnp.where(r <= jnp.where(c >= tq, c - tq, c), s, -jnp.inf)
            if use_max:
                m_prev = m_ref[hh]
                m_new = jnp.maximum(m_prev, jnp.max(s, axis=0, keepdims=True))
                alpha = jnp.exp(m_prev - m_new)
                p = jnp.exp(s - m_new)
                l_ref[hh] = alpha * l_ref[hh] + jnp.sum(p, axis=0, keepdims=True)
                acc_ref[hh] = alpha * acc_ref[hh] + jnp.dot(vt_ref[hs, pl.ds(r0, tq)], p.astype(BF16),
                                                          preferred_element_type=F32)
                m_ref[hh] = m_new
            else:
                p = jnp.exp(s)
                l_ref[hh] += jnp.sum(p, axis=0, keepdims=True)
                acc_ref[hh] += jnp.dot(vt_ref[hs, pl.ds(r0, tq)], p.astype(BF16), preferred_element_type=F32)

    def all_tiles(use_max):
        def full_tile(j, carry):
            tile(j, False, use_max)
            return carry
        lax.fori_loop(0, qi, full_tile, 0)
        tile(qi, True, use_max)

    @pl.when(bounded)
    def _():
        all_tiles(False)

    @pl.when(jnp.logical_not(bounded))
    def _():
        all_tiles(True)

    lam = (jnp.exp(jnp.sum(lq1_ref[...] * lk1_ref[...], axis=-1, keepdims=True))
           - jnp.exp(jnp.sum(lq2_ref[...] * lk2_ref[...], axis=-1, keepdims=True)) + lambda_init)
    for hh in range(heads):
        o = acc_ref[hh] / l_ref[hh]
        o = o[:, :tq] - lam * o[:, tq:]
        ms = jnp.mean(o * o, axis=0, keepdims=True)
        o = o * lax.rsqrt(ms + LN_EPS) * g_ref[...] * (1.0 - lambda_init)
        o_ref[:, hw * hh:hw * (hh + 1)] = o.T.astype(BF16)


def _diff_attention(dq, dk, dvt, lq1, lk1, lq2, lk2, norm_g, mix, lead, *, batch, seq, lambda_init, tq, heads=2):
    nq = seq // tq
    hw = 2 * DIFF_D
    pw = heads * hw
    lam_spec = pl.BlockSpec((None, 1, DIFF_D), lambda b, p, qi: (lead, 0, 0))
    lq1, lk1, lq2, lk2 = (t.reshape(-1, 1, DIFF_D) for t in (lq1, lk1, lq2, lk2))
    return pl.pallas_call(
        functools.partial(_diff_kernel, tq=tq, lambda_init=lambda_init),
        out_shape=jax.ShapeDtypeStruct(mix.shape, mix.dtype),
        grid=(batch, DIFF_HEADS // heads, nq),
        in_specs=[lam_spec] * 4 + [
            pl.BlockSpec((tq, pw), lambda b, p, qi: (b * nq + qi, p)),
            pl.BlockSpec((seq, pw), lambda b, p, qi: (b, p)),
            pl.BlockSpec((None, pw, seq), lambda b, p, qi: (b, p, 0)),
            pl.BlockSpec((None, hw, 1), lambda b, p, qi: (lead, 0, 0)),
            pl.BlockSpec(memory_space=pl.ANY)],
        out_specs=pl.BlockSpec((tq, pw), lambda b, p, qi: (b * nq + qi, MIX_DIFF // pw + p)),
        scratch_shapes=[pltpu.VMEM((heads, 2 * tq, hw), BF16), pltpu.VMEM((heads, 1, 2 * tq), F32),
                        pltpu.VMEM((heads, 1, 2 * tq), F32), pltpu.VMEM((heads, hw, 2 * tq), F32),
                        pltpu.SMEM((heads,), F32)],
        input_output_aliases={8: 0},
        compiler_params=_cparams(("parallel", "parallel", "arbitrary")),
        name="diff_attn",
    )(lq1, lk1, lq2, lk2, dq, dk, dvt, norm_g.reshape(-1, hw, 1), mix)


def _dsa_kernel(iq_ref, sq_ref, ikd_ref, wt_ref, sk_ref, svt_ref, mix_ref, o_ref,
                iqm_ref, keys_ref, bias_ref, q5_ref, m_ref, l_ref, acc_ref, ksq_ref, hi_ref, lo_ref,
                *, tq, tk, seq, k_sel):
    del mix_ref
    qi = pl.program_id(1)
    NT = (((1,), (1,)), ((), ()))
    n_tiles = (qi * tq + tq + tk - 1) // tk
    qpos = qi * tq + lax.broadcasted_iota(jnp.int32, (1, tq), 1)
    krow = lax.broadcasted_iota(jnp.int32, (tk, tq), 0)
    idx_scale = (IDX_DIM ** -0.5) * (IDX_HEADS ** -0.5)
    neg_inf = F32(-jnp.inf)

    def tile_start(j):
        return pl.multiple_of(j * tk, tk)

    lane = lax.broadcasted_iota(jnp.int32, (tq, 128), 1)
    for t in range(IDX_HEADS // 2):
        a = iq_ref[:, 128 * t:128 * (t + 1)]
        zero = jnp.zeros_like(a)
        iqm_ref[t, :tq] = jnp.where(lane < IDX_DIM, a, zero)
        iqm_ref[t, tq:] = jnp.where(lane >= IDX_DIM, a, zero)
    wt = wt_ref[...]

    def score_tile(j, carry):
        r0 = tile_start(j)
        kd = ikd_ref[pl.ds(r0, tk), :]
        acc = jnp.zeros((tk, 2 * tq), F32)
        for t in range(IDX_HEADS // 2):
            lg = lax.dot_general(kd, iqm_ref[t], NT, preferred_element_type=F32)
            w2 = jnp.concatenate([wt[2 * t:2 * t + 1], wt[2 * t + 1:2 * t + 2]], axis=1)
            acc = acc + jnp.maximum(lg, 0.0) * w2
        acc = acc[:, :tq] + acc[:, tq:]
        bits = pltpu.bitcast(acc * idx_scale, jnp.int32)
        key = jnp.where(bits < 0, (bits ^ 0x7FFFFFFF) + 1, bits)
        key = jnp.where(r0 + krow <= qpos, key, INT_MIN)
        keys_ref[pl.ds(r0, tk), :] = key
        hi_ref[pl.ds(r0, tk), :] = (key >> 16).astype(jnp.int16)
        return carry

    lax.fori_loop(0, n_tiles, score_tile, 0)

    def count_ones(ones):
        def body(j, cnt):
            r0 = tile_start(j)
            return cnt + jnp.sum(ones(keys_ref[pl.ds(r0, tk), :], r0 + krow), axis=0, keepdims=True)
        return lax.fori_loop(0, n_tiles, body, jnp.zeros((1, tq), jnp.int32))

    def count(pred):
        return count_ones(lambda kk, kp: jnp.where(pred(kk, kp), 1, 0))

    def count16(ref, pred):
        def body(j, cnt):
            ones = jnp.where(pred(ref[pl.ds(tile_start(j), tk), :]), jnp.int16(1), jnp.int16(0))
            ones = ones.reshape(tk // 16, 16, tq)
            part = ones[0]
            for r in range(1, tk // 16):
                part = part + ones[r]
            return cnt + jnp.sum(part.astype(jnp.int32), axis=0, keepdims=True)
        return lax.fori_loop(0, n_tiles, body, jnp.zeros((1, tq), jnp.int32))

    def as_ordered_i16(u):
        return jnp.right_shift(jnp.left_shift(u ^ 0x8000, 16), 16).astype(jnp.int16)

    def search16(ref, count_above):
        def bit(i, t_u):
            cand_u = t_u | jnp.left_shift(jnp.int32(1), 15 - i)
            cand = as_ordered_i16(cand_u)
            c = count_above + count16(ref, lambda v: v >= cand)
            return jnp.where(c >= k_sel, cand_u, t_u)
        return lax.fori_loop(0, 16, bit, jnp.zeros((1, tq), jnp.int32))

    thr_hi = as_ordered_i16(search16(hi_ref, 0))
    above = count16(hi_ref, lambda v: v > thr_hi)

    def low_half_tile(j, carry):
        rows = pl.ds(tile_start(j), tk)
        lo = as_ordered_i16(keys_ref[rows, :] & 0xFFFF)
        lo_ref[rows, :] = jnp.where(hi_ref[rows, :] == thr_hi, lo, jnp.int16(-32768))
        return carry

    lax.fori_loop(0, n_tiles, low_half_tile, 0)
    thr = jnp.left_shift(thr_hi.astype(jnp.int32), 16) | search16(lo_ref, above)
    need = k_sel - count(lambda kk, kp: kk > thr)
    n_eq = count(lambda kk, kp: kk == thr)
    excess = jnp.where(thr == INT_MIN, 0, jnp.where(n_eq > need, 1, 0))

    def tie_search():
        def idx_bit(i, j_lo):
            cand = j_lo | jnp.left_shift(jnp.int32(1), (seq - 1).bit_length() - 1 - i)
            c = count_ones(lambda kk, kp: jnp.where(kk == thr, jnp.where(kp < cand, 1, 0), 0))
            return jnp.where(c < need, cand, j_lo)
        return lax.fori_loop(0, (seq - 1).bit_length(), idx_bit, jnp.zeros((1, tq), jnp.int32))

    j_star = lax.cond(jnp.max(excess) > 0, tie_search, lambda: jnp.full((1, tq), seq, jnp.int32))
    j_star = jnp.where(thr == INT_MIN, -1, jnp.where(excess > 0, j_star, seq))

    def bias_tile(j, carry):
        r0 = tile_start(j)
        kk = keys_ref[pl.ds(r0, tk), :]
        tie = jnp.where(r0 + krow <= j_star, 0.0, neg_inf)
        bias_ref[pl.ds(r0, tk), :] = jnp.where(kk > thr, 0.0, jnp.where(kk == thr, tie, neg_inf))
        return carry

    lax.fori_loop(0, n_tiles, bias_tile, 0)

    scale = DSA_DH ** -0.5
    rep = DSA_HEADS // DSA_KV
    logit_bound_sq = F32(0.0)
    for g in range(DSA_KV):
        q_sq = F32(0.0)
        for i in range(rep):
            h = g * rep + i
            qh = sq_ref[:, DSA_DH * h:DSA_DH * (h + 1)]
            q5_ref[g, tq * i:tq * (i + 1)] = qh
            qf = qh.astype(F32)
            q_sq = jnp.maximum(q_sq, jnp.max(jnp.sum(qf * qf, axis=-1, keepdims=True)))

        @pl.when(qi == 0)
        def _():
            kf = sk_ref[:, DSA_DH * g:DSA_DH * (g + 1)].astype(F32)
            ksq_ref[g] = jnp.max(jnp.sum(kf * kf, axis=-1, keepdims=True))

        logit_bound_sq = jnp.maximum(logit_bound_sq, q_sq * ksq_ref[g] * (scale * scale))
    m_ref[...] = jnp.full_like(m_ref, neg_inf)
    l_ref[...] = jnp.zeros_like(l_ref)
    acc_ref[...] = jnp.zeros_like(acc_ref)
    bounded = logit_bound_sq <= SOFTMAX_SAFE_LOGIT ** 2

    def attn_tiles(use_max):
        def attn_tile(j, carry):
            r0 = tile_start(j)
            bias = bias_ref[pl.ds(r0, tk), :]
            bias = jnp.concatenate([bias] * rep, axis=1)
            for g in range(DSA_KV):
                gs = slice(DSA_DH * g, DSA_DH * (g + 1))
                s = lax.dot_general(sk_ref[pl.ds(r0, tk), gs], q5_ref[g], NT, preferred_element_type=F32)
                s = s * scale + bias
                vt = svt_ref[gs, pl.ds(r0, tk)]
                if use_max:
                    m_prev = m_ref[g]
                    m_new = jnp.maximum(m_prev, jnp.max(s, axis=0, keepdims=True))
                    m_safe = jnp.where(m_new == neg_inf, 0.0, m_new)
                    alpha = jnp.exp(m_prev - m_safe)
                    p = jnp.exp(s - m_safe)
                    l_ref[g] = alpha * l_ref[g] + jnp.sum(p, axis=0, keepdims=True)
                    acc_ref[g] = alpha * acc_ref[g] + jnp.dot(vt, p.astype(BF16), preferred_element_type=F32)
                    m_ref[g] = m_new
                else:
                    p = jnp.exp(s)
                    l_ref[g] += jnp.sum(p, axis=0, keepdims=True)
                    acc_ref[g] += jnp.dot(vt, p.astype(BF16), preferred_element_type=F32)
            return carry

        lax.fori_loop(0, n_tiles, attn_tile, 0)

    @pl.when(bounded)
    def _():
        attn_tiles(False)

    @pl.when(jnp.logical_not(bounded))
    def _():
        attn_tiles(True)
    for g in range(DSA_KV):
        out_t = acc_ref[g] / l_ref[g]
        for i in range(rep):
            h = g * rep + i
            o_ref[:, DSA_DH * h:DSA_DH * (h + 1)] = out_t[:, tq * i:tq * (i + 1)].T.astype(BF16)


def _dsa(iq, sq, ikd, wt, sk, svt, mix, *, batch, seq, tq, tk):
    nq = seq // tq
    k_sel = min(IDX_TOPK, seq // 4)
    rep = DSA_HEADS // DSA_KV
    return pl.pallas_call(
        functools.partial(_dsa_kernel, tq=tq, tk=tk, seq=seq, k_sel=k_sel),
        out_shape=jax.ShapeDtypeStruct(mix.shape, mix.dtype),
        grid=(batch, nq),
        in_specs=[pl.BlockSpec((tq, W_IQ), lambda b, qi: (b * nq + qi, 0)),
                  pl.BlockSpec((tq, W_SQ), lambda b, qi: (b * nq + qi, 0)),
                  pl.BlockSpec((seq, 128), lambda b, qi: (b, 0)),
                  pl.BlockSpec((None, IDX_HEADS, tq), lambda b, qi: (b, 0, qi)),
                  pl.BlockSpec((seq, W_SKV), lambda b, qi: (b, 0)),
                  pl.BlockSpec((None, W_SKV, seq), lambda b, qi: (b, 0, 0)),
                  pl.BlockSpec(memory_space=pl.ANY)],
        out_specs=pl.BlockSpec((tq, W_SQ), lambda b, qi: (b * nq + qi, MIX_DSA // W_SQ)),
        scratch_shapes=[pltpu.VMEM((IDX_HEADS // 2, 2 * tq, 128), BF16), pltpu.VMEM((seq, tq), jnp.int32),
                        pltpu.VMEM((seq, tq), F32), pltpu.VMEM((DSA_KV, rep * tq, DSA_DH), BF16),
                        pltpu.VMEM((DSA_KV, 1, rep * tq), F32), pltpu.VMEM((DSA_KV, 1, rep * tq), F32),
                        pltpu.VMEM((DSA_KV, DSA_DH, rep * tq), F32), pltpu.SMEM((DSA_KV,), F32),
                        pltpu.VMEM((seq, tq), jnp.int16), pltpu.VMEM((seq, tq), jnp.int16)],
        input_output_aliases={6: 0},
        compiler_params=_cparams(("parallel", "arbitrary")),
        name="dsa",
    )(iq, sq, ikd, wt, sk, svt, mix)


def _router_kernel(x_ref, w_ref, b_ref, e_ref, g_ref, xp_ref):
    x = x_ref[...]
    logits = jnp.dot(x, w_ref[...], precision=HIGHEST, preferred_element_type=F32) + b_ref[...]
    lane = lax.broadcasted_iota(jnp.int32, logits.shape, 1)
    neg_inf = F32(-jnp.inf)
    lg = jnp.where(lane < MOE_E, logits, neg_inf)
    m1 = jnp.max(lg, axis=-1, keepdims=True)
    i1 = jnp.min(jnp.where(lg == m1, lane, 128), axis=-1, keepdims=True)
    lg2 = jnp.where(lane == i1, neg_inf, lg)
    m2 = jnp.max(lg2, axis=-1, keepdims=True)
    i2 = jnp.min(jnp.where(lg2 == m2, lane, 128), axis=-1, keepdims=True)
    e21 = jnp.exp(m2 - m1)
    g1 = 1.0 / (1.0 + e21)
    e_ref[...] = jnp.where(lane == 0, i1, jnp.where(lane == 1, i2, 0))
    g_ref[...] = jnp.where(lane == 0, g1, jnp.where(lane == 1, e21 * g1, 0.0))
    half = x.shape[1] // 2
    lo = pltpu.bitcast(x[:, :half].astype(BF16).astype(F32), jnp.uint32)
    hi = pltpu.bitcast(x[:, half:].astype(BF16).astype(F32), jnp.uint32)
    xp_ref[...] = (hi & jnp.uint32(0xFFFF0000)) | (lo >> 16)


def _router(x, router_w, router_b, *, tm=256):
    n, d = x.shape
    row = lambda w: pl.BlockSpec((tm, w), lambda i: (i, 0))
    return pl.pallas_call(
        _router_kernel,
        out_shape=(jax.ShapeDtypeStruct((n, 128), jnp.int32), jax.ShapeDtypeStruct((n, 128), F32),
                   jax.ShapeDtypeStruct((n, d // 2), jnp.uint32)),
        grid=(n // tm,),
        in_specs=[row(d), pl.BlockSpec((d, 128), lambda i: (0, 0)), pl.BlockSpec((1, 128), lambda i: (0, 0))],
        out_specs=(row(128), row(128), row(d // 2)),
        compiler_params=_cparams(("parallel",)),
        name="router",
    )(x, router_w, router_b)


def _dispatch_kernel(pos_ref, xp_ref, buf_ref, o_ref, sem, *, tb):
    del buf_ref
    base = pl.program_id(0) * tb

    def row_copy(i, s):
        return pltpu.make_async_copy(xp_ref.at[pl.ds(i, 1)], o_ref.at[pl.ds(pos_ref[2 * (base + i) + s], 1)], sem)

    def start(i, c):
        row_copy(i, 0).start(priority=0)
        row_copy(i, 1).start(priority=1)
        return c

    def wait(i, c):
        row_copy(i, 0).wait()
        row_copy(i, 1).wait()
        return c

    lax.fori_loop(0, tb, start, 0)
    lax.fori_loop(0, tb, wait, 0)


def _dispatch(pos, xp, buf, *, tb=256):
    n, w = xp.shape
    return pl.pallas_call(
        functools.partial(_dispatch_kernel, tb=tb),
        out_shape=jax.ShapeDtypeStruct(buf.shape, buf.dtype),
        grid_spec=pltpu.PrefetchScalarGridSpec(
            num_scalar_prefetch=1,
            grid=(n // tb,),
            in_specs=[pl.BlockSpec((tb, w), lambda i, pos: (i, 0)), pl.BlockSpec(memory_space=pl.ANY)],
            out_specs=pl.BlockSpec(memory_space=pl.ANY),
            scratch_shapes=[pltpu.SemaphoreType.DMA],
        ),
        input_output_aliases={2: 0},
        compiler_params=_cparams(("arbitrary",)),
        name="moe_dispatch",
    )(pos, xp, buf)


def _unpack_rows(word):
    lo = pltpu.bitcast(word << 16, F32).astype(BF16)
    hi = pltpu.bitcast(word & jnp.uint32(0xFFFF0000), F32).astype(BF16)
    return lo, hi


def _expert_tile_refresh(be_ref, bf_ref, nxt_ref, w_hbms, stage_ref, wb_refs, sem, *, lead, tn):
    n, i = pl.program_id(0), pl.program_id(1)

    def fetch(e, col_tile):
        cols = pl.ds(pl.multiple_of(col_tile * tn, tn), tn)
        return [pltpu.make_async_copy(w.at[lead, e, :, cols], stage_ref.at[t], sem)
                for t, w in enumerate(w_hbms)]

    @pl.when(bf_ref[i] == 1)
    def _():
        @pl.when(jnp.logical_and(n == 0, i == 0))
        def _():
            for c in fetch(be_ref[0], 0):
                c.start()

        for c in fetch(be_ref[i], n):
            c.wait()
        for t, wb in enumerate(wb_refs):
            wb[...] = stage_ref[t].astype(BF16)
        last_run = nxt_ref[i] < 0
        e_next = jnp.where(last_run, be_ref[0], nxt_ref[i])
        n_next = jnp.where(last_run, n + 1, n)

        @pl.when(n_next < pl.num_programs(0))
        def _():
            for c in fetch(e_next, n_next):
                c.start()


def _moe_up_kernel(be_ref, bf_ref, na_ref, nxt_ref, a_ref, w1_hbm, w3_hbm, o_ref,
                   stage_ref, w1b_ref, w3b_ref, sem, *, lead, tn):
    i = pl.program_id(1)

    @pl.when(i < na_ref[0])
    def _():
        _expert_tile_refresh(be_ref, bf_ref, nxt_ref, (w1_hbm, w3_hbm), stage_ref, (w1b_ref, w3b_ref), sem,
                             lead=lead, tn=tn)
        lo, hi = _unpack_rows(a_ref[...])
        half = lo.shape[1]
        g = (jnp.dot(lo, w1b_ref[:half], preferred_element_type=F32)
             + jnp.dot(hi, w1b_ref[half:], preferred_element_type=F32))
        u = (jnp.dot(lo, w3b_ref[:half], preferred_element_type=F32)
             + jnp.dot(hi, w3b_ref[half:], preferred_element_type=F32))
        o_ref[...] = (g * jax.nn.sigmoid(g) * u).astype(BF16)

    @pl.when(i >= na_ref[0])
    def _():
        o_ref[...] = jnp.zeros_like(o_ref)


def _moe_down_kernel(be_ref, bf_ref, na_ref, nxt_ref, h_ref, w2_hbm, o_ref, stage_ref, w2b_ref, sem, *, lead, tn):
    i = pl.program_id(1)

    @pl.when(i < na_ref[0])
    def _():
        _expert_tile_refresh(be_ref, bf_ref, nxt_ref, (w2_hbm,), stage_ref, (w2b_ref,), sem, lead=lead, tn=tn)
        o_ref[...] = jnp.dot(h_ref[...], w2b_ref[...], preferred_element_type=F32)

    @pl.when(i >= na_ref[0])
    def _():
        o_ref[...] = jnp.zeros_like(o_ref)


def _moe_ffn(block_e, block_first, n_active, next_e, a_sorted, w1, w3, w2, lead, *, tm, tn):
    r, half = a_sorted.shape
    d = 2 * half
    f = w1.shape[-1]
    nb = r // tm
    hbm = pl.BlockSpec(memory_space=pl.ANY)

    def rows(i, na):
        return jnp.minimum(i, na[0] - 1)

    h = pl.pallas_call(
        functools.partial(_moe_up_kernel, lead=lead, tn=tn),
        out_shape=jax.ShapeDtypeStruct((r, f), BF16),
        grid_spec=pltpu.PrefetchScalarGridSpec(
            num_scalar_prefetch=4,
            grid=(f // tn, nb),
            in_specs=[pl.BlockSpec((tm, half), lambda n, i, be, bf, na, nx: (rows(i, na), 0)), hbm, hbm],
            out_specs=pl.BlockSpec((tm, tn), lambda n, i, be, bf, na, nx: (i, n)),
            scratch_shapes=[pltpu.VMEM((2, d, tn), F32), pltpu.VMEM((d, tn), BF16), pltpu.VMEM((d, tn), BF16),
                            pltpu.SemaphoreType.DMA],
        ),
        compiler_params=_cparams(("arbitrary", "arbitrary")),
        name="moe_up",
    )(block_e, block_first, n_active, next_e, a_sorted, w1, w3)
    return pl.pallas_call(
        functools.partial(_moe_down_kernel, lead=lead, tn=tn),
        out_shape=jax.ShapeDtypeStruct((r, d), F32),
        grid_spec=pltpu.PrefetchScalarGridSpec(
            num_scalar_prefetch=4,
            grid=(d // tn, nb),
            in_specs=[pl.BlockSpec((tm, f), lambda n, i, be, bf, na, nx: (rows(i, na), 0)), hbm],
            out_specs=pl.BlockSpec((tm, tn), lambda n, i, be, bf, na, nx: (i, n)),
            scratch_shapes=[pltpu.VMEM((1, f, tn), F32), pltpu.VMEM((f, tn), BF16), pltpu.SemaphoreType.DMA],
        ),
        compiler_params=_cparams(("arbitrary", "arbitrary")),
        name="moe_down",
    )(block_e, block_first, n_active, next_e, h, w2)


def _combine_ln_kernel(pos_ref, x_ref, gate_ref, g_ref, b_ref, y_ref, o_ref, ybuf_ref, sem, *, tb, alpha):
    step, n_steps = pl.program_id(0), pl.num_programs(0)
    slot = step % 2

    def row_copy(blk, buf, i, s):
        return pltpu.make_async_copy(y_ref.at[pl.ds(pos_ref[2 * (blk * tb + i) + s], 1)],
                                     ybuf_ref.at[buf, s, pl.ds(i, 1)], sem.at[buf])

    def start_block(blk, buf):
        def body(i, c):
            row_copy(blk, buf, i, 0).start(priority=0)
            row_copy(blk, buf, i, 1).start(priority=1)
            return c
        lax.fori_loop(0, tb, body, 0)

    def wait_block(blk, buf):
        def body(i, c):
            row_copy(blk, buf, i, 0).wait()
            row_copy(blk, buf, i, 1).wait()
            return c
        lax.fori_loop(0, tb, body, 0)

    @pl.when(step == 0)
    def _():
        start_block(0, 0)

    @pl.when(step + 1 < n_steps)
    def _():
        start_block(step + 1, 1 - slot)

    wait_block(step, slot)
    gate = gate_ref[...]
    f = ybuf_ref[slot, 0] * gate[:, 0:1] + ybuf_ref[slot, 1] * gate[:, 1:2]
    z = alpha * x_ref[...] + f
    mu = jnp.mean(z, axis=-1, keepdims=True)
    zc = z - mu
    var = jnp.mean(zc * zc, axis=-1, keepdims=True)
    o_ref[...] = zc * lax.rsqrt(var + LN_EPS) * g_ref[...] + b_ref[...]


def _combine_ln(pos, x, gates, g, b, y, lead, *, alpha, tb=128):
    n, d = x.shape
    par = pl.BlockSpec((None, 1, d), lambda i, pos: (lead, 0, 0))
    return pl.pallas_call(
        functools.partial(_combine_ln_kernel, tb=tb, alpha=alpha),
        out_shape=jax.ShapeDtypeStruct((n, d), F32),
        grid_spec=pltpu.PrefetchScalarGridSpec(
            num_scalar_prefetch=1,
            grid=(n // tb,),
            in_specs=[pl.BlockSpec((tb, d), lambda i, pos: (i, 0)),
                      pl.BlockSpec((tb, 128), lambda i, pos: (i, 0)),
                      par, par, pl.BlockSpec(memory_space=pl.ANY)],
            out_specs=pl.BlockSpec((tb, d), lambda i, pos: (i, 0)),
            scratch_shapes=[pltpu.VMEM((2, 2, tb, d), F32), pltpu.SemaphoreType.DMA((2,))],
        ),
        compiler_params=_cparams(("arbitrary",)),
        name="moe_combine_ln",
    )(pos, x, gates, g.reshape(-1, 1, d), b.reshape(-1, 1, d), y)


def _moe_layer(x, router_w, router_b, w1, w3, w2, ln_g, ln_b, lead_moe, lead_ln, *, alpha, tm=512, tn=512):
    n, d = x.shape
    rw = jnp.pad(router_w[lead_moe], ((0, 0), (0, 128 - MOE_E)))
    rb = jnp.pad(router_b[lead_moe], (0, 128 - MOE_E)).reshape(1, 128)
    top_e, gates, xp = _router(x, rw, rb)
    flat_e = top_e[:, :MOE_TOPK].reshape(-1)
    onehot = (flat_e[:, None] == jnp.arange(MOE_E, dtype=jnp.int32)[None, :]).astype(jnp.int32)
    rank = jnp.sum((jnp.cumsum(onehot, axis=0) - onehot) * onehot, axis=1)
    counts = jnp.sum(onehot, axis=0)
    padded = (counts + tm - 1) // tm * tm
    end_padded = jnp.cumsum(padded)
    start_padded = end_padded - padded
    pos = (start_padded[flat_e] + rank).astype(jnp.int32)
    nb = -(-(n * MOE_TOPK + MOE_E * (tm - 1)) // tm)
    block_start = jnp.arange(nb, dtype=jnp.int32) * tm
    block_e = jnp.sum((block_start[:, None] >= end_padded[None, :]).astype(jnp.int32), axis=1)
    block_e = jnp.minimum(block_e, MOE_E - 1)
    n_active = (end_padded[-1] // tm).astype(jnp.int32).reshape(1)
    block_first = jnp.concatenate([jnp.ones((1,), jnp.int32), (block_e[1:] != block_e[:-1]).astype(jnp.int32)])
    blk = jnp.arange(nb, dtype=jnp.int32)
    run_start = (block_first == 1) & (blk < n_active[0])
    later_start = jnp.where((blk[None, :] > blk[:, None]) & run_start[None, :], blk[None, :], nb)
    next_start = jnp.min(later_start, axis=1)
    next_e = jnp.where(next_start < nb, block_e[jnp.minimum(next_start, nb - 1)], -1).astype(jnp.int32)
    a_sorted = _dispatch(pos, xp, jnp.zeros((nb * tm, d // 2), jnp.uint32))
    y = _moe_ffn(block_e, block_first, n_active, next_e, a_sorted, w1, w3, w2, lead_moe, tm=tm, tn=tn)
    return _combine_ln(pos, x, gates, ln_g, ln_b, y, lead_ln, alpha=alpha)


def kernel(x, positions, w_in, w_gate_up, b_gate, gla_norm_g, lambda_q1, lambda_k1, lambda_q2, lambda_k2,
           diff_norm_g, w_out, ln1_g, ln1_b, ln2_g, ln2_b, ffn_w1, ffn_w3, ffn_w2, router_w, router_b,
           moe_w1, moe_w3, moe_w2):
    batch, seq, d = x.shape
    n = batch * seq
    depth = w_in.shape[0]
    alpha = (2 * depth) ** 0.25
    xf = x.reshape(n, d)
    xb = xf.astype(BF16)
    pos = positions.reshape(n)
    tab_a, half_a = _rope_tables(pos, DIFF_D)
    tab_b, half_b = _rope_tables(pos, DSA_DH)
    plain = lambda i, j, k: (i, k)
    w_in_t = jnp.swapaxes(w_in, 1, 2)
    for l in range(depth):
        ha = _matmul(xb, w_in_t, n_out=W_GLA, k_steps=1, tm=1024, tn=HM_TN, tk=d, out_dtype=BF16,
                     a_map=plain, w_map=lambda i, j, k, l=l: (l, j, k), w_lead=True, w_t=True,
                     name="in_proj_gla")
        hm = _matmul(xb, w_in_t, n_out=HM_TILES * HM_TN, k_steps=1, tm=1024, tn=HM_TN, tk=d,
                     out_dtype=BF16, a_map=plain,
                     w_map=lambda i, j, k, l=l: (l, HM_BASE // HM_TN + j, k), w_lead=True, w_t=True,
                     n_valid=N_IN, col_start=lambda j: HM_BASE + j * HM_TN, name="in_proj_attn")
        small_tiles = (OFF_GLR // 128, OFF_IK // 128)
        pick = lambda j: jnp.where(j == 0, small_tiles[0], small_tiles[1])
        hs = _matmul(xb, w_in_t, n_out=256, k_steps=1, tm=1024, tn=128, tk=d, out_dtype=F32,
                     a_map=plain, w_map=lambda i, j, k, l=l: (l, pick(j), k), w_lead=True, w_t=True,
                     n_valid=N_IN, col_start=lambda j: pick(j) * 128, name="in_proj_small")
        glr = hs[:, OFF_GLR % 128:OFF_GLR % 128 + GLA_RANK]
        iw0 = 128 + OFF_IW % 128
        wt = hs[:, iw0:iw0 + IDX_HEADS].reshape(batch, seq, IDX_HEADS).transpose(0, 2, 1)
        dq, dk, dv, sq, sk, sv, iq, ikd = _split_rope(hm, hs, tab_a, half_a, tab_b, half_b)
        svt = sv.reshape(batch, seq, W_SKV).transpose(0, 2, 1)
        dvt = dv.reshape(batch, seq, W_DIFF).transpose(0, 2, 1)

        mix = jnp.zeros((n, W_MIX), BF16)
        mix = _gla(ha, glr, w_gate_up, b_gate, gla_norm_g, mix, l, batch=batch, seq=seq)
        lambda_init = 0.8 - 0.6 * math.exp(-0.3 * l)
        mix = _diff_attention(dq, dk, dvt, lambda_q1, lambda_k1, lambda_q2, lambda_k2, diff_norm_g, mix, l,
                              batch=batch, seq=seq, lambda_init=lambda_init, tq=1024, heads=1)
        mix = _dsa(iq, sq, ikd, wt, sk, svt, mix, batch=batch, seq=seq, tq=256, tk=512)

        proj = _out_proj(mix, w_out, l)
        xf, xb = _add_ln(xf, proj, ln1_g, ln1_b, l, alpha=alpha)

        j = l // 2
        if l % 2 == 0:
            f_dim = ffn_w2.shape[1]
            f_pad = -(-f_dim // 1024) * 1024
            hid = _swiglu_up(xb, ffn_w1, ffn_w3, j, tm=2048, tn=256, tk=d, f_out=f_pad)
            down = _matmul(hid, ffn_w2, n_out=d, k_steps=8, tm=2048, tn=1024, tk=f_pad // 8, out_dtype=F32,
                           a_map=plain, w_map=lambda i, j_, k, j=j: (j, k, j_), w_lead=True, k_valid=f_dim,
                           name="ffn_down")
            xf, xb = _add_ln(xf, down, ln2_g, ln2_b, l, alpha=alpha)
        else:
            xf = _moe_layer(xf, router_w, router_b, moe_w1, moe_w3, moe_w2, ln2_g, ln2_b, j, l, alpha=alpha)
            xb = xf.astype(BF16)
    return xf.reshape(batch, seq, d)
```
